```python
import jax, jax.numpy as jnp
from jax import lax
import numpy as np

D_MODEL = 1024
BATCH = 32
SEQ = 2048
DEPTH = 2

HEAD_DIM = 64
N_BRANCHES = 4
BRANCH_WIDTH = D_MODEL // 4
ROPE_THETA = 500000.0
ROPE_DIM = HEAD_DIM // 4
NORM_EPS = 1e-6
NEG_INF = -1e30
QBLK = 128

NSA_HEADS = BRANCH_WIDTH // HEAD_DIM
NSA_GROUPS = 2
NSA_REP = NSA_HEADS // NSA_GROUPS
NSA_KV = NSA_GROUPS * HEAD_DIM
CMP_LEN = 32
CMP_STRIDE = 16
CMP_HIDDEN = 128
SEL_BLOCK = 64
SEL_TOPN = 8
SEL_QBLK = 32
SEL_FORCE = 1e30
WINDOW = 512

POOL_WINDOWS = (2, 4, 8, 16)
POOL_GROUPS = len(POOL_WINDOWS)
POOL_GROUP_DIM = BRANCH_WIDTH // POOL_GROUPS

RWKV_HEADS = BRANCH_WIDTH // HEAD_DIM
RWKV_DECAY_RANK = 32
RWKV_A_RANK = 32
RWKV_V_RANK = 32
RWKV_G_RANK = 64
RWKV_GN_EPS = 64e-5

GLA_HEADS = 4
GLA_DV = BRANCH_WIDTH // GLA_HEADS
GLA_DK = GLA_DV // 2
GLA_GATE_RANK = 16
GLA_TAU = 16.0
GLA_CHUNK = 16

NSA_IN = NSA_HEADS * HEAD_DIM + 6 * NSA_KV + 3 * NSA_HEADS
POOL_IN = BRANCH_WIDTH
RWKV_IN = 3 * BRANCH_WIDTH + RWKV_DECAY_RANK + RWKV_A_RANK + RWKV_G_RANK
GLA_IN = 2 * GLA_HEADS * GLA_DK + 2 * BRANCH_WIDTH + GLA_GATE_RANK
IN_WIDTH = NSA_IN + POOL_IN + RWKV_IN + GLA_IN

MEM_LEN = 256
CA_HEADS = 4
CA_WIDTH = CA_HEADS * HEAD_DIM

D_FF = 2816
N_EXPERTS = 8
TOP_K = 2
N_DENSE = (DEPTH + 1) // 2
N_MOE = DEPTH // 2

kernel_name = 'hybrid_nsa_pool_rwkv7_gla_moe_block'


def _split(t, sizes):
    return jnp.split(t, np.cumsum(sizes)[:-1].tolist(), axis=-1)


def rms_norm(x, g, eps=NORM_EPS):
    xf = x.astype(jnp.float32)
    y = xf * lax.rsqrt(jnp.mean(xf * xf, axis=-1, keepdims=True) + eps)
    return (y * g.astype(jnp.float32)).astype(x.dtype)


def rope_tables(positions):
    inv_freq = ROPE_THETA ** (-jnp.arange(0, ROPE_DIM, 2, dtype=jnp.float32) / ROPE_DIM)
    ang = positions.astype(jnp.float32)[..., None] * inv_freq
    return jnp.cos(ang), jnp.sin(ang)


def apply_partial_rope(t, cos, sin):
    half = ROPE_DIM // 2
    c = cos[:, :, None, :].astype(t.dtype)
    s = sin[:, :, None, :].astype(t.dtype)
    t1, t2, rest = t[..., :half], t[..., half:ROPE_DIM], t[..., ROPE_DIM:]
    return jnp.concatenate([t1 * c - t2 * s, t2 * c + t1 * s, rest], axis=-1)


def masked_softmax(scores, mask):
    s = jnp.where(mask, scores.astype(jnp.float32), NEG_INF)
    return jnp.where(mask, jax.nn.softmax(s, axis=-1), 0.0)


def nsa_mixer(p, cos, sin, cmp_pe, cmp_w1, cmp_w2):
    B, S, _ = p.shape
    G, R, hd = NSA_GROUPS, NSA_REP, HEAD_DIM
    q, kc, vc, ks, vs, kw, vw, gates = _split(p, (NSA_HEADS * hd,) + (NSA_KV,) * 6 + (3 * NSA_HEADS,))
    q = apply_partial_rope(q.reshape(B, S, NSA_HEADS, hd), cos, sin) * (hd ** -0.5)
    q = q.reshape(B, S, G, R, hd)
    kvs = lambda t: t.reshape(B, S, G, hd)
    t_idx = np.arange(S)

    n_cmp = (S - CMP_LEN) // CMP_STRIDE + 1
    cmp_start = np.arange(n_cmp) * CMP_STRIDE
    cmp_idx = cmp_start[:, None] + np.arange(CMP_LEN)[None, :]
    cmp_end = cmp_start + CMP_LEN - 1

    def compress(t, j):
        blocks = t[:, cmp_idx] + cmp_pe[j][:, None, :]
        hid = jax.nn.gelu(jnp.einsum('bnlgd,ldf->bngf', blocks, cmp_w1[j]))
        return jnp.einsum('bngf,fd->bngd', hid, cmp_w2[j])

    k_cmp = apply_partial_rope(compress(kvs(kc), 0), cos[:, cmp_end], sin[:, cmp_end])
    v_cmp = compress(kvs(vc), 1)
    cmp_mask = cmp_end[None, :] <= t_idx[:, None]
    p_cmp = masked_softmax(jnp.einsum('bsgrd,bngd->bgrsn', q, k_cmp), cmp_mask)
    o_cmp = jnp.einsum('bgrsn,bngd->bsgrd', p_cmp.astype(v_cmp.dtype), v_cmp)

    n_slc = S // SEL_BLOCK
    slc_start = np.arange(n_slc) * SEL_BLOCK
    cover = np.clip(np.minimum(cmp_start[:, None] + CMP_LEN, slc_start[None, :] + SEL_BLOCK)
                    - np.maximum(cmp_start[:, None], slc_start[None, :]), 0, None) / CMP_LEN
    imp = jnp.einsum('bgrsn,nj->bgsj', p_cmp, jnp.asarray(cover, jnp.float32))
    cur = t_idx[:, None] // SEL_BLOCK
    blk = np.arange(n_slc)[None, :]
    forced = (blk == 0) | (blk == cur) | (blk == cur - 1)
    causal_blk = slc_start[None, :] <= t_idx[:, None]
    imp = jnp.where(forced, SEL_FORCE, jnp.where(causal_blk, imp, -SEL_FORCE))
    n_top = min(SEL_TOPN, n_slc)
    _, sel = lax.top_k(imp, n_top)

    k_sel = apply_partial_rope(kvs(ks), cos, sin).transpose(0, 2, 1, 3)
    v_sel = kvs(vs).transpose(0, 2, 1, 3)
    n_sq = S // SEL_QBLK
    n_keys = n_top * SEL_BLOCK
    gather_rows = jax.vmap(jax.vmap(lambda src, idx: src[idx]))

    def sel_block(args):
        i, qb, sb = args
        tok = (sb[..., None] * SEL_BLOCK + np.arange(SEL_BLOCK)).reshape(B, G, SEL_QBLK * n_keys)
        kg = gather_rows(k_sel, tok).reshape(B, G, SEL_QBLK, n_keys, hd)
        vg = gather_rows(v_sel, tok).reshape(B, G, SEL_QBLK, n_keys, hd)
        qpos = i * SEL_QBLK + jnp.arange(SEL_QBLK)
        mask = tok.reshape(B, G, 1, SEL_QBLK, n_keys) <= qpos[:, None]
        pr = masked_softmax(jnp.einsum('bqgrd,bgqkd->bgrqk', qb, kg), mask)
        return jnp.einsum('bgrqk,bgqkd->bqgrd', pr.astype(vg.dtype), vg)

    q_sb = jnp.moveaxis(q.reshape(B, n_sq, SEL_QBLK, G, R, hd), 1, 0)
    sel_b = jnp.moveaxis(sel.reshape(B, G, n_sq, SEL_QBLK, n_top), 2, 0)
    o_sel = lax.map(sel_block, (jnp.arange(n_sq), q_sb, sel_b))
    o_sel = jnp.moveaxis(o_sel, 0, 1).reshape(B, S, G, R, hd)

    pad = ((0, 0), (WINDOW, 0), (0, 0), (0, 0))
    k_win = jnp.pad(apply_partial_rope(kvs(kw), cos, sin), pad)
    v_win = jnp.pad(kvs(vw), pad)
    span = WINDOW + QBLK
    rel = jnp.arange(span) - WINDOW

    def win_block(args):
        i, qb = args
        start = i * QBLK
        kb = lax.dynamic_slice_in_dim(k_win, start, span, axis=1)
        vb = lax.dynamic_slice_in_dim(v_win, start, span, axis=1)
        qpos = start + jnp.arange(QBLK)
        kpos = start + rel
        dist = qpos[:, None] - kpos[None, :]
        mask = (dist >= 0) & (dist < WINDOW) & (kpos[None, :] >= 0)
        pr = masked_softmax(jnp.einsum('bqgrd,bkgd->bgrqk', qb, kb), mask)
        return jnp.einsum('bgrqk,bkgd->bqgrd', pr.astype(vb.dtype), vb)

    n_wb = S // QBLK
    q_wb = jnp.moveaxis(q.reshape(B, n_wb, QBLK, G, R, hd), 1, 0)
    o_win = jnp.moveaxis(lax.map(win_block, (jnp.arange(n_wb), q_wb)), 0, 1).reshape(B, S, G, R, hd)

    g = jax.nn.sigmoid(gates.reshape(B, S, G, R, 3))
    o = g[..., 0:1] * o_cmp + g[..., 1:2] * o_sel + g[..., 2:3] * o_win
    return o.reshape(B, S, NSA_HEADS * hd)


def pool_mixer(u, w, scale):
    B, S, _ = u.shape
    uf = u.astype(jnp.float32)
    c = jnp.cumsum(uf, axis=1)
    count = jnp.arange(1, S + 1, dtype=jnp.float32)[None, :, None]
    outs = []
    for gi, win in enumerate(POOL_WINDOWS):
        sl = slice(gi * POOL_GROUP_DIM, (gi + 1) * POOL_GROUP_DIM)
        cg = c[..., sl]
        prev = jnp.pad(cg, ((0, 0), (win, 0), (0, 0)))[:, :S]
        outs.append((cg - prev) / jnp.minimum(count, win) - uf[..., sl])
    d = jnp.stack(outs, axis=2).astype(u.dtype)
    y = jnp.einsum('bsgc,gcd->bsgd', d, w).reshape(B, S, BRANCH_WIDTH)
    return y * scale


def token_shift(t):
    return jnp.pad(t, ((0, 0), (1, 0), (0, 0)))[:, :-1]


def rwkv7_mixer(p, mu, w0, w2, a0, a2, g2, k_k, k_a, r_k, ln_g, v_first, v_res_logit):
    B, S, _ = p.shape
    H, hd, W = RWKV_HEADS, HEAD_DIM, BRANCH_WIDTH
    f32 = jnp.float32
    dtype = p.dtype
    p = p.astype(f32)
    p = p + (token_shift(p) - p) * mu
    r, k, v, wd, ad, gd = _split(p, (W, W, W, RWKV_DECAY_RANK, RWKV_A_RANK, RWKV_G_RANK))
    w_log = -jax.nn.softplus(-(w0 + jnp.tanh(wd) @ w2)) - 0.5
    decay = jnp.exp(-jnp.exp(w_log))
    a = jax.nn.sigmoid(a0 + ad @ a2)
    g = jax.nn.sigmoid(gd) @ g2
    if v_first is not None:
        v = v + (v_first - v) * jax.nn.sigmoid(v_res_logit)
    heads = lambda t: t.reshape(B, S, H, hd)
    kk = heads(k * k_k)
    kk = kk / jnp.maximum(jnp.linalg.norm(kk, axis=-1, keepdims=True), 1e-12)
    k = k * (1.0 + (a - 1.0) * k_a)
    b = kk * heads(a)
    xs = tuple(jnp.moveaxis(t, 1, 0) for t in (heads(r), heads(decay), heads(k), heads(v), kk, b))

    def step(state, inp):
        r_t, w_t, k_t, v_t, kk_t, b_t = inp
        s_kk = jnp.einsum('bhvk,bhk->bhv', state, kk_t)
        state = (state * w_t[:, :, None, :] - s_kk[..., None] * b_t[:, :, None, :]
                 + v_t[..., None] * k_t[:, :, None, :])
        return state, jnp.einsum('bhvk,bhk->bhv', state, r_t)

    _, y = lax.scan(step, jnp.zeros((B, H, hd, hd), f32), xs)
    y = jnp.moveaxis(y, 0, 1)
    mean = jnp.mean(y, axis=-1, keepdims=True)
    var = jnp.mean(jnp.square(y - mean), axis=-1, keepdims=True)
    y = (y - mean) * lax.rsqrt(var + RWKV_GN_EPS) * ln_g.reshape(H, hd)
    y = y + jnp.sum(heads(r) * heads(k) * r_k, axis=-1, keepdims=True) * heads(v)
    return (y.reshape(B, S, W) * g).astype(dtype), v


def gla_mixer(p, a2, ab, norm_g):
    B, S, _ = p.shape
    H, dk, dv, C = GLA_HEADS, GLA_DK, GLA_DV, GLA_CHUNK
    f32 = jnp.float32
    dtype = p.dtype
    q, k, v, ad, og = _split(p, (H * dk, H * dk, BRANCH_WIDTH, GLA_GATE_RANK, BRANCH_WIDTH))
    log_a = jax.nn.log_sigmoid((ad @ a2 + ab).astype(f32)) / GLA_TAU
    nc = S // C
    chunks = lambda t, d: t.astype(f32).reshape(B, nc, C, H, d)
    q = chunks(q, dk) * (dk ** -0.5)
    k = chunks(k, dk)
    v = chunks(v, dv)
    bcum = jnp.cumsum(chunks(log_a, dk), axis=2)
    blast = bcum[:, :, -1]
    q_t = q * jnp.exp(bcum)
    k_t = k * jnp.exp(-bcum)
    k_end = k * jnp.exp(blast[:, :, None] - bcum)
    causal = np.tril(np.ones((C, C), dtype=bool))
    att = jnp.where(causal, jnp.einsum('bnthk,bnshk->bnhts', q_t, k_t), 0.0)
    o_intra = jnp.einsum('bnhts,bnshv->bnthv', att, v)

    def step(state, inp):
        kc, vc, dl = inp
        new = state * dl[..., None] + jnp.einsum('bchk,bchv->bhkv', kc, vc)
        return new, state

    xs = (jnp.moveaxis(k_end, 1, 0), jnp.moveaxis(v, 1, 0), jnp.moveaxis(jnp.exp(blast), 1, 0))
    _, s_prev = lax.scan(step, jnp.zeros((B, H, dk, dv), f32), xs)
    o_inter = jnp.einsum('bnthk,nbhkv->bnthv', q_t, s_prev)
    o = (o_intra + o_inter).reshape(B, S, H, dv)
    o = o * lax.rsqrt(jnp.mean(o * o, axis=-1, keepdims=True) + NORM_EPS) * norm_g
    return (o.reshape(B, S, BRANCH_WIDTH) * jax.nn.silu(og.astype(f32))).astype(dtype)


def cross_attention(hc, mem_n, wq, wkv, wo):
    B, S, _ = hc.shape
    M = mem_n.shape[1]
    q = (hc @ wq).reshape(B, S, CA_HEADS, HEAD_DIM) * (HEAD_DIM ** -0.5)
    k, v = _split(mem_n @ wkv, (CA_WIDTH, CA_WIDTH))
    k = k.reshape(B, M, CA_HEADS, HEAD_DIM)
    v = v.reshape(B, M, CA_HEADS, HEAD_DIM)
    pr = jax.nn.softmax(jnp.einsum('bshd,bmhd->bhsm', q, k).astype(jnp.float32), axis=-1)
    o = jnp.einsum('bhsm,bmhd->bshd', pr.astype(v.dtype), v).reshape(B, S, CA_WIDTH)
    return o @ wo


def swiglu(x, wg, wu, wd):
    return (jax.nn.silu(x @ wg) * (x @ wu)) @ wd


def moe_swiglu(x, router, wg, wu, wd):
    logits = (x @ router).astype(jnp.float32)
    top_val, top_idx = lax.top_k(logits, TOP_K)
    weights = jax.nn.softmax(top_val, axis=-1)
    combine = jnp.sum(jax.nn.one_hot(top_idx, N_EXPERTS, dtype=jnp.float32) * weights[..., None], axis=-2)
    out = jnp.zeros_like(x)
    for e in range(N_EXPERTS):
        out = out + combine[..., e:e + 1].astype(x.dtype) * swiglu(x, wg[e], wu[e], wd[e])
    return out


def setup_inputs(seed: int = 0) -> dict:
    key = jax.random.key(seed)
    keys = iter(jax.random.split(key, 64))
    f32 = jnp.float32

    def normal(shape, scale):
        return jax.random.normal(next(keys), shape, f32) * scale

    def gain(shape):
        return 1.0 + normal(shape, 0.05)

    def uniform(shape, lo, hi):
        return jax.random.uniform(next(keys), shape, f32, lo, hi)

    L, D, W = DEPTH, D_MODEL, BRANCH_WIDTH
    x = normal((BATCH, SEQ, D), 1.0)
    mem = normal((BATCH, MEM_LEN, D), 1.0)
    start = jax.random.randint(next(keys), (BATCH, 1), 0, 4096, dtype=jnp.int32)
    positions = start + jnp.arange(SEQ, dtype=jnp.int32)[None, :]
    return {
        'x': x,
        'mem': mem,
        'positions': positions,
        'norm_mix': gain((L, D)),
        'w_in': normal((L, D, IN_WIDTH), D ** -0.5),
        'nsa_cmp_pe': normal((L, 2, CMP_LEN, HEAD_DIM), 0.1),
        'nsa_cmp_w1': normal((L, 2, CMP_LEN, HEAD_DIM, CMP_HIDDEN), (CMP_LEN * HEAD_DIM) ** -0.5),
        'nsa_cmp_w2': normal((L, 2, CMP_HIDDEN, HEAD_DIM), CMP_HIDDEN ** -0.5),
        'pool_w': normal((L, POOL_GROUPS, POOL_GROUP_DIM, POOL_GROUP_DIM), POOL_GROUP_DIM ** -0.5),
        'pool_scale': gain((L, W)),
        'rwkv_mu': uniform((L, RWKV_IN), 0.0, 1.0),
        'rwkv_w0': uniform((L, W), -6.0, 1.0),
        'rwkv_w2': normal((L, RWKV_DECAY_RANK, W), 0.5 * RWKV_DECAY_RANK ** -0.5),
        'rwkv_a0': normal((L, W), 0.1),
        'rwkv_a2': normal((L, RWKV_A_RANK, W), RWKV_A_RANK ** -0.5),
        'rwkv_g2': normal((L, RWKV_G_RANK, W), RWKV_G_RANK ** -0.5),
        'rwkv_v0': normal((L - 1, W), 0.1),
        'rwkv_v1': normal((L - 1, D, RWKV_V_RANK), D ** -0.5),
        'rwkv_v2': normal((L - 1, RWKV_V_RANK, W), RWKV_V_RANK ** -0.5),
        'rwkv_k_k': 0.85 + normal((L, W), 0.05),
        'rwkv_k_a': gain((L, W)),
        'rwkv_r_k': normal((L, RWKV_HEADS, HEAD_DIM), 0.1),
        'rwkv_ln': gain((L, W)),
        'gla_a2': normal((L, GLA_GATE_RANK, GLA_HEADS * GLA_DK), GLA_GATE_RANK ** -0.5),
        'gla_ab': uniform((L, GLA_HEADS * GLA_DK), 0.0, 4.0),
        'gla_norm': gain((L, GLA_DV)),
        'w_gate': normal((L, N_BRANCHES, D, D), D ** -0.5),
        'b_gate': normal((L, N_BRANCHES, D), 0.1),
        'w_branch': normal((L, N_BRANCHES, W, D), W ** -0.5),
        'w_out': normal((L, D, D), D ** -0.5),
        'norm_ca': gain((L, D)),
        'mem_norm': gain((D,)),
        'w_ca_q': normal((L, D, CA_WIDTH), D ** -0.5),
        'w_ca_kv': normal((L, D, 2 * CA_WIDTH), D ** -0.5),
        'w_ca_o': normal((L, CA_WIDTH, D), CA_WIDTH ** -0.5),
        'norm_ffn': gain((L, D)),
        'ffn_w_gate': normal((N_DENSE, D, D_FF), D ** -0.5),
        'ffn_w_up': normal((N_DENSE, D, D_FF), D ** -0.5),
        'ffn_w_down': normal((N_DENSE, D_FF, D), D_FF ** -0.5),
        'moe_router': normal((N_MOE, D, N_EXPERTS), D ** -0.5),
        'moe_w_gate': normal((N_MOE, N_EXPERTS, D, D_FF), D ** -0.5),
        'moe_w_up': normal((N_MOE, N_EXPERTS, D, D_FF), D ** -0.5),
        'moe_w_down': normal((N_MOE, N_EXPERTS, D_FF, D), D_FF ** -0.5),
        'norm_final': gain((D,)),
    }


def reference(x, mem, positions, norm_mix, w_in, nsa_cmp_pe, nsa_cmp_w1, nsa_cmp_w2, pool_w, pool_scale,
              rwkv_mu, rwkv_w0, rwkv_w2, rwkv_a0, rwkv_a2, rwkv_g2, rwkv_v0, rwkv_v1, rwkv_v2,
              rwkv_k_k, rwkv_k_a, rwkv_r_k, rwkv_ln, gla_a2, gla_ab, gla_norm,
              w_gate, b_gate, w_branch, w_out, norm_ca, mem_norm, w_ca_q, w_ca_kv, w_ca_o,
              norm_ffn, ffn_w_gate, ffn_w_up, ffn_w_down,
              moe_router, moe_w_gate, moe_w_up, moe_w_down, norm_final):
    cos, sin = rope_tables(positions)
    mem_n = rms_norm(mem, mem_norm)
    v_first = None
    for l in range(DEPTH):
        h = rms_norm(x, norm_mix[l])
        p_nsa, p_pool, p_rwkv, p_gla = _split(h @ w_in[l], (NSA_IN, POOL_IN, RWKV_IN, GLA_IN))
        o_nsa = nsa_mixer(p_nsa, cos, sin, nsa_cmp_pe[l], nsa_cmp_w1[l], nsa_cmp_w2[l])
        o_pool = pool_mixer(p_pool, pool_w[l], pool_scale[l])
        v_res_logit = None if l == 0 else rwkv_v0[l - 1] + (h @ rwkv_v1[l - 1]) @ rwkv_v2[l - 1]
        o_rwkv, v_l = rwkv7_mixer(p_rwkv, rwkv_mu[l], rwkv_w0[l], rwkv_w2[l], rwkv_a0[l], rwkv_a2[l],
                                  rwkv_g2[l], rwkv_k_k[l], rwkv_k_a[l], rwkv_r_k[l], rwkv_ln[l],
                                  v_first, v_res_logit)
        if l == 0:
            v_first = v_l
        o_gla = gla_mixer(p_gla, gla_a2[l], gla_ab[l], gla_norm[l])
        merged = jnp.zeros_like(x)
        for i, o in enumerate((o_nsa, o_pool, o_rwkv, o_gla)):
            gate = jax.nn.sigmoid(h @ w_gate[l, i] + b_gate[l, i])
            merged = merged + gate * (o @ w_branch[l, i])
        x = x + merged @ w_out[l]
        x = x + cross_attention(rms_norm(x, norm_ca[l]), mem_n, w_ca_q[l], w_ca_kv[l], w_ca_o[l])
        hf = rms_norm(x, norm_ffn[l])
        j = l // 2
        if l % 2 == 0:
            x = x + swiglu(hf, ffn_w_gate[j], ffn_w_up[j], ffn_w_down[j])
        else:
            x = x + moe_swiglu(hf, moe_router[j], moe_w_gate[j], moe_w_up[j], moe_w_down[j])
    return rms_norm(x, norm_final)
```

```python
import functools

import jax
import jax.numpy as jnp
import numpy as np
from jax import lax
from jax.experimental import pallas as pl
from jax.experimental.pallas import tpu as pltpu

F32 = jnp.float32
BF16 = jnp.bfloat16

D_MODEL = 1024
HEAD_DIM = 64
BRANCH_WIDTH = 256
ROPE_THETA = 500000.0
ROPE_DIM = 16
NORM_EPS = 1e-6
NEG_INF = -1e30

NSA_KV = 128
CMP_LEN = 32
CMP_STRIDE = 16
CMP_HIDDEN = 128
SEL_BLOCK = 64
SEL_TOPN = 8
WINDOW = 512

RWKV_IN = 896
RWKV_GN_EPS = 64e-5
RWKV_CHUNK = 64

GLA_DK = 32
GLA_TAU = 16.0
GLA_CHUNK = 64

CA_WIDTH = 256
D_FF = 2816
N_EXPERTS = 8

LANES = 128
VMEM_LIMIT = 56 * 1024 * 1024

_C_Q = 0
_C_KV = 256
_C_CV = 768
_C_GATE = 1024
_C_POOL = 1152
_C_RWKV = 1408
_C_GLA = 2304
_C_END = 3200


def _cparams(sem):
    return pltpu.CompilerParams(dimension_semantics=sem, vmem_limit_bytes=VMEM_LIMIT)


def _rms(x, g):
    ms = jnp.mean(x * x, axis=-1, keepdims=True)
    return x * lax.rsqrt(ms + NORM_EPS) * g


def _rope128(t, cos, sin):
    lane = lax.broadcasted_iota(jnp.int32, t.shape, 1) % HEAD_DIM
    partner = jnp.where(lane < ROPE_DIM // 2, pltpu.roll(t, LANES - ROPE_DIM // 2, 1),
                        pltpu.roll(t, ROPE_DIM // 2, 1))
    return t * cos + partner * sin


def _inproj_kernel(x_ref, g_ref, w_ref, cos_ref, sin_ref, mu_ref,
                   h_ref, q_ref, kv_ref, cv_ref, gate_ref, u_ref, rw_ref, gl_ref, prev_ref):
    s = pl.program_id(1)
    hb = _rms(x_ref[0], g_ref[...]).astype(BF16)
    h_ref[0] = hb

    def mm(a, b):
        return jnp.dot(hb, w_ref[:, a:b], preferred_element_type=F32)

    cos = cos_ref[0]
    sin = sin_ref[0]
    scale = HEAD_DIM ** -0.5
    q_ref[0, :, 0:128] = (_rope128(mm(_C_Q, _C_Q + 128), cos, sin) * scale).astype(BF16)
    q_ref[0, :, 128:256] = (_rope128(mm(_C_Q + 128, _C_Q + 256), cos, sin) * scale).astype(BF16)
    kv_ref[0, :, 0:128] = _rope128(mm(_C_KV, _C_KV + 128), cos, sin).astype(BF16)
    kv_ref[0, :, 128:256] = mm(_C_KV + 128, _C_KV + 256).astype(BF16)
    kv_ref[0, :, 256:384] = _rope128(mm(_C_KV + 256, _C_KV + 384), cos, sin).astype(BF16)
    kv_ref[0, :, 384:512] = mm(_C_KV + 384, _C_KV + 512).astype(BF16)
    cv_ref[0] = mm(_C_CV, _C_CV + 256).astype(BF16)
    gate_ref[0] = mm(_C_GATE, _C_GATE + 128)
    u_ref[0] = mm(_C_POOL, _C_POOL + 256)
    gl_ref[0] = mm(_C_GLA, _C_END)

    p = mm(_C_RWKV, _C_RWKV + RWKV_IN)
    tm = p.shape[0]

    @pl.when(s == 0)
    def _():
        prev_ref[...] = jnp.zeros_like(prev_ref)

    row = lax.broadcasted_iota(jnp.int32, p.shape, 0)
    shifted = jnp.where(row == 0, prev_ref[0:1, :], pltpu.roll(p, 1, 0))
    prev_ref[0:1, :] = p[tm - 1:tm, :]
    rw_ref[0] = p + (shifted - p) * mu_ref[...]


def _inproj(x, g, w_all, cosf, sinf, mu, tm=256):
    B, S, D = x.shape
    tok = lambda w: pl.BlockSpec((1, tm, w), lambda b, s: (b, s, 0))
    full = lambda a: pl.BlockSpec(a.shape, lambda b, s: (0,) * a.ndim)
    outs = [(D, BF16), (256, BF16), (512, BF16), (256, BF16), (128, F32), (256, F32), (RWKV_IN, F32), (896, F32)]
    return pl.pallas_call(
        _inproj_kernel,
        grid=(B, S // tm),
        in_specs=[tok(D), full(g), full(w_all), tok(128), tok(128), full(mu)],
        out_specs=[tok(w) for w, _ in outs],
        out_shape=[jax.ShapeDtypeStruct((B, S, w), dt) for w, dt in outs],
        scratch_shapes=[pltpu.VMEM((8, RWKV_IN), F32)],
        compiler_params=_cparams(("parallel", "arbitrary")),
        name="inproj",
    )(x, g, w_all, cosf, sinf, mu)


def _inproj_weights(w_in, l):
    w = w_in[l]
    nsa, pool, rwkv, gla = jnp.split(w, [1036, 1036 + 256, 1036 + 256 + 896], axis=1)
    q = nsa[:, 0:256].reshape(D_MODEL, 4, HEAD_DIM)
    qa = q[:, (0, 2), :].reshape(D_MODEL, 128)
    qb = q[:, (1, 3), :].reshape(D_MODEL, 128)
    kc, vc, ks, vs, kw, vw = (nsa[:, 256 + 128 * i:384 + 128 * i] for i in range(6))
    gates = jnp.pad(nsa[:, 1024:1036], ((0, 0), (0, 116)))
    gq, gk, gv, gad, gog = jnp.split(gla, [128, 256, 512, 528], axis=1)
    gla_r = jnp.concatenate([gq, gk, gv, gog, jnp.pad(gad, ((0, 0), (0, 112)))], axis=1)
    w_all = jnp.concatenate([qa, qb, ks, vs, kw, vw, kc, vc, gates, pool, rwkv, gla_r], axis=1)
    return w_all.astype(BF16)


def _rope_tables(positions):
    half = ROPE_DIM // 2
    inv_freq = ROPE_THETA ** (-jnp.arange(0, ROPE_DIM, 2, dtype=F32) / ROPE_DIM)
    ang = positions.astype(F32)[..., None] * inv_freq
    cos, sin = jnp.cos(ang), jnp.sin(ang)
    ones = jnp.ones(cos.shape[:-1] + (HEAD_DIM - ROPE_DIM,), F32)
    cos64 = jnp.concatenate([cos, cos, ones], axis=-1)
    sin64 = jnp.concatenate([-sin, sin, 0.0 * ones], axis=-1)
    return jnp.concatenate([cos64, cos64], axis=-1), jnp.concatenate([sin64, sin64], axis=-1)


def _gelu_tanh(x):
    return x * (0.5 * (1.0 + jnp.tanh(np.sqrt(2.0 / np.pi) * (x + 0.044715 * (x * x * x)))))


def _dot_nt(a, b):
    return lax.dot_general(a, b, (((1,), (1,)), ((), ())), preferred_element_type=F32)


def _compress_kernel(x_ref, wa_ref, wb_ref, bias_ref, w2k_ref, w2v_ref, cos_ref, sin_ref, kc_ref, vc_ref):
    x = x_ref[0]
    a = jnp.dot(x, wa_ref[...], preferred_element_type=F32)
    b = jnp.dot(x, wb_ref[...], preferred_element_type=F32)
    n = a.shape[0]
    hid = _gelu_tanh(a + pltpu.roll(b, n - 1, 0) + bias_ref[...])
    kc = jnp.dot(hid[:, 0:256].astype(BF16), w2k_ref[...], preferred_element_type=F32)
    vc = jnp.dot(hid[:, 256:512].astype(BF16), w2v_ref[...], preferred_element_type=F32)
    kc_ref[0] = _rope128(kc, cos_ref[0], sin_ref[0]).astype(BF16)
    vc_ref[0] = vc.astype(BF16)


def _nsa_compress(cv, wa, wb, bias, w2k, w2v, cosc, sinc):
    B, S, _ = cv.shape
    nseg = S // CMP_STRIDE
    x = cv.reshape(B, nseg, CMP_STRIDE * 256)
    per_b = lambda r, w: pl.BlockSpec((1, r, w), lambda b: (b, 0, 0))
    full = lambda a: pl.BlockSpec(a.shape, lambda b: (0,) * a.ndim)
    return pl.pallas_call(
        _compress_kernel,
        grid=(B,),
        in_specs=[per_b(nseg, CMP_STRIDE * 256), full(wa), full(wb), full(bias), full(w2k), full(w2v),
                  per_b(nseg, 128), per_b(nseg, 128)],
        out_specs=[per_b(nseg, 128), per_b(nseg, 128)],
        out_shape=[jax.ShapeDtypeStruct((B, nseg, 128), BF16)] * 2,
        compiler_params=_cparams(("parallel",)),
        name="nsa_compress",
    )(x, wa, wb, bias, w2k, w2v, cosc, sinc)


def _compress_weights(pe, w1, w2):
    eye2 = jnp.eye(2, dtype=F32)
    half = lambda lo: jnp.einsum('jldf,pj,qg->lpqdjgf', w1[:, lo:lo + CMP_STRIDE], eye2, eye2).reshape(
        CMP_STRIDE * 256, 4 * CMP_HIDDEN).astype(BF16)
    bias = jnp.einsum('jld,jldf->jf', pe, w1)
    bias = jnp.broadcast_to(bias[:, None, :], (2, 2, CMP_HIDDEN)).reshape(1, 4 * CMP_HIDDEN)
    bd = lambda w: jnp.einsum('fd,gq->gfqd', w, eye2).reshape(2 * CMP_HIDDEN, 2 * HEAD_DIM).astype(BF16)
    return half(0), half(CMP_STRIDE), bias, bd(w2[0]), bd(w2[1])


def _cover_t(S):
    n_cmp = (S - CMP_LEN) // CMP_STRIDE + 1
    cmp_start = np.arange(n_cmp) * CMP_STRIDE
    slc_start = np.arange(S // SEL_BLOCK) * SEL_BLOCK
    cover = np.clip(np.minimum(cmp_start[:, None] + CMP_LEN, slc_start[None, :] + SEL_BLOCK)
                    - np.maximum(cmp_start[:, None], slc_start[None, :]), 0, None) / CMP_LEN
    out = np.zeros((S // SEL_BLOCK, S // CMP_STRIDE), np.float32)
    out[:, :n_cmp] = cover.T
    return jnp.asarray(out, BF16)


def _flash(qg, k_at, v_at, valid_at, j_lo, j_hi):
    rows = qg.shape[0]

    def body(j, carry):
        m, l, acc = carry
        s = _dot_nt(qg, k_at(j))
        valid = valid_at(j)
        s = jnp.where(valid, s, NEG_INF)
        m_new = jnp.maximum(m, jnp.max(s, axis=1, keepdims=True))
        alpha = jnp.exp(m - m_new)
        p = jnp.where(valid, jnp.exp(s - m_new), 0.0)
        l = alpha * l + jnp.sum(p, axis=1, keepdims=True)
        acc = alpha * acc + jnp.dot(p.astype(BF16), v_at(j), preferred_element_type=F32)
        return m_new, l, acc

    init = (jnp.full((rows, 1), NEG_INF, F32), jnp.zeros((rows, 1), F32), jnp.zeros((rows, LANES), F32))
    _, l, acc = lax.fori_loop(j_lo, j_hi, body, init)
    return acc / l


def _nsa_kernel(q_ref, kv_ref, kc_ref, vc_ref, gate_ref, covt_ref, o_ref, *, tq, tk):
    i = pl.program_id(1)
    t0 = i * tq
    n_blk = covt_ref.shape[0]
    q = q_ref[0]
    qa, qb = q[:, 0:128], q[:, 128:256]
    lane_grp = lax.broadcasted_iota(jnp.int32, (tq, LANES), 1) // HEAD_DIM
    gates = jax.nn.sigmoid(gate_ref[0])
    tok = t0 + (lax.broadcasted_iota(jnp.int32, (2 * tq, 1), 0) & (tq - 1))
    kc, vc = kc_ref[0], vc_ref[0]

    blk = lax.broadcasted_iota(jnp.int32, (n_blk, tq), 0)
    tok_t = t0 + lax.broadcasted_iota(jnp.int32, (n_blk, tq), 1)
    cur = tok_t // SEL_BLOCK
    forced = (blk == 0) | (blk == cur) | (blk == cur - 1)
    causal_blk = blk * SEL_BLOCK <= tok_t
    eye = (lax.broadcasted_iota(jnp.int32, (tq, tq), 0) == lax.broadcasted_iota(jnp.int32, (tq, tq), 1)).astype(BF16)

    def kv_tile(j, width, col):
        return kv_ref[0, pl.ds(pl.multiple_of(j * width, width), width), col:col + 128]

    out = [jnp.zeros((tq, LANES), F32), jnp.zeros((tq, LANES), F32)]
    for g in range(2):
        in_g = lane_grp == g
        qg = jnp.concatenate([jnp.where(in_g, qa, 0), jnp.where(in_g, qb, 0)], axis=0)

        s = _dot_nt(qg, kc)
        ncol = lax.broadcasted_iota(jnp.int32, s.shape, 1)
        valid = ncol * CMP_STRIDE + (CMP_LEN - 1) <= tok
        s = jnp.where(valid, s, NEG_INF)
        e = jnp.where(valid, jnp.exp(s - jnp.max(s, axis=1, keepdims=True)), 0.0)
        den = jnp.sum(e, axis=1, keepdims=True)
        p = e / jnp.where(den > 0.0, den, 1.0)
        o_cmp = jnp.dot(p.astype(BF16), vc, preferred_element_type=F32)

        ptot = p[0:tq] + p[tq:2 * tq]
        p_hi = ptot.astype(BF16)
        p_lo = (ptot - p_hi.astype(F32)).astype(BF16)
        imp = _dot_nt(covt_ref[...], p_hi) + _dot_nt(covt_ref[...], p_lo)
        imp = jnp.where(forced, 1e30, jnp.where(causal_blk, imp, -1e30))
        cnt = jnp.zeros((n_blk, tq), F32)
        for i2 in range(n_blk):
            row = imp[i2:i2 + 1, :]
            beats = (row > imp) | ((row == imp) & (blk > i2))
            cnt = cnt + jnp.where(beats, 1.0, 0.0)
        sel_t = jnp.where(cnt < float(SEL_TOPN), 1.0, 0.0).astype(BF16)
        sel = _dot_nt(eye, sel_t).astype(BF16)
        sel2 = jnp.concatenate([sel, sel], axis=0)

        def sel_valid(j):
            col = j * tk + lax.broadcasted_iota(jnp.int32, (2 * tq, tk), 1)
            blk_of_col = (j * tk + lax.broadcasted_iota(jnp.int32, (n_blk, tk), 1)) // SEL_BLOCK
            expand = (lax.broadcasted_iota(jnp.int32, (n_blk, tk), 0) == blk_of_col).astype(BF16)
            chosen = jnp.dot(sel2, expand, preferred_element_type=F32)
            return (chosen > 0.5) & (col <= tok)

        o_sel = _flash(qg, lambda j: kv_tile(j, tk, 0), lambda j: kv_tile(j, tk, 128), sel_valid,
                       0, (t0 + tq + tk - 1) // tk)

        def win_valid(j):
            col = j * tq + lax.broadcasted_iota(jnp.int32, (2 * tq, tq), 1)
            return (col <= tok) & (tok - col < WINDOW)

        o_win = _flash(qg, lambda j: kv_tile(j, tq, 256), lambda j: kv_tile(j, tq, 384), win_valid,
                       jnp.maximum(i - WINDOW // tq, 0), i + 1)

        for r in range(2):
            rows = slice(r * tq, (r + 1) * tq)
            c = g * 6 + r * 3
            o = (gates[:, c:c + 1] * o_cmp[rows] + gates[:, c + 1:c + 2] * o_sel[rows]
                 + gates[:, c + 2:c + 3] * o_win[rows])
            out[r] = jnp.where(in_g, o, out[r])
    o_ref[0, :, 0:128] = out[0].astype(BF16)
    o_ref[0, :, 128:256] = out[1].astype(BF16)


def _nsa_attend(q, kv, kcmp, vcmp, gates, tq=128, tk=256):
    B, S, _ = q.shape
    covt = _cover_t(S)
    tile = lambda w: pl.BlockSpec((1, tq, w), lambda b, i: (b, i, 0))
    per_b = lambda r, w: pl.BlockSpec((1, r, w), lambda b, i: (b, 0, 0))
    return pl.pallas_call(
        functools.partial(_nsa_kernel, tq=tq, tk=tk),
        grid=(B, S // tq),
        in_specs=[tile(256), per_b(S, 512), per_b(S // CMP_STRIDE, 128), per_b(S // CMP_STRIDE, 128), tile(128),
                  pl.BlockSpec(covt.shape, lambda b, i: (0, 0))],
        out_specs=tile(256),
        out_shape=jax.ShapeDtypeStruct((B, S, 256), BF16),
        compiler_params=_cparams(("parallel", "arbitrary")),
        name="nsa_attend",
    )(q, kv, kcmp, vcmp, gates, covt)


def _pool_kernel(u_ref, w_ref, scale_ref, o_ref):
    u = u_ref[0]
    row = lax.broadcasted_iota(jnp.int32, u.shape, 0)
    grp = lax.broadcasted_iota(jnp.int32, u.shape, 1) // HEAD_DIM

    def back(x, k):
        return jnp.where(row >= k, pltpu.roll(x, k, 0), 0.0)

    s2 = u + back(u, 1)
    s4 = s2 + back(s2, 2)
    s8 = s4 + back(s4, 4)
    s16 = s8 + back(s8, 8)
    total = jnp.where(grp == 0, s2, jnp.where(grp == 1, s4, jnp.where(grp == 2, s8, s16)))
    win = jnp.where(grp == 0, 2, jnp.where(grp == 1, 4, jnp.where(grp == 2, 8, 16)))
    count = jnp.minimum(row + 1, win).astype(F32)
    d = total / count - u
    y = jnp.dot(d.astype(BF16), w_ref[...], preferred_element_type=F32)
    o_ref[0] = (y * scale_ref[...]).astype(BF16)


def _pool(u, w_bd, scale):
    B, S, W = u.shape
    per_b = pl.BlockSpec((1, S, W), lambda b: (b, 0, 0))
    full = lambda a: pl.BlockSpec(a.shape, lambda b: (0,) * a.ndim)
    return pl.pallas_call(
        _pool_kernel,
        grid=(B,),
        in_specs=[per_b, full(w_bd), full(scale)],
        out_specs=per_b,
        out_shape=jax.ShapeDtypeStruct((B, S, W), BF16),
        compiler_params=_cparams(("parallel",)),
        name="pool",
    )(u, w_bd, scale)


def _block_diag(w):
    G, a, b = w.shape
    return jnp.einsum('gab,gh->gahb', w, jnp.eye(G, dtype=w.dtype)).reshape(G * a, G * b)


def _dot_split_lhs(a, b_exact, terms):
    out, rem = None, a
    for i in range(terms):
        hi = rem.astype(BF16)
        d = jnp.dot(hi, b_exact, preferred_element_type=F32)
        out = d if out is None else out + d
        if i + 1 < terms:
            rem = rem - hi.astype(F32)
    return out


def _dot_split_rhs(a_exact, b, terms):
    out, rem = None, b
    for i in range(terms):
        hi = rem.astype(BF16)
        d = jnp.dot(a_exact, hi, preferred_element_type=F32)
        out = d if out is None else out + d
        if i + 1 < terms:
            rem = rem - hi.astype(F32)
    return out


def _dot_tn(a, b):
    return lax.dot_general(a, b, (((0,), (0,)), ((), ())), preferred_element_type=F32)


def _stack_heads(x, width, n_heads=4):
    head = lax.broadcasted_iota(jnp.int32, x.shape, 1) // width
    return jnp.concatenate([jnp.where(head == h, x, 0.0) for h in range(n_heads)], axis=0)


def _unstack_heads(x, C, n_heads=4):
    out = x[0:C]
    for h in range(1, n_heads):
        out = out + x[h * C:(h + 1) * C]
    return out


def _block_masks(C, n_heads=4):
    n = C * n_heads
    row = lax.broadcasted_iota(jnp.int32, (n, n), 0)
    col = lax.broadcasted_iota(jnp.int32, (n, n), 1)
    same = (row // C) == (col // C)
    return same & (row > col), same & (row >= col)


def _tri_ones(C):
    return jnp.asarray(np.tril(np.ones((C, C), np.float32)), BF16)


def _head_ones(width, n_heads=4):
    return jnp.asarray(np.kron(np.eye(n_heads, dtype=np.float32), np.ones((width, width), np.float32)), BF16)


def _gla_kernel(gl_ref, a2_ref, ab_ref, norm_ref, tri_ref, ones_ref, o_ref, state_ref, *, C):
    s = pl.program_id(1)

    @pl.when(s == 0)
    def _():
        state_ref[...] = jnp.zeros_like(state_ref)

    gl = gl_ref[0]
    tm = gl.shape[0]
    q = gl[:, 0:128] * (GLA_DK ** -0.5)
    k = gl[:, 128:256]
    v = gl[:, 256:512]
    og = gl[:, 512:768]
    ad = gl[:, 768:896]
    x = jnp.dot(ad.astype(BF16), a2_ref[...], preferred_element_type=F32) + ab_ref[...]
    log_a = jax.nn.log_sigmoid(x) * (1.0 / GLA_TAU)
    strict, incl = _block_masks(C)
    kv_mask = (lax.broadcasted_iota(jnp.int32, (256, 128), 0) // HEAD_DIM
               == lax.broadcasted_iota(jnp.int32, (256, 128), 1) // GLA_DK)
    outs = []
    for c in range(tm // C):
        rc = slice(c * C, (c + 1) * C)
        bcum = _dot_split_rhs(tri_ref[...], log_a[rc], 3)
        mid = bcum[C // 2 - 1:C // 2, :]
        last = bcum[C - 1:C, :]
        q_in = q[rc] * jnp.exp(bcum)
        q_m = _stack_heads(q[rc] * jnp.exp(bcum - mid), GLA_DK).astype(BF16)
        k_m = _stack_heads(k[rc] * jnp.exp(mid - bcum), GLA_DK).astype(BF16)
        k_end = k[rc] * jnp.exp(last - bcum)
        v_c = v[rc]
        att = jnp.where(incl, _dot_nt(q_m, k_m), 0.0)
        v_st = _stack_heads(v_c, HEAD_DIM).astype(BF16)
        o_intra = _unstack_heads(jnp.dot(att.astype(BF16), v_st, preferred_element_type=F32), C)
        state = state_ref[...]
        o_inter = _dot_nt(q_in.astype(BF16), state.astype(BF16))
        upd = _dot_tn(v_c.astype(BF16), k_end.astype(BF16))
        state_ref[...] = state * jnp.exp(last) + jnp.where(kv_mask, upd, 0.0)
        outs.append(o_intra + o_inter)
    o = jnp.concatenate(outs, axis=0)
    ms = _dot_split_lhs(o * o, ones_ref[...], 2) * (1.0 / HEAD_DIM)
    o = o * lax.rsqrt(ms + NORM_EPS) * norm_ref[...]
    o_ref[0] = (o * (og * jax.nn.sigmoid(og))).astype(BF16)


def _gla(gl, a2p, ab, norm4, tm=256):
    B, S, W = gl.shape
    C = GLA_CHUNK
    tri, ones = _tri_ones(C), _head_ones(HEAD_DIM)
    tile = lambda w: pl.BlockSpec((1, tm, w), lambda b, s: (b, s, 0))
    full = lambda a: pl.BlockSpec(a.shape, lambda b, s: (0,) * a.ndim)
    return pl.pallas_call(
        functools.partial(_gla_kernel, C=C),
        grid=(B, S // tm),
        in_specs=[tile(W), full(a2p), full(ab), full(norm4), full(tri), full(ones)],
        out_specs=tile(256),
        out_shape=jax.ShapeDtypeStruct((B, S, 256), BF16),
        scratch_shapes=[pltpu.VMEM((256, 128), F32)],
        compiler_params=_cparams(("parallel", "arbitrary")),
        name="gla",
    )(gl, a2p, ab, norm4, tri, ones)


_RV_W0, _RV_A0, _RV_KK, _RV_KA, _RV_RK, _RV_LN, _RV_V0 = range(7)


def _rwkv_kernel(*refs, has_vres, C):
    if has_vres:
        (rw_ref, h_ref, vf_ref, wz_ref, vec_ref, v1_ref, v2_ref, tri_ref, ones_ref, o_ref, state_ref) = refs
    else:
        (rw_ref, wz_ref, vec_ref, tri_ref, ones_ref, o_ref, vout_ref, state_ref) = refs
    s = pl.program_id(1)

    @pl.when(s == 0)
    def _():
        state_ref[...] = jnp.zeros_like(state_ref)

    rw = rw_ref[0]
    tm = rw.shape[0]
    r, k, v, z = rw[:, 0:256], rw[:, 256:512], rw[:, 512:768], rw[:, 768:896]
    vec = lambda i: vec_ref[i:i + 1, :]
    zl = lax.broadcasted_iota(jnp.int32, z.shape, 1)
    zf = jnp.where(zl < 32, jnp.tanh(z), jnp.where(zl < 64, z, jax.nn.sigmoid(z)))
    zz = jnp.dot(zf.astype(BF16), wz_ref[...], preferred_element_type=F32)
    w_log = -jax.nn.softplus(-(vec(_RV_W0) + zz[:, 0:256])) - 0.5
    lw = -jnp.exp(w_log)
    a_sig = jax.nn.sigmoid(vec(_RV_A0) + zz[:, 256:512])
    gate = zz[:, 512:768]
    if has_vres:
        low = jnp.dot(h_ref[0], v1_ref[...], preferred_element_type=F32)
        logit = vec(_RV_V0) + jnp.dot(low.astype(BF16), v2_ref[...], preferred_element_type=F32)
        v = v + (vf_ref[0] - v) * jax.nn.sigmoid(logit)
    else:
        vout_ref[0] = v
    kk = k * vec(_RV_KK)
    norm = jnp.sqrt(_dot_split_lhs(kk * kk, ones_ref[...], 2))
    kk = kk / jnp.maximum(norm, 1e-12)
    k = k * (1.0 + (a_sig - 1.0) * vec(_RV_KA))
    a_vec = -kk
    b_vec = kk * a_sig

    strict, incl = _block_masks(C)
    st = lambda t: _stack_heads(t, HEAD_DIM)
    bdot = lambda p, q: jnp.dot(p.astype(BF16), q.astype(BF16), preferred_element_type=F32)
    state_mask = (lax.broadcasted_iota(jnp.int32, (256, 256), 0) // HEAD_DIM
                  == lax.broadcasted_iota(jnp.int32, (256, 256), 1) // HEAD_DIM)
    ys = []
    for c in range(tm // C):
        rc = slice(c * C, (c + 1) * C)
        lcum = _dot_split_rhs(tri_ref[...], lw[rc], 3)
        lex = lcum - lw[rc]
        mid = lcum[C // 2 - 1:C // 2, :]
        last = lcum[C - 1:C, :]
        e_mid = jnp.exp(mid - lcum)
        a_abs = a_vec[rc] * jnp.exp(lex)
        r_abs = r[rc] * jnp.exp(lcum)
        left = jnp.concatenate([st(a_vec[rc] * jnp.exp(lex - mid)), st(r[rc] * jnp.exp(lcum - mid))], axis=0)
        right = jnp.concatenate([st(b_vec[rc] * e_mid), st(k[rc] * e_mid)], axis=0)
        g = _dot_nt(left.astype(BF16), right.astype(BF16))
        n4 = 4 * C
        n_ab = jnp.where(strict, g[0:n4, 0:n4], 0.0)
        n_ak = jnp.where(strict, g[0:n4, n4:2 * n4], 0.0)
        n_rb = jnp.where(incl, g[n4:2 * n4, 0:n4], 0.0)
        n_rk = jnp.where(incl, g[n4:2 * n4, n4:2 * n4], 0.0)
        v_st = st(v[rc])
        x = jnp.concatenate([st(a_abs), bdot(n_ak, v_st)], axis=1)
        p = n_ab
        n_fac = int(np.log2(C))
        for f in range(n_fac):
            x = x + bdot(p, x)
            if f + 1 < n_fac:
                p = bdot(p, p)
        y_part = bdot(n_rb, x)
        a_eff = _unstack_heads(x[:, 0:256], C)
        u0 = _unstack_heads(x[:, 256:512], C)
        r_eff = r_abs + _unstack_heads(y_part[:, 0:256], C)
        y0 = _unstack_heads(y_part[:, 256:512] + bdot(n_rk, v_st), C)
        state = state_ref[...]
        uy = _dot_nt(jnp.concatenate([a_eff, r_eff], axis=0).astype(BF16), state.astype(BF16))
        u = uy[0:C] + u0
        ys.append(uy[C:2 * C] + y0)
        e_end = jnp.exp(last - lcum)
        upd = _dot_tn(jnp.concatenate([u, v[rc]], axis=0).astype(BF16),
                      jnp.concatenate([b_vec[rc] * e_end, k[rc] * e_end], axis=0).astype(BF16))
        state_ref[...] = state * jnp.exp(last) + jnp.where(state_mask, upd, 0.0)
    y = jnp.concatenate(ys, axis=0)
    inv = 1.0 / HEAD_DIM
    mean = _dot_split_lhs(y, ones_ref[...], 2) * inv
    yc = y - mean
    var = _dot_split_lhs(yc * yc, ones_ref[...], 2) * inv
    y = yc * lax.rsqrt(var + RWKV_GN_EPS) * vec(_RV_LN)
    y = y + _dot_split_lhs(r * k * vec(_RV_RK), ones_ref[...], 2) * v
    o_ref[0] = (y * gate).astype(BF16)


def _rwkv(rw, wz, vecs, h=None, v_first=None, v1p=None, v2p=None, tm=256):
    B, S, W = rw.shape
    C = RWKV_CHUNK
    has_vres = h is not None
    tri, ones = _tri_ones(C), _head_ones(HEAD_DIM)
    tile = lambda w: pl.BlockSpec((1, tm, w), lambda b, s: (b, s, 0))
    full = lambda a: pl.BlockSpec(a.shape, lambda b, s: (0,) * a.ndim)
    if has_vres:
        args = (rw, h, v_first, wz, vecs, v1p, v2p, tri, ones)
        in_specs = [tile(W), tile(D_MODEL), tile(256), full(wz), full(vecs), full(v1p), full(v2p), full(tri), full(ones)]
        out_specs = tile(256)
        out_shape = jax.ShapeDtypeStruct((B, S, 256), BF16)
    else:
        args = (rw, wz, vecs, tri, ones)
        in_specs = [tile(W), full(wz), full(vecs), full(tri), full(ones)]
        out_specs = [tile(256), tile(256)]
        out_shape = [jax.ShapeDtypeStruct((B, S, 256), BF16), jax.ShapeDtypeStruct((B, S, 256), F32)]
    return pl.pallas_call(
        functools.partial(_rwkv_kernel, has_vres=has_vres, C=C),
        grid=(B, S // tm),
        in_specs=in_specs,
        out_specs=out_specs,
        out_shape=out_shape,
        scratch_shapes=[pltpu.VMEM((256, 256), F32)],
        compiler_params=_cparams(("parallel", "arbitrary")),
        name="rwkv_vres" if has_vres else "rwkv",
    )(*args)


def _rwkv_weights(l, rwkv_w0, rwkv_w2, rwkv_a0, rwkv_a2, rwkv_g2, rwkv_v0, rwkv_k_k, rwkv_k_a, rwkv_r_k, rwkv_ln):
    wz = jnp.zeros((128, 768), F32)
    wz = wz.at[0:32, 0:256].set(rwkv_w2[l]).at[32:64, 256:512].set(rwkv_a2[l]).at[64:128, 512:768].set(rwkv_g2[l])
    v0 = rwkv_v0[l - 1] if l > 0 else jnp.zeros((256,), F32)
    vecs = jnp.stack([rwkv_w0[l], rwkv_a0[l], rwkv_k_k[l], rwkv_k_a[l], rwkv_r_k[l].reshape(-1), rwkv_ln[l], v0,
                      jnp.zeros((256,), F32)])
    return wz.astype(BF16), vecs


def _merge_kernel(x_ref, h_ref, o0_ref, o1_ref, o2_ref, o3_ref, wg_ref, bg_ref, wb_ref, wo_ref, out_ref):
    h = h_ref[0]
    merged = None
    for i, o_ref in enumerate((o0_ref, o1_ref, o2_ref, o3_ref)):
        gate = jax.nn.sigmoid(jnp.dot(h, wg_ref[i], preferred_element_type=F32) + bg_ref[i:i + 1, :])
        term = gate * jnp.dot(o_ref[0], wb_ref[i], preferred_element_type=F32)
        merged = term if merged is None else merged + term
    out_ref[0] = x_ref[0] + jnp.dot(merged.astype(BF16), wo_ref[...], preferred_element_type=F32)


def _merge(x, h, o_nsa, o_pool, o_rwkv, o_gla, wg, bg, wb, wo, tm=512):
    B, S, D = x.shape
    tile = lambda w: pl.BlockSpec((1, tm, w), lambda b, s: (b, s, 0))
    full = lambda a: pl.BlockSpec(a.shape, lambda b, s: (0,) * a.ndim)
    return pl.pallas_call(
        _merge_kernel,
        grid=(B, S // tm),
        in_specs=[tile(D), tile(D), tile(256), tile(256), tile(256), tile(256), full(wg), full(bg), full(wb), full(wo)],
        out_specs=tile(D),
        out_shape=jax.ShapeDtypeStruct((B, S, D), F32),
        compiler_params=_cparams(("parallel", "parallel")),
        name="merge",
    )(x, h, o_nsa, o_pool, o_rwkv, o_gla, wg, bg, wb, wo)


def _memkv_kernel(mem_ref, g_ref, w_ref, o_ref):
    mn = _rms(mem_ref[0], g_ref[...]).astype(BF16)
    o_ref[0] = jnp.dot(mn, w_ref[...], preferred_element_type=F32).astype(BF16)


def _memkv(mem, g, wkv_all):
    B, M, D = mem.shape
    N = wkv_all.shape[1]
    return pl.pallas_call(
        _memkv_kernel,
        grid=(B,),
        in_specs=[pl.BlockSpec((1, M, D), lambda b: (b, 0, 0)), pl.BlockSpec(g.shape, lambda b: (0, 0)),
                  pl.BlockSpec(wkv_all.shape, lambda b: (0, 0))],
        out_specs=pl.BlockSpec((1, M, N), lambda b: (b, 0, 0)),
        out_shape=jax.ShapeDtypeStruct((B, M, N), BF16),
        compiler_params=_cparams(("parallel",)),
        name="memkv",
    )(mem, g, wkv_all)


def _cross_kernel(x_ref, g_ref, wq_ref, k_ref, v_ref, wo_ref, out_ref):
    x = x_ref[0]
    hc = _rms(x, g_ref[...]).astype(BF16)
    q = jnp.dot(hc, wq_ref[...], preferred_element_type=F32) * (HEAD_DIM ** -0.5)
    head = lax.broadcasted_iota(jnp.int32, q.shape, 1) // HEAD_DIM
    k, v = k_ref[0], v_ref[0]
    o = jnp.zeros(q.shape, F32)
    for hh in range(CA_WIDTH // HEAD_DIM):
        qh = jnp.where(head == hh, q, 0.0).astype(BF16)
        s = _dot_nt(qh, k)
        e = jnp.exp(s - jnp.max(s, axis=1, keepdims=True))
        p = e / jnp.sum(e, axis=1, keepdims=True)
        o = jnp.where(head == hh, jnp.dot(p.astype(BF16), v, preferred_element_type=F32), o)
    out_ref[0] = x + jnp.dot(o.astype(BF16), wo_ref[...], preferred_element_type=F32)


def _cross(x, g, wq, memkv, l, wo, tm=512):
    B, S, D = x.shape
    M = memkv.shape[1]
    tile = pl.BlockSpec((1, tm, D), lambda b, s: (b, s, 0))
    full = lambda a: pl.BlockSpec(a.shape, lambda b, s: (0,) * a.ndim)
    kspec = pl.BlockSpec((1, M, CA_WIDTH), lambda b, s: (b, 0, 2 * l))
    vspec = pl.BlockSpec((1, M, CA_WIDTH), lambda b, s: (b, 0, 2 * l + 1))
    return pl.pallas_call(
        _cross_kernel,
        grid=(B, S // tm),
        in_specs=[tile, full(g), full(wq), kspec, vspec, full(wo)],
        out_specs=tile,
        out_shape=jax.ShapeDtypeStruct((B, S, D), F32),
        compiler_params=_cparams(("parallel", "parallel")),
        name="cross_attn",
    )(x, g, wq, memkv, memkv, wo)


def _ffn_kernel(x_ref, g_ref, wg_ref, wu_ref, wd_ref, out_ref, hb_ref, acc_ref):
    f = pl.program_id(1)

    @pl.when(f == 0)
    def _():
        hb_ref[...] = _rms(x_ref[...], g_ref[...]).astype(BF16)
        acc_ref[...] = jnp.zeros_like(acc_ref)

    hb = hb_ref[...]
    gate = jnp.dot(hb, wg_ref[...], preferred_element_type=F32)
    up = jnp.dot(hb, wu_ref[...], preferred_element_type=F32)
    act = gate * jax.nn.sigmoid(gate) * up
    acc_ref[...] += jnp.dot(act.astype(BF16), wd_ref[...], preferred_element_type=F32)

    @pl.when(f == pl.num_programs(1) - 1)
    def _():
        out_ref[...] = x_ref[...] + acc_ref[...]


def _ffn(x2, g, wg, wu, wd, tm=1024, tf=256):
    T, D = x2.shape
    F = wg.shape[1]
    return pl.pallas_call(
        _ffn_kernel,
        grid=(T // tm, F // tf),
        in_specs=[pl.BlockSpec((tm, D), lambda i, f: (i, 0)), pl.BlockSpec(g.shape, lambda i, f: (0, 0)),
                  pl.BlockSpec((D, tf), lambda i, f: (0, f)), pl.BlockSpec((D, tf), lambda i, f: (0, f)),
                  pl.BlockSpec((tf, D), lambda i, f: (f, 0))],
        out_specs=pl.BlockSpec((tm, D), lambda i, f: (i, 0)),
        out_shape=jax.ShapeDtypeStruct((T, D), F32),
        scratch_shapes=[pltpu.VMEM((tm, D), BF16), pltpu.VMEM((tm, D), F32)],
        compiler_params=_cparams(("parallel", "arbitrary")),
        name="ffn",
    )(x2, g, wg, wu, wd)


def _top2_combine(logits):
    lane = lax.broadcasted_iota(jnp.int32, logits.shape, 1)
    big = logits.shape[1]
    logits = jnp.where(lane < N_EXPERTS, logits, -jnp.inf)
    m1 = jnp.max(logits, axis=1, keepdims=True)
    i1 = jnp.min(jnp.where(logits == m1, lane, big), axis=1, keepdims=True)
    rest = jnp.where(lane == i1, -jnp.inf, logits)
    m2 = jnp.max(rest, axis=1, keepdims=True)
    i2 = jnp.min(jnp.where(rest == m2, lane, big), axis=1, keepdims=True)
    e2 = jnp.exp(m2 - m1)
    w1 = 1.0 / (1.0 + e2)
    return jnp.where(lane == i1, w1, jnp.where(lane == i2, e2 * w1, 0.0))


def _moe_dense_kernel(x_ref, g_ref, r_ref, wg_ref, wu_ref, wd_ref, gf_ref, out_ref, hb_ref, acc_ref, comb_ref):
    e, f = pl.program_id(1), pl.program_id(2)

    @pl.when((e == 0) & (f == 0))
    def _():
        hf = _rms(x_ref[...], g_ref[...])
        hb_ref[...] = hf.astype(BF16)
        acc_ref[...] = jnp.zeros_like(acc_ref)
        logits = jnp.dot(hf, r_ref[...], preferred_element_type=F32, precision=lax.Precision.HIGHEST)
        comb_ref[...] = _top2_combine(logits)

    hb = hb_ref[...]
    gate = jnp.dot(hb, wg_ref[0], preferred_element_type=F32)
    up = jnp.dot(hb, wu_ref[0], preferred_element_type=F32)
    lane = lax.broadcasted_iota(jnp.int32, comb_ref.shape, 1)
    ce = jnp.sum(jnp.where(lane == e, comb_ref[...], 0.0), axis=1, keepdims=True)
    act = gate * jax.nn.sigmoid(gate) * up * ce
    acc_ref[...] += jnp.dot(act.astype(BF16), wd_ref[0], preferred_element_type=F32)

    @pl.when((e == pl.num_programs(1) - 1) & (f == pl.num_programs(2) - 1))
    def _():
        out_ref[...] = _rms(x_ref[...] + acc_ref[...], gf_ref[...])


def _moe_dense(x2, g, router_p, wg, wu, wd, g_final, tm=1024, tf=256):
    T, D = x2.shape
    E, _, F = wg.shape
    return pl.pallas_call(
        _moe_dense_kernel,
        grid=(T // tm, E, F // tf),
        in_specs=[pl.BlockSpec((tm, D), lambda i, e, f: (i, 0)), pl.BlockSpec(g.shape, lambda i, e, f: (0, 0)),
                  pl.BlockSpec(router_p.shape, lambda i, e, f: (0, 0)),
                  pl.BlockSpec((1, D, tf), lambda i, e, f: (e, 0, f)), pl.BlockSpec((1, D, tf), lambda i, e, f: (e, 0, f)),
                  pl.BlockSpec((1, tf, D), lambda i, e, f: (e, f, 0)), pl.BlockSpec(g_final.shape, lambda i, e, f: (0, 0))],
        out_specs=pl.BlockSpec((tm, D), lambda i, e, f: (i, 0)),
        out_shape=jax.ShapeDtypeStruct((T, D), F32),
        scratch_shapes=[pltpu.VMEM((tm, D), BF16), pltpu.VMEM((tm, D), F32), pltpu.VMEM((tm, LANES), F32)],
        compiler_params=_cparams(("parallel", "arbitrary", "arbitrary")),
        name="moe_dense",
    )(x2, g, router_p, wg, wu, wd, g_final)


def kernel(x, mem, positions, norm_mix, w_in, nsa_cmp_pe, nsa_cmp_w1, nsa_cmp_w2, pool_w, pool_scale, rwkv_mu, rwkv_w0,
           rwkv_w2, rwkv_a0, rwkv_a2, rwkv_g2, rwkv_v0, rwkv_v1, rwkv_v2, rwkv_k_k, rwkv_k_a, rwkv_r_k, rwkv_ln, gla_a2,
           gla_ab, gla_norm, w_gate, b_gate, w_branch, w_out, norm_ca, mem_norm, w_ca_q, w_ca_kv, w_ca_o, norm_ffn,
           ffn_w_gate, ffn_w_up, ffn_w_down, moe_router, moe_w_gate, moe_w_up, moe_w_down, norm_final):
    B, S, D = x.shape
    depth = norm_mix.shape[0]
    assert depth == 2 and S % 256 == 0, "the final norm is fused into the layer-1 expert mixer"
    cosf, sinf = _rope_tables(positions)
    at_cmp_end = lambda t: jnp.pad(t[:, CMP_LEN - 1::CMP_STRIDE], ((0, 0), (0, 1), (0, 0)))
    cosc, sinc = at_cmp_end(cosf), at_cmp_end(sinf)
    memkv = _memkv(mem, mem_norm[None], jnp.concatenate([w_ca_kv[l] for l in range(depth)], axis=1).astype(BF16))
    v_first = None
    for l in range(depth):
        h, q, kv, cv, gates, u, rw, gl = _inproj(x, norm_mix[l][None], _inproj_weights(w_in, l), cosf, sinf,
                                                 rwkv_mu[l][None])
        kcmp, vcmp = _nsa_compress(cv, *_compress_weights(nsa_cmp_pe[l], nsa_cmp_w1[l], nsa_cmp_w2[l]), cosc, sinc)
        o_nsa = _nsa_attend(q, kv, kcmp, vcmp, gates)
        o_pool = _pool(u, _block_diag(pool_w[l]).astype(BF16), pool_scale[l][None])
        wz, vecs = _rwkv_weights(l, rwkv_w0, rwkv_w2, rwkv_a0, rwkv_a2, rwkv_g2, rwkv_v0, rwkv_k_k, rwkv_k_a,
                                 rwkv_r_k, rwkv_ln)
        if l == 0:
            o_rwkv, v_first = _rwkv(rw, wz, vecs)
        else:
            v1p = jnp.pad(rwkv_v1[l - 1], ((0, 0), (0, LANES - rwkv_v1.shape[2]))).astype(BF16)
            v2p = jnp.pad(rwkv_v2[l - 1], ((0, LANES - rwkv_v2.shape[1]), (0, 0))).astype(BF16)
            o_rwkv = _rwkv(rw, wz, vecs, h, v_first, v1p, v2p)
        a2p = jnp.pad(gla_a2[l], ((0, LANES - gla_a2.shape[1]), (0, 0))).astype(BF16)
        o_gla = _gla(gl, a2p, gla_ab[l][None], jnp.tile(gla_norm[l], 4)[None])
        wb_nsa = w_branch[l, 0].reshape(4, HEAD_DIM, D)[jnp.array([0, 2, 1, 3])].reshape(BRANCH_WIDTH, D)
        wb = jnp.concatenate([wb_nsa[None], w_branch[l, 1:]], axis=0).astype(BF16)
        x = _merge(x, h, o_nsa, o_pool, o_rwkv, o_gla, w_gate[l].astype(BF16), b_gate[l], wb, w_out[l].astype(BF16))
        x = _cross(x, norm_ca[l][None], w_ca_q[l].astype(BF16), memkv, l, w_ca_o[l].astype(BF16))
        j = l // 2
        x2 = x.reshape(B * S, D)
        if l % 2 == 0:
            x2 = _ffn(x2, norm_ffn[l][None], ffn_w_gate[j].astype(BF16), ffn_w_up[j].astype(BF16),
                      ffn_w_down[j].astype(BF16))
        else:
            router_p = jnp.pad(moe_router[j], ((0, 0), (0, LANES - N_EXPERTS)))
            x2 = _moe_dense(x2, norm_ffn[l][None], router_p, moe_w_gate[j].astype(BF16), moe_w_up[j].astype(BF16),
                            moe_w_down[j].astype(BF16), norm_final[None])
        x = x2.reshape(B, S, D)
    return x
```

```python
import functools

import jax
import jax.numpy as jnp
import numpy as np
from jax import lax
from jax.experimental import pallas as pl
from jax.experimental.pallas import tpu as pltpu

F32 = jnp.float32
BF16 = jnp.bfloat16

D_MODEL = 1024
HEAD_DIM = 64
BRANCH_WIDTH = 256
ROPE_THETA = 500000.0
ROPE_DIM = 16
NORM_EPS = 1e-6
NEG_INF = -1e30

NSA_KV = 128
CMP_LEN = 32
CMP_STRIDE = 16
CMP_HIDDEN = 128
SEL_BLOCK = 64
SEL_TOPN = 8
WINDOW = 512

RWKV_IN = 896
RWKV_GN_EPS = 64e-5
RWKV_CHUNK = 64

GLA_DK = 32
GLA_TAU = 16.0
GLA_CHUNK = 64

CA_WIDTH = 256
D_FF = 2816
N_EXPERTS = 8

LANES = 128
VMEM_LIMIT = 56 * 1024 * 1024

_C_Q = 0
_C_KV = 256
_C_CV = 768
_C_GATE = 1024
_C_POOL = 1152
_C_RWKV = 1408
_C_GLA = 2304
_C_END = 3200


def _cparams(sem):
    return pltpu.CompilerParams(dimension_semantics=sem, vmem_limit_bytes=VMEM_LIMIT)


def _rms(x, g):
    ms = jnp.mean(x * x, axis=-1, keepdims=True)
    return x * lax.rsqrt(ms + NORM_EPS) * g


def _rope128(t, cos, sin):
    lane = lax.broadcasted_iota(jnp.int32, t.shape, 1) % HEAD_DIM
    partner = jnp.where(lane < ROPE_DIM // 2, pltpu.roll(t, LANES - ROPE_DIM // 2, 1),
                        pltpu.roll(t, ROPE_DIM // 2, 1))
    return t * cos + partner * sin


def _inproj_kernel(x_ref, g_ref, w_ref, cos_ref, sin_ref, mu_ref,
                   h_ref, qt_ref, k_ref, vt_ref, cv_ref, gt_ref, u_ref, rw_ref, gl_ref, prev_ref):
    s = pl.program_id(1)
    hb = _rms(x_ref[0], g_ref[...]).astype(BF16)
    h_ref[0] = hb

    def mm(a, b):
        return jnp.dot(hb, w_ref[:, a:b], preferred_element_type=F32)

    cos = cos_ref[0]
    sin = sin_ref[0]
    scale = HEAD_DIM ** -0.5
    qt_ref[0, 0:128, :] = jnp.transpose(_rope128(mm(_C_Q, _C_Q + 128), cos, sin) * scale).astype(BF16)
    qt_ref[0, 128:256, :] = jnp.transpose(_rope128(mm(_C_Q + 128, _C_Q + 256), cos, sin) * scale).astype(BF16)
    k_ref[0, :, 0:128] = _rope128(mm(_C_KV, _C_KV + 128), cos, sin).astype(BF16)
    vt_ref[0, 0:128, :] = jnp.transpose(mm(_C_KV + 128, _C_KV + 256)).astype(BF16)
    k_ref[0, :, 128:256] = _rope128(mm(_C_KV + 256, _C_KV + 384), cos, sin).astype(BF16)
    vt_ref[0, 128:256, :] = jnp.transpose(mm(_C_KV + 384, _C_KV + 512)).astype(BF16)
    cv_ref[0] = mm(_C_CV, _C_CV + 256).astype(BF16)
    gt_ref[0] = jnp.transpose(jax.nn.sigmoid(mm(_C_GATE, _C_GATE + 128)))
    u_ref[0] = mm(_C_POOL, _C_POOL + 256)
    gl_ref[0] = mm(_C_GLA, _C_END)

    p = mm(_C_RWKV, _C_RWKV + RWKV_IN)
    tm = p.shape[0]

    @pl.when(s == 0)
    def _():
        prev_ref[...] = jnp.zeros_like(prev_ref)

    row = lax.broadcasted_iota(jnp.int32, p.shape, 0)
    shifted = jnp.where(row == 0, prev_ref[0:1, :], pltpu.roll(p, 1, 0))
    prev_ref[0:1, :] = p[tm - 1:tm, :]
    rw_ref[0] = p + (shifted - p) * mu_ref[...]


def _inproj(x, g, w_all, cosf, sinf, mu, tm=256):
    B, S, D = x.shape
    tok = lambda w: pl.BlockSpec((1, tm, w), lambda b, s: (b, s, 0))
    tok_t = lambda w: pl.BlockSpec((1, w, tm), lambda b, s: (b, 0, s))
    full = lambda a: pl.BlockSpec(a.shape, lambda b, s: (0,) * a.ndim)
    outs = [(D, BF16, True), (256, BF16, False), (256, BF16, True), (256, BF16, False), (256, BF16, True),
            (128, F32, False), (256, F32, True), (RWKV_IN, F32, True), (896, F32, True)]
    return pl.pallas_call(
        _inproj_kernel,
        grid=(B, S // tm),
        in_specs=[tok(D), full(g), full(w_all), tok(128), tok(128), full(mu)],
        out_specs=[tok(w) if major else tok_t(w) for w, _, major in outs],
        out_shape=[jax.ShapeDtypeStruct((B, S, w) if major else (B, w, S), dt) for w, dt, major in outs],
        scratch_shapes=[pltpu.VMEM((8, RWKV_IN), F32)],
        compiler_params=_cparams(("parallel", "arbitrary")),
        name="inproj",
    )(x, g, w_all, cosf, sinf, mu)


def _inproj_weights(w_in, l):
    w = w_in[l]
    nsa, pool, rwkv, gla = jnp.split(w, [1036, 1036 + 256, 1036 + 256 + 896], axis=1)
    q = nsa[:, 0:256].reshape(D_MODEL, 4, HEAD_DIM)
    qa = q[:, (0, 2), :].reshape(D_MODEL, 128)
    qb = q[:, (1, 3), :].reshape(D_MODEL, 128)
    kc, vc, ks, vs, kw, vw = (nsa[:, 256 + 128 * i:384 + 128 * i] for i in range(6))
    gates = jnp.pad(nsa[:, 1024:1036], ((0, 0), (0, 116)))
    gq, gk, gv, gad, gog = jnp.split(gla, [128, 256, 512, 528], axis=1)
    gla_r = jnp.concatenate([gq, gk, gv, gog, jnp.pad(gad, ((0, 0), (0, 112)))], axis=1)
    w_all = jnp.concatenate([qa, qb, ks, vs, kw, vw, kc, vc, gates, pool, rwkv, gla_r], axis=1)
    return w_all.astype(BF16)


def _rope_tables(positions):
    half = ROPE_DIM // 2
    inv_freq = ROPE_THETA ** (-jnp.arange(0, ROPE_DIM, 2, dtype=F32) / ROPE_DIM)
    ang = positions.astype(F32)[..., None] * inv_freq
    cos, sin = jnp.cos(ang), jnp.sin(ang)
    ones = jnp.ones(cos.shape[:-1] + (HEAD_DIM - ROPE_DIM,), F32)
    cos64 = jnp.concatenate([cos, cos, ones], axis=-1)
    sin64 = jnp.concatenate([-sin, sin, 0.0 * ones], axis=-1)
    return jnp.concatenate([cos64, cos64], axis=-1), jnp.concatenate([sin64, sin64], axis=-1)


def _gelu_tanh(x):
    return x * (0.5 * (1.0 + jnp.tanh(np.sqrt(2.0 / np.pi) * (x + 0.044715 * (x * x * x)))))


def _dot_nt(a, b):
    return lax.dot_general(a, b, (((1,), (1,)), ((), ())), preferred_element_type=F32)


def _compress_kernel(x_ref, wa_ref, wb_ref, bias_ref, w2k_ref, w2v_ref, cos_ref, sin_ref, kc_ref, vc_ref):
    x = x_ref[0]
    a = jnp.dot(x, wa_ref[...], preferred_element_type=F32)
    b = jnp.dot(x, wb_ref[...], preferred_element_type=F32)
    n = a.shape[0]
    hid = _gelu_tanh(a + pltpu.roll(b, n - 1, 0) + bias_ref[...])
    kc = jnp.dot(hid[:, 0:256].astype(BF16), w2k_ref[...], preferred_element_type=F32)
    vc = jnp.dot(hid[:, 256:512].astype(BF16), w2v_ref[...], preferred_element_type=F32)
    kc_ref[0] = _rope128(kc, cos_ref[0], sin_ref[0]).astype(BF16)
    vc_ref[0] = jnp.transpose(vc).astype(BF16)


def _nsa_compress(cv, wa, wb, bias, w2k, w2v, cosc, sinc):
    B, S, _ = cv.shape
    nseg = S // CMP_STRIDE
    x = cv.reshape(B, nseg, CMP_STRIDE * 256)
    per_b = lambda r, w: pl.BlockSpec((1, r, w), lambda b: (b, 0, 0))
    full = lambda a: pl.BlockSpec(a.shape, lambda b: (0,) * a.ndim)
    return pl.pallas_call(
        _compress_kernel,
        grid=(B,),
        in_specs=[per_b(nseg, CMP_STRIDE * 256), full(wa), full(wb), full(bias), full(w2k), full(w2v),
                  per_b(nseg, 128), per_b(nseg, 128)],
        out_specs=[per_b(nseg, 128), per_b(nseg, 128)],
        out_shape=[jax.ShapeDtypeStruct((B, nseg, 128), BF16)] * 2,
        compiler_params=_cparams(("parallel",)),
        name="nsa_compress",
    )(x, wa, wb, bias, w2k, w2v, cosc, sinc)


def _compress_weights(pe, w1, w2):
    eye2 = jnp.eye(2, dtype=F32)
    half = lambda lo: jnp.einsum('jldf,pj,qg->lpqdjgf', w1[:, lo:lo + CMP_STRIDE], eye2, eye2).reshape(
        CMP_STRIDE * 256, 4 * CMP_HIDDEN).astype(BF16)
    bias = jnp.einsum('jld,jldf->jf', pe, w1)
    bias = jnp.broadcast_to(bias[:, None, :], (2, 2, CMP_HIDDEN)).reshape(1, 4 * CMP_HIDDEN)
    bd = lambda w: jnp.einsum('fd,gq->gfqd', w, eye2).reshape(2 * CMP_HIDDEN, 2 * HEAD_DIM).astype(BF16)
    return half(0), half(CMP_STRIDE), bias, bd(w2[0]), bd(w2[1])


def _cover_t(S):
    n_cmp = (S - CMP_LEN) // CMP_STRIDE + 1
    cmp_start = np.arange(n_cmp) * CMP_STRIDE
    slc_start = np.arange(S // SEL_BLOCK) * SEL_BLOCK
    cover = np.clip(np.minimum(cmp_start[:, None] + CMP_LEN, slc_start[None, :] + SEL_BLOCK)
                    - np.maximum(cmp_start[:, None], slc_start[None, :]), 0, None) / CMP_LEN
    out = np.zeros((S // SEL_BLOCK, S // CMP_STRIDE), np.float32)
    out[:, :n_cmp] = cover.T
    return jnp.asarray(out, BF16)


MASKED = 2.0 * NEG_INF


def _softmax_step(s, carry, v_t):
    m, l, acc = carry
    m_new = jnp.maximum(m, jnp.max(s, axis=0, keepdims=True))
    alpha = jnp.exp(m - m_new)
    p = jnp.exp(s - m_new)
    l = alpha * l + jnp.sum(p, axis=0, keepdims=True)
    acc = alpha * acc + jnp.dot(v_t, p.astype(BF16), preferred_element_type=F32)
    return m_new, l, acc


def _softmax_init(n_q):
    return (jnp.full((1, n_q), NEG_INF, F32), jnp.zeros((1, n_q), F32), jnp.zeros((LANES, n_q), F32))


def _nsa_kernel(qt_ref, k_ref, vt_ref, kc_ref, vct_ref, gt_ref, covt_ref, o_ref, *, tq, tk):
    i = pl.program_id(1)
    t0 = i * tq
    n_blk = covt_ref.shape[0]
    n_q = 2 * tq
    qa_t, qb_t = qt_ref[0, 0:128, :], qt_ref[0, 128:256, :]
    sub_grp = lax.broadcasted_iota(jnp.int32, (LANES, tq), 0) // HEAD_DIM
    gates_t = gt_ref[0]
    tok = t0 + (lax.broadcasted_iota(jnp.int32, (1, n_q), 1) & (tq - 1))
    kc, vc_t = kc_ref[0], vct_ref[0]

    blk = lax.broadcasted_iota(jnp.int32, (n_blk, tq), 0)
    tok_b = t0 + lax.broadcasted_iota(jnp.int32, (n_blk, tq), 1)
    cur = tok_b // SEL_BLOCK
    forced = (blk == 0) | (blk == cur) | (blk == cur - 1)
    causal_blk = blk * SEL_BLOCK <= tok_b

    def key_rows(j, width, col):
        return k_ref[0, pl.ds(pl.multiple_of(j * width, width), width), col:col + 128]

    def value_cols(j, width, row):
        return vt_ref[0, row:row + 128, pl.ds(pl.multiple_of(j * width, width), width)]

    qg_t, o_cmp, sel_bias = [], [], []
    for g in range(2):
        in_g = sub_grp == g
        q_t = jnp.concatenate([jnp.where(in_g, qa_t, 0), jnp.where(in_g, qb_t, 0)], axis=1)
        qg_t.append(q_t)

        s = jnp.dot(kc, q_t, preferred_element_type=F32)
        nrow = lax.broadcasted_iota(jnp.int32, s.shape, 0)
        s = jnp.where(nrow * CMP_STRIDE + (CMP_LEN - 1) <= tok, s, MASKED)
        e = jnp.exp(s - jnp.maximum(jnp.max(s, axis=0, keepdims=True), NEG_INF))
        den = jnp.sum(e, axis=0, keepdims=True)
        p = e / jnp.where(den > 0.0, den, 1.0)
        o_cmp.append(jnp.dot(vc_t, p.astype(BF16), preferred_element_type=F32))

        imp = _dot_split_rhs(covt_ref[...], p[:, 0:tq] + p[:, tq:n_q], 2)
        imp = jnp.where(forced, 1e30, jnp.where(causal_blk, imp, -1e30))
        cnt = jnp.zeros((n_blk, tq), F32)
        for i2 in range(n_blk):
            row = imp[i2:i2 + 1, :]
            beats = (row > imp) | ((row == imp) & (blk > i2))
            cnt = cnt + jnp.where(beats, 1.0, 0.0)
        bias = jnp.where(cnt < float(SEL_TOPN), 0.0, MASKED).astype(BF16)
        sel_bias.append(jnp.concatenate([bias, bias], axis=1))

    n_tiles = (t0 + tq + tk - 1) // tk
    w_lo = jnp.maximum(n_tiles - (WINDOW // tk + 1), 0)

    def sel_step(j, carry_g, g, key, expand):
        s = jnp.dot(key_rows(j, tk, 0), qg_t[g], preferred_element_type=F32)
        s = s + jnp.dot(expand, sel_bias[g], preferred_element_type=F32)
        s = jnp.where(key <= tok, s, MASKED)
        return _softmax_step(s, carry_g, value_cols(j, tk, 0))

    def win_step(j, carry_g, g, key):
        s = jnp.dot(key_rows(j, tk, 128), qg_t[g], preferred_element_type=F32)
        s = jnp.where((key <= tok) & (tok - key < WINDOW), s, MASKED)
        return _softmax_step(s, carry_g, value_cols(j, tk, 128))

    def tile_ids(j):
        key = j * tk + lax.broadcasted_iota(jnp.int32, (tk, n_q), 0)
        blk_of_key = (j * tk + lax.broadcasted_iota(jnp.int32, (tk, n_blk), 0)) // SEL_BLOCK
        expand = (lax.broadcasted_iota(jnp.int32, (tk, n_blk), 1) == blk_of_key).astype(BF16)
        return key, expand

    def body_sel(j, carry):
        key, expand = tile_ids(j)
        return tuple(sel_step(j, carry[g], g, key, expand) for g in range(2))

    def body_both(j, carry):
        key, expand = tile_ids(j)
        return (tuple(sel_step(j, carry[g], g, key, expand) for g in range(2))
                + tuple(win_step(j, carry[2 + g], g, key) for g in range(2)))

    sel_carry = lax.fori_loop(0, w_lo, body_sel, (_softmax_init(n_q), _softmax_init(n_q)))
    final = lax.fori_loop(w_lo, n_tiles, body_both, sel_carry + (_softmax_init(n_q), _softmax_init(n_q)))

    out = [jnp.zeros((LANES, tq), F32), jnp.zeros((LANES, tq), F32)]
    for g in range(2):
        o_sel = final[g][2] / final[g][1]
        o_win = final[2 + g][2] / final[2 + g][1]
        for r in range(2):
            cols = slice(r * tq, (r + 1) * tq)
            c = g * 6 + r * 3
            o = (gates_t[c:c + 1, :] * o_cmp[g][:, cols] + gates_t[c + 1:c + 2, :] * o_sel[:, cols]
                 + gates_t[c + 2:c + 3, :] * o_win[:, cols])
            out[r] = jnp.where(sub_grp == g, o, out[r])
    o_ref[0, :, 0:128] = jnp.transpose(out[0]).astype(BF16)
    o_ref[0, :, 128:256] = jnp.transpose(out[1]).astype(BF16)


def _nsa_attend(q_t, k2, v_t, kcmp, vcmp_t, gates_t, tq=128, tk=256):
    B, S, _ = k2.shape
    covt = _cover_t(S)
    tile_t = lambda w: pl.BlockSpec((1, w, tq), lambda b, i: (b, 0, i))
    per_b = lambda r, w: pl.BlockSpec((1, r, w), lambda b, i: (b, 0, 0))
    return pl.pallas_call(
        functools.partial(_nsa_kernel, tq=tq, tk=tk),
        grid=(B, S // tq),
        in_specs=[tile_t(256), per_b(S, 256), per_b(256, S), per_b(S // CMP_STRIDE, 128), per_b(128, S // CMP_STRIDE),
                  tile_t(128), pl.BlockSpec(covt.shape, lambda b, i: (0, 0))],
        out_specs=pl.BlockSpec((1, tq, 256), lambda b, i: (b, i, 0)),
        out_shape=jax.ShapeDtypeStruct((B, S, 256), BF16),
        compiler_params=_cparams(("parallel", "arbitrary")),
        name="nsa_attend",
    )(q_t, k2, v_t, kcmp, vcmp_t, gates_t, covt)


def _pool_kernel(u_ref, w_ref, scale_ref, o_ref):
    u = u_ref[0]
    row = lax.broadcasted_iota(jnp.int32, u.shape, 0)
    grp = lax.broadcasted_iota(jnp.int32, u.shape, 1) // HEAD_DIM

    def back(x, k):
        return jnp.where(row >= k, pltpu.roll(x, k, 0), 0.0)

    s2 = u + back(u, 1)
    s4 = s2 + back(s2, 2)
    s8 = s4 + back(s4, 4)
    s16 = s8 + back(s8, 8)
    total = jnp.where(grp == 0, s2, jnp.where(grp == 1, s4, jnp.where(grp == 2, s8, s16)))
    win = jnp.where(grp == 0, 2, jnp.where(grp == 1, 4, jnp.where(grp == 2, 8, 16)))
    count = jnp.minimum(row + 1, win).astype(F32)
    d = total / count - u
    y = jnp.dot(d.astype(BF16), w_ref[...], preferred_element_type=F32)
    o_ref[0] = (y * scale_ref[...]).astype(BF16)


def _pool(u, w_bd, scale):
    B, S, W = u.shape
    per_b = pl.BlockSpec((1, S, W), lambda b: (b, 0, 0))
    full = lambda a: pl.BlockSpec(a.shape, lambda b: (0,) * a.ndim)
    return pl.pallas_call(
        _pool_kernel,
        grid=(B,),
        in_specs=[per_b, full(w_bd), full(scale)],
        out_specs=per_b,
        out_shape=jax.ShapeDtypeStruct((B, S, W), BF16),
        compiler_params=_cparams(("parallel",)),
        name="pool",
    )(u, w_bd, scale)


def _block_diag(w):
    G, a, b = w.shape
    return jnp.einsum('gab,gh->gahb', w, jnp.eye(G, dtype=w.dtype)).reshape(G * a, G * b)


def _dot_split_lhs(a, b_exact, terms):
    out, rem = None, a
    for i in range(terms):
        hi = rem.astype(BF16)
        d = jnp.dot(hi, b_exact, preferred_element_type=F32)
        out = d if out is None else out + d
        if i + 1 < terms:
            rem = rem - hi.astype(F32)
    return out


def _dot_split_rhs(a_exact, b, terms):
    out, rem = None, b
    for i in range(terms):
        hi = rem.astype(BF16)
        d = jnp.dot(a_exact, hi, preferred_element_type=F32)
        out = d if out is None else out + d
        if i + 1 < terms:
            rem = rem - hi.astype(F32)
    return out


def _dot_tn(a, b):
    return lax.dot_general(a, b, (((0,), (0,)), ((), ())), preferred_element_type=F32)


def _stack_heads(x, width, n_heads=4):
    head = lax.broadcasted_iota(jnp.int32, x.shape, 1) // width
    return jnp.concatenate([jnp.where(head == h, x, 0.0) for h in range(n_heads)], axis=0)


def _unstack_heads(x, C, n_heads=4):
    out = x[0:C]
    for h in range(1, n_heads):
        out = out + x[h * C:(h + 1) * C]
    return out


def _block_masks(C, n_heads=4):
    n = C * n_heads
    row = lax.broadcasted_iota(jnp.int32, (n, n), 0)
    col = lax.broadcasted_iota(jnp.int32, (n, n), 1)
    same = (row // C) == (col // C)
    return same & (row > col), same & (row >= col)


def _tri_ones(C):
    return jnp.asarray(np.tril(np.ones((C, C), np.float32)), BF16)


def _head_ones(width, n_heads=4):
    return jnp.asarray(np.kron(np.eye(n_heads, dtype=np.float32), np.ones((width, width), np.float32)), BF16)


def _gla_kernel(gl_ref, a2_ref, ab_ref, norm_ref, tri_ref, ones_ref, o_ref, state_ref, *, C):
    s = pl.program_id(1)

    @pl.when(s == 0)
    def _():
        state_ref[...] = jnp.zeros_like(state_ref)

    gl = gl_ref[0]
    tm = gl.shape[0]
    q = gl[:, 0:128] * (GLA_DK ** -0.5)
    k = gl[:, 128:256]
    v = gl[:, 256:512]
    og = gl[:, 512:768]
    ad = gl[:, 768:896]
    x = jnp.dot(ad.astype(BF16), a2_ref[...], preferred_element_type=F32) + ab_ref[...]
    log_a = jax.nn.log_sigmoid(x) * (1.0 / GLA_TAU)
    strict, incl = _block_masks(C)
    kv_mask = (lax.broadcasted_iota(jnp.int32, (256, 128), 0) // HEAD_DIM
               == lax.broadcasted_iota(jnp.int32, (256, 128), 1) // GLA_DK)
    outs = []
    for c in range(tm // C):
        rc = slice(c * C, (c + 1) * C)
        bcum = _dot_split_rhs(tri_ref[...], log_a[rc], 3)
        mid = bcum[C // 2 - 1:C // 2, :]
        last = bcum[C - 1:C, :]
        q_in = q[rc] * jnp.exp(bcum)
        q_m = _stack_heads(q[rc] * jnp.exp(bcum - mid), GLA_DK).astype(BF16)
        k_m = _stack_heads(k[rc] * jnp.exp(mid - bcum), GLA_DK).astype(BF16)
        k_end = k[rc] * jnp.exp(last - bcum)
        v_c = v[rc]
        att = jnp.where(incl, _dot_nt(q_m, k_m), 0.0)
        v_st = _stack_heads(v_c, HEAD_DIM).astype(BF16)
        o_intra = _unstack_heads(jnp.dot(att.astype(BF16), v_st, preferred_element_type=F32), C)
        state = state_ref[...]
        o_inter = _dot_nt(q_in.astype(BF16), state.astype(BF16))
        upd = _dot_tn(v_c.astype(BF16), k_end.astype(BF16))
        state_ref[...] = state * jnp.exp(last) + jnp.where(kv_mask, upd, 0.0)
        outs.append(o_intra + o_inter)
    o = jnp.concatenate(outs, axis=0)
    ms = _dot_split_lhs(o * o, ones_ref[...], 2) * (1.0 / HEAD_DIM)
    o = o * lax.rsqrt(ms + NORM_EPS) * norm_ref[...]
    o_ref[0] = (o * (og * jax.nn.sigmoid(og))).astype(BF16)


def _gla(gl, a2p, ab, norm4, tm=256):
    B, S, W = gl.shape
    C = GLA_CHUNK
    tri, ones = _tri_ones(C), _head_ones(HEAD_DIM)
    tile = lambda w: pl.BlockSpec((1, tm, w), lambda b, s: (b, s, 0))
    full = lambda a: pl.BlockSpec(a.shape, lambda b, s: (0,) * a.ndim)
    return pl.pallas_call(
        functools.partial(_gla_kernel, C=C),
        grid=(B, S // tm),
        in_specs=[tile(W), full(a2p), full(ab), full(norm4), full(tri), full(ones)],
        out_specs=tile(256),
        out_shape=jax.ShapeDtypeStruct((B, S, 256), BF16),
        scratch_shapes=[pltpu.VMEM((256, 128), F32)],
        compiler_params=_cparams(("parallel", "arbitrary")),
        name="gla",
    )(gl, a2p, ab, norm4, tri, ones)


_RV_W0, _RV_A0, _RV_KK, _RV_KA, _RV_RK, _RV_LN, _RV_V0 = range(7)


def _rwkv_kernel(*refs, has_vres, C):
    if has_vres:
        (rw_ref, h_ref, vf_ref, wz_ref, vec_ref, v1_ref, v2_ref, tri_ref, ones_ref, o_ref, state_ref) = refs
    else:
        (rw_ref, wz_ref, vec_ref, tri_ref, ones_ref, o_ref, vout_ref, state_ref) = refs
    s = pl.program_id(1)

    @pl.when(s == 0)
    def _():
        state_ref[...] = jnp.zeros_like(state_ref)

    rw = rw_ref[0]
    tm = rw.shape[0]
    r, k, v, z = rw[:, 0:256], rw[:, 256:512], rw[:, 512:768], rw[:, 768:896]
    vec = lambda i: vec_ref[i:i + 1, :]
    zl = lax.broadcasted_iota(jnp.int32, z.shape, 1)
    zf = jnp.where(zl < 32, jnp.tanh(z), jnp.where(zl < 64, z, jax.nn.sigmoid(z)))
    zz = jnp.dot(zf.astype(BF16), wz_ref[...], preferred_element_type=F32)
    w_log = -jax.nn.softplus(-(vec(_RV_W0) + zz[:, 0:256])) - 0.5
    lw = -jnp.exp(w_log)
    a_sig = jax.nn.sigmoid(vec(_RV_A0) + zz[:, 256:512])
    gate = zz[:, 512:768]
    if has_vres:
        low = jnp.dot(h_ref[0], v1_ref[...], preferred_element_type=F32)
        logit = vec(_RV_V0) + jnp.dot(low.astype(BF16), v2_ref[...], preferred_element_type=F32)
        v = v + (vf_ref[0] - v) * jax.nn.sigmoid(logit)
    else:
        vout_ref[0] = v
    kk = k * vec(_RV_KK)
    norm = jnp.sqrt(_dot_split_lhs(kk * kk, ones_ref[...], 2))
    kk = kk / jnp.maximum(norm, 1e-12)
    k = k * (1.0 + (a_sig - 1.0) * vec(_RV_KA))
    a_vec = -kk
    b_vec = kk * a_sig

    strict, incl = _block_masks(C)
    st = lambda t: _stack_heads(t, HEAD_DIM)
    bdot = lambda p, q: jnp.dot(p.astype(BF16), q.astype(BF16), preferred_element_type=F32)
    state_mask = (lax.broadcasted_iota(jnp.int32, (256, 256), 0) // HEAD_DIM
                  == lax.broadcasted_iota(jnp.int32, (256, 256), 1) // HEAD_DIM)
    ys = []
    for c in range(tm // C):
        rc = slice(c * C, (c + 1) * C)
        lcum = _dot_split_rhs(tri_ref[...], lw[rc], 3)
        lex = lcum - lw[rc]
        mid = lcum[C // 2 - 1:C // 2, :]
        last = lcum[C - 1:C, :]
        e_mid = jnp.exp(mid - lcum)
        a_abs = a_vec[rc] * jnp.exp(lex)
        r_abs = r[rc] * jnp.exp(lcum)
        left = jnp.concatenate([st(a_vec[rc] * jnp.exp(lex - mid)), st(r[rc] * jnp.exp(lcum - mid))], axis=0)
        right = jnp.concatenate([st(b_vec[rc] * e_mid), st(k[rc] * e_mid)], axis=0)
        g = _dot_nt(left.astype(BF16), right.astype(BF16))
        n4 = 4 * C
        n_ab = jnp.where(strict, g[0:n4, 0:n4], 0.0)
        n_ak = jnp.where(strict, g[0:n4, n4:2 * n4], 0.0)
        n_rb = jnp.where(incl, g[n4:2 * n4, 0:n4], 0.0)
        n_rk = jnp.where(incl, g[n4:2 * n4, n4:2 * n4], 0.0)
        v_st = st(v[rc])
        x = jnp.concatenate([st(a_abs), bdot(n_ak, v_st)], axis=1)
        p = n_ab
        n_fac = int(np.log2(C))
        for f in range(n_fac):
            x = x + bdot(p, x)
            if f + 1 < n_fac:
                p = bdot(p, p)
        y_part = bdot(n_rb, x)
        a_eff = _unstack_heads(x[:, 0:256], C)
        u0 = _unstack_heads(x[:, 256:512], C)
        r_eff = r_abs + _unstack_heads(y_part[:, 0:256], C)
        y0 = _unstack_heads(y_part[:, 256:512] + bdot(n_rk, v_st), C)
        state = state_ref[...]
        uy = _dot_nt(jnp.concatenate([a_eff, r_eff], axis=0).astype(BF16), state.astype(BF16))
        u = uy[0:C] + u0
        ys.append(uy[C:2 * C] + y0)
        e_end = jnp.exp(last - lcum)
        upd = _dot_tn(jnp.concatenate([u, v[rc]], axis=0).astype(BF16),
                      jnp.concatenate([b_vec[rc] * e_end, k[rc] * e_end], axis=0).astype(BF16))
        state_ref[...] = state * jnp.exp(last) + jnp.where(state_mask, upd, 0.0)
    y = jnp.concatenate(ys, axis=0)
    inv = 1.0 / HEAD_DIM
    mean = _dot_split_lhs(y, ones_ref[...], 2) * inv
    yc = y - mean
    var = _dot_split_lhs(yc * yc, ones_ref[...], 2) * inv
    y = yc * lax.rsqrt(var + RWKV_GN_EPS) * vec(_RV_LN)
    y = y + _dot_split_lhs(r * k * vec(_RV_RK), ones_ref[...], 2) * v
    o_ref[0] = (y * gate).astype(BF16)


def _rwkv(rw, wz, vecs, h=None, v_first=None, v1p=None, v2p=None, tm=256):
    B, S, W = rw.shape
    C = RWKV_CHUNK
    has_vres = h is not None
    tri, ones = _tri_ones(C), _head_ones(HEAD_DIM)
    tile = lambda w: pl.BlockSpec((1, tm, w), lambda b, s: (b, s, 0))
    full = lambda a: pl.BlockSpec(a.shape, lambda b, s: (0,) * a.ndim)
    if has_vres:
        args = (rw, h, v_first, wz, vecs, v1p, v2p, tri, ones)
        in_specs = [tile(W), tile(D_MODEL), tile(256), full(wz), full(vecs), full(v1p), full(v2p), full(tri), full(ones)]
        out_specs = tile(256)
        out_shape = jax.ShapeDtypeStruct((B, S, 256), BF16)
    else:
        args = (rw, wz, vecs, tri, ones)
        in_specs = [tile(W), full(wz), full(vecs), full(tri), full(ones)]
        out_specs = [tile(256), tile(256)]
        out_shape = [jax.ShapeDtypeStruct((B, S, 256), BF16), jax.ShapeDtypeStruct((B, S, 256), F32)]
    return pl.pallas_call(
        functools.partial(_rwkv_kernel, has_vres=has_vres, C=C),
        grid=(B, S // tm),
        in_specs=in_specs,
        out_specs=out_specs,
        out_shape=out_shape,
        scratch_shapes=[pltpu.VMEM((256, 256), F32)],
        compiler_params=_cparams(("parallel", "arbitrary")),
        name="rwkv_vres" if has_vres else "rwkv",
    )(*args)


def _rwkv_weights(l, rwkv_w0, rwkv_w2, rwkv_a0, rwkv_a2, rwkv_g2, rwkv_v0, rwkv_k_k, rwkv_k_a, rwkv_r_k, rwkv_ln):
    wz = jnp.zeros((128, 768), F32)
    wz = wz.at[0:32, 0:256].set(rwkv_w2[l]).at[32:64, 256:512].set(rwkv_a2[l]).at[64:128, 512:768].set(rwkv_g2[l])
    v0 = rwkv_v0[l - 1] if l > 0 else jnp.zeros((256,), F32)
    vecs = jnp.stack([rwkv_w0[l], rwkv_a0[l], rwkv_k_k[l], rwkv_k_a[l], rwkv_r_k[l].reshape(-1), rwkv_ln[l], v0,
                      jnp.zeros((256,), F32)])
    return wz.astype(BF16), vecs


def _merge_kernel(x_ref, h_ref, o0_ref, o1_ref, o2_ref, o3_ref, wg_ref, bg_ref, wb_ref, wo_ref, out_ref):
    h = h_ref[0]
    merged = None
    for i, o_ref in enumerate((o0_ref, o1_ref, o2_ref, o3_ref)):
        gate = jax.nn.sigmoid(jnp.dot(h, wg_ref[i], preferred_element_type=F32) + bg_ref[i:i + 1, :])
        term = gate * jnp.dot(o_ref[0], wb_ref[i], preferred_element_type=F32)
        merged = term if merged is None else merged + term
    out_ref[0] = x_ref[0] + jnp.dot(merged.astype(BF16), wo_ref[...], preferred_element_type=F32)


def _merge(x, h, o_nsa, o_pool, o_rwkv, o_gla, wg, bg, wb, wo, tm=512):
    B, S, D = x.shape
    tile = lambda w: pl.BlockSpec((1, tm, w), lambda b, s: (b, s, 0))
    full = lambda a: pl.BlockSpec(a.shape, lambda b, s: (0,) * a.ndim)
    return pl.pallas_call(
        _merge_kernel,
        grid=(B, S // tm),
        in_specs=[tile(D), tile(D), tile(256), tile(256), tile(256), tile(256), full(wg), full(bg), full(wb), full(wo)],
        out_specs=tile(D),
        out_shape=jax.ShapeDtypeStruct((B, S, D), F32),
        compiler_params=_cparams(("parallel", "parallel")),
        name="merge",
    )(x, h, o_nsa, o_pool, o_rwkv, o_gla, wg, bg, wb, wo)


def _memkv_kernel(mem_ref, g_ref, w_ref, o_ref):
    mn = _rms(mem_ref[0], g_ref[...]).astype(BF16)
    o_ref[0] = jnp.dot(mn, w_ref[...], preferred_element_type=F32).astype(BF16)


def _memkv(mem, g, wkv_all):
    B, M, D = mem.shape
    N = wkv_all.shape[1]
    return pl.pallas_call(
        _memkv_kernel,
        grid=(B,),
        in_specs=[pl.BlockSpec((1, M, D), lambda b: (b, 0, 0)), pl.BlockSpec(g.shape, lambda b: (0, 0)),
                  pl.BlockSpec(wkv_all.shape, lambda b: (0, 0))],
        out_specs=pl.BlockSpec((1, M, N), lambda b: (b, 0, 0)),
        out_shape=jax.ShapeDtypeStruct((B, M, N), BF16),
        compiler_params=_cparams(("parallel",)),
        name="memkv",
    )(mem, g, wkv_all)


def _cross_kernel(x_ref, g_ref, wq_ref, k_ref, v_ref, wo_ref, out_ref):
    x = x_ref[0]
    hc = _rms(x, g_ref[...]).astype(BF16)
    q = jnp.dot(hc, wq_ref[...], preferred_element_type=F32) * (HEAD_DIM ** -0.5)
    head = lax.broadcasted_iota(jnp.int32, q.shape, 1) // HEAD_DIM
    k, v = k_ref[0], v_ref[0]
    o = jnp.zeros(q.shape, F32)
    for hh in range(CA_WIDTH // HEAD_DIM):
        qh = jnp.where(head == hh, q, 0.0).astype(BF16)
        s = _dot_nt(qh, k)
        e = jnp.exp(s - jnp.max(s, axis=1, keepdims=True))
        p = e / jnp.sum(e, axis=1, keepdims=True)
        o = jnp.where(head == hh, jnp.dot(p.astype(BF16), v, preferred_element_type=F32), o)
    out_ref[0] = x + jnp.dot(o.astype(BF16), wo_ref[...], preferred_element_type=F32)


def _cross(x, g, wq, memkv, l, wo, tm=512):
    B, S, D = x.shape
    M = memkv.shape[1]
    tile = pl.BlockSpec((1, tm, D), lambda b, s: (b, s, 0))
    full = lambda a: pl.BlockSpec(a.shape, lambda b, s: (0,) * a.ndim)
    kspec = pl.BlockSpec((1, M, CA_WIDTH), lambda b, s: (b, 0, 2 * l))
    vspec = pl.BlockSpec((1, M, CA_WIDTH), lambda b, s: (b, 0, 2 * l + 1))
    return pl.pallas_call(
        _cross_kernel,
        grid=(B, S // tm),
        in_specs=[tile, full(g), full(wq), kspec, vspec, full(wo)],
        out_specs=tile,
        out_shape=jax.ShapeDtypeStruct((B, S, D), F32),
        compiler_params=_cparams(("parallel", "parallel")),
        name="cross_attn",
    )(x, g, wq, memkv, memkv, wo)


def _ffn_kernel(x_ref, g_ref, wg_ref, wu_ref, wd_ref, out_ref, hb_ref, acc_ref):
    f = pl.program_id(1)

    @pl.when(f == 0)
    def _():
        hb_ref[...] = _rms(x_ref[...], g_ref[...]).astype(BF16)
        acc_ref[...] = jnp.zeros_like(acc_ref)

    hb = hb_ref[...]
    gate = jnp.dot(hb, wg_ref[...], preferred_element_type=F32)
    up = jnp.dot(hb, wu_ref[...], preferred_element_type=F32)
    act = gate * jax.nn.sigmoid(gate) * up
    acc_ref[...] += jnp.dot(act.astype(BF16), wd_ref[...], preferred_element_type=F32)

    @pl.when(f == pl.num_programs(1) - 1)
    def _():
        out_ref[...] = x_ref[...] + acc_ref[...]


def _ffn(x2, g, wg, wu, wd, tm=1024, tf=256):
    T, D = x2.shape
    F = wg.shape[1]
    return pl.pallas_call(
        _ffn_kernel,
        grid=(T // tm, F // tf),
        in_specs=[pl.BlockSpec((tm, D), lambda i, f: (i, 0)), pl.BlockSpec(g.shape, lambda i, f: (0, 0)),
                  pl.BlockSpec((D, tf), lambda i, f: (0, f)), pl.BlockSpec((D, tf), lambda i, f: (0, f)),
                  pl.BlockSpec((tf, D), lambda i, f: (f, 0))],
        out_specs=pl.BlockSpec((tm, D), lambda i, f: (i, 0)),
        out_shape=jax.ShapeDtypeStruct((T, D), F32),
        scratch_shapes=[pltpu.VMEM((tm, D), BF16), pltpu.VMEM((tm, D), F32)],
        compiler_params=_cparams(("parallel", "arbitrary")),
        name="ffn",
    )(x2, g, wg, wu, wd)


def _top2_combine(logits):
    lane = lax.broadcasted_iota(jnp.int32, logits.shape, 1)
    big = logits.shape[1]
    logits = jnp.where(lane < N_EXPERTS, logits, -jnp.inf)
    m1 = jnp.max(logits, axis=1, keepdims=True)
    i1 = jnp.min(jnp.where(logits == m1, lane, big), axis=1, keepdims=True)
    rest = jnp.where(lane == i1, -jnp.inf, logits)
    m2 = jnp.max(rest, axis=1, keepdims=True)
    i2 = jnp.min(jnp.where(rest == m2, lane, big), axis=1, keepdims=True)
    e2 = jnp.exp(m2 - m1)
    w1 = 1.0 / (1.0 + e2)
    return jnp.where(lane == i1, w1, jnp.where(lane == i2, e2 * w1, 0.0))


def _moe_dense_kernel(x_ref, g_ref, r_ref, wg_ref, wu_ref, wd_ref, gf_ref, out_ref, hb_ref, acc_ref, comb_ref):
    e, f = pl.program_id(1), pl.program_id(2)

    @pl.when((e == 0) & (f == 0))
    def _():
        hf = _rms(x_ref[...], g_ref[...])
        hb_ref[...] = hf.astype(BF16)
        acc_ref[...] = jnp.zeros_like(acc_ref)
        logits = jnp.dot(hf, r_ref[...], preferred_element_type=F32, precision=lax.Precision.HIGHEST)
        comb_ref[...] = _top2_combine(logits)

    hb = hb_ref[...]
    gate = jnp.dot(hb, wg_ref[0], preferred_element_type=F32)
    up = jnp.dot(hb, wu_ref[0], preferred_element_type=F32)
    lane = lax.broadcasted_iota(jnp.int32, comb_ref.shape, 1)
    ce = jnp.sum(jnp.where(lane == e, comb_ref[...], 0.0), axis=1, keepdims=True)
    act = gate * jax.nn.sigmoid(gate) * up * ce
    acc_ref[...] += jnp.dot(act.astype(BF16), wd_ref[0], preferred_element_type=F32)

    @pl.when((e == pl.num_programs(1) - 1) & (f == pl.num_programs(2) - 1))
    def _():
        out_ref[...] = _rms(x_ref[...] + acc_ref[...], gf_ref[...])


def _moe_dense(x2, g, router_p, wg, wu, wd, g_final, tm=1024, tf=256):
    T, D = x2.shape
    E, _, F = wg.shape
    return pl.pallas_call(
        _moe_dense_kernel,
        grid=(T // tm, E, F // tf),
        in_specs=[pl.BlockSpec((tm, D), lambda i, e, f: (i, 0)), pl.BlockSpec(g.shape, lambda i, e, f: (0, 0)),
                  pl.BlockSpec(router_p.shape, lambda i, e, f: (0, 0)),
                  pl.BlockSpec((1, D, tf), lambda i, e, f: (e, 0, f)), pl.BlockSpec((1, D, tf), lambda i, e, f: (e, 0, f)),
                  pl.BlockSpec((1, tf, D), lambda i, e, f: (e, f, 0)), pl.BlockSpec(g_final.shape, lambda i, e, f: (0, 0))],
        out_specs=pl.BlockSpec((tm, D), lambda i, e, f: (i, 0)),
        out_shape=jax.ShapeDtypeStruct((T, D), F32),
        scratch_shapes=[pltpu.VMEM((tm, D), BF16), pltpu.VMEM((tm, D), F32), pltpu.VMEM((tm, LANES), F32)],
        compiler_params=_cparams(("parallel", "arbitrary", "arbitrary")),
        name="moe_dense",
    )(x2, g, router_p, wg, wu, wd, g_final)


MOE_TM = 1024
MOE_TD = 512


def _route_kernel(x_ref, g_ref, r_ref, hf_ref, idx_ref, w_ref):
    hf = _rms(x_ref[...], g_ref[...])
    hf_ref[...] = hf
    logits = jnp.dot(hf, r_ref[...], preferred_element_type=F32, precision=lax.Precision.HIGHEST)
    lane = lax.broadcasted_iota(jnp.int32, logits.shape, 1)
    big = logits.shape[1]
    logits = jnp.where(lane < N_EXPERTS, logits, -jnp.inf)
    m1 = jnp.max(logits, axis=1, keepdims=True)
    i1 = jnp.min(jnp.where(logits == m1, lane, big), axis=1, keepdims=True)
    rest = jnp.where(lane == i1, -jnp.inf, logits)
    m2 = jnp.max(rest, axis=1, keepdims=True)
    i2 = jnp.min(jnp.where(rest == m2, lane, big), axis=1, keepdims=True)
    e2 = jnp.exp(m2 - m1)
    w1 = 1.0 / (1.0 + e2)
    idx_ref[...] = jnp.where(lane == 0, i1, jnp.where(lane == 1, i2, 0))
    w_ref[...] = jnp.where(lane == 0, w1, jnp.where(lane == 1, e2 * w1, 0.0))


def _moe_route(x2, g, router_p, tm=1024):
    T, D = x2.shape
    row = lambda w: pl.BlockSpec((tm, w), lambda i: (i, 0))
    full = lambda a: pl.BlockSpec(a.shape, lambda i: (0,) * a.ndim)
    return pl.pallas_call(
        _route_kernel,
        grid=(T // tm,),
        in_specs=[row(D), full(g), full(router_p)],
        out_specs=[row(D), row(LANES), row(LANES)],
        out_shape=[jax.ShapeDtypeStruct((T, D), F32), jax.ShapeDtypeStruct((T, LANES), jnp.int32),
                   jax.ShapeDtypeStruct((T, LANES), F32)],
        compiler_params=_cparams(("parallel",)),
        name="moe_route",
    )(x2, g, router_p)


def _moe_plan(expert_of_pair, tm):
    onehot = (expert_of_pair[:, None] == jnp.arange(N_EXPERTS, dtype=jnp.int32)[None, :]).astype(jnp.int32)
    csum = jnp.cumsum(onehot, axis=0)
    rank = jnp.sum((csum - 1) * onehot, axis=1)
    counts = csum[-1]
    padded = ((counts + tm - 1) // tm) * tm
    ends = jnp.cumsum(padded)
    dest = (ends - padded)[expert_of_pair] + rank
    n_tiles = expert_of_pair.shape[0] // tm + N_EXPERTS
    tile_expert = jnp.sum((jnp.arange(n_tiles, dtype=jnp.int32)[:, None] * tm >= ends[None, :]).astype(jnp.int32), axis=1)
    return dest.astype(jnp.int32), jnp.minimum(tile_expert, N_EXPERTS - 1).astype(jnp.int32), (ends[-1] // tm).astype(jnp.int32)


def _row_copy(src_hbm, src_row, dst, dst_row, sem):
    return pltpu.make_async_copy(src_hbm.at[pl.ds(src_row, 1)], dst.at[pl.ds(dst_row, 1)], sem)


def _dispatch_kernel(dest_ref, hf_hbm, xs_in_hbm, xs_hbm, sem):
    del xs_in_hbm
    base = pl.program_id(0) * MOE_TD
    n = 2 * MOE_TD

    def start(p, carry):
        _row_copy(hf_hbm, base + p // 2, xs_hbm, dest_ref[0, 0, p], sem).start()
        return carry

    def wait(p, carry):
        _row_copy(hf_hbm, base + p // 2, xs_hbm, dest_ref[0, 0, p], sem).wait()
        return carry

    lax.fori_loop(0, n, start, 0)
    lax.fori_loop(0, n, wait, 0)


def _moe_dispatch(hf, dest, n_rows):
    T, D = hf.shape
    steps = T // MOE_TD
    dest3 = dest.reshape(steps, 1, 2 * MOE_TD)
    any_spec = pl.BlockSpec(memory_space=pl.ANY)
    return pl.pallas_call(
        _dispatch_kernel,
        grid=(steps,),
        in_specs=[pl.BlockSpec((1, 1, 2 * MOE_TD), lambda i: (i, 0, 0), memory_space=pltpu.SMEM), any_spec, any_spec],
        out_specs=any_spec,
        out_shape=jax.ShapeDtypeStruct((n_rows, D), F32),
        scratch_shapes=[pltpu.SemaphoreType.DMA(())],
        input_output_aliases={2: 0},
        compiler_params=_cparams(("arbitrary",)),
        name="moe_dispatch",
    )(dest3, hf, jnp.zeros((n_rows, D), F32))


def _experts_kernel(te_ref, nv_ref, xs_ref, wg_ref, wu_ref, wd_ref, ys_ref, xb_ref, acc_ref):
    i, f = pl.program_id(0), pl.program_id(1)
    live = i < nv_ref[0]

    @pl.when(f == 0)
    def _():
        xb_ref[...] = xs_ref[...].astype(BF16)
        acc_ref[...] = jnp.zeros_like(acc_ref)

    @pl.when(live)
    def _():
        xb = xb_ref[...]
        gate = jnp.dot(xb, wg_ref[0], preferred_element_type=F32)
        up = jnp.dot(xb, wu_ref[0], preferred_element_type=F32)
        act = gate * jax.nn.sigmoid(gate) * up
        acc_ref[...] += jnp.dot(act.astype(BF16), wd_ref[0], preferred_element_type=F32)

    @pl.when(f == pl.num_programs(1) - 1)
    def _():
        ys_ref[...] = acc_ref[...]


def _moe_experts(xs, tile_expert, n_live, wg, wu, wd, tf=256):
    R, D = xs.shape
    F = wg.shape[2]
    tm = MOE_TM
    grid_spec = pltpu.PrefetchScalarGridSpec(
        num_scalar_prefetch=2,
        grid=(R // tm, F // tf),
        in_specs=[pl.BlockSpec((tm, D), lambda i, f, te, nv: (i, 0)),
                  pl.BlockSpec((1, D, tf), lambda i, f, te, nv: (te[i], 0, f)),
                  pl.BlockSpec((1, D, tf), lambda i, f, te, nv: (te[i], 0, f)),
                  pl.BlockSpec((1, tf, D), lambda i, f, te, nv: (te[i], f, 0))],
        out_specs=pl.BlockSpec((tm, D), lambda i, f, te, nv: (i, 0)),
        scratch_shapes=[pltpu.VMEM((tm, D), BF16), pltpu.VMEM((tm, D), F32)],
    )
    return pl.pallas_call(
        _experts_kernel,
        grid_spec=grid_spec,
        out_shape=jax.ShapeDtypeStruct((R, D), F32),
        compiler_params=_cparams(("parallel", "arbitrary")),
        name="moe_experts",
    )(tile_expert, n_live, xs, wg, wu, wd)


def _combine_kernel(dest_ref, x_ref, w_ref, gf_ref, ys_hbm, out_ref, ybuf, sem):
    n = 2 * MOE_TD

    def start(p, carry):
        _row_copy(ys_hbm, dest_ref[0, 0, p], ybuf.at[p % 2], p // 2, sem).start()
        return carry

    def wait(p, carry):
        _row_copy(ys_hbm, dest_ref[0, 0, p], ybuf.at[p % 2], p // 2, sem).wait()
        return carry

    lax.fori_loop(0, n, start, 0)
    lax.fori_loop(0, n, wait, 0)
    w = w_ref[...]
    y = x_ref[...] + w[:, 0:1] * ybuf[0] + w[:, 1:2] * ybuf[1]
    out_ref[...] = _rms(y, gf_ref[...])


def _moe_combine(x2, w, dest, ys, g_final):
    T, D = x2.shape
    steps = T // MOE_TD
    dest3 = dest.reshape(steps, 1, 2 * MOE_TD)
    row = lambda wd: pl.BlockSpec((MOE_TD, wd), lambda i: (i, 0))
    return pl.pallas_call(
        _combine_kernel,
        grid=(steps,),
        in_specs=[pl.BlockSpec((1, 1, 2 * MOE_TD), lambda i: (i, 0, 0), memory_space=pltpu.SMEM), row(D), row(LANES),
                  pl.BlockSpec(g_final.shape, lambda i: (0, 0)), pl.BlockSpec(memory_space=pl.ANY)],
        out_specs=row(D),
        out_shape=jax.ShapeDtypeStruct((T, D), F32),
        scratch_shapes=[pltpu.VMEM((2, MOE_TD, D), F32), pltpu.SemaphoreType.DMA(())],
        compiler_params=_cparams(("arbitrary",)),
        name="moe_combine",
    )(dest3, x2, w, g_final, ys)


def _moe_routed(x2, g, router_p, wg, wu, wd, g_final):
    T, D = x2.shape
    hf, idx, w = _moe_route(x2, g, router_p)
    dest, tile_expert, n_live = _moe_plan(idx[:, 0:2].reshape(2 * T), MOE_TM)
    n_rows = 2 * T + N_EXPERTS * MOE_TM
    xs = _moe_dispatch(hf, dest, n_rows)
    ys = _moe_experts(xs, tile_expert, n_live.reshape(1), wg, wu, wd)
    return _moe_combine(x2, w, dest, ys, g_final)


def kernel(x, mem, positions, norm_mix, w_in, nsa_cmp_pe, nsa_cmp_w1, nsa_cmp_w2, pool_w, pool_scale, rwkv_mu, rwkv_w0,
           rwkv_w2, rwkv_a0, rwkv_a2, rwkv_g2, rwkv_v0, rwkv_v1, rwkv_v2, rwkv_k_k, rwkv_k_a, rwkv_r_k, rwkv_ln, gla_a2,
           gla_ab, gla_norm, w_gate, b_gate, w_branch, w_out, norm_ca, mem_norm, w_ca_q, w_ca_kv, w_ca_o, norm_ffn,
           ffn_w_gate, ffn_w_up, ffn_w_down, moe_router, moe_w_gate, moe_w_up, moe_w_down, norm_final):
    B, S, D = x.shape
    depth = norm_mix.shape[0]
    assert depth == 2 and S % 256 == 0, "the final norm is fused into the layer-1 expert mixer"
    cosf, sinf = _rope_tables(positions)
    at_cmp_end = lambda t: jnp.pad(t[:, CMP_LEN - 1::CMP_STRIDE], ((0, 0), (0, 1), (0, 0)))
    cosc, sinc = at_cmp_end(cosf), at_cmp_end(sinf)
    memkv = _memkv(mem, mem_norm[None], jnp.concatenate([w_ca_kv[l] for l in range(depth)], axis=1).astype(BF16))
    v_first = None
    for l in range(depth):
        h, q_t, k2, v_t, cv, gates_t, u, rw, gl = _inproj(x, norm_mix[l][None], _inproj_weights(w_in, l), cosf, sinf,
                                                          rwkv_mu[l][None])
        kcmp, vcmp_t = _nsa_compress(cv, *_compress_weights(nsa_cmp_pe[l], nsa_cmp_w1[l], nsa_cmp_w2[l]), cosc, sinc)
        o_nsa = _nsa_attend(q_t, k2, v_t, kcmp, vcmp_t, gates_t)
        o_pool = _pool(u, _block_diag(pool_w[l]).astype(BF16), pool_scale[l][None])
        wz, vecs = _rwkv_weights(l, rwkv_w0, rwkv_w2, rwkv_a0, rwkv_a2, rwkv_g2, rwkv_v0, rwkv_k_k, rwkv_k_a,
                                 rwkv_r_k, rwkv_ln)
        if l == 0:
            o_rwkv, v_first = _rwkv(rw, wz, vecs)
        else:
            v1p = jnp.pad(rwkv_v1[l - 1], ((0, 0), (0, LANES - rwkv_v1.shape[2]))).astype(BF16)
            v2p = jnp.pad(rwkv_v2[l - 1], ((0, LANES - rwkv_v2.shape[1]), (0, 0))).astype(BF16)
            o_rwkv = _rwkv(rw, wz, vecs, h, v_first, v1p, v2p)
        a2p = jnp.pad(gla_a2[l], ((0, LANES - gla_a2.shape[1]), (0, 0))).astype(BF16)
        o_gla = _gla(gl, a2p, gla_ab[l][None], jnp.tile(gla_norm[l], 4)[None])
        wb_nsa = w_branch[l, 0].reshape(4, HEAD_DIM, D)[jnp.array([0, 2, 1, 3])].reshape(BRANCH_WIDTH, D)
        wb = jnp.concatenate([wb_nsa[None], w_branch[l, 1:]], axis=0).astype(BF16)
        x = _merge(x, h, o_nsa, o_pool, o_rwkv, o_gla, w_gate[l].astype(BF16), b_gate[l], wb, w_out[l].astype(BF16))
        x = _cross(x, norm_ca[l][None], w_ca_q[l].astype(BF16), memkv, l, w_ca_o[l].astype(BF16))
        j = l // 2
        x2 = x.reshape(B * S, D)
        if l % 2 == 0:
            x2 = _ffn(x2, norm_ffn[l][None], ffn_w_gate[j].astype(BF16), ffn_w_up[j].astype(BF16),
                      ffn_w_down[j].astype(BF16))
        else:
            router_p = jnp.pad(moe_router[j], ((0, 0), (0, LANES - N_EXPERTS)))
            x2 = _moe_routed(x2, norm_ffn[l][None], router_p, moe_w_gate[j].astype(BF16), moe_w_up[j].astype(BF16),
                             moe_w_down[j].astype(BF16), norm_final[None])
        x = x2.reshape(B, S, D)
    return x
```

```python
import functools

import jax
import jax.numpy as jnp
import numpy as np
from jax import lax
from jax.experimental import pallas as pl
from jax.experimental.pallas import tpu as pltpu

F32 = jnp.float32
BF16 = jnp.bfloat16

D_MODEL = 1024
HEAD_DIM = 64
BRANCH_WIDTH = 256
ROPE_THETA = 500000.0
ROPE_DIM = 16
NORM_EPS = 1e-6
NEG_INF = -1e30

NSA_KV = 128
CMP_LEN = 32
CMP_STRIDE = 16
CMP_HIDDEN = 128
SEL_BLOCK = 64
SEL_TOPN = 8
WINDOW = 512

RWKV_IN = 896
RWKV_GN_EPS = 64e-5
RWKV_CHUNK = 64

GLA_DK = 32
GLA_TAU = 16.0
GLA_CHUNK = 64

CA_WIDTH = 256
D_FF = 2816
N_EXPERTS = 8

LANES = 128
VMEM_LIMIT = 56 * 1024 * 1024

_C_Q = 0
_C_KV = 256
_C_CV = 768
_C_GATE = 1024
_C_POOL = 1152
_C_RWKV = 1408
_C_GLA = 2304
_C_END = 3200


def _cparams(sem):
    return pltpu.CompilerParams(dimension_semantics=sem, vmem_limit_bytes=VMEM_LIMIT)


def _rms(x, g):
    ms = jnp.mean(x * x, axis=-1, keepdims=True)
    return x * lax.rsqrt(ms + NORM_EPS) * g


def _rope128(t, cos, sin):
    lane = lax.broadcasted_iota(jnp.int32, t.shape, 1) % HEAD_DIM
    partner = jnp.where(lane < ROPE_DIM // 2, pltpu.roll(t, LANES - ROPE_DIM // 2, 1),
                        pltpu.roll(t, ROPE_DIM // 2, 1))
    return t * cos + partner * sin


def _inproj_kernel(x_ref, g_ref, w_ref, cos_ref, sin_ref, mu_ref,
                   h_ref, qt_ref, k_ref, vt_ref, cv_ref, gt_ref, u_ref, rw_ref, gl_ref, prev_ref):
    s = pl.program_id(1)
    hb = _rms(x_ref[0], g_ref[...]).astype(BF16)
    h_ref[0] = hb

    def mm(a, b):
        return jnp.dot(hb, w_ref[:, a:b], preferred_element_type=F32)

    cos = cos_ref[0]
    sin = sin_ref[0]
    scale = HEAD_DIM ** -0.5
    qt_ref[0, 0:128, :] = jnp.transpose(_rope128(mm(_C_Q, _C_Q + 128), cos, sin) * scale).astype(BF16)
    qt_ref[0, 128:256, :] = jnp.transpose(_rope128(mm(_C_Q + 128, _C_Q + 256), cos, sin) * scale).astype(BF16)
    k_ref[0, :, 0:128] = _rope128(mm(_C_KV, _C_KV + 128), cos, sin).astype(BF16)
    vt_ref[0, 0:128, :] = jnp.transpose(mm(_C_KV + 128, _C_KV + 256)).astype(BF16)
    k_ref[0, :, 128:256] = _rope128(mm(_C_KV + 256, _C_KV + 384), cos, sin).astype(BF16)
    vt_ref[0, 128:256, :] = jnp.transpose(mm(_C_KV + 384, _C_KV + 512)).astype(BF16)
    cv_ref[0] = mm(_C_CV, _C_CV + 256).astype(BF16)
    gt_ref[0] = jnp.transpose(jax.nn.sigmoid(mm(_C_GATE, _C_GATE + 128)))
    u_ref[0] = mm(_C_POOL, _C_POOL + 256)
    gl_ref[0] = mm(_C_GLA, _C_END)

    p = mm(_C_RWKV, _C_RWKV + RWKV_IN)
    tm = p.shape[0]

    @pl.when(s == 0)
    def _():
        prev_ref[...] = jnp.zeros_like(prev_ref)

    row = lax.broadcasted_iota(jnp.int32, p.shape, 0)
    shifted = jnp.where(row == 0, prev_ref[0:1, :], pltpu.roll(p, 1, 0))
    prev_ref[0:1, :] = p[tm - 1:tm, :]
    rw_ref[0] = p + (shifted - p) * mu_ref[...]


def _inproj(x, g, w_all, cosf, sinf, mu, tm=256):
    B, S, D = x.shape
    tok = lambda w: pl.BlockSpec((1, tm, w), lambda b, s: (b, s, 0))
    tok_t = lambda w: pl.BlockSpec((1, w, tm), lambda b, s: (b, 0, s))
    full = lambda a: pl.BlockSpec(a.shape, lambda b, s: (0,) * a.ndim)
    outs = [(D, BF16, True), (256, BF16, False), (256, BF16, True), (256, BF16, False), (256, BF16, True),
            (128, F32, False), (256, F32, True), (RWKV_IN, F32, True), (896, F32, True)]
    return pl.pallas_call(
        _inproj_kernel,
        grid=(B, S // tm),
        in_specs=[tok(D), full(g), full(w_all), tok(128), tok(128), full(mu)],
        out_specs=[tok(w) if major else tok_t(w) for w, _, major in outs],
        out_shape=[jax.ShapeDtypeStruct((B, S, w) if major else (B, w, S), dt) for w, dt, major in outs],
        scratch_shapes=[pltpu.VMEM((8, RWKV_IN), F32)],
        compiler_params=_cparams(("parallel", "arbitrary")),
        name="inproj",
    )(x, g, w_all, cosf, sinf, mu)


def _inproj_weights(w_in, l):
    w = w_in[l]
    nsa, pool, rwkv, gla = jnp.split(w, [1036, 1036 + 256, 1036 + 256 + 896], axis=1)
    q = nsa[:, 0:256].reshape(D_MODEL, 4, HEAD_DIM)
    qa = q[:, (0, 2), :].reshape(D_MODEL, 128)
    qb = q[:, (1, 3), :].reshape(D_MODEL, 128)
    kc, vc, ks, vs, kw, vw = (nsa[:, 256 + 128 * i:384 + 128 * i] for i in range(6))
    gates = jnp.pad(nsa[:, 1024:1036], ((0, 0), (0, 116)))
    gq, gk, gv, gad, gog = jnp.split(gla, [128, 256, 512, 528], axis=1)
    gla_r = jnp.concatenate([gq, gk, gv, gog, jnp.pad(gad, ((0, 0), (0, 112)))], axis=1)
    w_all = jnp.concatenate([qa, qb, ks, vs, kw, vw, kc, vc, gates, pool, rwkv, gla_r], axis=1)
    return w_all.astype(BF16)


def _rope_tables(positions):
    half = ROPE_DIM // 2
    inv_freq = ROPE_THETA ** (-jnp.arange(0, ROPE_DIM, 2, dtype=F32) / ROPE_DIM)
    ang = positions.astype(F32)[..., None] * inv_freq
    cos, sin = jnp.cos(ang), jnp.sin(ang)
    ones = jnp.ones(cos.shape[:-1] + (HEAD_DIM - ROPE_DIM,), F32)
    cos64 = jnp.concatenate([cos, cos, ones], axis=-1)
    sin64 = jnp.concatenate([-sin, sin, 0.0 * ones], axis=-1)
    return jnp.concatenate([cos64, cos64], axis=-1), jnp.concatenate([sin64, sin64], axis=-1)


def _gelu_tanh(x):
    return x * (0.5 * (1.0 + jnp.tanh(np.sqrt(2.0 / np.pi) * (x + 0.044715 * (x * x * x)))))


def _dot_nt(a, b):
    return lax.dot_general(a, b, (((1,), (1,)), ((), ())), preferred_element_type=F32)


def _compress_kernel(x_ref, wa_ref, wb_ref, bias_ref, w2k_ref, w2v_ref, cos_ref, sin_ref, kc_ref, vc_ref):
    x = x_ref[0]
    a = jnp.dot(x, wa_ref[...], preferred_element_type=F32)
    b = jnp.dot(x, wb_ref[...], preferred_element_type=F32)
    n = a.shape[0]
    hid = _gelu_tanh(a + pltpu.roll(b, n - 1, 0) + bias_ref[...])
    kc = jnp.dot(hid[:, 0:256].astype(BF16), w2k_ref[...], preferred_element_type=F32)
    vc = jnp.dot(hid[:, 256:512].astype(BF16), w2v_ref[...], preferred_element_type=F32)
    kc_ref[0] = _rope128(kc, cos_ref[0], sin_ref[0]).astype(BF16)
    vc_ref[0] = jnp.transpose(vc).astype(BF16)


def _nsa_compress(cv, wa, wb, bias, w2k, w2v, cosc, sinc):
    B, S, _ = cv.shape
    nseg = S // CMP_STRIDE
    x = cv.reshape(B, nseg, CMP_STRIDE * 256)
    per_b = lambda r, w: pl.BlockSpec((1, r, w), lambda b: (b, 0, 0))
    full = lambda a: pl.BlockSpec(a.shape, lambda b: (0,) * a.ndim)
    return pl.pallas_call(
        _compress_kernel,
        grid=(B,),
        in_specs=[per_b(nseg, CMP_STRIDE * 256), full(wa), full(wb), full(bias), full(w2k), full(w2v),
                  per_b(nseg, 128), per_b(nseg, 128)],
        out_specs=[per_b(nseg, 128), per_b(nseg, 128)],
        out_shape=[jax.ShapeDtypeStruct((B, nseg, 128), BF16)] * 2,
        compiler_params=_cparams(("parallel",)),
        name="nsa_compress",
    )(x, wa, wb, bias, w2k, w2v, cosc, sinc)


def _compress_weights(pe, w1, w2):
    eye2 = jnp.eye(2, dtype=F32)
    half = lambda lo: jnp.einsum('jldf,pj,qg->lpqdjgf', w1[:, lo:lo + CMP_STRIDE], eye2, eye2).reshape(
        CMP_STRIDE * 256, 4 * CMP_HIDDEN).astype(BF16)
    bias = jnp.einsum('jld,jldf->jf', pe, w1)
    bias = jnp.broadcast_to(bias[:, None, :], (2, 2, CMP_HIDDEN)).reshape(1, 4 * CMP_HIDDEN)
    bd = lambda w: jnp.einsum('fd,gq->gfqd', w, eye2).reshape(2 * CMP_HIDDEN, 2 * HEAD_DIM).astype(BF16)
    return half(0), half(CMP_STRIDE), bias, bd(w2[0]), bd(w2[1])


def _cover_t(S):
    n_cmp = (S - CMP_LEN) // CMP_STRIDE + 1
    cmp_start = np.arange(n_cmp) * CMP_STRIDE
    slc_start = np.arange(S // SEL_BLOCK) * SEL_BLOCK
    cover = np.clip(np.minimum(cmp_start[:, None] + CMP_LEN, slc_start[None, :] + SEL_BLOCK)
                    - np.maximum(cmp_start[:, None], slc_start[None, :]), 0, None) / CMP_LEN
    out = np.zeros((S // SEL_BLOCK, S // CMP_STRIDE), np.float32)
    out[:, :n_cmp] = cover.T
    return jnp.asarray(out, BF16)


MASKED = 2.0 * NEG_INF


def _softmax_step(s, carry, v_t):
    m, l, acc = carry
    m_new = jnp.maximum(m, jnp.max(s, axis=0, keepdims=True))
    alpha = jnp.exp(m - m_new)
    p = jnp.exp(s - m_new)
    l = alpha * l + jnp.sum(p, axis=0, keepdims=True)
    acc = alpha * acc + jnp.dot(v_t, p.astype(BF16), preferred_element_type=F32)
    return m_new, l, acc


def _softmax_init(n_q):
    return (jnp.full((1, n_q), NEG_INF, F32), jnp.zeros((1, n_q), F32), jnp.zeros((LANES, n_q), F32))


def _nsa_kernel(qt_ref, k_ref, vt_ref, kc_ref, vct_ref, gt_ref, covt_ref, o_ref, *, tq, tk):
    i = pl.program_id(1)
    t0 = i * tq
    n_blk = covt_ref.shape[0]
    n_q = 4 * tq
    qa_t, qb_t = qt_ref[0, 0:128, :], qt_ref[0, 128:256, :]
    sub_grp = lax.broadcasted_iota(jnp.int32, (LANES, tq), 0) // HEAD_DIM
    gates_t = gt_ref[0]
    tok = t0 + (lax.broadcasted_iota(jnp.int32, (1, n_q), 1) & (tq - 1))
    q_all = jnp.concatenate([jnp.where(sub_grp == g, q_t, 0) for g in range(2) for q_t in (qa_t, qb_t)], axis=1)

    s = jnp.dot(kc_ref[0], q_all, preferred_element_type=F32)
    nrow = lax.broadcasted_iota(jnp.int32, s.shape, 0)
    s = jnp.where(nrow * CMP_STRIDE + (CMP_LEN - 1) <= tok, s, MASKED)
    e = jnp.exp(s - jnp.maximum(jnp.max(s, axis=0, keepdims=True), NEG_INF))
    den = jnp.sum(e, axis=0, keepdims=True)
    p = e / jnp.where(den > 0.0, den, 1.0)
    o_cmp = jnp.dot(vct_ref[0], p.astype(BF16), preferred_element_type=F32)

    p_grp = jnp.concatenate([p[:, 2 * g * tq:(2 * g + 1) * tq] + p[:, (2 * g + 1) * tq:(2 * g + 2) * tq]
                             for g in range(2)], axis=1)
    imp = _dot_split_rhs(covt_ref[...], p_grp, 2)
    blk = lax.broadcasted_iota(jnp.int32, imp.shape, 0)
    tok_b = t0 + (lax.broadcasted_iota(jnp.int32, imp.shape, 1) & (tq - 1))
    cur = tok_b // SEL_BLOCK
    forced = (blk == 0) | (blk == cur) | (blk == cur - 1)
    imp = jnp.where(forced, 1e30, jnp.where(blk * SEL_BLOCK <= tok_b, imp, -1e30))
    cnt = jnp.zeros(imp.shape, F32)
    for i2 in range(n_blk):
        row = imp[i2:i2 + 1, :]
        cnt = cnt + jnp.where((row > imp) | ((row == imp) & (blk > i2)), 1.0, 0.0)
    bias = jnp.where(cnt < float(SEL_TOPN), 0.0, MASKED).astype(BF16)
    sel_bias = jnp.concatenate([bias[:, g * tq:(g + 1) * tq] for g in (0, 0, 1, 1)], axis=1)

    def sel_step(j, carry):
        rows = pl.ds(pl.multiple_of(j * tk, tk), tk)
        key = j * tk + lax.broadcasted_iota(jnp.int32, (tk, n_q), 0)
        blk_of_key = (j * tk + lax.broadcasted_iota(jnp.int32, (tk, n_blk), 0)) // SEL_BLOCK
        expand = (lax.broadcasted_iota(jnp.int32, (tk, n_blk), 1) == blk_of_key).astype(BF16)
        s = jnp.dot(k_ref[0, rows, 0:128], q_all, preferred_element_type=F32)
        s = s + jnp.dot(expand, sel_bias, preferred_element_type=F32)
        s = jnp.where(key <= tok, s, MASKED)
        return _softmax_step(s, carry, vt_ref[0, 0:128, rows])

    _, l_sel, acc_sel = lax.fori_loop(0, (t0 + tq + tk - 1) // tk, sel_step, _softmax_init(n_q))
    o_sel = acc_sel / l_sel

    span = WINDOW + tq
    start = pl.multiple_of(jnp.maximum(t0 - WINDOW, 0), tq)
    key = start + lax.broadcasted_iota(jnp.int32, (span, n_q), 0)
    s = jnp.dot(k_ref[0, pl.ds(start, span), 128:256], q_all, preferred_element_type=F32)
    s = jnp.where((key <= tok) & (tok - key < WINDOW), s, MASKED)
    e = jnp.exp(s - jnp.max(s, axis=0, keepdims=True))
    o_win = (jnp.dot(vt_ref[0, 128:256, pl.ds(start, span)], e.astype(BF16), preferred_element_type=F32)
             / jnp.sum(e, axis=0, keepdims=True))

    out = [jnp.zeros((LANES, tq), F32), jnp.zeros((LANES, tq), F32)]
    for g in range(2):
        for r in range(2):
            cols = slice((2 * g + r) * tq, (2 * g + r + 1) * tq)
            c = g * 6 + r * 3
            o = (gates_t[c:c + 1, :] * o_cmp[:, cols] + gates_t[c + 1:c + 2, :] * o_sel[:, cols]
                 + gates_t[c + 2:c + 3, :] * o_win[:, cols])
            out[r] = jnp.where(sub_grp == g, o, out[r])
    o_ref[0, :, 0:128] = jnp.transpose(out[0]).astype(BF16)
    o_ref[0, :, 128:256] = jnp.transpose(out[1]).astype(BF16)


def _nsa_attend(q_t, k2, v_t, kcmp, vcmp_t, gates_t, tq=128, tk=512):
    B, S, _ = k2.shape
    covt = _cover_t(S)
    tile_t = lambda w: pl.BlockSpec((1, w, tq), lambda b, i: (b, 0, i))
    per_b = lambda r, w: pl.BlockSpec((1, r, w), lambda b, i: (b, 0, 0))
    return pl.pallas_call(
        functools.partial(_nsa_kernel, tq=tq, tk=tk),
        grid=(B, S // tq),
        in_specs=[tile_t(256), per_b(S, 256), per_b(256, S), per_b(S // CMP_STRIDE, 128), per_b(128, S // CMP_STRIDE),
                  tile_t(128), pl.BlockSpec(covt.shape, lambda b, i: (0, 0))],
        out_specs=pl.BlockSpec((1, tq, 256), lambda b, i: (b, i, 0)),
        out_shape=jax.ShapeDtypeStruct((B, S, 256), BF16),
        compiler_params=_cparams(("parallel", "arbitrary")),
        name="nsa_attend",
    )(q_t, k2, v_t, kcmp, vcmp_t, gates_t, covt)


def _pool_kernel(u_ref, w_ref, scale_ref, o_ref):
    u = u_ref[0]
    row = lax.broadcasted_iota(jnp.int32, u.shape, 0)
    grp = lax.broadcasted_iota(jnp.int32, u.shape, 1) // HEAD_DIM

    def back(x, k):
        return jnp.where(row >= k, pltpu.roll(x, k, 0), 0.0)

    s2 = u + back(u, 1)
    s4 = s2 + back(s2, 2)
    s8 = s4 + back(s4, 4)
    s16 = s8 + back(s8, 8)
    total = jnp.where(grp == 0, s2, jnp.where(grp == 1, s4, jnp.where(grp == 2, s8, s16)))
    win = jnp.where(grp == 0, 2, jnp.where(grp == 1, 4, jnp.where(grp == 2, 8, 16)))
    count = jnp.minimum(row + 1, win).astype(F32)
    d = total / count - u
    y = jnp.dot(d.astype(BF16), w_ref[...], preferred_element_type=F32)
    o_ref[0] = (y * scale_ref[...]).astype(BF16)


def _pool(u, w_bd, scale):
    B, S, W = u.shape
    per_b = pl.BlockSpec((1, S, W), lambda b: (b, 0, 0))
    full = lambda a: pl.BlockSpec(a.shape, lambda b: (0,) * a.ndim)
    return pl.pallas_call(
        _pool_kernel,
        grid=(B,),
        in_specs=[per_b, full(w_bd), full(scale)],
        out_specs=per_b,
        out_shape=jax.ShapeDtypeStruct((B, S, W), BF16),
        compiler_params=_cparams(("parallel",)),
        name="pool",
    )(u, w_bd, scale)


def _block_diag(w):
    G, a, b = w.shape
    return jnp.einsum('gab,gh->gahb', w, jnp.eye(G, dtype=w.dtype)).reshape(G * a, G * b)


def _dot_split_lhs(a, b_exact, terms):
    out, rem = None, a
    for i in range(terms):
        hi = rem.astype(BF16)
        d = jnp.dot(hi, b_exact, preferred_element_type=F32)
        out = d if out is None else out + d
        if i + 1 < terms:
            rem = rem - hi.astype(F32)
    return out


def _dot_split_rhs(a_exact, b, terms):
    out, rem = None, b
    for i in range(terms):
        hi = rem.astype(BF16)
        d = jnp.dot(a_exact, hi, preferred_element_type=F32)
        out = d if out is None else out + d
        if i + 1 < terms:
            rem = rem - hi.astype(F32)
    return out


def _dot_tn(a, b):
    return lax.dot_general(a, b, (((0,), (0,)), ((), ())), preferred_element_type=F32)


def _stack_heads(x, width, n_heads=4):
    head = lax.broadcasted_iota(jnp.int32, x.shape, 1) // width
    return jnp.concatenate([jnp.where(head == h, x, 0.0) for h in range(n_heads)], axis=0)


def _unstack_heads(x, C, n_heads=4):
    out = x[0:C]
    for h in range(1, n_heads):
        out = out + x[h * C:(h + 1) * C]
    return out


def _block_masks(C, n_heads=4):
    n = C * n_heads
    row = lax.broadcasted_iota(jnp.int32, (n, n), 0)
    col = lax.broadcasted_iota(jnp.int32, (n, n), 1)
    same = (row // C) == (col // C)
    return same & (row > col), same & (row >= col)


def _tri_ones(C):
    return jnp.asarray(np.tril(np.ones((C, C), np.float32)), BF16)


def _head_ones(width, n_heads=4):
    return jnp.asarray(np.kron(np.eye(n_heads, dtype=np.float32), np.ones((width, width), np.float32)), BF16)


def _gla_kernel(gl_ref, a2_ref, ab_ref, norm_ref, tri_ref, ones_ref, o_ref, state_ref, *, C):
    s = pl.program_id(1)

    @pl.when(s == 0)
    def _():
        state_ref[...] = jnp.zeros_like(state_ref)

    gl = gl_ref[0]
    tm = gl.shape[0]
    q = gl[:, 0:128] * (GLA_DK ** -0.5)
    k = gl[:, 128:256]
    v = gl[:, 256:512]
    og = gl[:, 512:768]
    ad = gl[:, 768:896]
    x = jnp.dot(ad.astype(BF16), a2_ref[...], preferred_element_type=F32) + ab_ref[...]
    log_a = jax.nn.log_sigmoid(x) * (1.0 / GLA_TAU)
    strict, incl = _block_masks(C)
    kv_mask = (lax.broadcasted_iota(jnp.int32, (256, 128), 0) // HEAD_DIM
               == lax.broadcasted_iota(jnp.int32, (256, 128), 1) // GLA_DK)
    outs = []
    for c in range(tm // C):
        rc = slice(c * C, (c + 1) * C)
        bcum = _dot_split_rhs(tri_ref[...], log_a[rc], 3)
        mid = bcum[C // 2 - 1:C // 2, :]
        last = bcum[C - 1:C, :]
        q_in = q[rc] * jnp.exp(bcum)
        q_m = _stack_heads(q[rc] * jnp.exp(bcum - mid), GLA_DK).astype(BF16)
        k_m = _stack_heads(k[rc] * jnp.exp(mid - bcum), GLA_DK).astype(BF16)
        k_end = k[rc] * jnp.exp(last - bcum)
        v_c = v[rc]
        att = jnp.where(incl, _dot_nt(q_m, k_m), 0.0)
        v_st = _stack_heads(v_c, HEAD_DIM).astype(BF16)
        o_intra = _unstack_heads(jnp.dot(att.astype(BF16), v_st, preferred_element_type=F32), C)
        state = state_ref[...]
        o_inter = _dot_nt(q_in.astype(BF16), state.astype(BF16))
        upd = _dot_tn(v_c.astype(BF16), k_end.astype(BF16))
        state_ref[...] = state * jnp.exp(last) + jnp.where(kv_mask, upd, 0.0)
        outs.append(o_intra + o_inter)
    o = jnp.concatenate(outs, axis=0)
    ms = _dot_split_lhs(o * o, ones_ref[...], 2) * (1.0 / HEAD_DIM)
    o = o * lax.rsqrt(ms + NORM_EPS) * norm_ref[...]
    o_ref[0] = (o * (og * jax.nn.sigmoid(og))).astype(BF16)


def _gla(gl, a2p, ab, norm4, tm=256):
    B, S, W = gl.shape
    C = GLA_CHUNK
    tri, ones = _tri_ones(C), _head_ones(HEAD_DIM)
    tile = lambda w: pl.BlockSpec((1, tm, w), lambda b, s: (b, s, 0))
    full = lambda a: pl.BlockSpec(a.shape, lambda b, s: (0,) * a.ndim)
    return pl.pallas_call(
        functools.partial(_gla_kernel, C=C),
        grid=(B, S // tm),
        in_specs=[tile(W), full(a2p), full(ab), full(norm4), full(tri), full(ones)],
        out_specs=tile(256),
        out_shape=jax.ShapeDtypeStruct((B, S, 256), BF16),
        scratch_shapes=[pltpu.VMEM((256, 128), F32)],
        compiler_params=_cparams(("parallel", "arbitrary")),
        name="gla",
    )(gl, a2p, ab, norm4, tri, ones)


_RV_W0, _RV_A0, _RV_KK, _RV_KA, _RV_RK, _RV_LN, _RV_V0 = range(7)


def _rwkv_kernel(*refs, has_vres, C):
    if has_vres:
        (rw_ref, h_ref, vf_ref, wz_ref, vec_ref, v1_ref, v2_ref, tri_ref, ones_ref, o_ref, state_ref) = refs
    else:
        (rw_ref, wz_ref, vec_ref, tri_ref, ones_ref, o_ref, vout_ref, state_ref) = refs
    s = pl.program_id(1)

    @pl.when(s == 0)
    def _():
        state_ref[...] = jnp.zeros_like(state_ref)

    rw = rw_ref[0]
    tm = rw.shape[0]
    r, k, v, z = rw[:, 0:256], rw[:, 256:512], rw[:, 512:768], rw[:, 768:896]
    vec = lambda i: vec_ref[i:i + 1, :]
    zl = lax.broadcasted_iota(jnp.int32, z.shape, 1)
    zf = jnp.where(zl < 32, jnp.tanh(z), jnp.where(zl < 64, z, jax.nn.sigmoid(z)))
    zz = jnp.dot(zf.astype(BF16), wz_ref[...], preferred_element_type=F32)
    w_log = -jax.nn.softplus(-(vec(_RV_W0) + zz[:, 0:256])) - 0.5
    lw = -jnp.exp(w_log)
    a_sig = jax.nn.sigmoid(vec(_RV_A0) + zz[:, 256:512])
    gate = zz[:, 512:768]
    if has_vres:
        low = jnp.dot(h_ref[0], v1_ref[...], preferred_element_type=F32)
        logit = vec(_RV_V0) + jnp.dot(low.astype(BF16), v2_ref[...], preferred_element_type=F32)
        v = v + (vf_ref[0] - v) * jax.nn.sigmoid(logit)
    else:
        vout_ref[0] = v
    kk = k * vec(_RV_KK)
    norm = jnp.sqrt(_dot_split_lhs(kk * kk, ones_ref[...], 2))
    kk = kk / jnp.maximum(norm, 1e-12)
    k = k * (1.0 + (a_sig - 1.0) * vec(_RV_KA))
    a_vec = -kk
    b_vec = kk * a_sig

    strict, incl = _block_masks(C)
    st = lambda t: _stack_heads(t, HEAD_DIM)
    bdot = lambda p, q: jnp.dot(p.astype(BF16), q.astype(BF16), preferred_element_type=F32)
    state_mask = (lax.broadcasted_iota(jnp.int32, (256, 256), 0) // HEAD_DIM
                  == lax.broadcasted_iota(jnp.int32, (256, 256), 1) // HEAD_DIM)
    n4 = 4 * C
    chunks = []
    for c in range(tm // C):
        rc = slice(c * C, (c + 1) * C)
        lcum = _dot_split_rhs(tri_ref[...], lw[rc], 3)
        lex = lcum - lw[rc]
        mid = lcum[C // 2 - 1:C // 2, :]
        last = lcum[C - 1:C, :]
        e_mid = jnp.exp(mid - lcum)
        e_end = jnp.exp(last - lcum)
        left = jnp.concatenate([st(a_vec[rc] * jnp.exp(lex - mid)), st(r[rc] * jnp.exp(lcum - mid))], axis=0)
        right = jnp.concatenate([st(b_vec[rc] * e_mid), st(k[rc] * e_mid)], axis=0)
        g = _dot_nt(left.astype(BF16), right.astype(BF16))
        v_st = st(v[rc])
        n_ak = jnp.where(strict, g[0:n4, n4:2 * n4], 0.0)
        chunks.append(dict(
            p=jnp.where(strict, g[0:n4, 0:n4], 0.0),
            x=jnp.concatenate([st(a_vec[rc] * jnp.exp(lex)), bdot(n_ak, v_st)], axis=1),
            n_rb=jnp.where(incl, g[n4:2 * n4, 0:n4], 0.0),
            y_rk=bdot(jnp.where(incl, g[n4:2 * n4, n4:2 * n4], 0.0), v_st),
            r_abs=r[rc] * jnp.exp(lcum), decay=jnp.exp(last), v=v[rc],
            bk_end=jnp.concatenate([b_vec[rc] * e_end, k[rc] * e_end], axis=0).astype(BF16)))
    eye = (lax.broadcasted_iota(jnp.int32, (n4, n4), 0) == lax.broadcasted_iota(jnp.int32, (n4, n4), 1)).astype(F32)
    for ch in chunks:
        ch['t'] = eye + ch['p']
    for f in range(1, int(np.log2(C))):
        for ch in chunks:
            ch['p'] = bdot(ch['p'], ch['p'])
        for ch in chunks:
            ch['t'] = ch['t'] + bdot(ch['p'], ch['t'])
    for ch in chunks:
        ch['x'] = bdot(ch['t'], ch['x'])
        y_part = bdot(ch['n_rb'], ch['x'])
        a_eff = _unstack_heads(ch['x'][:, 0:256], C)
        u0 = _unstack_heads(ch['x'][:, 256:512], C)
        ch['r_eff'] = (ch['r_abs'] + _unstack_heads(y_part[:, 0:256], C)).astype(BF16)
        ch['y0'] = _unstack_heads(y_part[:, 256:512] + ch['y_rk'], C)
        ch['mix'] = jnp.where(state_mask, _dot_tn(a_eff.astype(BF16), ch['bk_end'][0:C]), 0.0).astype(BF16)
        ch['add'] = jnp.where(state_mask, _dot_tn(jnp.concatenate([u0, ch['v']], axis=0).astype(BF16), ch['bk_end']), 0.0)
    ys = []
    state = state_ref[...]
    for ch in chunks:
        sb = state.astype(BF16)
        ys.append(_dot_nt(ch['r_eff'], sb) + ch['y0'])
        state = state * ch['decay'] + jnp.dot(sb, ch['mix'], preferred_element_type=F32) + ch['add']
    state_ref[...] = state
    y = jnp.concatenate(ys, axis=0)
    inv = 1.0 / HEAD_DIM
    mean = _dot_split_lhs(y, ones_ref[...], 2) * inv
    yc = y - mean
    var = _dot_split_lhs(yc * yc, ones_ref[...], 2) * inv
    y = yc * lax.rsqrt(var + RWKV_GN_EPS) * vec(_RV_LN)
    y = y + _dot_split_lhs(r * k * vec(_RV_RK), ones_ref[...], 2) * v
    o_ref[0] = (y * gate).astype(BF16)


def _rwkv(rw, wz, vecs, h=None, v_first=None, v1p=None, v2p=None, tm=256):
    B, S, W = rw.shape
    C = RWKV_CHUNK
    has_vres = h is not None
    tri, ones = _tri_ones(C), _head_ones(HEAD_DIM)
    tile = lambda w: pl.BlockSpec((1, tm, w), lambda b, s: (b, s, 0))
    full = lambda a: pl.BlockSpec(a.shape, lambda b, s: (0,) * a.ndim)
    if has_vres:
        args = (rw, h, v_first, wz, vecs, v1p, v2p, tri, ones)
        in_specs = [tile(W), tile(D_MODEL), tile(256), full(wz), full(vecs), full(v1p), full(v2p), full(tri), full(ones)]
        out_specs = tile(256)
        out_shape = jax.ShapeDtypeStruct((B, S, 256), BF16)
    else:
        args = (rw, wz, vecs, tri, ones)
        in_specs = [tile(W), full(wz), full(vecs), full(tri), full(ones)]
        out_specs = [tile(256), tile(256)]
        out_shape = [jax.ShapeDtypeStruct((B, S, 256), BF16), jax.ShapeDtypeStruct((B, S, 256), F32)]
    return pl.pallas_call(
        functools.partial(_rwkv_kernel, has_vres=has_vres, C=C),
        grid=(B, S // tm),
        in_specs=in_specs,
        out_specs=out_specs,
        out_shape=out_shape,
        scratch_shapes=[pltpu.VMEM((256, 256), F32)],
        compiler_params=_cparams(("parallel", "arbitrary")),
        name="rwkv_vres" if has_vres else "rwkv",
    )(*args)


def _rwkv_weights(l, rwkv_w0, rwkv_w2, rwkv_a0, rwkv_a2, rwkv_g2, rwkv_v0, rwkv_k_k, rwkv_k_a, rwkv_r_k, rwkv_ln):
    wz = jnp.zeros((128, 768), F32)
    wz = wz.at[0:32, 0:256].set(rwkv_w2[l]).at[32:64, 256:512].set(rwkv_a2[l]).at[64:128, 512:768].set(rwkv_g2[l])
    v0 = rwkv_v0[l - 1] if l > 0 else jnp.zeros((256,), F32)
    vecs = jnp.stack([rwkv_w0[l], rwkv_a0[l], rwkv_k_k[l], rwkv_k_a[l], rwkv_r_k[l].reshape(-1), rwkv_ln[l], v0,
                      jnp.zeros((256,), F32)])
    return wz.astype(BF16), vecs


def _merge_kernel(x_ref, h_ref, o0_ref, o1_ref, o2_ref, o3_ref, wg_ref, bg_ref, wb_ref, wo_ref, out_ref):
    h = h_ref[0]
    merged = None
    for i, o_ref in enumerate((o0_ref, o1_ref, o2_ref, o3_ref)):
        gate = jax.nn.sigmoid(jnp.dot(h, wg_ref[i], preferred_element_type=F32) + bg_ref[i:i + 1, :])
        term = gate * jnp.dot(o_ref[0], wb_ref[i], preferred_element_type=F32)
        merged = term if merged is None else merged + term
    out_ref[0] = x_ref[0] + jnp.dot(merged.astype(BF16), wo_ref[...], preferred_element_type=F32)


def _merge(x, h, o_nsa, o_pool, o_rwkv, o_gla, wg, bg, wb, wo, tm=512):
    B, S, D = x.shape
    tile = lambda w: pl.BlockSpec((1, tm, w), lambda b, s: (b, s, 0))
    full = lambda a: pl.BlockSpec(a.shape, lambda b, s: (0,) * a.ndim)
    return pl.pallas_call(
        _merge_kernel,
        grid=(B, S // tm),
        in_specs=[tile(D), tile(D), tile(256), tile(256), tile(256), tile(256), full(wg), full(bg), full(wb), full(wo)],
        out_specs=tile(D),
        out_shape=jax.ShapeDtypeStruct((B, S, D), F32),
        compiler_params=_cparams(("parallel", "parallel")),
        name="merge",
    )(x, h, o_nsa, o_pool, o_rwkv, o_gla, wg, bg, wb, wo)


def _memkv_kernel(mem_ref, g_ref, w_ref, o_ref):
    mn = _rms(mem_ref[0], g_ref[...]).astype(BF16)
    o_ref[0] = jnp.dot(mn, w_ref[...], preferred_element_type=F32).astype(BF16)


def _memkv(mem, g, wkv_all):
    B, M, D = mem.shape
    N = wkv_all.shape[1]
    return pl.pallas_call(
        _memkv_kernel,
        grid=(B,),
        in_specs=[pl.BlockSpec((1, M, D), lambda b: (b, 0, 0)), pl.BlockSpec(g.shape, lambda b: (0, 0)),
                  pl.BlockSpec(wkv_all.shape, lambda b: (0, 0))],
        out_specs=pl.BlockSpec((1, M, N), lambda b: (b, 0, 0)),
        out_shape=jax.ShapeDtypeStruct((B, M, N), BF16),
        compiler_params=_cparams(("parallel",)),
        name="memkv",
    )(mem, g, wkv_all)


def _cross_kernel(x_ref, g_ref, wq_ref, k_ref, v_ref, wo_ref, out_ref):
    x = x_ref[0]
    hc = _rms(x, g_ref[...]).astype(BF16)
    q = jnp.dot(hc, wq_ref[...], preferred_element_type=F32) * (HEAD_DIM ** -0.5)
    head = lax.broadcasted_iota(jnp.int32, q.shape, 1) // HEAD_DIM
    k, v = k_ref[0], v_ref[0]
    o = jnp.zeros(q.shape, F32)
    for hh in range(CA_WIDTH // HEAD_DIM):
        qh = jnp.where(head == hh, q, 0.0).astype(BF16)
        s = _dot_nt(qh, k)
        e = jnp.exp(s - jnp.max(s, axis=1, keepdims=True))
        p = e / jnp.sum(e, axis=1, keepdims=True)
        o = jnp.where(head == hh, jnp.dot(p.astype(BF16), v, preferred_element_type=F32), o)
    out_ref[0] = x + jnp.dot(o.astype(BF16), wo_ref[...], preferred_element_type=F32)


def _cross(x, g, wq, memkv, l, wo, tm=512):
    B, S, D = x.shape
    M = memkv.shape[1]
    tile = pl.BlockSpec((1, tm, D), lambda b, s: (b, s, 0))
    full = lambda a: pl.BlockSpec(a.shape, lambda b, s: (0,) * a.ndim)
    kspec = pl.BlockSpec((1, M, CA_WIDTH), lambda b, s: (b, 0, 2 * l))
    vspec = pl.BlockSpec((1, M, CA_WIDTH), lambda b, s: (b, 0, 2 * l + 1))
    return pl.pallas_call(
        _cross_kernel,
        grid=(B, S // tm),
        in_specs=[tile, full(g), full(wq), kspec, vspec, full(wo)],
        out_specs=tile,
        out_shape=jax.ShapeDtypeStruct((B, S, D), F32),
        compiler_params=_cparams(("parallel", "parallel")),
        name="cross_attn",
    )(x, g, wq, memkv, memkv, wo)


def _ffn_kernel(x_ref, g_ref, wg_ref, wu_ref, wd_ref, out_ref, hb_ref, acc_ref):
    f = pl.program_id(1)

    @pl.when(f == 0)
    def _():
        hb_ref[...] = _rms(x_ref[...], g_ref[...]).astype(BF16)
        acc_ref[...] = jnp.zeros_like(acc_ref)

    hb = hb_ref[...]
    gate = jnp.dot(hb, wg_ref[...], preferred_element_type=F32)
    up = jnp.dot(hb, wu_ref[...], preferred_element_type=F32)
    act = gate * jax.nn.sigmoid(gate) * up
    acc_ref[...] += jnp.dot(act.astype(BF16), wd_ref[...], preferred_element_type=F32)

    @pl.when(f == pl.num_programs(1) - 1)
    def _():
        out_ref[...] = x_ref[...] + acc_ref[...]


def _ffn(x2, g, wg, wu, wd, tm=1024, tf=256):
    T, D = x2.shape
    F = wg.shape[1]
    return pl.pallas_call(
        _ffn_kernel,
        grid=(T // tm, F // tf),
        in_specs=[pl.BlockSpec((tm, D), lambda i, f: (i, 0)), pl.BlockSpec(g.shape, lambda i, f: (0, 0)),
                  pl.BlockSpec((D, tf), lambda i, f: (0, f)), pl.BlockSpec((D, tf), lambda i, f: (0, f)),
                  pl.BlockSpec((tf, D), lambda i, f: (f, 0))],
        out_specs=pl.BlockSpec((tm, D), lambda i, f: (i, 0)),
        out_shape=jax.ShapeDtypeStruct((T, D), F32),
        scratch_shapes=[pltpu.VMEM((tm, D), BF16), pltpu.VMEM((tm, D), F32)],
        compiler_params=_cparams(("parallel", "arbitrary")),
        name="ffn",
    )(x2, g, wg, wu, wd)


MOE_TM = 1024
MOE_TD = 512


def _route_kernel(x_ref, g_ref, r_ref, hf_ref, idx_ref, w_ref):
    hf = _rms(x_ref[...], g_ref[...])
    hf_ref[...] = hf
    logits = jnp.dot(hf, r_ref[...], preferred_element_type=F32, precision=lax.Precision.HIGHEST)
    lane = lax.broadcasted_iota(jnp.int32, logits.shape, 1)
    big = logits.shape[1]
    logits = jnp.where(lane < N_EXPERTS, logits, -jnp.inf)
    m1 = jnp.max(logits, axis=1, keepdims=True)
    i1 = jnp.min(jnp.where(logits == m1, lane, big), axis=1, keepdims=True)
    rest = jnp.where(lane == i1, -jnp.inf, logits)
    m2 = jnp.max(rest, axis=1, keepdims=True)
    i2 = jnp.min(jnp.where(rest == m2, lane, big), axis=1, keepdims=True)
    e2 = jnp.exp(m2 - m1)
    w1 = 1.0 / (1.0 + e2)
    idx_ref[...] = jnp.where(lane == 0, i1, jnp.where(lane == 1, i2, 0))
    w_ref[...] = jnp.where(lane == 0, w1, jnp.where(lane == 1, e2 * w1, 0.0))


def _moe_route(x2, g, router_p, tm=1024):
    T, D = x2.shape
    row = lambda w: pl.BlockSpec((tm, w), lambda i: (i, 0))
    full = lambda a: pl.BlockSpec(a.shape, lambda i: (0,) * a.ndim)
    return pl.pallas_call(
        _route_kernel,
        grid=(T // tm,),
        in_specs=[row(D), full(g), full(router_p)],
        out_specs=[row(D), row(LANES), row(LANES)],
        out_shape=[jax.ShapeDtypeStruct((T, D), F32), jax.ShapeDtypeStruct((T, LANES), jnp.int32),
                   jax.ShapeDtypeStruct((T, LANES), F32)],
        compiler_params=_cparams(("parallel",)),
        name="moe_route",
    )(x2, g, router_p)


def _moe_plan(expert_of_pair, tm):
    onehot = (expert_of_pair[:, None] == jnp.arange(N_EXPERTS, dtype=jnp.int32)[None, :]).astype(jnp.int32)
    csum = jnp.cumsum(onehot, axis=0)
    rank = jnp.sum((csum - 1) * onehot, axis=1)
    counts = csum[-1]
    padded = ((counts + tm - 1) // tm) * tm
    ends = jnp.cumsum(padded)
    dest = ((ends - padded)[expert_of_pair] + rank).astype(jnp.int32)
    n_rows = expert_of_pair.shape[0] + N_EXPERTS * tm
    tile_expert = jnp.sum((jnp.arange(n_rows // tm, dtype=jnp.int32)[:, None] * tm >= ends[None, :]).astype(jnp.int32), axis=1)
    token_of_row = jnp.zeros((n_rows,), jnp.int32).at[dest].set(jnp.arange(expert_of_pair.shape[0], dtype=jnp.int32) // 2)
    return (dest, token_of_row, jnp.minimum(tile_expert, N_EXPERTS - 1).astype(jnp.int32),
            (ends[-1] // tm).astype(jnp.int32))


DMA_UNROLL = 8


def _gather_rows(table_hbm, row_of, dst, n, sem, wait):
    def body(c, carry):
        for u in range(DMA_UNROLL):
            r = c * DMA_UNROLL + u
            copy = pltpu.make_async_copy(table_hbm.at[pl.ds(row_of(r), 1)], dst.at[pl.ds(r, 1)], sem)
            copy.wait() if wait else copy.start()
        return carry

    lax.fori_loop(0, n // DMA_UNROLL, body, 0)


def _dispatch_kernel(tok_ref, hf_hbm, xs_ref, sem):
    row_of = lambda r: tok_ref[0, 0, r]
    _gather_rows(hf_hbm, row_of, xs_ref, MOE_TD, sem, wait=False)
    _gather_rows(hf_hbm, row_of, xs_ref, MOE_TD, sem, wait=True)


def _moe_dispatch(hf, token_of_row):
    T, D = hf.shape
    n_rows = token_of_row.shape[0]
    steps = n_rows // MOE_TD
    return pl.pallas_call(
        _dispatch_kernel,
        grid=(steps,),
        in_specs=[pl.BlockSpec((1, 1, MOE_TD), lambda i: (i, 0, 0), memory_space=pltpu.SMEM),
                  pl.BlockSpec(memory_space=pl.ANY)],
        out_specs=pl.BlockSpec((MOE_TD, D), lambda i: (i, 0)),
        out_shape=jax.ShapeDtypeStruct((n_rows, D), F32),
        scratch_shapes=[pltpu.SemaphoreType.DMA(())],
        compiler_params=_cparams(("arbitrary",)),
        name="moe_dispatch",
    )(token_of_row.reshape(steps, 1, MOE_TD), hf)


def _experts_kernel(te_ref, nv_ref, xs_ref, wg_ref, wu_ref, wd_ref, ys_ref, xb_ref, acc_ref):
    i, f = pl.program_id(0), pl.program_id(1)
    live = i < nv_ref[0]

    @pl.when(f == 0)
    def _():
        xb_ref[...] = xs_ref[...].astype(BF16)
        acc_ref[...] = jnp.zeros_like(acc_ref)

    @pl.when(live)
    def _():
        xb = xb_ref[...]
        gate = jnp.dot(xb, wg_ref[0], preferred_element_type=F32)
        up = jnp.dot(xb, wu_ref[0], preferred_element_type=F32)
        act = gate * jax.nn.sigmoid(gate) * up
        acc_ref[...] += jnp.dot(act.astype(BF16), wd_ref[0], preferred_element_type=F32)

    @pl.when(f == pl.num_programs(1) - 1)
    def _():
        ys_ref[...] = acc_ref[...]


def _moe_experts(xs, tile_expert, n_live, wg, wu, wd, tf=256):
    R, D = xs.shape
    F = wg.shape[2]
    tm = MOE_TM
    grid_spec = pltpu.PrefetchScalarGridSpec(
        num_scalar_prefetch=2,
        grid=(R // tm, F // tf),
        in_specs=[pl.BlockSpec((tm, D), lambda i, f, te, nv: (i, 0)),
                  pl.BlockSpec((1, D, tf), lambda i, f, te, nv: (te[i], 0, f)),
                  pl.BlockSpec((1, D, tf), lambda i, f, te, nv: (te[i], 0, f)),
                  pl.BlockSpec((1, tf, D), lambda i, f, te, nv: (te[i], f, 0))],
        out_specs=pl.BlockSpec((tm, D), lambda i, f, te, nv: (i, 0)),
        scratch_shapes=[pltpu.VMEM((tm, D), BF16), pltpu.VMEM((tm, D), F32)],
    )
    return pl.pallas_call(
        _experts_kernel,
        grid_spec=grid_spec,
        out_shape=jax.ShapeDtypeStruct((R, D), F32),
        compiler_params=_cparams(("parallel", "arbitrary")),
        name="moe_experts",
    )(tile_expert, n_live, xs, wg, wu, wd)


def _combine_kernel(dest_ref, x_ref, w_ref, gf_ref, ys_hbm, out_ref, y0_buf, y1_buf, sem0, sem1):
    slots = ((lambda r: dest_ref[0, 0, r], y0_buf, sem0), (lambda r: dest_ref[0, 1, r], y1_buf, sem1))
    for row_of, buf, sem in slots:
        _gather_rows(ys_hbm, row_of, buf, MOE_TD, sem, wait=False)
    for row_of, buf, sem in slots:
        _gather_rows(ys_hbm, row_of, buf, MOE_TD, sem, wait=True)
    w = w_ref[...]
    y = x_ref[...] + w[:, 0:1] * y0_buf[...] + w[:, 1:2] * y1_buf[...]
    out_ref[...] = _rms(y, gf_ref[...])


def _moe_combine(x2, w, dest, ys, g_final):
    T, D = x2.shape
    steps = T // MOE_TD
    dest3 = dest.reshape(steps, MOE_TD, 2).transpose(0, 2, 1)
    row = lambda wd: pl.BlockSpec((MOE_TD, wd), lambda i: (i, 0))
    return pl.pallas_call(
        _combine_kernel,
        grid=(steps,),
        in_specs=[pl.BlockSpec((1, 2, MOE_TD), lambda i: (i, 0, 0), memory_space=pltpu.SMEM), row(D), row(LANES),
                  pl.BlockSpec(g_final.shape, lambda i: (0, 0)), pl.BlockSpec(memory_space=pl.ANY)],
        out_specs=row(D),
        out_shape=jax.ShapeDtypeStruct((T, D), F32),
        scratch_shapes=[pltpu.VMEM((MOE_TD, D), F32), pltpu.VMEM((MOE_TD, D), F32),
                        pltpu.SemaphoreType.DMA(()), pltpu.SemaphoreType.DMA(())],
        compiler_params=_cparams(("arbitrary",)),
        name="moe_combine",
    )(dest3, x2, w, g_final, ys)


def _moe_routed(x2, g, router_p, wg, wu, wd, g_final):
    T, D = x2.shape
    hf, idx, w = _moe_route(x2, g, router_p)
    dest, token_of_row, tile_expert, n_live = _moe_plan(idx[:, 0:2].reshape(2 * T), MOE_TM)
    xs = _moe_dispatch(hf, token_of_row)
    ys = _moe_experts(xs, tile_expert, n_live.reshape(1), wg, wu, wd)
    return _moe_combine(x2, w, dest, ys, g_final)


def kernel(x, mem, positions, norm_mix, w_in, nsa_cmp_pe, nsa_cmp_w1, nsa_cmp_w2, pool_w, pool_scale, rwkv_mu, rwkv_w0,
           rwkv_w2, rwkv_a0, rwkv_a2, rwkv_g2, rwkv_v0, rwkv_v1, rwkv_v2, rwkv_k_k, rwkv_k_a, rwkv_r_k, rwkv_ln, gla_a2,
           gla_ab, gla_norm, w_gate, b_gate, w_branch, w_out, norm_ca, mem_norm, w_ca_q, w_ca_kv, w_ca_o, norm_ffn,
           ffn_w_gate, ffn_w_up, ffn_w_down, moe_router, moe_w_gate, moe_w_up, moe_w_down, norm_final):
    B, S, D = x.shape
    depth = norm_mix.shape[0]
    assert depth == 2 and S % 256 == 0, "the final norm is fused into the layer-1 expert mixer"
    cosf, sinf = _rope_tables(positions)
    at_cmp_end = lambda t: jnp.pad(t[:, CMP_LEN - 1::CMP_STRIDE], ((0, 0), (0, 1), (0, 0)))
    cosc, sinc = at_cmp_end(cosf), at_cmp_end(sinf)
    memkv = _memkv(mem, mem_norm[None], jnp.concatenate([w_ca_kv[l] for l in range(depth)], axis=1).astype(BF16))
    v_first = None
    for l in range(depth):
        h, q_t, k2, v_t, cv, gates_t, u, rw, gl = _inproj(x, norm_mix[l][None], _inproj_weights(w_in, l), cosf, sinf,
                                                          rwkv_mu[l][None])
        kcmp, vcmp_t = _nsa_compress(cv, *_compress_weights(nsa_cmp_pe[l], nsa_cmp_w1[l], nsa_cmp_w2[l]), cosc, sinc)
        o_nsa = _nsa_attend(q_t, k2, v_t, kcmp, vcmp_t, gates_t)
        o_pool = _pool(u, _block_diag(pool_w[l]).astype(BF16), pool_scale[l][None])
        wz, vecs = _rwkv_weights(l, rwkv_w0, rwkv_w2, rwkv_a0, rwkv_a2, rwkv_g2, rwkv_v0, rwkv_k_k, rwkv_k_a,
                                 rwkv_r_k, rwkv_ln)
        if l == 0:
            o_rwkv, v_first = _rwkv(rw, wz, vecs)
        else:
            v1p = jnp.pad(rwkv_v1[l - 1], ((0, 0), (0, LANES - rwkv_v1.shape[2]))).astype(BF16)
            v2p = jnp.pad(rwkv_v2[l - 1], ((0, LANES - rwkv_v2.shape[1]), (0, 0))).astype(BF16)
            o_rwkv = _rwkv(rw, wz, vecs, h, v_first, v1p, v2p)
        a2p = jnp.pad(gla_a2[l], ((0, LANES - gla_a2.shape[1]), (0, 0))).astype(BF16)
        o_gla = _gla(gl, a2p, gla_ab[l][None], jnp.tile(gla_norm[l], 4)[None])
        wb_nsa = w_branch[l, 0].reshape(4, HEAD_DIM, D)[jnp.array([0, 2, 1, 3])].reshape(BRANCH_WIDTH, D)
        wb = jnp.concatenate([wb_nsa[None], w_branch[l, 1:]], axis=0).astype(BF16)
        x = _merge(x, h, o_nsa, o_pool, o_rwkv, o_gla, w_gate[l].astype(BF16), b_gate[l], wb, w_out[l].astype(BF16))
        x = _cross(x, norm_ca[l][None], w_ca_q[l].astype(BF16), memkv, l, w_ca_o[l].astype(BF16))
        j = l // 2
        x2 = x.reshape(B * S, D)
        if l % 2 == 0:
            x2 = _ffn(x2, norm_ffn[l][None], ffn_w_gate[j].astype(BF16), ffn_w_up[j].astype(BF16),
                      ffn_w_down[j].astype(BF16))
        else:
            router_p = jnp.pad(moe_router[j], ((0, 0), (0, LANES - N_EXPERTS)))
            x2 = _moe_routed(x2, norm_ffn[l][None], router_p, moe_w_gate[j].astype(BF16), moe_w_up[j].astype(BF16),
                             moe_w_down[j].astype(BF16), norm_final[None])
        x = x2.reshape(B, S, D)
    return x
```

```python
import functools

import jax
import jax.numpy as jnp
import numpy as np
from jax import lax
from jax.experimental import pallas as pl
from jax.experimental.pallas import tpu as pltpu

F32 = jnp.float32
BF16 = jnp.bfloat16

D_MODEL = 1024
HEAD_DIM = 64
BRANCH_WIDTH = 256
ROPE_THETA = 500000.0
ROPE_DIM = 16
NORM_EPS = 1e-6
NEG_INF = -1e30

NSA_KV = 128
CMP_LEN = 32
CMP_STRIDE = 16
CMP_HIDDEN = 128
SEL_BLOCK = 64
SEL_TOPN = 8
WINDOW = 512

RWKV_IN = 896
RWKV_GN_EPS = 64e-5
RWKV_CHUNK = 64

GLA_DK = 32
GLA_TAU = 16.0
GLA_CHUNK = 64

CA_WIDTH = 256
D_FF = 2816
N_EXPERTS = 8

LANES = 128
VMEM_LIMIT = 56 * 1024 * 1024

_C_Q = 0
_C_KV = 256
_C_CV = 768
_C_GATE = 1024
_C_POOL = 1152
_C_RWKV = 1408
_C_GLA = 2304
_C_END = 3200


def _cparams(sem):
    return pltpu.CompilerParams(dimension_semantics=sem, vmem_limit_bytes=VMEM_LIMIT)


def _rms(x, g):
    ms = jnp.mean(x * x, axis=-1, keepdims=True)
    return x * lax.rsqrt(ms + NORM_EPS) * g


def _rope128(t, cos, sin):
    lane = lax.broadcasted_iota(jnp.int32, t.shape, 1) % HEAD_DIM
    partner = jnp.where(lane < ROPE_DIM // 2, pltpu.roll(t, LANES - ROPE_DIM // 2, 1),
                        pltpu.roll(t, ROPE_DIM // 2, 1))
    return t * cos + partner * sin


def _inproj_kernel(x_ref, g_ref, w_ref, cos_ref, sin_ref, mu_ref,
                   h_ref, qt_ref, k_ref, vt_ref, cv_ref, gt_ref, u_ref, rw_ref, gl_ref, prev_ref):
    s = pl.program_id(1)
    hb = _rms(x_ref[0], g_ref[...]).astype(BF16)
    h_ref[0] = hb

    def mm(a, b):
        return jnp.dot(hb, w_ref[:, a:b], preferred_element_type=F32)

    cos = cos_ref[0]
    sin = sin_ref[0]
    scale = HEAD_DIM ** -0.5
    qt_ref[0, 0:128, :] = jnp.transpose(_rope128(mm(_C_Q, _C_Q + 128), cos, sin) * scale).astype(BF16)
    qt_ref[0, 128:256, :] = jnp.transpose(_rope128(mm(_C_Q + 128, _C_Q + 256), cos, sin) * scale).astype(BF16)
    k_ref[0, :, 0:128] = _rope128(mm(_C_KV, _C_KV + 128), cos, sin).astype(BF16)
    vt_ref[0, 0:128, :] = jnp.transpose(mm(_C_KV + 128, _C_KV + 256)).astype(BF16)
    k_ref[0, :, 128:256] = _rope128(mm(_C_KV + 256, _C_KV + 384), cos, sin).astype(BF16)
    vt_ref[0, 128:256, :] = jnp.transpose(mm(_C_KV + 384, _C_KV + 512)).astype(BF16)
    cv_ref[0] = mm(_C_CV, _C_CV + 256).astype(BF16)
    gt_ref[0] = jnp.transpose(jax.nn.sigmoid(mm(_C_GATE, _C_GATE + 128)))
    u_ref[0] = mm(_C_POOL, _C_POOL + 256)
    gl_ref[0] = mm(_C_GLA, _C_END)

    p = mm(_C_RWKV, _C_RWKV + RWKV_IN)
    tm = p.shape[0]

    @pl.when(s == 0)
    def _():
        prev_ref[...] = jnp.zeros_like(prev_ref)

    row = lax.broadcasted_iota(jnp.int32, p.shape, 0)
    shifted = jnp.where(row == 0, prev_ref[0:1, :], pltpu.roll(p, 1, 0))
    prev_ref[0:1, :] = p[tm - 1:tm, :]
    rw_ref[0] = p + (shifted - p) * mu_ref[...]


def _inproj(x, g, w_all, cosf, sinf, mu, tm=256):
    B, S, D = x.shape
    tok = lambda w: pl.BlockSpec((1, tm, w), lambda b, s: (b, s, 0))
    tok_t = lambda w: pl.BlockSpec((1, w, tm), lambda b, s: (b, 0, s))
    full = lambda a: pl.BlockSpec(a.shape, lambda b, s: (0,) * a.ndim)
    outs = [(D, BF16, True), (256, BF16, False), (256, BF16, True), (256, BF16, False), (256, BF16, True),
            (128, F32, False), (256, F32, True), (RWKV_IN, F32, True), (896, F32, True)]
    return pl.pallas_call(
        _inproj_kernel,
        grid=(B, S // tm),
        in_specs=[tok(D), full(g), full(w_all), tok(128), tok(128), full(mu)],
        out_specs=[tok(w) if major else tok_t(w) for w, _, major in outs],
        out_shape=[jax.ShapeDtypeStruct((B, S, w) if major else (B, w, S), dt) for w, dt, major in outs],
        scratch_shapes=[pltpu.VMEM((8, RWKV_IN), F32)],
        compiler_params=_cparams(("parallel", "arbitrary")),
        name="inproj",
    )(x, g, w_all, cosf, sinf, mu)


def _inproj_weights(w_in, l):
    w = w_in[l].astype(BF16)
    nsa, pool, rwkv, gla = jnp.split(w, [1036, 1036 + 256, 1036 + 256 + 896], axis=1)
    q = nsa[:, 0:256].reshape(D_MODEL, 4, HEAD_DIM)
    qa = q[:, (0, 2), :].reshape(D_MODEL, 128)
    qb = q[:, (1, 3), :].reshape(D_MODEL, 128)
    kc, vc, ks, vs, kw, vw = (nsa[:, 256 + 128 * i:384 + 128 * i] for i in range(6))
    gates = jnp.pad(nsa[:, 1024:1036], ((0, 0), (0, 116)))
    gq, gk, gv, gad, gog = jnp.split(gla, [128, 256, 512, 528], axis=1)
    gla_r = jnp.concatenate([gq, gk, gv, gog, jnp.pad(gad, ((0, 0), (0, 112)))], axis=1)
    w_all = jnp.concatenate([qa, qb, ks, vs, kw, vw, kc, vc, gates, pool, rwkv, gla_r], axis=1)
    return w_all.astype(BF16)


def _rope_tables(positions):
    half = ROPE_DIM // 2
    inv_freq = ROPE_THETA ** (-jnp.arange(0, ROPE_DIM, 2, dtype=F32) / ROPE_DIM)
    ang = positions.astype(F32)[..., None] * inv_freq
    cos, sin = jnp.cos(ang), jnp.sin(ang)
    ones = jnp.ones(cos.shape[:-1] + (HEAD_DIM - ROPE_DIM,), F32)
    cos64 = jnp.concatenate([cos, cos, ones], axis=-1)
    sin64 = jnp.concatenate([-sin, sin, 0.0 * ones], axis=-1)
    return jnp.concatenate([cos64, cos64], axis=-1), jnp.concatenate([sin64, sin64], axis=-1)


def _gelu_tanh(x):
    return x * (0.5 * (1.0 + jnp.tanh(np.sqrt(2.0 / np.pi) * (x + 0.044715 * (x * x * x)))))


def _dot_nt(a, b):
    return lax.dot_general(a, b, (((1,), (1,)), ((), ())), preferred_element_type=F32)


def _compress_kernel(x_ref, wa_ref, wb_ref, bias_ref, w2k_ref, w2v_ref, cos_ref, sin_ref, kc_ref, vc_ref):
    x = x_ref[0]
    a = jnp.dot(x, wa_ref[...], preferred_element_type=F32)
    b = jnp.dot(x, wb_ref[...], preferred_element_type=F32)
    n = a.shape[0]
    hid = _gelu_tanh(a + pltpu.roll(b, n - 1, 0) + bias_ref[...])
    kc = jnp.dot(hid[:, 0:256].astype(BF16), w2k_ref[...], preferred_element_type=F32)
    vc = jnp.dot(hid[:, 256:512].astype(BF16), w2v_ref[...], preferred_element_type=F32)
    kc_ref[0] = _rope128(kc, cos_ref[0], sin_ref[0]).astype(BF16)
    vc_ref[0] = jnp.transpose(vc).astype(BF16)


def _nsa_compress(cv, wa, wb, bias, w2k, w2v, cosc, sinc):
    B, S, _ = cv.shape
    nseg = S // CMP_STRIDE
    x = cv.reshape(B, nseg, CMP_STRIDE * 256)
    per_b = lambda r, w: pl.BlockSpec((1, r, w), lambda b: (b, 0, 0))
    full = lambda a: pl.BlockSpec(a.shape, lambda b: (0,) * a.ndim)
    return pl.pallas_call(
        _compress_kernel,
        grid=(B,),
        in_specs=[per_b(nseg, CMP_STRIDE * 256), full(wa), full(wb), full(bias), full(w2k), full(w2v),
                  per_b(nseg, 128), per_b(nseg, 128)],
        out_specs=[per_b(nseg, 128), per_b(nseg, 128)],
        out_shape=[jax.ShapeDtypeStruct((B, nseg, 128), BF16)] * 2,
        compiler_params=_cparams(("parallel",)),
        name="nsa_compress",
    )(x, wa, wb, bias, w2k, w2v, cosc, sinc)


def _compress_weights(pe, w1, w2):
    eye2 = jnp.eye(2, dtype=F32)
    half = lambda lo: jnp.einsum('jldf,pj,qg->lpqdjgf', w1[:, lo:lo + CMP_STRIDE], eye2, eye2).reshape(
        CMP_STRIDE * 256, 4 * CMP_HIDDEN).astype(BF16)
    bias = jnp.einsum('jld,jldf->jf', pe, w1)
    bias = jnp.broadcast_to(bias[:, None, :], (2, 2, CMP_HIDDEN)).reshape(1, 4 * CMP_HIDDEN)
    bd = lambda w: jnp.einsum('fd,gq->gfqd', w, eye2).reshape(2 * CMP_HIDDEN, 2 * HEAD_DIM).astype(BF16)
    return half(0), half(CMP_STRIDE), bias, bd(w2[0]), bd(w2[1])


def _cover_t(S):
    n_cmp = (S - CMP_LEN) // CMP_STRIDE + 1
    cmp_start = np.arange(n_cmp) * CMP_STRIDE
    slc_start = np.arange(S // SEL_BLOCK) * SEL_BLOCK
    cover = np.clip(np.minimum(cmp_start[:, None] + CMP_LEN, slc_start[None, :] + SEL_BLOCK)
                    - np.maximum(cmp_start[:, None], slc_start[None, :]), 0, None) / CMP_LEN
    out = np.zeros((S // SEL_BLOCK, S // CMP_STRIDE), np.float32)
    out[:, :n_cmp] = cover.T
    return jnp.asarray(out, BF16)


MASKED = 2.0 * NEG_INF


def _softmax_step(s, carry, v_t):
    m, l, acc = carry
    m_new = jnp.maximum(m, jnp.max(s, axis=0, keepdims=True))
    alpha = jnp.exp(m - m_new)
    p = jnp.exp(s - m_new)
    l = alpha * l + jnp.sum(p, axis=0, keepdims=True)
    acc = alpha * acc + jnp.dot(v_t, p.astype(BF16), preferred_element_type=F32)
    return m_new, l, acc


def _softmax_init(n_q):
    return (jnp.full((1, n_q), NEG_INF, F32), jnp.zeros((1, n_q), F32), jnp.zeros((LANES, n_q), F32))


def _nsa_kernel(qt_ref, k_ref, vt_ref, kc_ref, vct_ref, gt_ref, covt_ref, o_ref, *, tq, tk):
    i = pl.program_id(1)
    t0 = i * tq
    n_blk = covt_ref.shape[0]
    n_q = 4 * tq
    qa_t, qb_t = qt_ref[0, 0:128, :], qt_ref[0, 128:256, :]
    sub_grp = lax.broadcasted_iota(jnp.int32, (LANES, tq), 0) // HEAD_DIM
    gates_t = gt_ref[0]
    tok = t0 + (lax.broadcasted_iota(jnp.int32, (1, n_q), 1) & (tq - 1))
    q_all = jnp.concatenate([jnp.where(sub_grp == g, q_t, 0) for g in range(2) for q_t in (qa_t, qb_t)], axis=1)

    s = jnp.dot(kc_ref[0], q_all, preferred_element_type=F32)
    nrow = lax.broadcasted_iota(jnp.int32, s.shape, 0)
    s = jnp.where(nrow * CMP_STRIDE + (CMP_LEN - 1) <= tok, s, MASKED)
    e = jnp.exp(s - jnp.maximum(jnp.max(s, axis=0, keepdims=True), NEG_INF))
    den = jnp.sum(e, axis=0, keepdims=True)
    p = e / jnp.where(den > 0.0, den, 1.0)
    o_cmp = jnp.dot(vct_ref[0], p.astype(BF16), preferred_element_type=F32)

    p_grp = jnp.concatenate([p[:, 2 * g * tq:(2 * g + 1) * tq] + p[:, (2 * g + 1) * tq:(2 * g + 2) * tq]
                             for g in range(2)], axis=1)
    imp = _dot_split_rhs(covt_ref[...], p_grp, 2)
    blk = lax.broadcasted_iota(jnp.int32, imp.shape, 0)
    tok_b = t0 + (lax.broadcasted_iota(jnp.int32, imp.shape, 1) & (tq - 1))
    cur = tok_b // SEL_BLOCK
    forced = (blk == 0) | (blk == cur) | (blk == cur - 1)
    imp = jnp.where(forced, 1e30, jnp.where(blk * SEL_BLOCK <= tok_b, imp, -1e30))
    cnt = jnp.zeros(imp.shape, F32)
    for i2 in range(n_blk):
        row = imp[i2:i2 + 1, :]
        cnt = cnt + jnp.where((row > imp) | ((row == imp) & (blk > i2)), 1.0, 0.0)
    bias = jnp.where(cnt < float(SEL_TOPN), 0.0, MASKED).astype(BF16)
    sel_bias = jnp.concatenate([bias[:, g * tq:(g + 1) * tq] for g in (0, 0, 1, 1)], axis=1)

    q_ext = jnp.concatenate([q_all, sel_bias, jnp.zeros((LANES - n_blk, n_q), BF16)], axis=0)

    def sel_scores(j):
        rows = pl.ds(pl.multiple_of(j * tk, tk), tk)
        blk_of_key = (j * tk + lax.broadcasted_iota(jnp.int32, (tk, LANES), 0)) // SEL_BLOCK
        onehot = (lax.broadcasted_iota(jnp.int32, (tk, LANES), 1) == blk_of_key).astype(BF16)
        k_ext = jnp.concatenate([k_ref[0, rows, 0:128], onehot], axis=1)
        return jnp.dot(k_ext, q_ext, preferred_element_type=F32), vt_ref[0, 0:128, rows]

    def sel_step(j, carry):
        s, v_t = sel_scores(j)
        return _softmax_step(s, carry, v_t)

    last = (t0 + tq - 1) // tk
    carry = lax.fori_loop(0, last, sel_step, _softmax_init(n_q))
    s, v_t = sel_scores(last)
    key = last * tk + lax.broadcasted_iota(jnp.int32, (tk, n_q), 0)
    _, l_sel, acc_sel = _softmax_step(jnp.where(key <= tok, s, MASKED), carry, v_t)
    o_sel = acc_sel / l_sel

    span = WINDOW + tq
    start = pl.multiple_of(jnp.maximum(t0 - WINDOW, 0), tq)
    key = start + lax.broadcasted_iota(jnp.int32, (span, n_q), 0)
    s = jnp.dot(k_ref[0, pl.ds(start, span), 128:256], q_all, preferred_element_type=F32)
    s = jnp.where((key <= tok) & (tok - key < WINDOW), s, MASKED)
    e = jnp.exp(s - jnp.max(s, axis=0, keepdims=True))
    o_win = (jnp.dot(vt_ref[0, 128:256, pl.ds(start, span)], e.astype(BF16), preferred_element_type=F32)
             / jnp.sum(e, axis=0, keepdims=True))

    out = [jnp.zeros((LANES, tq), F32), jnp.zeros((LANES, tq), F32)]
    for g in range(2):
        for r in range(2):
            cols = slice((2 * g + r) * tq, (2 * g + r + 1) * tq)
            c = g * 6 + r * 3
            o = (gates_t[c:c + 1, :] * o_cmp[:, cols] + gates_t[c + 1:c + 2, :] * o_sel[:, cols]
                 + gates_t[c + 2:c + 3, :] * o_win[:, cols])
            out[r] = jnp.where(sub_grp == g, o, out[r])
    o_ref[0, :, 0:128] = jnp.transpose(out[0]).astype(BF16)
    o_ref[0, :, 128:256] = jnp.transpose(out[1]).astype(BF16)


def _nsa_attend(q_t, k2, v_t, kcmp, vcmp_t, gates_t, tq=128, tk=512):
    B, S, _ = k2.shape
    assert tk % tq == 0 and S % tk == 0 and WINDOW % tq == 0
    covt = _cover_t(S)
    tile_t = lambda w: pl.BlockSpec((1, w, tq), lambda b, i: (b, 0, i))
    per_b = lambda r, w: pl.BlockSpec((1, r, w), lambda b, i: (b, 0, 0))
    return pl.pallas_call(
        functools.partial(_nsa_kernel, tq=tq, tk=tk),
        grid=(B, S // tq),
        in_specs=[tile_t(256), per_b(S, 256), per_b(256, S), per_b(S // CMP_STRIDE, 128), per_b(128, S // CMP_STRIDE),
                  tile_t(128), pl.BlockSpec(covt.shape, lambda b, i: (0, 0))],
        out_specs=pl.BlockSpec((1, tq, 256), lambda b, i: (b, i, 0)),
        out_shape=jax.ShapeDtypeStruct((B, S, 256), BF16),
        compiler_params=_cparams(("parallel", "arbitrary")),
        name="nsa_attend",
    )(q_t, k2, v_t, kcmp, vcmp_t, gates_t, covt)


def _pool_kernel(u_ref, w_ref, scale_ref, o_ref):
    u = u_ref[0]
    row = lax.broadcasted_iota(jnp.int32, u.shape, 0)
    grp = lax.broadcasted_iota(jnp.int32, u.shape, 1) // HEAD_DIM

    def back(x, k):
        return jnp.where(row >= k, pltpu.roll(x, k, 0), 0.0)

    s2 = u + back(u, 1)
    s4 = s2 + back(s2, 2)
    s8 = s4 + back(s4, 4)
    s16 = s8 + back(s8, 8)
    total = jnp.where(grp == 0, s2, jnp.where(grp == 1, s4, jnp.where(grp == 2, s8, s16)))
    win = jnp.where(grp == 0, 2, jnp.where(grp == 1, 4, jnp.where(grp == 2, 8, 16)))
    count = jnp.minimum(row + 1, win).astype(F32)
    d = total / count - u
    y = jnp.dot(d.astype(BF16), w_ref[...], preferred_element_type=F32)
    o_ref[0] = (y * scale_ref[...]).astype(BF16)


def _pool(u, w_bd, scale):
    B, S, W = u.shape
    per_b = pl.BlockSpec((1, S, W), lambda b: (b, 0, 0))
    full = lambda a: pl.BlockSpec(a.shape, lambda b: (0,) * a.ndim)
    return pl.pallas_call(
        _pool_kernel,
        grid=(B,),
        in_specs=[per_b, full(w_bd), full(scale)],
        out_specs=per_b,
        out_shape=jax.ShapeDtypeStruct((B, S, W), BF16),
        compiler_params=_cparams(("parallel",)),
        name="pool",
    )(u, w_bd, scale)


def _block_diag(w):
    G, a, b = w.shape
    return jnp.einsum('gab,gh->gahb', w, jnp.eye(G, dtype=w.dtype)).reshape(G * a, G * b)


def _dot_split_lhs(a, b_exact, terms):
    out, rem = None, a
    for i in range(terms):
        hi = rem.astype(BF16)
        d = jnp.dot(hi, b_exact, preferred_element_type=F32)
        out = d if out is None else out + d
        if i + 1 < terms:
            rem = rem - hi.astype(F32)
    return out


def _dot_split_rhs(a_exact, b, terms):
    out, rem = None, b
    for i in range(terms):
        hi = rem.astype(BF16)
        d = jnp.dot(a_exact, hi, preferred_element_type=F32)
        out = d if out is None else out + d
        if i + 1 < terms:
            rem = rem - hi.astype(F32)
    return out


def _dot_tn(a, b):
    return lax.dot_general(a, b, (((0,), (0,)), ((), ())), preferred_element_type=F32)


def _stack_heads(x, width, n_heads=4):
    head = lax.broadcasted_iota(jnp.int32, x.shape, 1) // width
    return jnp.concatenate([jnp.where(head == h, x, 0.0) for h in range(n_heads)], axis=0)


def _unstack_heads(x, C, n_heads=4):
    out = x[0:C]
    for h in range(1, n_heads):
        out = out + x[h * C:(h + 1) * C]
    return out


def _block_masks(C, n_heads=4):
    n = C * n_heads
    row = lax.broadcasted_iota(jnp.int32, (n, n), 0)
    col = lax.broadcasted_iota(jnp.int32, (n, n), 1)
    same = (row // C) == (col // C)
    return same & (row > col), same & (row >= col)


def _tri_ones(C):
    return jnp.asarray(np.tril(np.ones((C, C), np.float32)), BF16)


def _head_ones(width, n_heads=4):
    return jnp.asarray(np.kron(np.eye(n_heads, dtype=np.float32), np.ones((width, width), np.float32)), BF16)


def _gla_kernel(gl_ref, a2_ref, ab_ref, norm_ref, tri_ref, ones_ref, o_ref, state_ref, *, C):
    s = pl.program_id(1)

    @pl.when(s == 0)
    def _():
        state_ref[...] = jnp.zeros_like(state_ref)

    gl = gl_ref[0]
    tm = gl.shape[0]
    q = gl[:, 0:128] * (GLA_DK ** -0.5)
    k = gl[:, 128:256]
    v = gl[:, 256:512]
    og = gl[:, 512:768]
    ad = gl[:, 768:896]
    x = jnp.dot(ad.astype(BF16), a2_ref[...], preferred_element_type=F32) + ab_ref[...]
    log_a = jax.nn.log_sigmoid(x) * (1.0 / GLA_TAU)
    strict, incl = _block_masks(C)
    kv_mask = (lax.broadcasted_iota(jnp.int32, (256, 128), 0) // HEAD_DIM
               == lax.broadcasted_iota(jnp.int32, (256, 128), 1) // GLA_DK)
    chunks = []
    for c in range(tm // C):
        rc = slice(c * C, (c + 1) * C)
        bcum = _dot_split_rhs(tri_ref[...], log_a[rc], 3)
        mid = bcum[C // 2 - 1:C // 2, :]
        last = bcum[C - 1:C, :]
        q_m = _stack_heads(q[rc] * jnp.exp(bcum - mid), GLA_DK).astype(BF16)
        k_m = _stack_heads(k[rc] * jnp.exp(mid - bcum), GLA_DK).astype(BF16)
        k_end = k[rc] * jnp.exp(last - bcum)
        att = jnp.where(incl, _dot_nt(q_m, k_m), 0.0)
        v_st = _stack_heads(v[rc], HEAD_DIM).astype(BF16)
        chunks.append(dict(
            o_intra=_unstack_heads(jnp.dot(att.astype(BF16), v_st, preferred_element_type=F32), C),
            q_in=(q[rc] * jnp.exp(bcum)).astype(BF16), decay=jnp.exp(last),
            upd=jnp.where(kv_mask, _dot_tn(v[rc].astype(BF16), k_end.astype(BF16)), 0.0)))
    outs = []
    state = state_ref[...]
    for ch in chunks:
        outs.append(ch['o_intra'] + _dot_nt(ch['q_in'], state.astype(BF16)))
        state = state * ch['decay'] + ch['upd']
    state_ref[...] = state
    o = jnp.concatenate(outs, axis=0)
    ms = _dot_split_lhs(o * o, ones_ref[...], 2) * (1.0 / HEAD_DIM)
    o = o * lax.rsqrt(ms + NORM_EPS) * norm_ref[...]
    o_ref[0] = (o * (og * jax.nn.sigmoid(og))).astype(BF16)


def _gla(gl, a2p, ab, norm4, tm=256):
    B, S, W = gl.shape
    C = GLA_CHUNK
    tri, ones = _tri_ones(C), _head_ones(HEAD_DIM)
    tile = lambda w: pl.BlockSpec((1, tm, w), lambda b, s: (b, s, 0))
    full = lambda a: pl.BlockSpec(a.shape, lambda b, s: (0,) * a.ndim)
    return pl.pallas_call(
        functools.partial(_gla_kernel, C=C),
        grid=(B, S // tm),
        in_specs=[tile(W), full(a2p), full(ab), full(norm4), full(tri), full(ones)],
        out_specs=tile(256),
        out_shape=jax.ShapeDtypeStruct((B, S, 256), BF16),
        scratch_shapes=[pltpu.VMEM((256, 128), F32)],
        compiler_params=_cparams(("parallel", "arbitrary")),
        name="gla",
    )(gl, a2p, ab, norm4, tri, ones)


_RV_W0, _RV_A0, _RV_KK, _RV_KA, _RV_RK, _RV_LN, _RV_V0 = range(7)


def _rwkv_kernel(*refs, has_vres, C):
    if has_vres:
        (rw_ref, h_ref, vf_ref, wz_ref, vec_ref, v1_ref, v2_ref, tri_ref, ones_ref, o_ref, state_ref) = refs
    else:
        (rw_ref, wz_ref, vec_ref, tri_ref, ones_ref, o_ref, vout_ref, state_ref) = refs
    s = pl.program_id(1)

    @pl.when(s == 0)
    def _():
        state_ref[...] = jnp.zeros_like(state_ref)

    rw = rw_ref[0]
    tm = rw.shape[0]
    r, k, v, z = rw[:, 0:256], rw[:, 256:512], rw[:, 512:768], rw[:, 768:896]
    vec = lambda i: vec_ref[i:i + 1, :]
    zl = lax.broadcasted_iota(jnp.int32, z.shape, 1)
    zf = jnp.where(zl < 32, jnp.tanh(z), jnp.where(zl < 64, z, jax.nn.sigmoid(z)))
    zz = jnp.dot(zf.astype(BF16), wz_ref[...], preferred_element_type=F32)
    w_log = -jax.nn.softplus(-(vec(_RV_W0) + zz[:, 0:256])) - 0.5
    lw = -jnp.exp(w_log)
    a_sig = jax.nn.sigmoid(vec(_RV_A0) + zz[:, 256:512])
    gate = zz[:, 512:768]
    if has_vres:
        low = jnp.dot(h_ref[0], v1_ref[...], preferred_element_type=F32)
        logit = vec(_RV_V0) + jnp.dot(low.astype(BF16), v2_ref[...], preferred_element_type=F32)
        v = v + (vf_ref[0] - v) * jax.nn.sigmoid(logit)
    else:
        vout_ref[0] = v
    kk = k * vec(_RV_KK)
    norm = jnp.sqrt(_dot_split_lhs(kk * kk, ones_ref[...], 2))
    kk = kk / jnp.maximum(norm, 1e-12)
    k = k * (1.0 + (a_sig - 1.0) * vec(_RV_KA))
    a_vec = -kk
    b_vec = kk * a_sig

    strict, incl = _block_masks(C)
    st = lambda t: _stack_heads(t, HEAD_DIM)
    bdot = lambda p, q: jnp.dot(p.astype(BF16), q.astype(BF16), preferred_element_type=F32)
    state_mask = (lax.broadcasted_iota(jnp.int32, (256, 256), 0) // HEAD_DIM
                  == lax.broadcasted_iota(jnp.int32, (256, 256), 1) // HEAD_DIM)
    n4 = 4 * C
    chunks = []
    for c in range(tm // C):
        rc = slice(c * C, (c + 1) * C)
        lcum = _dot_split_rhs(tri_ref[...], lw[rc], 3)
        lex = lcum - lw[rc]
        mid = lcum[C // 2 - 1:C // 2, :]
        last = lcum[C - 1:C, :]
        e_mid = jnp.exp(mid - lcum)
        e_end = jnp.exp(last - lcum)
        left = jnp.concatenate([st(a_vec[rc] * jnp.exp(lex - mid)), st(r[rc] * jnp.exp(lcum - mid))], axis=0)
        right = jnp.concatenate([st(b_vec[rc] * e_mid), st(k[rc] * e_mid)], axis=0)
        g = _dot_nt(left.astype(BF16), right.astype(BF16))
        v_st = st(v[rc])
        n_ak = jnp.where(strict, g[0:n4, n4:2 * n4], 0.0)
        chunks.append(dict(
            p=jnp.where(strict, g[0:n4, 0:n4], 0.0),
            x=jnp.concatenate([st(a_vec[rc] * jnp.exp(lex)), bdot(n_ak, v_st)], axis=1),
            n_rb=jnp.where(incl, g[n4:2 * n4, 0:n4], 0.0),
            y_rk=bdot(jnp.where(incl, g[n4:2 * n4, n4:2 * n4], 0.0), v_st),
            r_abs=r[rc] * jnp.exp(lcum), decay=jnp.exp(last), v=v[rc],
            bk_end=jnp.concatenate([b_vec[rc] * e_end, k[rc] * e_end], axis=0).astype(BF16)))
    eye = (lax.broadcasted_iota(jnp.int32, (n4, n4), 0) == lax.broadcasted_iota(jnp.int32, (n4, n4), 1)).astype(F32)
    for ch in chunks:
        ch['t'] = eye + ch['p']
    for f in range(1, int(np.log2(C))):
        for ch in chunks:
            ch['p'] = bdot(ch['p'], ch['p'])
        for ch in chunks:
            ch['t'] = ch['t'] + bdot(ch['p'], ch['t'])
    for ch in chunks:
        ch['x'] = bdot(ch['t'], ch['x'])
        y_part = bdot(ch['n_rb'], ch['x'])
        a_eff = _unstack_heads(ch['x'][:, 0:256], C)
        u0 = _unstack_heads(ch['x'][:, 256:512], C)
        ch['r_eff'] = (ch['r_abs'] + _unstack_heads(y_part[:, 0:256], C)).astype(BF16)
        ch['y0'] = _unstack_heads(y_part[:, 256:512] + ch['y_rk'], C)
        ch['mix'] = jnp.where(state_mask, _dot_tn(a_eff.astype(BF16), ch['bk_end'][0:C]), 0.0).astype(BF16)
        ch['add'] = jnp.where(state_mask, _dot_tn(jnp.concatenate([u0, ch['v']], axis=0).astype(BF16), ch['bk_end']), 0.0)
    ys = []
    state = state_ref[...]
    for ch in chunks:
        sb = state.astype(BF16)
        ys.append(_dot_nt(ch['r_eff'], sb) + ch['y0'])
        state = state * ch['decay'] + jnp.dot(sb, ch['mix'], preferred_element_type=F32) + ch['add']
    state_ref[...] = state
    y = jnp.concatenate(ys, axis=0)
    inv = 1.0 / HEAD_DIM
    mean = _dot_split_lhs(y, ones_ref[...], 2) * inv
    yc = y - mean
    var = _dot_split_lhs(yc * yc, ones_ref[...], 2) * inv
    y = yc * lax.rsqrt(var + RWKV_GN_EPS) * vec(_RV_LN)
    y = y + _dot_split_lhs(r * k * vec(_RV_RK), ones_ref[...], 2) * v
    o_ref[0] = (y * gate).astype(BF16)


def _rwkv(rw, wz, vecs, h=None, v_first=None, v1p=None, v2p=None, tm=256):
    B, S, W = rw.shape
    C = RWKV_CHUNK
    has_vres = h is not None
    tri, ones = _tri_ones(C), _head_ones(HEAD_DIM)
    tile = lambda w: pl.BlockSpec((1, tm, w), lambda b, s: (b, s, 0))
    full = lambda a: pl.BlockSpec(a.shape, lambda b, s: (0,) * a.ndim)
    if has_vres:
        args = (rw, h, v_first, wz, vecs, v1p, v2p, tri, ones)
        in_specs = [tile(W), tile(D_MODEL), tile(256), full(wz), full(vecs), full(v1p), full(v2p), full(tri), full(ones)]
        out_specs = tile(256)
        out_shape = jax.ShapeDtypeStruct((B, S, 256), BF16)
    else:
        args = (rw, wz, vecs, tri, ones)
        in_specs = [tile(W), full(wz), full(vecs), full(tri), full(ones)]
        out_specs = [tile(256), tile(256)]
        out_shape = [jax.ShapeDtypeStruct((B, S, 256), BF16), jax.ShapeDtypeStruct((B, S, 256), F32)]
    return pl.pallas_call(
        functools.partial(_rwkv_kernel, has_vres=has_vres, C=C),
        grid=(B, S // tm),
        in_specs=in_specs,
        out_specs=out_specs,
        out_shape=out_shape,
        scratch_shapes=[pltpu.VMEM((256, 256), F32)],
        compiler_params=_cparams(("parallel", "arbitrary")),
        name="rwkv_vres" if has_vres else "rwkv",
    )(*args)


def _rwkv_weights(l, rwkv_w0, rwkv_w2, rwkv_a0, rwkv_a2, rwkv_g2, rwkv_v0, rwkv_k_k, rwkv_k_a, rwkv_r_k, rwkv_ln):
    wz = jnp.zeros((128, 768), F32)
    wz = wz.at[0:32, 0:256].set(rwkv_w2[l]).at[32:64, 256:512].set(rwkv_a2[l]).at[64:128, 512:768].set(rwkv_g2[l])
    v0 = rwkv_v0[l - 1] if l > 0 else jnp.zeros((256,), F32)
    vecs = jnp.stack([rwkv_w0[l], rwkv_a0[l], rwkv_k_k[l], rwkv_k_a[l], rwkv_r_k[l].reshape(-1), rwkv_ln[l], v0,
                      jnp.zeros((256,), F32)])
    return wz.astype(BF16), vecs


def _merge_kernel(x_ref, h_ref, o0_ref, o1_ref, o2_ref, o3_ref, wg_ref, bg_ref, wb_ref, wo_ref, out_ref):
    h = h_ref[0]
    merged = None
    for i, o_ref in enumerate((o0_ref, o1_ref, o2_ref, o3_ref)):
        gate = jax.nn.sigmoid(jnp.dot(h, wg_ref[i], preferred_element_type=F32) + bg_ref[i:i + 1, :])
        term = gate * jnp.dot(o_ref[0], wb_ref[i], preferred_element_type=F32)
        merged = term if merged is None else merged + term
    out_ref[0] = x_ref[0] + jnp.dot(merged.astype(BF16), wo_ref[...], preferred_element_type=F32)


def _merge(x, h, o_nsa, o_pool, o_rwkv, o_gla, wg, bg, wb, wo, tm=512):
    B, S, D = x.shape
    tile = lambda w: pl.BlockSpec((1, tm, w), lambda b, s: (b, s, 0))
    full = lambda a: pl.BlockSpec(a.shape, lambda b, s: (0,) * a.ndim)
    return pl.pallas_call(
        _merge_kernel,
        grid=(B, S // tm),
        in_specs=[tile(D), tile(D), tile(256), tile(256), tile(256), tile(256), full(wg), full(bg), full(wb), full(wo)],
        out_specs=tile(D),
        out_shape=jax.ShapeDtypeStruct((B, S, D), F32),
        compiler_params=_cparams(("parallel", "parallel")),
        name="merge",
    )(x, h, o_nsa, o_pool, o_rwkv, o_gla, wg, bg, wb, wo)


def _memkv_kernel(mem_ref, g_ref, w_ref, o_ref):
    mn = _rms(mem_ref[0], g_ref[...]).astype(BF16)
    o_ref[0] = jnp.dot(mn, w_ref[...], preferred_element_type=F32).astype(BF16)


def _memkv(mem, g, wkv_all):
    B, M, D = mem.shape
    N = wkv_all.shape[1]
    return pl.pallas_call(
        _memkv_kernel,
        grid=(B,),
        in_specs=[pl.BlockSpec((1, M, D), lambda b: (b, 0, 0)), pl.BlockSpec(g.shape, lambda b: (0, 0)),
                  pl.BlockSpec(wkv_all.shape, lambda b: (0, 0))],
        out_specs=pl.BlockSpec((1, M, N), lambda b: (b, 0, 0)),
        out_shape=jax.ShapeDtypeStruct((B, M, N), BF16),
        compiler_params=_cparams(("parallel",)),
        name="memkv",
    )(mem, g, wkv_all)


def _cross_kernel(x_ref, g_ref, wq_ref, k_ref, v_ref, wo_ref, out_ref):
    x = x_ref[0]
    hc = _rms(x, g_ref[...]).astype(BF16)
    q = jnp.dot(hc, wq_ref[...], preferred_element_type=F32) * (HEAD_DIM ** -0.5)
    head = lax.broadcasted_iota(jnp.int32, q.shape, 1) // HEAD_DIM
    k, v = k_ref[0], v_ref[0]
    o = jnp.zeros(q.shape, F32)
    for hh in range(CA_WIDTH // HEAD_DIM):
        qh = jnp.where(head == hh, q, 0.0).astype(BF16)
        s = _dot_nt(qh, k)
        e = jnp.exp(s - jnp.max(s, axis=1, keepdims=True))
        p = e / jnp.sum(e, axis=1, keepdims=True)
        o = jnp.where(head == hh, jnp.dot(p.astype(BF16), v, preferred_element_type=F32), o)
    out_ref[0] = x + jnp.dot(o.astype(BF16), wo_ref[...], preferred_element_type=F32)


def _cross(x, g, wq, memkv, l, wo, tm=512):
    B, S, D = x.shape
    M = memkv.shape[1]
    tile = pl.BlockSpec((1, tm, D), lambda b, s: (b, s, 0))
    full = lambda a: pl.BlockSpec(a.shape, lambda b, s: (0,) * a.ndim)
    kspec = pl.BlockSpec((1, M, CA_WIDTH), lambda b, s: (b, 0, 2 * l))
    vspec = pl.BlockSpec((1, M, CA_WIDTH), lambda b, s: (b, 0, 2 * l + 1))
    return pl.pallas_call(
        _cross_kernel,
        grid=(B, S // tm),
        in_specs=[tile, full(g), full(wq), kspec, vspec, full(wo)],
        out_specs=tile,
        out_shape=jax.ShapeDtypeStruct((B, S, D), F32),
        compiler_params=_cparams(("parallel", "parallel")),
        name="cross_attn",
    )(x, g, wq, memkv, memkv, wo)


def _ffn_kernel(x_ref, g_ref, wg_ref, wu_ref, wd_ref, out_ref, hb_ref, acc_ref):
    f = pl.program_id(1)

    @pl.when(f == 0)
    def _():
        hb_ref[...] = _rms(x_ref[...], g_ref[...]).astype(BF16)
        acc_ref[...] = jnp.zeros_like(acc_ref)

    hb = hb_ref[...]
    gate = jnp.dot(hb, wg_ref[...], preferred_element_type=F32)
    up = jnp.dot(hb, wu_ref[...], preferred_element_type=F32)
    act = gate * jax.nn.sigmoid(gate) * up
    acc_ref[...] += jnp.dot(act.astype(BF16), wd_ref[...], preferred_element_type=F32)

    @pl.when(f == pl.num_programs(1) - 1)
    def _():
        out_ref[...] = x_ref[...] + acc_ref[...]


def _ffn(x2, g, wg, wu, wd, tm=1024, tf=256):
    T, D = x2.shape
    F = wg.shape[1]
    return pl.pallas_call(
        _ffn_kernel,
        grid=(T // tm, F // tf),
        in_specs=[pl.BlockSpec((tm, D), lambda i, f: (i, 0)), pl.BlockSpec(g.shape, lambda i, f: (0, 0)),
                  pl.BlockSpec((D, tf), lambda i, f: (0, f)), pl.BlockSpec((D, tf), lambda i, f: (0, f)),
                  pl.BlockSpec((tf, D), lambda i, f: (f, 0))],
        out_specs=pl.BlockSpec((tm, D), lambda i, f: (i, 0)),
        out_shape=jax.ShapeDtypeStruct((T, D), F32),
        scratch_shapes=[pltpu.VMEM((tm, D), BF16), pltpu.VMEM((tm, D), F32)],
        compiler_params=_cparams(("parallel", "arbitrary")),
        name="ffn",
    )(x2, g, wg, wu, wd)


MOE_TM = 1024
MOE_TD = 512


def _route_kernel(x_ref, g_ref, r_ref, hf_ref, idx_ref, w_ref):
    hf = _rms(x_ref[...], g_ref[...])
    hf_ref[...] = hf
    logits = jnp.dot(hf, r_ref[...], preferred_element_type=F32, precision=lax.Precision.HIGHEST)
    lane = lax.broadcasted_iota(jnp.int32, logits.shape, 1)
    big = logits.shape[1]
    logits = jnp.where(lane < N_EXPERTS, logits, -jnp.inf)
    m1 = jnp.max(logits, axis=1, keepdims=True)
    i1 = jnp.min(jnp.where(logits == m1, lane, big), axis=1, keepdims=True)
    rest = jnp.where(lane == i1, -jnp.inf, logits)
    m2 = jnp.max(rest, axis=1, keepdims=True)
    i2 = jnp.min(jnp.where(rest == m2, lane, big), axis=1, keepdims=True)
    e2 = jnp.exp(m2 - m1)
    w1 = 1.0 / (1.0 + e2)
    idx_ref[...] = jnp.where(lane == 0, i1, jnp.where(lane == 1, i2, 0))
    w_ref[...] = jnp.where(lane == 0, w1, jnp.where(lane == 1, e2 * w1, 0.0))


def _moe_route(x2, g, router_p, tm=1024):
    T, D = x2.shape
    row = lambda w: pl.BlockSpec((tm, w), lambda i: (i, 0))
    full = lambda a: pl.BlockSpec(a.shape, lambda i: (0,) * a.ndim)
    return pl.pallas_call(
        _route_kernel,
        grid=(T // tm,),
        in_specs=[row(D), full(g), full(router_p)],
        out_specs=[row(D), row(LANES), row(LANES)],
        out_shape=[jax.ShapeDtypeStruct((T, D), F32), jax.ShapeDtypeStruct((T, LANES), jnp.int32),
                   jax.ShapeDtypeStruct((T, LANES), F32)],
        compiler_params=_cparams(("parallel",)),
        name="moe_route",
    )(x2, g, router_p)


def _moe_plan(expert_of_pair, tm):
    onehot = (expert_of_pair[:, None] == jnp.arange(N_EXPERTS, dtype=jnp.int32)[None, :]).astype(jnp.int32)
    csum = jnp.cumsum(onehot, axis=0)
    rank = jnp.sum((csum - 1) * onehot, axis=1)
    counts = csum[-1]
    padded = ((counts + tm - 1) // tm) * tm
    ends = jnp.cumsum(padded)
    dest = ((ends - padded)[expert_of_pair] + rank).astype(jnp.int32)
    n_rows = expert_of_pair.shape[0] + N_EXPERTS * tm
    tile_expert = jnp.sum((jnp.arange(n_rows // tm, dtype=jnp.int32)[:, None] * tm >= ends[None, :]).astype(jnp.int32), axis=1)
    token_of_row = jnp.zeros((n_rows,), jnp.int32).at[dest].set(jnp.arange(expert_of_pair.shape[0], dtype=jnp.int32) // 2)
    return (dest, token_of_row, jnp.minimum(tile_expert, N_EXPERTS - 1).astype(jnp.int32),
            (ends[-1] // tm).astype(jnp.int32))


DMA_UNROLL = 8


def _gather_rows(table_hbm, row_of, dst, n, sem, wait):
    def body(c, carry):
        for u in range(DMA_UNROLL):
            r = c * DMA_UNROLL + u
            copy = pltpu.make_async_copy(table_hbm.at[pl.ds(row_of(r), 1)], dst.at[pl.ds(r, 1)], sem)
            copy.wait() if wait else copy.start()
        return carry

    lax.fori_loop(0, n // DMA_UNROLL, body, 0)


def _dispatch_kernel(tok_ref, hf_hbm, xs_ref, sem):
    row_of = lambda r: tok_ref[0, 0, r]
    _gather_rows(hf_hbm, row_of, xs_ref, MOE_TD, sem, wait=False)
    _gather_rows(hf_hbm, row_of, xs_ref, MOE_TD, sem, wait=True)


def _moe_dispatch(hf, token_of_row):
    T, D = hf.shape
    n_rows = token_of_row.shape[0]
    steps = n_rows // MOE_TD
    return pl.pallas_call(
        _dispatch_kernel,
        grid=(steps,),
        in_specs=[pl.BlockSpec((1, 1, MOE_TD), lambda i: (i, 0, 0), memory_space=pltpu.SMEM),
                  pl.BlockSpec(memory_space=pl.ANY)],
        out_specs=pl.BlockSpec((MOE_TD, D), lambda i: (i, 0)),
        out_shape=jax.ShapeDtypeStruct((n_rows, D), F32),
        scratch_shapes=[pltpu.SemaphoreType.DMA(())],
        compiler_params=_cparams(("arbitrary",)),
        name="moe_dispatch",
    )(token_of_row.reshape(steps, 1, MOE_TD), hf)


def _experts_kernel(te_ref, nv_ref, xs_ref, wg_ref, wu_ref, wd_ref, ys_ref, xb_ref, acc_ref):
    i, f = pl.program_id(0), pl.program_id(1)
    live = i < nv_ref[0]

    @pl.when(f == 0)
    def _():
        xb_ref[...] = xs_ref[...].astype(BF16)
        acc_ref[...] = jnp.zeros_like(acc_ref)

    @pl.when(live)
    def _():
        xb = xb_ref[...]
        gate = jnp.dot(xb, wg_ref[0], preferred_element_type=F32)
        up = jnp.dot(xb, wu_ref[0], preferred_element_type=F32)
        act = gate * jax.nn.sigmoid(gate) * up
        acc_ref[...] += jnp.dot(act.astype(BF16), wd_ref[0], preferred_element_type=F32)

    @pl.when(f == pl.num_programs(1) - 1)
    def _():
        ys_ref[...] = acc_ref[...]


def _moe_experts(xs, tile_expert, n_live, wg, wu, wd, tf=256):
    R, D = xs.shape
    F = wg.shape[2]
    tm = MOE_TM
    grid_spec = pltpu.PrefetchScalarGridSpec(
        num_scalar_prefetch=2,
        grid=(R // tm, F // tf),
        in_specs=[pl.BlockSpec((tm, D), lambda i, f, te, nv: (i, 0)),
                  pl.BlockSpec((1, D, tf), lambda i, f, te, nv: (te[i], 0, f)),
                  pl.BlockSpec((1, D, tf), lambda i, f, te, nv: (te[i], 0, f)),
                  pl.BlockSpec((1, tf, D), lambda i, f, te, nv: (te[i], f, 0))],
        out_specs=pl.BlockSpec((tm, D), lambda i, f, te, nv: (i, 0)),
        scratch_shapes=[pltpu.VMEM((tm, D), BF16), pltpu.VMEM((tm, D), F32)],
    )
    return pl.pallas_call(
        _experts_kernel,
        grid_spec=grid_spec,
        out_shape=jax.ShapeDtypeStruct((R, D), F32),
        compiler_params=_cparams(("parallel", "arbitrary")),
        name="moe_experts",
    )(tile_expert, n_live, xs, wg, wu, wd)


def _combine_kernel(dest_ref, x_ref, w_ref, gf_ref, ys_hbm, out_ref, y0_buf, y1_buf, sem0, sem1):
    slots = ((lambda r: dest_ref[0, 0, r], y0_buf, sem0), (lambda r: dest_ref[0, 1, r], y1_buf, sem1))
    for row_of, buf, sem in slots:
        _gather_rows(ys_hbm, row_of, buf, MOE_TD, sem, wait=False)
    for row_of, buf, sem in slots:
        _gather_rows(ys_hbm, row_of, buf, MOE_TD, sem, wait=True)
    w = w_ref[...]
    y = x_ref[...] + w[:, 0:1] * y0_buf[...] + w[:, 1:2] * y1_buf[...]
    out_ref[...] = _rms(y, gf_ref[...])


def _moe_combine(x2, w, dest, ys, g_final):
    T, D = x2.shape
    steps = T // MOE_TD
    dest3 = dest.reshape(steps, MOE_TD, 2).transpose(0, 2, 1)
    row = lambda wd: pl.BlockSpec((MOE_TD, wd), lambda i: (i, 0))
    return pl.pallas_call(
        _combine_kernel,
        grid=(steps,),
        in_specs=[pl.BlockSpec((1, 2, MOE_TD), lambda i: (i, 0, 0), memory_space=pltpu.SMEM), row(D), row(LANES),
                  pl.BlockSpec(g_final.shape, lambda i: (0, 0)), pl.BlockSpec(memory_space=pl.ANY)],
        out_specs=row(D),
        out_shape=jax.ShapeDtypeStruct((T, D), F32),
        scratch_shapes=[pltpu.VMEM((MOE_TD, D), F32), pltpu.VMEM((MOE_TD, D), F32),
                        pltpu.SemaphoreType.DMA(()), pltpu.SemaphoreType.DMA(())],
        compiler_params=_cparams(("arbitrary",)),
        name="moe_combine",
    )(dest3, x2, w, g_final, ys)


def _moe_routed(x2, g, router_p, wg, wu, wd, g_final):
    T, D = x2.shape
    hf, idx, w = _moe_route(x2, g, router_p)
    dest, token_of_row, tile_expert, n_live = _moe_plan(idx[:, 0:2].reshape(2 * T), MOE_TM)
    xs = _moe_dispatch(hf, token_of_row)
    ys = _moe_experts(xs, tile_expert, n_live.reshape(1), wg, wu, wd)
    return _moe_combine(x2, w, dest, ys, g_final)


def kernel(x, mem, positions, norm_mix, w_in, nsa_cmp_pe, nsa_cmp_w1, nsa_cmp_w2, pool_w, pool_scale, rwkv_mu, rwkv_w0,
           rwkv_w2, rwkv_a0, rwkv_a2, rwkv_g2, rwkv_v0, rwkv_v1, rwkv_v2, rwkv_k_k, rwkv_k_a, rwkv_r_k, rwkv_ln, gla_a2,
           gla_ab, gla_norm, w_gate, b_gate, w_branch, w_out, norm_ca, mem_norm, w_ca_q, w_ca_kv, w_ca_o, norm_ffn,
           ffn_w_gate, ffn_w_up, ffn_w_down, moe_router, moe_w_gate, moe_w_up, moe_w_down, norm_final):
    B, S, D = x.shape
    depth = norm_mix.shape[0]
    assert depth == 2 and S % 256 == 0, "the final norm is fused into the layer-1 expert mixer"
    cosf, sinf = _rope_tables(positions)
    at_cmp_end = lambda t: jnp.pad(t[:, CMP_LEN - 1::CMP_STRIDE], ((0, 0), (0, 1), (0, 0)))
    cosc, sinc = at_cmp_end(cosf), at_cmp_end(sinf)
    memkv = _memkv(mem, mem_norm[None], jnp.concatenate([w_ca_kv[l] for l in range(depth)], axis=1).astype(BF16))
    v_first = None
    for l in range(depth):
        h, q_t, k2, v_t, cv, gates_t, u, rw, gl = _inproj(x, norm_mix[l][None], _inproj_weights(w_in, l), cosf, sinf,
                                                          rwkv_mu[l][None])
        kcmp, vcmp_t = _nsa_compress(cv, *_compress_weights(nsa_cmp_pe[l], nsa_cmp_w1[l], nsa_cmp_w2[l]), cosc, sinc)
        o_nsa = _nsa_attend(q_t, k2, v_t, kcmp, vcmp_t, gates_t)
        o_pool = _pool(u, _block_diag(pool_w[l]).astype(BF16), pool_scale[l][None])
        wz, vecs = _rwkv_weights(l, rwkv_w0, rwkv_w2, rwkv_a0, rwkv_a2, rwkv_g2, rwkv_v0, rwkv_k_k, rwkv_k_a,
                                 rwkv_r_k, rwkv_ln)
        if l == 0:
            o_rwkv, v_first = _rwkv(rw, wz, vecs)
        else:
            v1p = jnp.pad(rwkv_v1[l - 1], ((0, 0), (0, LANES - rwkv_v1.shape[2]))).astype(BF16)
            v2p = jnp.pad(rwkv_v2[l - 1], ((0, LANES - rwkv_v2.shape[1]), (0, 0))).astype(BF16)
            o_rwkv = _rwkv(rw, wz, vecs, h, v_first, v1p, v2p)
        a2p = jnp.pad(gla_a2[l], ((0, LANES - gla_a2.shape[1]), (0, 0))).astype(BF16)
        o_gla = _gla(gl, a2p, gla_ab[l][None], jnp.tile(gla_norm[l], 4)[None])
        wb_nsa = w_branch[l, 0].reshape(4, HEAD_DIM, D)[jnp.array([0, 2, 1, 3])].reshape(BRANCH_WIDTH, D)
        wb = jnp.concatenate([wb_nsa[None], w_branch[l, 1:]], axis=0).astype(BF16)
        x = _merge(x, h, o_nsa, o_pool, o_rwkv, o_gla, w_gate[l].astype(BF16), b_gate[l], wb, w_out[l].astype(BF16))
        x = _cross(x, norm_ca[l][None], w_ca_q[l].astype(BF16), memkv, l, w_ca_o[l].astype(BF16))
        j = l // 2
        x2 = x.reshape(B * S, D)
        if l % 2 == 0:
            x2 = _ffn(x2, norm_ffn[l][None], ffn_w_gate[j].astype(BF16), ffn_w_up[j].astype(BF16),
                      ffn_w_down[j].astype(BF16))
        else:
            router_p = jnp.pad(moe_router[j], ((0, 0), (0, LANES - N_EXPERTS)))
            x2 = _moe_routed(x2, norm_ffn[l][None], router_p, moe_w_gate[j].astype(BF16), moe_w_up[j].astype(BF16),
                             moe_w_down[j].astype(BF16), norm_final[None])
        x = x2.reshape(B, S, D)
    return x
```

```python
import functools

import jax
import jax.numpy as jnp
import numpy as np
from jax import lax
from jax.experimental import pallas as pl
from jax.experimental.pallas import tpu as pltpu

F32 = jnp.float32
BF16 = jnp.bfloat16

D_MODEL = 1024
HEAD_DIM = 64
BRANCH_WIDTH = 256
ROPE_THETA = 500000.0
ROPE_DIM = 16
NORM_EPS = 1e-6
NEG_INF = -1e30

NSA_KV = 128
CMP_LEN = 32
CMP_STRIDE = 16
CMP_HIDDEN = 128
SEL_BLOCK = 64
SEL_TOPN = 8
WINDOW = 512

RWKV_IN = 896
RWKV_GN_EPS = 64e-5
RWKV_CHUNK = 64

GLA_DK = 32
GLA_TAU = 16.0
GLA_CHUNK = 64

CA_WIDTH = 256
D_FF = 2816
N_EXPERTS = 8

LANES = 128
VMEM_LIMIT = 56 * 1024 * 1024

_C_Q = 0
_C_KV = 256
_C_CV = 768
_C_GATE = 1024
_C_POOL = 1152
_C_RWKV = 1408
_C_GLA = 2304
_C_END = 3200


def _cparams(sem):
    return pltpu.CompilerParams(dimension_semantics=sem, vmem_limit_bytes=VMEM_LIMIT)


def _rms(x, g):
    ms = jnp.mean(x * x, axis=-1, keepdims=True)
    return x * lax.rsqrt(ms + NORM_EPS) * g


def _rope128(t, cos, sin):
    lane = lax.broadcasted_iota(jnp.int32, t.shape, 1) % HEAD_DIM
    partner = jnp.where(lane < ROPE_DIM // 2, pltpu.roll(t, LANES - ROPE_DIM // 2, 1),
                        pltpu.roll(t, ROPE_DIM // 2, 1))
    return t * cos + partner * sin


def _inproj_kernel(x_ref, g_ref, w_ref, cos_ref, sin_ref, mu_ref,
                   h_ref, qt_ref, k_ref, vt_ref, cv_ref, gt_ref, u_ref, rw_ref, gl_ref, prev_ref):
    s = pl.program_id(1)
    hb = _rms(x_ref[0], g_ref[...]).astype(BF16)
    h_ref[0] = hb

    def mm(a, b):
        return jnp.dot(hb, w_ref[:, a:b], preferred_element_type=F32)

    cos = cos_ref[0]
    sin = sin_ref[0]
    scale = HEAD_DIM ** -0.5
    qt_ref[0, 0:128, :] = jnp.transpose(_rope128(mm(_C_Q, _C_Q + 128), cos, sin) * scale).astype(BF16)
    qt_ref[0, 128:256, :] = jnp.transpose(_rope128(mm(_C_Q + 128, _C_Q + 256), cos, sin) * scale).astype(BF16)
    k_ref[0, :, 0:128] = _rope128(mm(_C_KV, _C_KV + 128), cos, sin).astype(BF16)
    vt_ref[0, 0:128, :] = jnp.transpose(mm(_C_KV + 128, _C_KV + 256)).astype(BF16)
    k_ref[0, :, 128:256] = _rope128(mm(_C_KV + 256, _C_KV + 384), cos, sin).astype(BF16)
    vt_ref[0, 128:256, :] = jnp.transpose(mm(_C_KV + 384, _C_KV + 512)).astype(BF16)
    cv_ref[0] = mm(_C_CV, _C_CV + 256).astype(BF16)
    gt_ref[0] = jnp.transpose(jax.nn.sigmoid(mm(_C_GATE, _C_GATE + 128)))
    u_ref[0] = mm(_C_POOL, _C_POOL + 256)
    gl_ref[0] = mm(_C_GLA, _C_END)

    p = mm(_C_RWKV, _C_RWKV + RWKV_IN)
    tm = p.shape[0]

    @pl.when(s == 0)
    def _():
        prev_ref[...] = jnp.zeros_like(prev_ref)

    row = lax.broadcasted_iota(jnp.int32, p.shape, 0)
    shifted = jnp.where(row == 0, prev_ref[0:1, :], pltpu.roll(p, 1, 0))
    prev_ref[0:1, :] = p[tm - 1:tm, :]
    rw_ref[0] = p + (shifted - p) * mu_ref[...]


def _inproj(x, g, w_all, cosf, sinf, mu, tm=256):
    B, S, D = x.shape
    tok = lambda w: pl.BlockSpec((1, tm, w), lambda b, s: (b, s, 0))
    tok_t = lambda w: pl.BlockSpec((1, w, tm), lambda b, s: (b, 0, s))
    full = lambda a: pl.BlockSpec(a.shape, lambda b, s: (0,) * a.ndim)
    outs = [(D, BF16, True), (256, BF16, False), (256, BF16, True), (256, BF16, False), (256, BF16, True),
            (128, F32, False), (256, F32, True), (RWKV_IN, F32, True), (896, F32, True)]
    return pl.pallas_call(
        _inproj_kernel,
        grid=(B, S // tm),
        in_specs=[tok(D), full(g), full(w_all), tok(128), tok(128), full(mu)],
        out_specs=[tok(w) if major else tok_t(w) for w, _, major in outs],
        out_shape=[jax.ShapeDtypeStruct((B, S, w) if major else (B, w, S), dt) for w, dt, major in outs],
        scratch_shapes=[pltpu.VMEM((8, RWKV_IN), F32)],
        compiler_params=_cparams(("parallel", "arbitrary")),
        name="inproj",
    )(x, g, w_all, cosf, sinf, mu)


def _inproj_weights(w_in, l):
    w = w_in[l].astype(BF16)
    nsa, pool, rwkv, gla = jnp.split(w, [1036, 1036 + 256, 1036 + 256 + 896], axis=1)
    q = nsa[:, 0:256].reshape(D_MODEL, 4, HEAD_DIM)
    qa = q[:, (0, 2), :].reshape(D_MODEL, 128)
    qb = q[:, (1, 3), :].reshape(D_MODEL, 128)
    kc, vc, ks, vs, kw, vw = (nsa[:, 256 + 128 * i:384 + 128 * i] for i in range(6))
    gates = jnp.pad(nsa[:, 1024:1036], ((0, 0), (0, 116)))
    gq, gk, gv, gad, gog = jnp.split(gla, [128, 256, 512, 528], axis=1)
    gla_r = jnp.concatenate([gq, gk, gv, gog, jnp.pad(gad, ((0, 0), (0, 112)))], axis=1)
    w_all = jnp.concatenate([qa, qb, ks, vs, kw, vw, kc, vc, gates, pool, rwkv, gla_r], axis=1)
    return w_all.astype(BF16)


def _rope_tables(positions):
    half = ROPE_DIM // 2
    inv_freq = ROPE_THETA ** (-jnp.arange(0, ROPE_DIM, 2, dtype=F32) / ROPE_DIM)
    ang = positions.astype(F32)[..., None] * inv_freq
    cos, sin = jnp.cos(ang), jnp.sin(ang)
    ones = jnp.ones(cos.shape[:-1] + (HEAD_DIM - ROPE_DIM,), F32)
    cos64 = jnp.concatenate([cos, cos, ones], axis=-1)
    sin64 = jnp.concatenate([-sin, sin, 0.0 * ones], axis=-1)
    return jnp.concatenate([cos64, cos64], axis=-1), jnp.concatenate([sin64, sin64], axis=-1)


def _gelu_tanh(x):
    return x * (0.5 * (1.0 + jnp.tanh(np.sqrt(2.0 / np.pi) * (x + 0.044715 * (x * x * x)))))


def _dot_nt(a, b):
    return lax.dot_general(a, b, (((1,), (1,)), ((), ())), preferred_element_type=F32)


def _compress_kernel(x_ref, wa_ref, wb_ref, bias_ref, w2k_ref, w2v_ref, cos_ref, sin_ref, kc_ref, vc_ref):
    x = x_ref[0]
    a = jnp.dot(x, wa_ref[...], preferred_element_type=F32)
    b = jnp.dot(x, wb_ref[...], preferred_element_type=F32)
    n = a.shape[0]
    hid = _gelu_tanh(a + pltpu.roll(b, n - 1, 0) + bias_ref[...])
    kc = jnp.dot(hid[:, 0:256].astype(BF16), w2k_ref[...], preferred_element_type=F32)
    vc = jnp.dot(hid[:, 256:512].astype(BF16), w2v_ref[...], preferred_element_type=F32)
    kc_ref[0] = _rope128(kc, cos_ref[0], sin_ref[0]).astype(BF16)
    vc_ref[0] = jnp.transpose(vc).astype(BF16)


def _nsa_compress(cv, wa, wb, bias, w2k, w2v, cosc, sinc):
    B, S, _ = cv.shape
    nseg = S // CMP_STRIDE
    x = cv.reshape(B, nseg, CMP_STRIDE * 256)
    per_b = lambda r, w: pl.BlockSpec((1, r, w), lambda b: (b, 0, 0))
    full = lambda a: pl.BlockSpec(a.shape, lambda b: (0,) * a.ndim)
    return pl.pallas_call(
        _compress_kernel,
        grid=(B,),
        in_specs=[per_b(nseg, CMP_STRIDE * 256), full(wa), full(wb), full(bias), full(w2k), full(w2v),
                  per_b(nseg, 128), per_b(nseg, 128)],
        out_specs=[per_b(nseg, 128), per_b(nseg, 128)],
        out_shape=[jax.ShapeDtypeStruct((B, nseg, 128), BF16)] * 2,
        compiler_params=_cparams(("parallel",)),
        name="nsa_compress",
    )(x, wa, wb, bias, w2k, w2v, cosc, sinc)


def _compress_weights(pe, w1, w2):
    eye2 = jnp.eye(2, dtype=F32)
    half = lambda lo: jnp.einsum('jldf,pj,qg->lpqdjgf', w1[:, lo:lo + CMP_STRIDE], eye2, eye2).reshape(
        CMP_STRIDE * 256, 4 * CMP_HIDDEN).astype(BF16)
    bias = jnp.einsum('jld,jldf->jf', pe, w1)
    bias = jnp.broadcast_to(bias[:, None, :], (2, 2, CMP_HIDDEN)).reshape(1, 4 * CMP_HIDDEN)
    bd = lambda w: jnp.einsum('fd,gq->gfqd', w, eye2).reshape(2 * CMP_HIDDEN, 2 * HEAD_DIM).astype(BF16)
    return half(0), half(CMP_STRIDE), bias, bd(w2[0]), bd(w2[1])


def _cover_t(S):
    n_cmp = (S - CMP_LEN) // CMP_STRIDE + 1
    cmp_start = np.arange(n_cmp) * CMP_STRIDE
    slc_start = np.arange(S // SEL_BLOCK) * SEL_BLOCK
    cover = np.clip(np.minimum(cmp_start[:, None] + CMP_LEN, slc_start[None, :] + SEL_BLOCK)
                    - np.maximum(cmp_start[:, None], slc_start[None, :]), 0, None) / CMP_LEN
    out = np.zeros((S // SEL_BLOCK, S // CMP_STRIDE), np.float32)
    out[:, :n_cmp] = cover.T
    return jnp.asarray(out, BF16)


MASKED = 2.0 * NEG_INF


def _softmax_step(s, carry, v_t):
    m, l, acc = carry
    m_new = jnp.maximum(m, jnp.max(s, axis=0, keepdims=True))
    alpha = jnp.exp(m - m_new)
    p = jnp.exp(s - m_new)
    l = alpha * l + jnp.sum(p, axis=0, keepdims=True)
    acc = alpha * acc + jnp.dot(v_t, p.astype(BF16), preferred_element_type=F32)
    return m_new, l, acc


def _softmax_init(n_q):
    return (jnp.full((1, n_q), NEG_INF, F32), jnp.zeros((1, n_q), F32), jnp.zeros((LANES, n_q), F32))


def _nsa_kernel(qt_ref, k_ref, vt_ref, kc_ref, vct_ref, gt_ref, covt_ref, o_ref, *, tq, tk):
    i = pl.program_id(1)
    t0 = i * tq
    n_blk = covt_ref.shape[0]
    n_q = 4 * tq
    qa_t, qb_t = qt_ref[0, 0:128, :], qt_ref[0, 128:256, :]
    sub_grp = lax.broadcasted_iota(jnp.int32, (LANES, tq), 0) // HEAD_DIM
    gates_t = gt_ref[0]
    tok = t0 + (lax.broadcasted_iota(jnp.int32, (1, n_q), 1) & (tq - 1))
    q_all = jnp.concatenate([jnp.where(sub_grp == g, q_t, 0) for g in range(2) for q_t in (qa_t, qb_t)], axis=1)

    s = jnp.dot(kc_ref[0], q_all, preferred_element_type=F32)
    nrow = lax.broadcasted_iota(jnp.int32, s.shape, 0)
    s = jnp.where(nrow * CMP_STRIDE + (CMP_LEN - 1) <= tok, s, MASKED)
    e = jnp.exp(s - jnp.maximum(jnp.max(s, axis=0, keepdims=True), NEG_INF))
    den = jnp.sum(e, axis=0, keepdims=True)
    p = e / jnp.where(den > 0.0, den, 1.0)
    o_cmp = jnp.dot(vct_ref[0], p.astype(BF16), preferred_element_type=F32)

    p_grp = jnp.concatenate([p[:, 2 * g * tq:(2 * g + 1) * tq] + p[:, (2 * g + 1) * tq:(2 * g + 2) * tq]
                             for g in range(2)], axis=1)
    imp = _dot_split_rhs(covt_ref[...], p_grp, 2)
    blk = lax.broadcasted_iota(jnp.int32, imp.shape, 0)
    tok_b = t0 + (lax.broadcasted_iota(jnp.int32, imp.shape, 1) & (tq - 1))
    cur = tok_b // SEL_BLOCK
    forced = (blk == 0) | (blk == cur) | (blk == cur - 1)
    imp = jnp.where(forced, 1e30, jnp.where(blk * SEL_BLOCK <= tok_b, imp, -1e30))
    cnt = jnp.zeros(imp.shape, F32)
    for i2 in range(n_blk):
        row = imp[i2:i2 + 1, :]
        cnt = cnt + jnp.where((row > imp) | ((row == imp) & (blk > i2)), 1.0, 0.0)
    bias = jnp.where(cnt < float(SEL_TOPN), 0.0, MASKED).astype(BF16)
    sel_bias = jnp.concatenate([bias[:, g * tq:(g + 1) * tq] for g in (0, 0, 1, 1)], axis=1)

    q_ext = jnp.concatenate([q_all, sel_bias, jnp.zeros((LANES - n_blk, n_q), BF16)], axis=0)

    def sel_scores(j):
        rows = pl.ds(pl.multiple_of(j * tk, tk), tk)
        blk_of_key = (j * tk + lax.broadcasted_iota(jnp.int32, (tk, LANES), 0)) // SEL_BLOCK
        onehot = (lax.broadcasted_iota(jnp.int32, (tk, LANES), 1) == blk_of_key).astype(BF16)
        k_ext = jnp.concatenate([k_ref[0, rows, 0:128], onehot], axis=1)
        return jnp.dot(k_ext, q_ext, preferred_element_type=F32), vt_ref[0, 0:128, rows]

    def sel_step(j, carry):
        s, v_t = sel_scores(j)
        return _softmax_step(s, carry, v_t)

    last = (t0 + tq - 1) // tk
    carry = lax.fori_loop(0, last, sel_step, _softmax_init(n_q))
    s, v_t = sel_scores(last)
    key = last * tk + lax.broadcasted_iota(jnp.int32, (tk, n_q), 0)
    _, l_sel, acc_sel = _softmax_step(jnp.where(key <= tok, s, MASKED), carry, v_t)
    o_sel = acc_sel / l_sel

    span = WINDOW + tq
    start = pl.multiple_of(jnp.maximum(t0 - WINDOW, 0), tq)
    key = start + lax.broadcasted_iota(jnp.int32, (span, n_q), 0)
    s = jnp.dot(k_ref[0, pl.ds(start, span), 128:256], q_all, preferred_element_type=F32)
    s = jnp.where((key <= tok) & (tok - key < WINDOW), s, MASKED)
    e = jnp.exp(s - jnp.max(s, axis=0, keepdims=True))
    o_win = (jnp.dot(vt_ref[0, 128:256, pl.ds(start, span)], e.astype(BF16), preferred_element_type=F32)
             / jnp.sum(e, axis=0, keepdims=True))

    out = [jnp.zeros((LANES, tq), F32), jnp.zeros((LANES, tq), F32)]
    for g in range(2):
        for r in range(2):
            cols = slice((2 * g + r) * tq, (2 * g + r + 1) * tq)
            c = g * 6 + r * 3
            o = (gates_t[c:c + 1, :] * o_cmp[:, cols] + gates_t[c + 1:c + 2, :] * o_sel[:, cols]
                 + gates_t[c + 2:c + 3, :] * o_win[:, cols])
            out[r] = jnp.where(sub_grp == g, o, out[r])
    o_ref[0, :, 0:128] = jnp.transpose(out[0]).astype(BF16)
    o_ref[0, :, 128:256] = jnp.transpose(out[1]).astype(BF16)


def _nsa_attend(q_t, k2, v_t, kcmp, vcmp_t, gates_t, tq=128, tk=512):
    B, S, _ = k2.shape
    assert tk % tq == 0 and S % tk == 0 and WINDOW % tq == 0
    covt = _cover_t(S)
    tile_t = lambda w: pl.BlockSpec((1, w, tq), lambda b, i: (b, 0, i))
    per_b = lambda r, w: pl.BlockSpec((1, r, w), lambda b, i: (b, 0, 0))
    return pl.pallas_call(
        functools.partial(_nsa_kernel, tq=tq, tk=tk),
        grid=(B, S // tq),
        in_specs=[tile_t(256), per_b(S, 256), per_b(256, S), per_b(S // CMP_STRIDE, 128), per_b(128, S // CMP_STRIDE),
                  tile_t(128), pl.BlockSpec(covt.shape, lambda b, i: (0, 0))],
        out_specs=pl.BlockSpec((1, tq, 256), lambda b, i: (b, i, 0)),
        out_shape=jax.ShapeDtypeStruct((B, S, 256), BF16),
        compiler_params=_cparams(("parallel", "arbitrary")),
        name="nsa_attend",
    )(q_t, k2, v_t, kcmp, vcmp_t, gates_t, covt)


def _pool_kernel(u_ref, w_ref, scale_ref, o_ref):
    u = u_ref[0]
    row = lax.broadcasted_iota(jnp.int32, u.shape, 0)
    grp = lax.broadcasted_iota(jnp.int32, u.shape, 1) // HEAD_DIM

    def back(x, k):
        return jnp.where(row >= k, pltpu.roll(x, k, 0), 0.0)

    s2 = u + back(u, 1)
    s4 = s2 + back(s2, 2)
    s8 = s4 + back(s4, 4)
    s16 = s8 + back(s8, 8)
    total = jnp.where(grp == 0, s2, jnp.where(grp == 1, s4, jnp.where(grp == 2, s8, s16)))
    win = jnp.where(grp == 0, 2, jnp.where(grp == 1, 4, jnp.where(grp == 2, 8, 16)))
    count = jnp.minimum(row + 1, win).astype(F32)
    d = total / count - u
    y = jnp.dot(d.astype(BF16), w_ref[...], preferred_element_type=F32)
    o_ref[0] = (y * scale_ref[...]).astype(BF16)


def _pool(u, w_bd, scale):
    B, S, W = u.shape
    per_b = pl.BlockSpec((1, S, W), lambda b: (b, 0, 0))
    full = lambda a: pl.BlockSpec(a.shape, lambda b: (0,) * a.ndim)
    return pl.pallas_call(
        _pool_kernel,
        grid=(B,),
        in_specs=[per_b, full(w_bd), full(scale)],
        out_specs=per_b,
        out_shape=jax.ShapeDtypeStruct((B, S, W), BF16),
        compiler_params=_cparams(("parallel",)),
        name="pool",
    )(u, w_bd, scale)


def _block_diag(w):
    G, a, b = w.shape
    return jnp.einsum('gab,gh->gahb', w, jnp.eye(G, dtype=w.dtype)).reshape(G * a, G * b)


def _dot_split_lhs(a, b_exact, terms):
    out, rem = None, a
    for i in range(terms):
        hi = rem.astype(BF16)
        d = jnp.dot(hi, b_exact, preferred_element_type=F32)
        out = d if out is None else out + d
        if i + 1 < terms:
            rem = rem - hi.astype(F32)
    return out


def _dot_split_rhs(a_exact, b, terms):
    out, rem = None, b
    for i in range(terms):
        hi = rem.astype(BF16)
        d = jnp.dot(a_exact, hi, preferred_element_type=F32)
        out = d if out is None else out + d
        if i + 1 < terms:
            rem = rem - hi.astype(F32)
    return out


def _dot_tn(a, b):
    return lax.dot_general(a, b, (((0,), (0,)), ((), ())), preferred_element_type=F32)


def _stack_heads(x, width, n_heads=4):
    head = lax.broadcasted_iota(jnp.int32, x.shape, 1) // width
    return jnp.concatenate([jnp.where(head == h, x, 0.0) for h in range(n_heads)], axis=0)


def _unstack_heads(x, C, n_heads=4):
    out = x[0:C]
    for h in range(1, n_heads):
        out = out + x[h * C:(h + 1) * C]
    return out


def _block_masks(C, n_heads=4):
    n = C * n_heads
    row = lax.broadcasted_iota(jnp.int32, (n, n), 0)
    col = lax.broadcasted_iota(jnp.int32, (n, n), 1)
    same = (row // C) == (col // C)
    return same & (row > col), same & (row >= col)


def _tri_ones(C):
    return jnp.asarray(np.tril(np.ones((C, C), np.float32)), BF16)


def _head_ones(width, n_heads=4):
    return jnp.asarray(np.kron(np.eye(n_heads, dtype=np.float32), np.ones((width, width), np.float32)), BF16)


def _gla_kernel(gl_ref, a2_ref, ab_ref, norm_ref, tri_ref, ones_ref, o_ref, state_ref, *, C):
    s = pl.program_id(1)

    @pl.when(s == 0)
    def _():
        state_ref[...] = jnp.zeros_like(state_ref)

    gl = gl_ref[0]
    tm = gl.shape[0]
    q = gl[:, 0:128] * (GLA_DK ** -0.5)
    k = gl[:, 128:256]
    v = gl[:, 256:512]
    og = gl[:, 512:768]
    ad = gl[:, 768:896]
    x = jnp.dot(ad.astype(BF16), a2_ref[...], preferred_element_type=F32) + ab_ref[...]
    log_a = jax.nn.log_sigmoid(x) * (1.0 / GLA_TAU)
    strict, incl = _block_masks(C)
    kv_mask = (lax.broadcasted_iota(jnp.int32, (256, 128), 0) // HEAD_DIM
               == lax.broadcasted_iota(jnp.int32, (256, 128), 1) // GLA_DK)
    chunks = []
    for c in range(tm // C):
        rc = slice(c * C, (c + 1) * C)
        bcum = _dot_split_rhs(tri_ref[...], log_a[rc], 3)
        mid = bcum[C // 2 - 1:C // 2, :]
        last = bcum[C - 1:C, :]
        q_m = _stack_heads(q[rc] * jnp.exp(bcum - mid), GLA_DK).astype(BF16)
        k_m = _stack_heads(k[rc] * jnp.exp(mid - bcum), GLA_DK).astype(BF16)
        k_end = k[rc] * jnp.exp(last - bcum)
        att = jnp.where(incl, _dot_nt(q_m, k_m), 0.0)
        v_st = _stack_heads(v[rc], HEAD_DIM).astype(BF16)
        chunks.append(dict(
            o_intra=_unstack_heads(jnp.dot(att.astype(BF16), v_st, preferred_element_type=F32), C),
            q_in=(q[rc] * jnp.exp(bcum)).astype(BF16), decay=jnp.exp(last),
            upd=jnp.where(kv_mask, _dot_tn(v[rc].astype(BF16), k_end.astype(BF16)), 0.0)))
    outs = []
    state = state_ref[...]
    for ch in chunks:
        outs.append(ch['o_intra'] + _dot_nt(ch['q_in'], state.astype(BF16)))
        state = state * ch['decay'] + ch['upd']
    state_ref[...] = state
    o = jnp.concatenate(outs, axis=0)
    ms = _dot_split_lhs(o * o, ones_ref[...], 2) * (1.0 / HEAD_DIM)
    o = o * lax.rsqrt(ms + NORM_EPS) * norm_ref[...]
    o_ref[0] = (o * (og * jax.nn.sigmoid(og))).astype(BF16)


def _gla(gl, a2p, ab, norm4, tm=256):
    B, S, W = gl.shape
    C = GLA_CHUNK
    tri, ones = _tri_ones(C), _head_ones(HEAD_DIM)
    tile = lambda w: pl.BlockSpec((1, tm, w), lambda b, s: (b, s, 0))
    full = lambda a: pl.BlockSpec(a.shape, lambda b, s: (0,) * a.ndim)
    return pl.pallas_call(
        functools.partial(_gla_kernel, C=C),
        grid=(B, S // tm),
        in_specs=[tile(W), full(a2p), full(ab), full(norm4), full(tri), full(ones)],
        out_specs=tile(256),
        out_shape=jax.ShapeDtypeStruct((B, S, 256), BF16),
        scratch_shapes=[pltpu.VMEM((256, 128), F32)],
        compiler_params=_cparams(("parallel", "arbitrary")),
        name="gla",
    )(gl, a2p, ab, norm4, tri, ones)


_RV_W0, _RV_A0, _RV_KK, _RV_KA, _RV_RK, _RV_LN, _RV_V0 = range(7)


def _rwkv_kernel(*refs, has_vres, C):
    if has_vres:
        (rw_ref, h_ref, vf_ref, wz_ref, vec_ref, v1_ref, v2_ref, tri_ref, ones_ref, o_ref, state_ref) = refs
    else:
        (rw_ref, wz_ref, vec_ref, tri_ref, ones_ref, o_ref, vout_ref, state_ref) = refs
    s = pl.program_id(1)

    @pl.when(s == 0)
    def _():
        state_ref[...] = jnp.zeros_like(state_ref)

    rw = rw_ref[0]
    tm = rw.shape[0]
    r, k, v, z = rw[:, 0:256], rw[:, 256:512], rw[:, 512:768], rw[:, 768:896]
    vec = lambda i: vec_ref[i:i + 1, :]
    zl = lax.broadcasted_iota(jnp.int32, z.shape, 1)
    zf = jnp.where(zl < 32, jnp.tanh(z), jnp.where(zl < 64, z, jax.nn.sigmoid(z)))
    zz = jnp.dot(zf.astype(BF16), wz_ref[...], preferred_element_type=F32)
    w_log = -jax.nn.softplus(-(vec(_RV_W0) + zz[:, 0:256])) - 0.5
    lw = -jnp.exp(w_log)
    a_sig = jax.nn.sigmoid(vec(_RV_A0) + zz[:, 256:512])
    gate = zz[:, 512:768]
    if has_vres:
        low = jnp.dot(h_ref[0], v1_ref[...], preferred_element_type=F32)
        logit = vec(_RV_V0) + jnp.dot(low.astype(BF16), v2_ref[...], preferred_element_type=F32)
        v = v + (vf_ref[0] - v) * jax.nn.sigmoid(logit)
    else:
        vout_ref[0] = v
    kk = k * vec(_RV_KK)
    norm = jnp.sqrt(_dot_split_lhs(kk * kk, ones_ref[...], 2))
    kk = kk / jnp.maximum(norm, 1e-12)
    k = k * (1.0 + (a_sig - 1.0) * vec(_RV_KA))
    a_vec = -kk
    b_vec = kk * a_sig

    strict, incl = _block_masks(C)
    st = lambda t: _stack_heads(t, HEAD_DIM)
    bdot = lambda p, q: jnp.dot(p.astype(BF16), q.astype(BF16), preferred_element_type=F32)
    state_mask = (lax.broadcasted_iota(jnp.int32, (256, 256), 0) // HEAD_DIM
                  == lax.broadcasted_iota(jnp.int32, (256, 256), 1) // HEAD_DIM)
    n4 = 4 * C
    chunks = []
    for c in range(tm // C):
        rc = slice(c * C, (c + 1) * C)
        lcum = _dot_split_rhs(tri_ref[...], lw[rc], 3)
        lex = lcum - lw[rc]
        mid = lcum[C // 2 - 1:C // 2, :]
        last = lcum[C - 1:C, :]
        e_mid = jnp.exp(mid - lcum)
        e_end = jnp.exp(last - lcum)
        left = jnp.concatenate([st(a_vec[rc] * jnp.exp(lex - mid)), st(r[rc] * jnp.exp(lcum - mid))], axis=0)
        right = jnp.concatenate([st(b_vec[rc] * e_mid), st(k[rc] * e_mid)], axis=0)
        g = _dot_nt(left.astype(BF16), right.astype(BF16))
        v_st = st(v[rc])
        n_ak = jnp.where(strict, g[0:n4, n4:2 * n4], 0.0)
        chunks.append(dict(
            p=jnp.where(strict, g[0:n4, 0:n4], 0.0),
            x=jnp.concatenate([st(a_vec[rc] * jnp.exp(lex)), bdot(n_ak, v_st)], axis=1),
            n_rb=jnp.where(incl, g[n4:2 * n4, 0:n4], 0.0),
            y_rk=bdot(jnp.where(incl, g[n4:2 * n4, n4:2 * n4], 0.0), v_st),
            r_abs=r[rc] * jnp.exp(lcum), decay=jnp.exp(last), v=v[rc],
            bk_end=jnp.concatenate([b_vec[rc] * e_end, k[rc] * e_end], axis=0).astype(BF16)))
    eye = (lax.broadcasted_iota(jnp.int32, (n4, n4), 0) == lax.broadcasted_iota(jnp.int32, (n4, n4), 1)).astype(F32)
    for ch in chunks:
        ch['t'] = eye + ch['p']
    for f in range(1, int(np.log2(C))):
        for ch in chunks:
            ch['p'] = bdot(ch['p'], ch['p'])
        for ch in chunks:
            ch['t'] = ch['t'] + bdot(ch['p'], ch['t'])
    for ch in chunks:
        ch['x'] = bdot(ch['t'], ch['x'])
        y_part = bdot(ch['n_rb'], ch['x'])
        a_eff = _unstack_heads(ch['x'][:, 0:256], C)
        u0 = _unstack_heads(ch['x'][:, 256:512], C)
        ch['r_eff'] = (ch['r_abs'] + _unstack_heads(y_part[:, 0:256], C)).astype(BF16)
        ch['y0'] = _unstack_heads(y_part[:, 256:512] + ch['y_rk'], C)
        ch['mix'] = jnp.where(state_mask, _dot_tn(a_eff.astype(BF16), ch['bk_end'][0:C]), 0.0).astype(BF16)
        ch['add'] = jnp.where(state_mask, _dot_tn(jnp.concatenate([u0, ch['v']], axis=0).astype(BF16), ch['bk_end']), 0.0)
    ys = []
    state = state_ref[...]
    for ch in chunks:
        sb = state.astype(BF16)
        ys.append(_dot_nt(ch['r_eff'], sb) + ch['y0'])
        state = state * ch['decay'] + jnp.dot(sb, ch['mix'], preferred_element_type=F32) + ch['add']
    state_ref[...] = state
    y = jnp.concatenate(ys, axis=0)
    inv = 1.0 / HEAD_DIM
    mean = _dot_split_lhs(y, ones_ref[...], 2) * inv
    yc = y - mean
    var = _dot_split_lhs(yc * yc, ones_ref[...], 2) * inv
    y = yc * lax.rsqrt(var + RWKV_GN_EPS) * vec(_RV_LN)
    y = y + _dot_split_lhs(r * k * vec(_RV_RK), ones_ref[...], 2) * v
    o_ref[0] = (y * gate).astype(BF16)


def _rwkv(rw, wz, vecs, h=None, v_first=None, v1p=None, v2p=None, tm=256):
    B, S, W = rw.shape
    C = RWKV_CHUNK
    has_vres = h is not None
    tri, ones = _tri_ones(C), _head_ones(HEAD_DIM)
    tile = lambda w: pl.BlockSpec((1, tm, w), lambda b, s: (b, s, 0))
    full = lambda a: pl.BlockSpec(a.shape, lambda b, s: (0,) * a.ndim)
    if has_vres:
        args = (rw, h, v_first, wz, vecs, v1p, v2p, tri, ones)
        in_specs = [tile(W), tile(D_MODEL), tile(256), full(wz), full(vecs), full(v1p), full(v2p), full(tri), full(ones)]
        out_specs = tile(256)
        out_shape = jax.ShapeDtypeStruct((B, S, 256), BF16)
    else:
        args = (rw, wz, vecs, tri, ones)
        in_specs = [tile(W), full(wz), full(vecs), full(tri), full(ones)]
        out_specs = [tile(256), tile(256)]
        out_shape = [jax.ShapeDtypeStruct((B, S, 256), BF16), jax.ShapeDtypeStruct((B, S, 256), F32)]
    return pl.pallas_call(
        functools.partial(_rwkv_kernel, has_vres=has_vres, C=C),
        grid=(B, S // tm),
        in_specs=in_specs,
        out_specs=out_specs,
        out_shape=out_shape,
        scratch_shapes=[pltpu.VMEM((256, 256), F32)],
        compiler_params=_cparams(("parallel", "arbitrary")),
        name="rwkv_vres" if has_vres else "rwkv",
    )(*args)


def _rwkv_weights(l, rwkv_w0, rwkv_w2, rwkv_a0, rwkv_a2, rwkv_g2, rwkv_v0, rwkv_k_k, rwkv_k_a, rwkv_r_k, rwkv_ln):
    wz = jnp.zeros((128, 768), F32)
    wz = wz.at[0:32, 0:256].set(rwkv_w2[l]).at[32:64, 256:512].set(rwkv_a2[l]).at[64:128, 512:768].set(rwkv_g2[l])
    v0 = rwkv_v0[l - 1] if l > 0 else jnp.zeros((256,), F32)
    vecs = jnp.stack([rwkv_w0[l], rwkv_a0[l], rwkv_k_k[l], rwkv_k_a[l], rwkv_r_k[l].reshape(-1), rwkv_ln[l], v0,
                      jnp.zeros((256,), F32)])
    return wz.astype(BF16), vecs


def _merge_kernel(x_ref, h_ref, o0_ref, o1_ref, o2_ref, o3_ref, wg_ref, bg_ref, wb_ref, wo_ref, out_ref):
    h = h_ref[0]
    merged = None
    for i, o_ref in enumerate((o0_ref, o1_ref, o2_ref, o3_ref)):
        gate = jax.nn.sigmoid(jnp.dot(h, wg_ref[i], preferred_element_type=F32) + bg_ref[i:i + 1, :])
        term = gate * jnp.dot(o_ref[0], wb_ref[i], preferred_element_type=F32)
        merged = term if merged is None else merged + term
    out_ref[0] = x_ref[0] + jnp.dot(merged.astype(BF16), wo_ref[...], preferred_element_type=F32)


def _merge(x, h, o_nsa, o_pool, o_rwkv, o_gla, wg, bg, wb, wo, tm=512):
    B, S, D = x.shape
    tile = lambda w: pl.BlockSpec((1, tm, w), lambda b, s: (b, s, 0))
    full = lambda a: pl.BlockSpec(a.shape, lambda b, s: (0,) * a.ndim)
    return pl.pallas_call(
        _merge_kernel,
        grid=(B, S // tm),
        in_specs=[tile(D), tile(D), tile(256), tile(256), tile(256), tile(256), full(wg), full(bg), full(wb), full(wo)],
        out_specs=tile(D),
        out_shape=jax.ShapeDtypeStruct((B, S, D), F32),
        compiler_params=_cparams(("parallel", "parallel")),
        name="merge",
    )(x, h, o_nsa, o_pool, o_rwkv, o_gla, wg, bg, wb, wo)


def _memkv_kernel(mem_ref, g_ref, w_ref, o_ref):
    mn = _rms(mem_ref[0], g_ref[...]).astype(BF16)
    o_ref[0] = jnp.dot(mn, w_ref[...], preferred_element_type=F32).astype(BF16)


def _memkv(mem, g, wkv_all):
    B, M, D = mem.shape
    N = wkv_all.shape[1]
    return pl.pallas_call(
        _memkv_kernel,
        grid=(B,),
        in_specs=[pl.BlockSpec((1, M, D), lambda b: (b, 0, 0)), pl.BlockSpec(g.shape, lambda b: (0, 0)),
                  pl.BlockSpec(wkv_all.shape, lambda b: (0, 0))],
        out_specs=pl.BlockSpec((1, M, N), lambda b: (b, 0, 0)),
        out_shape=jax.ShapeDtypeStruct((B, M, N), BF16),
        compiler_params=_cparams(("parallel",)),
        name="memkv",
    )(mem, g, wkv_all)


def _cross_kernel(x_ref, g_ref, wq_ref, k_ref, v_ref, wo_ref, out_ref):
    x = x_ref[0]
    hc = _rms(x, g_ref[...]).astype(BF16)
    q = jnp.dot(hc, wq_ref[...], preferred_element_type=F32) * (HEAD_DIM ** -0.5)
    head = lax.broadcasted_iota(jnp.int32, q.shape, 1) // HEAD_DIM
    k, v = k_ref[0], v_ref[0]
    o = jnp.zeros(q.shape, F32)
    for hh in range(CA_WIDTH // HEAD_DIM):
        qh = jnp.where(head == hh, q, 0.0).astype(BF16)
        s = _dot_nt(qh, k)
        e = jnp.exp(s - jnp.max(s, axis=1, keepdims=True))
        p = e / jnp.sum(e, axis=1, keepdims=True)
        o = jnp.where(head == hh, jnp.dot(p.astype(BF16), v, preferred_element_type=F32), o)
    out_ref[0] = x + jnp.dot(o.astype(BF16), wo_ref[...], preferred_element_type=F32)


def _cross(x, g, wq, memkv, l, wo, tm=512):
    B, S, D = x.shape
    M = memkv.shape[1]
    tile = pl.BlockSpec((1, tm, D), lambda b, s: (b, s, 0))
    full = lambda a: pl.BlockSpec(a.shape, lambda b, s: (0,) * a.ndim)
    kspec = pl.BlockSpec((1, M, CA_WIDTH), lambda b, s: (b, 0, 2 * l))
    vspec = pl.BlockSpec((1, M, CA_WIDTH), lambda b, s: (b, 0, 2 * l + 1))
    return pl.pallas_call(
        _cross_kernel,
        grid=(B, S // tm),
        in_specs=[tile, full(g), full(wq), kspec, vspec, full(wo)],
        out_specs=tile,
        out_shape=jax.ShapeDtypeStruct((B, S, D), F32),
        compiler_params=_cparams(("parallel", "parallel")),
        name="cross_attn",
    )(x, g, wq, memkv, memkv, wo)


def _swiglu_cols(xb, wg, wu, wd, tf):
    acc = jnp.zeros((xb.shape[0], wd.shape[-1]), F32)
    for c in range(wg.shape[-1] // tf):
        cols = slice(c * tf, (c + 1) * tf)
        gate = jnp.dot(xb, wg[:, cols], preferred_element_type=F32)
        up = jnp.dot(xb, wu[:, cols], preferred_element_type=F32)
        act = gate * jax.nn.sigmoid(gate) * up
        acc = acc + jnp.dot(act.astype(BF16), wd[cols, :], preferred_element_type=F32)
    return acc


def _ffn_kernel(x_ref, g_ref, wg_ref, wu_ref, wd_ref, out_ref, *, tf):
    x = x_ref[...]
    out_ref[...] = x + _swiglu_cols(_rms(x, g_ref[...]).astype(BF16), wg_ref, wu_ref, wd_ref, tf)


def _ffn(x2, g, wg, wu, wd, tm=512, tf=256):
    T, D = x2.shape
    full = lambda a: pl.BlockSpec(a.shape, lambda i: (0,) * a.ndim)
    return pl.pallas_call(
        functools.partial(_ffn_kernel, tf=tf),
        grid=(T // tm,),
        in_specs=[pl.BlockSpec((tm, D), lambda i: (i, 0)), full(g), full(wg), full(wu), full(wd)],
        out_specs=pl.BlockSpec((tm, D), lambda i: (i, 0)),
        out_shape=jax.ShapeDtypeStruct((T, D), F32),
        compiler_params=_cparams(("parallel",)),
        name="ffn",
    )(x2, g, wg, wu, wd)


MOE_TM = 512
MOE_TD = 512


def _route_kernel(x_ref, g_ref, r_ref, hf_ref, idx_ref, w_ref):
    hf = _rms(x_ref[...], g_ref[...])
    hf_ref[...] = hf
    logits = jnp.dot(hf, r_ref[...], preferred_element_type=F32, precision=lax.Precision.HIGHEST)
    lane = lax.broadcasted_iota(jnp.int32, logits.shape, 1)
    big = logits.shape[1]
    logits = jnp.where(lane < N_EXPERTS, logits, -jnp.inf)
    m1 = jnp.max(logits, axis=1, keepdims=True)
    i1 = jnp.min(jnp.where(logits == m1, lane, big), axis=1, keepdims=True)
    rest = jnp.where(lane == i1, -jnp.inf, logits)
    m2 = jnp.max(rest, axis=1, keepdims=True)
    i2 = jnp.min(jnp.where(rest == m2, lane, big), axis=1, keepdims=True)
    e2 = jnp.exp(m2 - m1)
    w1 = 1.0 / (1.0 + e2)
    idx_ref[...] = jnp.where(lane == 0, i1, jnp.where(lane == 1, i2, 0))
    w_ref[...] = jnp.where(lane == 0, w1, jnp.where(lane == 1, e2 * w1, 0.0))


def _moe_route(x2, g, router_p, tm=1024):
    T, D = x2.shape
    row = lambda w: pl.BlockSpec((tm, w), lambda i: (i, 0))
    full = lambda a: pl.BlockSpec(a.shape, lambda i: (0,) * a.ndim)
    return pl.pallas_call(
        _route_kernel,
        grid=(T // tm,),
        in_specs=[row(D), full(g), full(router_p)],
        out_specs=[row(D), row(LANES), row(LANES)],
        out_shape=[jax.ShapeDtypeStruct((T, D), F32), jax.ShapeDtypeStruct((T, LANES), jnp.int32),
                   jax.ShapeDtypeStruct((T, LANES), F32)],
        compiler_params=_cparams(("parallel",)),
        name="moe_route",
    )(x2, g, router_p)


def _moe_plan(expert_of_pair, tm):
    onehot = (expert_of_pair[:, None] == jnp.arange(N_EXPERTS, dtype=jnp.int32)[None, :]).astype(jnp.int32)
    csum = jnp.cumsum(onehot, axis=0)
    rank = jnp.sum((csum - 1) * onehot, axis=1)
    counts = csum[-1]
    padded = ((counts + tm - 1) // tm) * tm
    ends = jnp.cumsum(padded)
    dest = ((ends - padded)[expert_of_pair] + rank).astype(jnp.int32)
    n_rows = expert_of_pair.shape[0] + N_EXPERTS * tm
    tile_expert = jnp.sum((jnp.arange(n_rows // tm, dtype=jnp.int32)[:, None] * tm >= ends[None, :]).astype(jnp.int32), axis=1)
    token_of_row = jnp.zeros((n_rows,), jnp.int32).at[dest].set(jnp.arange(expert_of_pair.shape[0], dtype=jnp.int32) // 2)
    return dest, token_of_row, jnp.minimum(tile_expert, N_EXPERTS - 1).astype(jnp.int32)


DMA_UNROLL = 8


def _gather_rows(table_hbm, row_of, dst, n, sem, wait):
    def body(c, carry):
        for u in range(DMA_UNROLL):
            r = c * DMA_UNROLL + u
            copy = pltpu.make_async_copy(table_hbm.at[pl.ds(row_of(r), 1)], dst.at[pl.ds(r, 1)], sem)
            copy.wait() if wait else copy.start()
        return carry

    lax.fori_loop(0, n // DMA_UNROLL, body, 0)


def _experts_kernel(te_ref, tok_cur_ref, tok_next_ref, hf_hbm, wg_ref, wu_ref, wd_ref, ys_ref, xin, sems, *, tf):
    del te_ref
    i = pl.program_id(0)
    slot = i % 2
    tm = xin.shape[1]
    n_f = wg_ref.shape[2] // tf

    def row_copy(tok_ref, r, s):
        return pltpu.make_async_copy(hf_hbm.at[pl.ds(tok_ref[0, 0, r], 1)], xin.at[s, pl.ds(r, 1)], sems.at[s])

    @pl.when(i == 0)
    def _():
        _gather_rows(hf_hbm, lambda r: tok_cur_ref[0, 0, r], xin.at[0], tm, sems.at[0], wait=False)

    _gather_rows(hf_hbm, lambda r: tok_cur_ref[0, 0, r], xin.at[slot], tm, sems.at[slot], wait=True)
    xb = xin[slot].astype(BF16)
    acc = jnp.zeros(ys_ref.shape, F32)
    bounds = [round(c * tm / n_f) for c in range(n_f + 1)]
    for c in range(n_f):
        for r in range(bounds[c], bounds[c + 1]):
            row_copy(tok_next_ref, r, 1 - slot).start()
        cols = slice(c * tf, (c + 1) * tf)
        gate = jnp.dot(xb, wg_ref[0, :, cols], preferred_element_type=F32)
        up = jnp.dot(xb, wu_ref[0, :, cols], preferred_element_type=F32)
        act = gate * jax.nn.sigmoid(gate) * up
        acc = acc + jnp.dot(act.astype(BF16), wd_ref[0, cols, :], preferred_element_type=F32)
    ys_ref[...] = acc

    @pl.when(i == pl.num_programs(0) - 1)
    def _():
        _gather_rows(hf_hbm, lambda r: tok_next_ref[0, 0, r], xin.at[1 - slot], tm, sems.at[1 - slot], wait=True)


def _moe_experts(hf, token_of_row, tile_expert, wg, wu, wd, tf=256):
    T, D = hf.shape
    tm = MOE_TM
    n_tiles = token_of_row.shape[0] // tm
    tok = token_of_row.reshape(n_tiles, 1, tm)
    tok = jnp.concatenate([tok, tok[-1:]], axis=0)
    smem = lambda off: pl.BlockSpec((1, 1, tm), lambda i, te: (i + off, 0, 0), memory_space=pltpu.SMEM)
    expert = lambda shape: pl.BlockSpec((1,) + shape, lambda i, te: (te[i], 0, 0))
    grid_spec = pltpu.PrefetchScalarGridSpec(
        num_scalar_prefetch=1,
        grid=(n_tiles,),
        in_specs=[smem(0), smem(1), pl.BlockSpec(memory_space=pl.ANY),
                  expert(wg.shape[1:]), expert(wu.shape[1:]), expert(wd.shape[1:])],
        out_specs=pl.BlockSpec((tm, D), lambda i, te: (i, 0)),
        scratch_shapes=[pltpu.VMEM((2, tm, D), F32), pltpu.SemaphoreType.DMA((2,))],
    )
    return pl.pallas_call(
        functools.partial(_experts_kernel, tf=tf),
        grid_spec=grid_spec,
        out_shape=jax.ShapeDtypeStruct((n_tiles * tm, D), F32),
        compiler_params=_cparams(("arbitrary",)),
        name="moe_experts",
    )(tile_expert, tok, tok, hf, wg, wu, wd)


def _combine_kernel(dest_ref, x_ref, w_ref, gf_ref, ys_hbm, out_ref, y0_buf, y1_buf, sem0, sem1):
    slots = ((lambda r: dest_ref[0, 0, r], y0_buf, sem0), (lambda r: dest_ref[0, 1, r], y1_buf, sem1))
    for row_of, buf, sem in slots:
        _gather_rows(ys_hbm, row_of, buf, MOE_TD, sem, wait=False)
    for row_of, buf, sem in slots:
        _gather_rows(ys_hbm, row_of, buf, MOE_TD, sem, wait=True)
    w = w_ref[...]
    y = x_ref[...] + w[:, 0:1] * y0_buf[...] + w[:, 1:2] * y1_buf[...]
    out_ref[...] = _rms(y, gf_ref[...])


def _moe_combine(x2, w, dest, ys, g_final):
    T, D = x2.shape
    steps = T // MOE_TD
    dest3 = dest.reshape(steps, MOE_TD, 2).transpose(0, 2, 1)
    row = lambda wd: pl.BlockSpec((MOE_TD, wd), lambda i: (i, 0))
    return pl.pallas_call(
        _combine_kernel,
        grid=(steps,),
        in_specs=[pl.BlockSpec((1, 2, MOE_TD), lambda i: (i, 0, 0), memory_space=pltpu.SMEM), row(D), row(LANES),
                  pl.BlockSpec(g_final.shape, lambda i: (0, 0)), pl.BlockSpec(memory_space=pl.ANY)],
        out_specs=row(D),
        out_shape=jax.ShapeDtypeStruct((T, D), F32),
        scratch_shapes=[pltpu.VMEM((MOE_TD, D), F32), pltpu.VMEM((MOE_TD, D), F32),
                        pltpu.SemaphoreType.DMA(()), pltpu.SemaphoreType.DMA(())],
        compiler_params=_cparams(("arbitrary",)),
        name="moe_combine",
    )(dest3, x2, w, g_final, ys)


def _moe_routed(x2, g, router_p, wg, wu, wd, g_final):
    T, D = x2.shape
    hf, idx, w = _moe_route(x2, g, router_p)
    dest, token_of_row, tile_expert = _moe_plan(idx[:, 0:2].reshape(2 * T), MOE_TM)
    ys = _moe_experts(hf, token_of_row, tile_expert, wg, wu, wd)
    return _moe_combine(x2, w, dest, ys, g_final)


def kernel(x, mem, positions, norm_mix, w_in, nsa_cmp_pe, nsa_cmp_w1, nsa_cmp_w2, pool_w, pool_scale, rwkv_mu, rwkv_w0,
           rwkv_w2, rwkv_a0, rwkv_a2, rwkv_g2, rwkv_v0, rwkv_v1, rwkv_v2, rwkv_k_k, rwkv_k_a, rwkv_r_k, rwkv_ln, gla_a2,
           gla_ab, gla_norm, w_gate, b_gate, w_branch, w_out, norm_ca, mem_norm, w_ca_q, w_ca_kv, w_ca_o, norm_ffn,
           ffn_w_gate, ffn_w_up, ffn_w_down, moe_router, moe_w_gate, moe_w_up, moe_w_down, norm_final):
    B, S, D = x.shape
    depth = norm_mix.shape[0]
    assert depth == 2 and S % 256 == 0, "the final norm is fused into the layer-1 expert mixer"
    cosf, sinf = _rope_tables(positions)
    at_cmp_end = lambda t: jnp.pad(t[:, CMP_LEN - 1::CMP_STRIDE], ((0, 0), (0, 1), (0, 0)))
    cosc, sinc = at_cmp_end(cosf), at_cmp_end(sinf)
    memkv = _memkv(mem, mem_norm[None], jnp.concatenate([w_ca_kv[l] for l in range(depth)], axis=1).astype(BF16))
    v_first = None
    for l in range(depth):
        h, q_t, k2, v_t, cv, gates_t, u, rw, gl = _inproj(x, norm_mix[l][None], _inproj_weights(w_in, l), cosf, sinf,
                                                          rwkv_mu[l][None])
        kcmp, vcmp_t = _nsa_compress(cv, *_compress_weights(nsa_cmp_pe[l], nsa_cmp_w1[l], nsa_cmp_w2[l]), cosc, sinc)
        o_nsa = _nsa_attend(q_t, k2, v_t, kcmp, vcmp_t, gates_t)
        o_pool = _pool(u, _block_diag(pool_w[l]).astype(BF16), pool_scale[l][None])
        wz, vecs = _rwkv_weights(l, rwkv_w0, rwkv_w2, rwkv_a0, rwkv_a2, rwkv_g2, rwkv_v0, rwkv_k_k, rwkv_k_a,
                                 rwkv_r_k, rwkv_ln)
        if l == 0:
            o_rwkv, v_first = _rwkv(rw, wz, vecs)
        else:
            v1p = jnp.pad(rwkv_v1[l - 1], ((0, 0), (0, LANES - rwkv_v1.shape[2]))).astype(BF16)
            v2p = jnp.pad(rwkv_v2[l - 1], ((0, LANES - rwkv_v2.shape[1]), (0, 0))).astype(BF16)
            o_rwkv = _rwkv(rw, wz, vecs, h, v_first, v1p, v2p)
        a2p = jnp.pad(gla_a2[l], ((0, LANES - gla_a2.shape[1]), (0, 0))).astype(BF16)
        o_gla = _gla(gl, a2p, gla_ab[l][None], jnp.tile(gla_norm[l], 4)[None])
        wb_nsa = w_branch[l, 0].reshape(4, HEAD_DIM, D)[jnp.array([0, 2, 1, 3])].reshape(BRANCH_WIDTH, D)
        wb = jnp.concatenate([wb_nsa[None], w_branch[l, 1:]], axis=0).astype(BF16)
        x = _merge(x, h, o_nsa, o_pool, o_rwkv, o_gla, w_gate[l].astype(BF16), b_gate[l], wb, w_out[l].astype(BF16))
        x = _cross(x, norm_ca[l][None], w_ca_q[l].astype(BF16), memkv, l, w_ca_o[l].astype(BF16))
        j = l // 2
        x2 = x.reshape(B * S, D)
        if l % 2 == 0:
            x2 = _ffn(x2, norm_ffn[l][None], ffn_w_gate[j].astype(BF16), ffn_w_up[j].astype(BF16),
                      ffn_w_down[j].astype(BF16))
        else:
            router_p = jnp.pad(moe_router[j], ((0, 0), (0, LANES - N_EXPERTS)))
            x2 = _moe_routed(x2, norm_ffn[l][None], router_p, moe_w_gate[j].astype(BF16), moe_w_up[j].astype(BF16),
                             moe_w_down[j].astype(BF16), norm_final[None])
        x = x2.reshape(B, S, D)
    return x
```

```python
import functools

import jax
import jax.numpy as jnp
import numpy as np
from jax import lax
from jax.experimental import pallas as pl
from jax.experimental.pallas import tpu as pltpu

F32 = jnp.float32
BF16 = jnp.bfloat16

D_MODEL = 1024
HEAD_DIM = 64
BRANCH_WIDTH = 256
ROPE_THETA = 500000.0
ROPE_DIM = 16
NORM_EPS = 1e-6
NEG_INF = -1e30

NSA_KV = 128
CMP_LEN = 32
CMP_STRIDE = 16
CMP_HIDDEN = 128
SEL_BLOCK = 64
SEL_TOPN = 8
WINDOW = 512

RWKV_IN = 896
RWKV_GN_EPS = 64e-5
RWKV_CHUNK = 64

GLA_DK = 32
GLA_TAU = 16.0
GLA_CHUNK = 64

CA_WIDTH = 256
D_FF = 2816
N_EXPERTS = 8

LANES = 128
VMEM_LIMIT = 56 * 1024 * 1024

_C_Q = 0
_C_KV = 256
_C_CV = 768
_C_GATE = 1024
_C_POOL = 1152
_C_RWKV = 1408
_C_GLA = 2304
_C_END = 3200


def _cparams(sem):
    return pltpu.CompilerParams(dimension_semantics=sem, vmem_limit_bytes=VMEM_LIMIT)


def _rms(x, g):
    ms = jnp.mean(x * x, axis=-1, keepdims=True)
    return x * lax.rsqrt(ms + NORM_EPS) * g


def _rope128(t, cos, sin):
    lane = lax.broadcasted_iota(jnp.int32, t.shape, 1) % HEAD_DIM
    partner = jnp.where(lane < ROPE_DIM // 2, pltpu.roll(t, LANES - ROPE_DIM // 2, 1),
                        pltpu.roll(t, ROPE_DIM // 2, 1))
    return t * cos + partner * sin


def _inproj_kernel(x_ref, g_ref, w_ref, cos_ref, sin_ref, mu_ref,
                   h_ref, qt_ref, k_ref, vt_ref, cv_ref, gt_ref, u_ref, rw_ref, gl_ref, prev_ref):
    s = pl.program_id(1)
    hb = _rms(x_ref[0], g_ref[...]).astype(BF16)
    h_ref[0] = hb

    def mm(a, b):
        return jnp.dot(hb, w_ref[:, a:b], preferred_element_type=F32)

    cos = cos_ref[0]
    sin = sin_ref[0]
    scale = HEAD_DIM ** -0.5
    qt_ref[0, 0:128, :] = jnp.transpose(_rope128(mm(_C_Q, _C_Q + 128), cos, sin) * scale).astype(BF16)
    qt_ref[0, 128:256, :] = jnp.transpose(_rope128(mm(_C_Q + 128, _C_Q + 256), cos, sin) * scale).astype(BF16)
    k_ref[0, :, 0:128] = _rope128(mm(_C_KV, _C_KV + 128), cos, sin).astype(BF16)
    vt_ref[0, 0:128, :] = jnp.transpose(mm(_C_KV + 128, _C_KV + 256)).astype(BF16)
    k_ref[0, :, 128:256] = _rope128(mm(_C_KV + 256, _C_KV + 384), cos, sin).astype(BF16)
    vt_ref[0, 128:256, :] = jnp.transpose(mm(_C_KV + 384, _C_KV + 512)).astype(BF16)
    cv_ref[0] = mm(_C_CV, _C_CV + 256).astype(BF16)
    gt_ref[0] = jnp.transpose(jax.nn.sigmoid(mm(_C_GATE, _C_GATE + 128)))
    u_ref[0] = mm(_C_POOL, _C_POOL + 256)
    gl_ref[0] = mm(_C_GLA, _C_END)

    p = mm(_C_RWKV, _C_RWKV + RWKV_IN)
    tm = p.shape[0]

    @pl.when(s == 0)
    def _():
        prev_ref[...] = jnp.zeros_like(prev_ref)

    row = lax.broadcasted_iota(jnp.int32, p.shape, 0)
    shifted = jnp.where(row == 0, prev_ref[0:1, :], pltpu.roll(p, 1, 0))
    prev_ref[0:1, :] = p[tm - 1:tm, :]
    rw_ref[0] = p + (shifted - p) * mu_ref[...]


def _inproj(x, g, w_all, cosf, sinf, mu, tm=512):
    B, S, D = x.shape
    tok = lambda w: pl.BlockSpec((1, tm, w), lambda b, s: (b, s, 0))
    tok_t = lambda w: pl.BlockSpec((1, w, tm), lambda b, s: (b, 0, s))
    full = lambda a: pl.BlockSpec(a.shape, lambda b, s: (0,) * a.ndim)
    outs = [(D, BF16, True), (256, BF16, False), (256, BF16, True), (256, BF16, False), (256, BF16, True),
            (128, F32, False), (256, F32, True), (RWKV_IN, F32, True), (896, F32, True)]
    return pl.pallas_call(
        _inproj_kernel,
        grid=(B, S // tm),
        in_specs=[tok(D), full(g), full(w_all), tok(128), tok(128), full(mu)],
        out_specs=[tok(w) if major else tok_t(w) for w, _, major in outs],
        out_shape=[jax.ShapeDtypeStruct((B, S, w) if major else (B, w, S), dt) for w, dt, major in outs],
        scratch_shapes=[pltpu.VMEM((8, RWKV_IN), F32)],
        compiler_params=_cparams(("parallel", "arbitrary")),
        name="inproj",
    )(x, g, w_all, cosf, sinf, mu)


def _inproj_columns():
    span = lambda lo, n: list(range(lo, lo + n))
    pad = lambda n: [-1] * n
    nsa, pool, rwkv, gla = 0, 1036, 1292, 2188
    cols = (span(nsa, 64) + span(nsa + 128, 64) + span(nsa + 64, 64) + span(nsa + 192, 64)
            + span(nsa + 512, 512)
            + span(nsa + 256, 256)
            + span(nsa + 1024, 12) + pad(116)
            + span(pool, 256) + span(rwkv, RWKV_IN)
            + span(gla, 512) + span(gla + 528, 256) + span(gla + 512, 16) + pad(112))
    assert len(cols) == _C_END
    return np.asarray(cols, np.int32)


def _inproj_weights(w):
    onehot = (jnp.arange(w.shape[1], dtype=jnp.int32)[:, None] == jnp.asarray(_inproj_columns())[None, :]).astype(BF16)
    return jnp.dot(w.astype(BF16), onehot, preferred_element_type=F32).astype(BF16)


def _rope_tables(positions):
    half = ROPE_DIM // 2
    inv_freq = ROPE_THETA ** (-jnp.arange(0, ROPE_DIM, 2, dtype=F32) / ROPE_DIM)
    ang = positions.astype(F32)[..., None] * inv_freq
    cos, sin = jnp.cos(ang), jnp.sin(ang)
    ones = jnp.ones(cos.shape[:-1] + (HEAD_DIM - ROPE_DIM,), F32)
    cos64 = jnp.concatenate([cos, cos, ones], axis=-1)
    sin64 = jnp.concatenate([-sin, sin, 0.0 * ones], axis=-1)
    return jnp.concatenate([cos64, cos64], axis=-1), jnp.concatenate([sin64, sin64], axis=-1)


def _gelu_tanh(x):
    return x * (0.5 * (1.0 + jnp.tanh(np.sqrt(2.0 / np.pi) * (x + 0.044715 * (x * x * x)))))


def _dot_nt(a, b):
    return lax.dot_general(a, b, (((1,), (1,)), ((), ())), preferred_element_type=F32)


def _compress_kernel(x_ref, wa_ref, wb_ref, bias_ref, w2k_ref, w2v_ref, cos_ref, sin_ref, kc_ref, vc_ref):
    x = x_ref[0]
    a = jnp.dot(x, wa_ref[...], preferred_element_type=F32)
    b = jnp.dot(x, wb_ref[...], preferred_element_type=F32)
    n = a.shape[0]
    hid = _gelu_tanh(a + pltpu.roll(b, n - 1, 0) + bias_ref[...])
    kc = jnp.dot(hid[:, 0:256].astype(BF16), w2k_ref[...], preferred_element_type=F32)
    vc = jnp.dot(hid[:, 256:512].astype(BF16), w2v_ref[...], preferred_element_type=F32)
    kc_ref[0] = _rope128(kc, cos_ref[0], sin_ref[0]).astype(BF16)
    vc_ref[0] = jnp.transpose(vc).astype(BF16)


def _nsa_compress(cv, wa, wb, bias, w2k, w2v, cosc, sinc):
    B, S, _ = cv.shape
    nseg = S // CMP_STRIDE
    x = cv.reshape(B, nseg, CMP_STRIDE * 256)
    per_b = lambda r, w: pl.BlockSpec((1, r, w), lambda b: (b, 0, 0))
    full = lambda a: pl.BlockSpec(a.shape, lambda b: (0,) * a.ndim)
    return pl.pallas_call(
        _compress_kernel,
        grid=(B,),
        in_specs=[per_b(nseg, CMP_STRIDE * 256), full(wa), full(wb), full(bias), full(w2k), full(w2v),
                  per_b(nseg, 128), per_b(nseg, 128)],
        out_specs=[per_b(nseg, 128), per_b(nseg, 128)],
        out_shape=[jax.ShapeDtypeStruct((B, nseg, 128), BF16)] * 2,
        compiler_params=_cparams(("parallel",)),
        name="nsa_compress",
    )(x, wa, wb, bias, w2k, w2v, cosc, sinc)


def _compress_weights(pe, w1, w2):
    eye2 = jnp.eye(2, dtype=F32)
    half = lambda lo: jnp.einsum('jldf,pj,qg->lpqdjgf', w1[:, lo:lo + CMP_STRIDE], eye2, eye2).reshape(
        CMP_STRIDE * 256, 4 * CMP_HIDDEN).astype(BF16)
    bias = jnp.einsum('jld,jldf->jf', pe, w1)
    bias = jnp.broadcast_to(bias[:, None, :], (2, 2, CMP_HIDDEN)).reshape(1, 4 * CMP_HIDDEN)
    bd = lambda w: jnp.einsum('fd,gq->gfqd', w, eye2).reshape(2 * CMP_HIDDEN, 2 * HEAD_DIM).astype(BF16)
    return half(0), half(CMP_STRIDE), bias, bd(w2[0]), bd(w2[1])


def _cover_t(S):
    n_cmp = (S - CMP_LEN) // CMP_STRIDE + 1
    cmp_start = np.arange(n_cmp) * CMP_STRIDE
    slc_start = np.arange(S // SEL_BLOCK) * SEL_BLOCK
    cover = np.clip(np.minimum(cmp_start[:, None] + CMP_LEN, slc_start[None, :] + SEL_BLOCK)
                    - np.maximum(cmp_start[:, None], slc_start[None, :]), 0, None) / CMP_LEN
    out = np.zeros((S // SEL_BLOCK, S // CMP_STRIDE), np.float32)
    out[:, :n_cmp] = cover.T
    return jnp.asarray(out, BF16)


MASKED = 2.0 * NEG_INF


def _softmax_step(s, carry, v_t):
    m, l, acc = carry
    m_new = jnp.maximum(m, jnp.max(s, axis=0, keepdims=True))
    alpha = jnp.exp(m - m_new)
    p = jnp.exp(s - m_new)
    l = alpha * l + jnp.sum(p, axis=0, keepdims=True)
    acc = alpha * acc + jnp.dot(v_t, p.astype(BF16), preferred_element_type=F32)
    return m_new, l, acc


def _softmax_init(n_q):
    return (jnp.full((1, n_q), NEG_INF, F32), jnp.zeros((1, n_q), F32), jnp.zeros((LANES, n_q), F32))


def _nsa_kernel(qt_ref, k_ref, vt_ref, kc_ref, vct_ref, gt_ref, covt_ref, o_ref, *, tq, tk):
    i = pl.program_id(1)
    t0 = i * tq
    n_blk = covt_ref.shape[0]
    n_q = 4 * tq
    qa_t, qb_t = qt_ref[0, 0:128, :], qt_ref[0, 128:256, :]
    sub_grp = lax.broadcasted_iota(jnp.int32, (LANES, tq), 0) // HEAD_DIM
    gates_t = gt_ref[0]
    tok = t0 + (lax.broadcasted_iota(jnp.int32, (1, n_q), 1) & (tq - 1))
    q_all = jnp.concatenate([jnp.where(sub_grp == g, q_t, 0) for g in range(2) for q_t in (qa_t, qb_t)], axis=1)

    s = jnp.dot(kc_ref[0], q_all, preferred_element_type=F32)
    nrow = lax.broadcasted_iota(jnp.int32, s.shape, 0)
    s = jnp.where(nrow * CMP_STRIDE + (CMP_LEN - 1) <= tok, s, MASKED)
    e = jnp.exp(s - jnp.maximum(jnp.max(s, axis=0, keepdims=True), NEG_INF))
    den = jnp.sum(e, axis=0, keepdims=True)
    p = e / jnp.where(den > 0.0, den, 1.0)
    o_cmp = jnp.dot(vct_ref[0], p.astype(BF16), preferred_element_type=F32)

    p_grp = jnp.concatenate([p[:, 2 * g * tq:(2 * g + 1) * tq] + p[:, (2 * g + 1) * tq:(2 * g + 2) * tq]
                             for g in range(2)], axis=1)
    imp = _dot_split_rhs(covt_ref[...], p_grp, 2)
    blk = lax.broadcasted_iota(jnp.int32, imp.shape, 0)
    tok_b = t0 + (lax.broadcasted_iota(jnp.int32, imp.shape, 1) & (tq - 1))
    cur = tok_b // SEL_BLOCK
    forced = (blk == 0) | (blk == cur) | (blk == cur - 1)
    imp = jnp.where(forced, 1e30, jnp.where(blk * SEL_BLOCK <= tok_b, imp, -1e30))
    cnt = jnp.zeros(imp.shape, F32)
    for i2 in range(n_blk):
        row = imp[i2:i2 + 1, :]
        cnt = cnt + jnp.where((row > imp) | ((row == imp) & (blk > i2)), 1.0, 0.0)
    bias = jnp.where(cnt < float(SEL_TOPN), 0.0, MASKED).astype(BF16)
    sel_bias = jnp.concatenate([bias[:, g * tq:(g + 1) * tq] for g in (0, 0, 1, 1)], axis=1)

    q_ext = jnp.concatenate([q_all, sel_bias, jnp.zeros((LANES - n_blk, n_q), BF16)], axis=0)

    def sel_scores(j):
        rows = pl.ds(pl.multiple_of(j * tk, tk), tk)
        blk_of_key = (j * tk + lax.broadcasted_iota(jnp.int32, (tk, LANES), 0)) // SEL_BLOCK
        onehot = (lax.broadcasted_iota(jnp.int32, (tk, LANES), 1) == blk_of_key).astype(BF16)
        k_ext = jnp.concatenate([k_ref[0, rows, 0:128], onehot], axis=1)
        return jnp.dot(k_ext, q_ext, preferred_element_type=F32), vt_ref[0, 0:128, rows]

    def sel_step(j, carry):
        s, v_t = sel_scores(j)
        return _softmax_step(s, carry, v_t)

    last = (t0 + tq - 1) // tk
    carry = lax.fori_loop(0, last, sel_step, _softmax_init(n_q))
    s, v_t = sel_scores(last)
    key = last * tk + lax.broadcasted_iota(jnp.int32, (tk, n_q), 0)
    _, l_sel, acc_sel = _softmax_step(jnp.where(key <= tok, s, MASKED), carry, v_t)
    o_sel = acc_sel / l_sel

    span = WINDOW + tq
    start = pl.multiple_of(jnp.maximum(t0 - WINDOW, 0), tq)
    key = start + lax.broadcasted_iota(jnp.int32, (span, n_q), 0)
    s = jnp.dot(k_ref[0, pl.ds(start, span), 128:256], q_all, preferred_element_type=F32)
    s = jnp.where((key <= tok) & (tok - key < WINDOW), s, MASKED)
    e = jnp.exp(s - jnp.max(s, axis=0, keepdims=True))
    o_win = (jnp.dot(vt_ref[0, 128:256, pl.ds(start, span)], e.astype(BF16), preferred_element_type=F32)
             / jnp.sum(e, axis=0, keepdims=True))

    out = [jnp.zeros((LANES, tq), F32), jnp.zeros((LANES, tq), F32)]
    for g in range(2):
        for r in range(2):
            cols = slice((2 * g + r) * tq, (2 * g + r + 1) * tq)
            c = g * 6 + r * 3
            o = (gates_t[c:c + 1, :] * o_cmp[:, cols] + gates_t[c + 1:c + 2, :] * o_sel[:, cols]
                 + gates_t[c + 2:c + 3, :] * o_win[:, cols])
            out[r] = jnp.where(sub_grp == g, o, out[r])
    o_ref[0, :, 0:128] = jnp.transpose(out[0]).astype(BF16)
    o_ref[0, :, 128:256] = jnp.transpose(out[1]).astype(BF16)


def _nsa_attend(q_t, k2, v_t, kcmp, vcmp_t, gates_t, tq=256, tk=512):
    B, S, _ = k2.shape
    assert tk % tq == 0 and S % tk == 0 and WINDOW % tq == 0
    covt = _cover_t(S)
    tile_t = lambda w: pl.BlockSpec((1, w, tq), lambda b, i: (b, 0, i))
    per_b = lambda r, w: pl.BlockSpec((1, r, w), lambda b, i: (b, 0, 0))
    return pl.pallas_call(
        functools.partial(_nsa_kernel, tq=tq, tk=tk),
        grid=(B, S // tq),
        in_specs=[tile_t(256), per_b(S, 256), per_b(256, S), per_b(S // CMP_STRIDE, 128), per_b(128, S // CMP_STRIDE),
                  tile_t(128), pl.BlockSpec(covt.shape, lambda b, i: (0, 0))],
        out_specs=pl.BlockSpec((1, tq, 256), lambda b, i: (b, i, 0)),
        out_shape=jax.ShapeDtypeStruct((B, S, 256), BF16),
        compiler_params=_cparams(("parallel", "arbitrary")),
        name="nsa_attend",
    )(q_t, k2, v_t, kcmp, vcmp_t, gates_t, covt)


def _pool_kernel(u_ref, w_ref, scale_ref, o_ref):
    u = u_ref[0]
    row = lax.broadcasted_iota(jnp.int32, u.shape, 0)
    grp = lax.broadcasted_iota(jnp.int32, u.shape, 1) // HEAD_DIM

    def back(x, k):
        return jnp.where(row >= k, pltpu.roll(x, k, 0), 0.0)

    s2 = u + back(u, 1)
    s4 = s2 + back(s2, 2)
    s8 = s4 + back(s4, 4)
    s16 = s8 + back(s8, 8)
    total = jnp.where(grp == 0, s2, jnp.where(grp == 1, s4, jnp.where(grp == 2, s8, s16)))
    win = jnp.where(grp == 0, 2, jnp.where(grp == 1, 4, jnp.where(grp == 2, 8, 16)))
    count = jnp.minimum(row + 1, win).astype(F32)
    d = total / count - u
    y = jnp.dot(d.astype(BF16), w_ref[...], preferred_element_type=F32)
    o_ref[0] = (y * scale_ref[...]).astype(BF16)


def _pool(u, w_bd, scale):
    B, S, W = u.shape
    per_b = pl.BlockSpec((1, S, W), lambda b: (b, 0, 0))
    full = lambda a: pl.BlockSpec(a.shape, lambda b: (0,) * a.ndim)
    return pl.pallas_call(
        _pool_kernel,
        grid=(B,),
        in_specs=[per_b, full(w_bd), full(scale)],
        out_specs=per_b,
        out_shape=jax.ShapeDtypeStruct((B, S, W), BF16),
        compiler_params=_cparams(("parallel",)),
        name="pool",
    )(u, w_bd, scale)


def _block_diag(w):
    G, a, b = w.shape
    return jnp.einsum('gab,gh->gahb', w, jnp.eye(G, dtype=w.dtype)).reshape(G * a, G * b)


def _dot_split_lhs(a, b_exact, terms):
    out, rem = None, a
    for i in range(terms):
        hi = rem.astype(BF16)
        d = jnp.dot(hi, b_exact, preferred_element_type=F32)
        out = d if out is None else out + d
        if i + 1 < terms:
            rem = rem - hi.astype(F32)
    return out


def _dot_split_rhs(a_exact, b, terms):
    out, rem = None, b
    for i in range(terms):
        hi = rem.astype(BF16)
        d = jnp.dot(a_exact, hi, preferred_element_type=F32)
        out = d if out is None else out + d
        if i + 1 < terms:
            rem = rem - hi.astype(F32)
    return out


def _dot_tn(a, b):
    return lax.dot_general(a, b, (((0,), (0,)), ((), ())), preferred_element_type=F32)


def _stack_heads(x, width, n_heads=4):
    head = lax.broadcasted_iota(jnp.int32, x.shape, 1) // width
    return jnp.concatenate([jnp.where(head == h, x, 0.0) for h in range(n_heads)], axis=0)


def _unstack_heads(x, C, n_heads=4):
    out = x[0:C]
    for h in range(1, n_heads):
        out = out + x[h * C:(h + 1) * C]
    return out


def _block_masks(C, n_heads=4):
    n = C * n_heads
    row = lax.broadcasted_iota(jnp.int32, (n, n), 0)
    col = lax.broadcasted_iota(jnp.int32, (n, n), 1)
    same = (row // C) == (col // C)
    return same & (row > col), same & (row >= col)


def _tri_ones(C):
    return jnp.asarray(np.tril(np.ones((C, C), np.float32)), BF16)


def _head_ones(width, n_heads=4):
    return jnp.asarray(np.kron(np.eye(n_heads, dtype=np.float32), np.ones((width, width), np.float32)), BF16)


def _gla_kernel(gl_ref, a2_ref, ab_ref, norm_ref, tri_ref, ones_ref, o_ref, state_ref, *, C):
    s = pl.program_id(1)

    @pl.when(s == 0)
    def _():
        state_ref[...] = jnp.zeros_like(state_ref)

    gl = gl_ref[0]
    tm = gl.shape[0]
    q = gl[:, 0:128] * (GLA_DK ** -0.5)
    k = gl[:, 128:256]
    v = gl[:, 256:512]
    og = gl[:, 512:768]
    ad = gl[:, 768:896]
    x = jnp.dot(ad.astype(BF16), a2_ref[...], preferred_element_type=F32) + ab_ref[...]
    log_a = jax.nn.log_sigmoid(x) * (1.0 / GLA_TAU)
    strict, incl = _block_masks(C)
    kv_mask = (lax.broadcasted_iota(jnp.int32, (256, 128), 0) // HEAD_DIM
               == lax.broadcasted_iota(jnp.int32, (256, 128), 1) // GLA_DK)
    chunks = []
    for c in range(tm // C):
        rc = slice(c * C, (c + 1) * C)
        bcum = _dot_split_rhs(tri_ref[...], log_a[rc], 3)
        mid = bcum[C // 2 - 1:C // 2, :]
        last = bcum[C - 1:C, :]
        q_m = _stack_heads(q[rc] * jnp.exp(bcum - mid), GLA_DK).astype(BF16)
        k_m = _stack_heads(k[rc] * jnp.exp(mid - bcum), GLA_DK).astype(BF16)
        k_end = k[rc] * jnp.exp(last - bcum)
        att = jnp.where(incl, _dot_nt(q_m, k_m), 0.0)
        v_st = _stack_heads(v[rc], HEAD_DIM).astype(BF16)
        chunks.append(dict(
            o_intra=_unstack_heads(jnp.dot(att.astype(BF16), v_st, preferred_element_type=F32), C),
            q_in=(q[rc] * jnp.exp(bcum)).astype(BF16), decay=jnp.exp(last),
            upd=jnp.where(kv_mask, _dot_tn(v[rc].astype(BF16), k_end.astype(BF16)), 0.0)))
    outs = []
    state = state_ref[...]
    for ch in chunks:
        outs.append(ch['o_intra'] + _dot_nt(ch['q_in'], state.astype(BF16)))
        state = state * ch['decay'] + ch['upd']
    state_ref[...] = state
    o = jnp.concatenate(outs, axis=0)
    ms = _dot_split_lhs(o * o, ones_ref[...], 2) * (1.0 / HEAD_DIM)
    o = o * lax.rsqrt(ms + NORM_EPS) * norm_ref[...]
    o_ref[0] = (o * (og * jax.nn.sigmoid(og))).astype(BF16)


def _gla(gl, a2p, ab, norm4, tm=512):
    B, S, W = gl.shape
    C = GLA_CHUNK
    tri, ones = _tri_ones(C), _head_ones(HEAD_DIM)
    tile = lambda w: pl.BlockSpec((1, tm, w), lambda b, s: (b, s, 0))
    full = lambda a: pl.BlockSpec(a.shape, lambda b, s: (0,) * a.ndim)
    return pl.pallas_call(
        functools.partial(_gla_kernel, C=C),
        grid=(B, S // tm),
        in_specs=[tile(W), full(a2p), full(ab), full(norm4), full(tri), full(ones)],
        out_specs=tile(256),
        out_shape=jax.ShapeDtypeStruct((B, S, 256), BF16),
        scratch_shapes=[pltpu.VMEM((256, 128), F32)],
        compiler_params=_cparams(("parallel", "arbitrary")),
        name="gla",
    )(gl, a2p, ab, norm4, tri, ones)


_RV_W0, _RV_A0, _RV_KK, _RV_KA, _RV_RK, _RV_LN, _RV_V0 = range(7)


def _rwkv_kernel(*refs, has_vres, C):
    if has_vres:
        (rw_ref, h_ref, vf_ref, wz_ref, vec_ref, v1_ref, v2_ref, tri_ref, ones_ref, o_ref, state_ref) = refs
    else:
        (rw_ref, wz_ref, vec_ref, tri_ref, ones_ref, o_ref, vout_ref, state_ref) = refs
    s = pl.program_id(1)

    @pl.when(s == 0)
    def _():
        state_ref[...] = jnp.zeros_like(state_ref)

    rw = rw_ref[0]
    tm = rw.shape[0]
    r, k, v, z = rw[:, 0:256], rw[:, 256:512], rw[:, 512:768], rw[:, 768:896]
    vec = lambda i: vec_ref[i:i + 1, :]
    zl = lax.broadcasted_iota(jnp.int32, z.shape, 1)
    zf = jnp.where(zl < 32, jnp.tanh(z), jnp.where(zl < 64, z, jax.nn.sigmoid(z)))
    zz = jnp.dot(zf.astype(BF16), wz_ref[...], preferred_element_type=F32)
    w_log = -jax.nn.softplus(-(vec(_RV_W0) + zz[:, 0:256])) - 0.5
    lw = -jnp.exp(w_log)
    a_sig = jax.nn.sigmoid(vec(_RV_A0) + zz[:, 256:512])
    gate = zz[:, 512:768]
    if has_vres:
        low = jnp.dot(h_ref[0], v1_ref[...], preferred_element_type=F32)
        logit = vec(_RV_V0) + jnp.dot(low.astype(BF16), v2_ref[...], preferred_element_type=F32)
        v = v + (vf_ref[0] - v) * jax.nn.sigmoid(logit)
    else:
        vout_ref[0] = v
    kk = k * vec(_RV_KK)
    norm = jnp.sqrt(_dot_split_lhs(kk * kk, ones_ref[...], 2))
    kk = kk / jnp.maximum(norm, 1e-12)
    k = k * (1.0 + (a_sig - 1.0) * vec(_RV_KA))
    a_vec = -kk
    b_vec = kk * a_sig

    strict, incl = _block_masks(C)
    st = lambda t: _stack_heads(t, HEAD_DIM)
    bdot = lambda p, q: jnp.dot(p.astype(BF16), q.astype(BF16), preferred_element_type=F32)
    state_mask = (lax.broadcasted_iota(jnp.int32, (256, 256), 0) // HEAD_DIM
                  == lax.broadcasted_iota(jnp.int32, (256, 256), 1) // HEAD_DIM)
    n4 = 4 * C
    chunks = []
    for c in range(tm // C):
        rc = slice(c * C, (c + 1) * C)
        lcum = _dot_split_rhs(tri_ref[...], lw[rc], 3)
        lex = lcum - lw[rc]
        mid = lcum[C // 2 - 1:C // 2, :]
        last = lcum[C - 1:C, :]
        e_mid = jnp.exp(mid - lcum)
        e_end = jnp.exp(last - lcum)
        left = jnp.concatenate([st(a_vec[rc] * jnp.exp(lex - mid)), st(r[rc] * jnp.exp(lcum - mid))], axis=0)
        right = jnp.concatenate([st(b_vec[rc] * e_mid), st(k[rc] * e_mid)], axis=0)
        g = _dot_nt(left.astype(BF16), right.astype(BF16))
        v_st = st(v[rc])
        n_ak = jnp.where(strict, g[0:n4, n4:2 * n4], 0.0)
        chunks.append(dict(
            p=jnp.where(strict, g[0:n4, 0:n4], 0.0),
            x=jnp.concatenate([st(a_vec[rc] * jnp.exp(lex)), bdot(n_ak, v_st)], axis=1),
            n_rb=jnp.where(incl, g[n4:2 * n4, 0:n4], 0.0),
            y_rk=bdot(jnp.where(incl, g[n4:2 * n4, n4:2 * n4], 0.0), v_st),
            r_abs=r[rc] * jnp.exp(lcum), decay=jnp.exp(last), v=v[rc],
            bk_end=jnp.concatenate([b_vec[rc] * e_end, k[rc] * e_end], axis=0).astype(BF16)))
    eye = (lax.broadcasted_iota(jnp.int32, (n4, n4), 0) == lax.broadcasted_iota(jnp.int32, (n4, n4), 1)).astype(F32)
    for ch in chunks:
        ch['t'] = eye + ch['p']
    for f in range(1, int(np.log2(C))):
        for ch in chunks:
            ch['p'] = bdot(ch['p'], ch['p'])
        for ch in chunks:
            ch['t'] = ch['t'] + bdot(ch['p'], ch['t'])
    for ch in chunks:
        ch['x'] = bdot(ch['t'], ch['x'])
        y_part = bdot(ch['n_rb'], ch['x'])
        a_eff = _unstack_heads(ch['x'][:, 0:256], C)
        u0 = _unstack_heads(ch['x'][:, 256:512], C)
        ch['r_eff'] = (ch['r_abs'] + _unstack_heads(y_part[:, 0:256], C)).astype(BF16)
        ch['y0'] = _unstack_heads(y_part[:, 256:512] + ch['y_rk'], C)
        ch['mix'] = jnp.where(state_mask, _dot_tn(a_eff.astype(BF16), ch['bk_end'][0:C]), 0.0).astype(BF16)
        ch['add'] = jnp.where(state_mask, _dot_tn(jnp.concatenate([u0, ch['v']], axis=0).astype(BF16), ch['bk_end']), 0.0)
    ys = []
    state = state_ref[...]
    for ch in chunks:
        sb = state.astype(BF16)
        ys.append(_dot_nt(ch['r_eff'], sb) + ch['y0'])
        state = state * ch['decay'] + jnp.dot(sb, ch['mix'], preferred_element_type=F32) + ch['add']
    state_ref[...] = state
    y = jnp.concatenate(ys, axis=0)
    inv = 1.0 / HEAD_DIM
    mean = _dot_split_lhs(y, ones_ref[...], 2) * inv
    yc = y - mean
    var = _dot_split_lhs(yc * yc, ones_ref[...], 2) * inv
    y = yc * lax.rsqrt(var + RWKV_GN_EPS) * vec(_RV_LN)
    y = y + _dot_split_lhs(r * k * vec(_RV_RK), ones_ref[...], 2) * v
    o_ref[0] = (y * gate).astype(BF16)


def _rwkv(rw, wz, vecs, h=None, v_first=None, v1p=None, v2p=None, tm=256):
    B, S, W = rw.shape
    C = RWKV_CHUNK
    has_vres = h is not None
    tri, ones = _tri_ones(C), _head_ones(HEAD_DIM)
    tile = lambda w: pl.BlockSpec((1, tm, w), lambda b, s: (b, s, 0))
    full = lambda a: pl.BlockSpec(a.shape, lambda b, s: (0,) * a.ndim)
    if has_vres:
        args = (rw, h, v_first, wz, vecs, v1p, v2p, tri, ones)
        in_specs = [tile(W), tile(D_MODEL), tile(256), full(wz), full(vecs), full(v1p), full(v2p), full(tri), full(ones)]
        out_specs = tile(256)
        out_shape = jax.ShapeDtypeStruct((B, S, 256), BF16)
    else:
        args = (rw, wz, vecs, tri, ones)
        in_specs = [tile(W), full(wz), full(vecs), full(tri), full(ones)]
        out_specs = [tile(256), tile(256)]
        out_shape = [jax.ShapeDtypeStruct((B, S, 256), BF16), jax.ShapeDtypeStruct((B, S, 256), F32)]
    return pl.pallas_call(
        functools.partial(_rwkv_kernel, has_vres=has_vres, C=C),
        grid=(B, S // tm),
        in_specs=in_specs,
        out_specs=out_specs,
        out_shape=out_shape,
        scratch_shapes=[pltpu.VMEM((256, 256), F32)],
        compiler_params=_cparams(("parallel", "arbitrary")),
        name="rwkv_vres" if has_vres else "rwkv",
    )(*args)


def _rwkv_weights(l, rwkv_w0, rwkv_w2, rwkv_a0, rwkv_a2, rwkv_g2, rwkv_v0, rwkv_k_k, rwkv_k_a, rwkv_r_k, rwkv_ln):
    wz = jnp.zeros((128, 768), F32)
    wz = wz.at[0:32, 0:256].set(rwkv_w2[l]).at[32:64, 256:512].set(rwkv_a2[l]).at[64:128, 512:768].set(rwkv_g2[l])
    v0 = rwkv_v0[l - 1] if l > 0 else jnp.zeros((256,), F32)
    vecs = jnp.stack([rwkv_w0[l], rwkv_a0[l], rwkv_k_k[l], rwkv_k_a[l], rwkv_r_k[l].reshape(-1), rwkv_ln[l], v0,
                      jnp.zeros((256,), F32)])
    return wz.astype(BF16), vecs


def _merge_kernel(x_ref, h_ref, o0_ref, o1_ref, o2_ref, o3_ref, wg_ref, bg_ref, wb_ref, wo_ref, out_ref):
    h = h_ref[0]
    merged = None
    for i, o_ref in enumerate((o0_ref, o1_ref, o2_ref, o3_ref)):
        gate = jax.nn.sigmoid(jnp.dot(h, wg_ref[i], preferred_element_type=F32) + bg_ref[i:i + 1, :])
        term = gate * jnp.dot(o_ref[0], wb_ref[i], preferred_element_type=F32)
        merged = term if merged is None else merged + term
    out_ref[0] = x_ref[0] + jnp.dot(merged.astype(BF16), wo_ref[...], preferred_element_type=F32)


def _merge(x, h, o_nsa, o_pool, o_rwkv, o_gla, wg, bg, wb, wo, tm=512):
    B, S, D = x.shape
    tile = lambda w: pl.BlockSpec((1, tm, w), lambda b, s: (b, s, 0))
    full = lambda a: pl.BlockSpec(a.shape, lambda b, s: (0,) * a.ndim)
    return pl.pallas_call(
        _merge_kernel,
        grid=(B, S // tm),
        in_specs=[tile(D), tile(D), tile(256), tile(256), tile(256), tile(256), full(wg), full(bg), full(wb), full(wo)],
        out_specs=tile(D),
        out_shape=jax.ShapeDtypeStruct((B, S, D), F32),
        compiler_params=_cparams(("parallel", "parallel")),
        name="merge",
    )(x, h, o_nsa, o_pool, o_rwkv, o_gla, wg, bg, wb, wo)


def _memkv_kernel(mem_ref, g_ref, w_ref, o_ref):
    mn = _rms(mem_ref[0], g_ref[...]).astype(BF16)
    o_ref[0] = jnp.dot(mn, w_ref[...], preferred_element_type=F32).astype(BF16)


def _memkv(mem, g, wkv_all):
    B, M, D = mem.shape
    N = wkv_all.shape[1]
    return pl.pallas_call(
        _memkv_kernel,
        grid=(B,),
        in_specs=[pl.BlockSpec((1, M, D), lambda b: (b, 0, 0)), pl.BlockSpec(g.shape, lambda b: (0, 0)),
                  pl.BlockSpec(wkv_all.shape, lambda b: (0, 0))],
        out_specs=pl.BlockSpec((1, M, N), lambda b: (b, 0, 0)),
        out_shape=jax.ShapeDtypeStruct((B, M, N), BF16),
        compiler_params=_cparams(("parallel",)),
        name="memkv",
    )(mem, g, wkv_all)


def _cross_kernel(x_ref, g_ref, wq_ref, k_ref, v_ref, wo_ref, out_ref):
    x = x_ref[0]
    hc = _rms(x, g_ref[...]).astype(BF16)
    q = jnp.dot(hc, wq_ref[...], preferred_element_type=F32) * (HEAD_DIM ** -0.5)
    head = lax.broadcasted_iota(jnp.int32, q.shape, 1) // HEAD_DIM
    k, v = k_ref[0], v_ref[0]
    o = jnp.zeros(q.shape, F32)
    for hh in range(CA_WIDTH // HEAD_DIM):
        qh = jnp.where(head == hh, q, 0.0).astype(BF16)
        s = _dot_nt(qh, k)
        e = jnp.exp(s - jnp.max(s, axis=1, keepdims=True))
        p = e / jnp.sum(e, axis=1, keepdims=True)
        o = jnp.where(head == hh, jnp.dot(p.astype(BF16), v, preferred_element_type=F32), o)
    out_ref[0] = x + jnp.dot(o.astype(BF16), wo_ref[...], preferred_element_type=F32)


def _cross(x, g, wq, memkv, l, wo, tm=512):
    B, S, D = x.shape
    M = memkv.shape[1]
    tile = pl.BlockSpec((1, tm, D), lambda b, s: (b, s, 0))
    full = lambda a: pl.BlockSpec(a.shape, lambda b, s: (0,) * a.ndim)
    kspec = pl.BlockSpec((1, M, CA_WIDTH), lambda b, s: (b, 0, 2 * l))
    vspec = pl.BlockSpec((1, M, CA_WIDTH), lambda b, s: (b, 0, 2 * l + 1))
    return pl.pallas_call(
        _cross_kernel,
        grid=(B, S // tm),
        in_specs=[tile, full(g), full(wq), kspec, vspec, full(wo)],
        out_specs=tile,
        out_shape=jax.ShapeDtypeStruct((B, S, D), F32),
        compiler_params=_cparams(("parallel", "parallel")),
        name="cross_attn",
    )(x, g, wq, memkv, memkv, wo)


def _swiglu_cols(xb, wg, wu, wd, tf):
    acc = jnp.zeros((xb.shape[0], wd.shape[-1]), F32)
    for c in range(wg.shape[-1] // tf):
        cols = slice(c * tf, (c + 1) * tf)
        gate = jnp.dot(xb, wg[:, cols], preferred_element_type=F32)
        up = jnp.dot(xb, wu[:, cols], preferred_element_type=F32)
        act = gate * jax.nn.sigmoid(gate) * up
        acc = acc + jnp.dot(act.astype(BF16), wd[cols, :], preferred_element_type=F32)
    return acc


def _ffn_kernel(x_ref, g_ref, wg_ref, wu_ref, wd_ref, out_ref, *, tf):
    x = x_ref[...]
    out_ref[...] = x + _swiglu_cols(_rms(x, g_ref[...]).astype(BF16), wg_ref, wu_ref, wd_ref, tf)


def _ffn(x2, g, wg, wu, wd, tm=512, tf=256):
    T, D = x2.shape
    full = lambda a: pl.BlockSpec(a.shape, lambda i: (0,) * a.ndim)
    return pl.pallas_call(
        functools.partial(_ffn_kernel, tf=tf),
        grid=(T // tm,),
        in_specs=[pl.BlockSpec((tm, D), lambda i: (i, 0)), full(g), full(wg), full(wu), full(wd)],
        out_specs=pl.BlockSpec((tm, D), lambda i: (i, 0)),
        out_shape=jax.ShapeDtypeStruct((T, D), F32),
        compiler_params=_cparams(("parallel",)),
        name="ffn",
    )(x2, g, wg, wu, wd)


MOE_TM = 512
MOE_TD = 512


def _route_kernel(x_ref, g_ref, r_ref, hf_ref, idx_ref, w_ref):
    hf = _rms(x_ref[...], g_ref[...])
    hf_ref[...] = hf
    logits = jnp.dot(hf, r_ref[...], preferred_element_type=F32, precision=lax.Precision.HIGHEST)
    lane = lax.broadcasted_iota(jnp.int32, logits.shape, 1)
    big = logits.shape[1]
    logits = jnp.where(lane < N_EXPERTS, logits, -jnp.inf)
    m1 = jnp.max(logits, axis=1, keepdims=True)
    i1 = jnp.min(jnp.where(logits == m1, lane, big), axis=1, keepdims=True)
    rest = jnp.where(lane == i1, -jnp.inf, logits)
    m2 = jnp.max(rest, axis=1, keepdims=True)
    i2 = jnp.min(jnp.where(rest == m2, lane, big), axis=1, keepdims=True)
    e2 = jnp.exp(m2 - m1)
    w1 = 1.0 / (1.0 + e2)
    idx_ref[...] = jnp.where(lane == 0, i1, jnp.where(lane == 1, i2, 0))
    w_ref[...] = jnp.where(lane == 0, w1, jnp.where(lane == 1, e2 * w1, 0.0))


def _moe_route(x2, g, router_p, tm=1024):
    T, D = x2.shape
    row = lambda w: pl.BlockSpec((tm, w), lambda i: (i, 0))
    full = lambda a: pl.BlockSpec(a.shape, lambda i: (0,) * a.ndim)
    return pl.pallas_call(
        _route_kernel,
        grid=(T // tm,),
        in_specs=[row(D), full(g), full(router_p)],
        out_specs=[row(D), row(LANES), row(LANES)],
        out_shape=[jax.ShapeDtypeStruct((T, D), F32), jax.ShapeDtypeStruct((T, LANES), jnp.int32),
                   jax.ShapeDtypeStruct((T, LANES), F32)],
        compiler_params=_cparams(("parallel",)),
        name="moe_route",
    )(x2, g, router_p)


def _moe_plan(expert_of_pair, tm):
    onehot = (expert_of_pair[:, None] == jnp.arange(N_EXPERTS, dtype=jnp.int32)[None, :]).astype(jnp.int32)
    csum = jnp.cumsum(onehot, axis=0)
    rank = jnp.sum((csum - 1) * onehot, axis=1)
    counts = csum[-1]
    padded = ((counts + tm - 1) // tm) * tm
    ends = jnp.cumsum(padded)
    dest = ((ends - padded)[expert_of_pair] + rank).astype(jnp.int32)
    n_rows = expert_of_pair.shape[0] + N_EXPERTS * tm
    tile_expert = jnp.sum((jnp.arange(n_rows // tm, dtype=jnp.int32)[:, None] * tm >= ends[None, :]).astype(jnp.int32), axis=1)
    token_of_row = jnp.zeros((n_rows,), jnp.int32).at[dest].set(jnp.arange(expert_of_pair.shape[0], dtype=jnp.int32) // 2)
    return dest, token_of_row, jnp.minimum(tile_expert, N_EXPERTS - 1).astype(jnp.int32)


DMA_UNROLL = 8


def _gather_rows(table_hbm, row_of, dst, n, sem, wait):
    def body(c, carry):
        for u in range(DMA_UNROLL):
            r = c * DMA_UNROLL + u
            copy = pltpu.make_async_copy(table_hbm.at[pl.ds(row_of(r), 1)], dst.at[pl.ds(r, 1)], sem)
            copy.wait() if wait else copy.start()
        return carry

    lax.fori_loop(0, n // DMA_UNROLL, body, 0)


def _experts_kernel(te_ref, tok_cur_ref, tok_next_ref, hf_hbm, wg_ref, wu_ref, wd_ref, ys_ref, xin, sems, *, tf):
    del te_ref
    i = pl.program_id(0)
    slot = i % 2
    tm = xin.shape[1]
    n_f = wg_ref.shape[2] // tf

    def row_copy(tok_ref, r, s):
        return pltpu.make_async_copy(hf_hbm.at[pl.ds(tok_ref[0, 0, r], 1)], xin.at[s, pl.ds(r, 1)], sems.at[s])

    @pl.when(i == 0)
    def _():
        _gather_rows(hf_hbm, lambda r: tok_cur_ref[0, 0, r], xin.at[0], tm, sems.at[0], wait=False)

    _gather_rows(hf_hbm, lambda r: tok_cur_ref[0, 0, r], xin.at[slot], tm, sems.at[slot], wait=True)
    xb = xin[slot].astype(BF16)
    acc = jnp.zeros(ys_ref.shape, F32)
    bounds = [round(c * tm / n_f) for c in range(n_f + 1)]
    for c in range(n_f):
        for r in range(bounds[c], bounds[c + 1]):
            row_copy(tok_next_ref, r, 1 - slot).start()
        cols = slice(c * tf, (c + 1) * tf)
        gate = jnp.dot(xb, wg_ref[0, :, cols], preferred_element_type=F32)
        up = jnp.dot(xb, wu_ref[0, :, cols], preferred_element_type=F32)
        act = gate * jax.nn.sigmoid(gate) * up
        acc = acc + jnp.dot(act.astype(BF16), wd_ref[0, cols, :], preferred_element_type=F32)
    ys_ref[...] = acc

    @pl.when(i == pl.num_programs(0) - 1)
    def _():
        _gather_rows(hf_hbm, lambda r: tok_next_ref[0, 0, r], xin.at[1 - slot], tm, sems.at[1 - slot], wait=True)


def _moe_experts(hf, token_of_row, tile_expert, wg, wu, wd, tf=256):
    T, D = hf.shape
    tm = MOE_TM
    n_tiles = token_of_row.shape[0] // tm
    tok = token_of_row.reshape(n_tiles, 1, tm)
    tok = jnp.concatenate([tok, tok[-1:]], axis=0)
    smem = lambda off: pl.BlockSpec((1, 1, tm), lambda i, te: (i + off, 0, 0), memory_space=pltpu.SMEM)
    expert = lambda shape: pl.BlockSpec((1,) + shape, lambda i, te: (te[i], 0, 0))
    grid_spec = pltpu.PrefetchScalarGridSpec(
        num_scalar_prefetch=1,
        grid=(n_tiles,),
        in_specs=[smem(0), smem(1), pl.BlockSpec(memory_space=pl.ANY),
                  expert(wg.shape[1:]), expert(wu.shape[1:]), expert(wd.shape[1:])],
        out_specs=pl.BlockSpec((tm, D), lambda i, te: (i, 0)),
        scratch_shapes=[pltpu.VMEM((2, tm, D), F32), pltpu.SemaphoreType.DMA((2,))],
    )
    return pl.pallas_call(
        functools.partial(_experts_kernel, tf=tf),
        grid_spec=grid_spec,
        out_shape=jax.ShapeDtypeStruct((n_tiles * tm, D), F32),
        compiler_params=_cparams(("arbitrary",)),
        name="moe_experts",
    )(tile_expert, tok, tok, hf, wg, wu, wd)


def _combine_kernel(dest_ref, x_ref, w_ref, gf_ref, ys_hbm, out_ref, y0_buf, y1_buf, sem0, sem1):
    slots = ((lambda r: dest_ref[0, 0, r], y0_buf, sem0), (lambda r: dest_ref[0, 1, r], y1_buf, sem1))
    for row_of, buf, sem in slots:
        _gather_rows(ys_hbm, row_of, buf, MOE_TD, sem, wait=False)
    for row_of, buf, sem in slots:
        _gather_rows(ys_hbm, row_of, buf, MOE_TD, sem, wait=True)
    w = w_ref[...]
    y = x_ref[...] + w[:, 0:1] * y0_buf[...] + w[:, 1:2] * y1_buf[...]
    out_ref[...] = _rms(y, gf_ref[...])


def _moe_combine(x2, w, dest, ys, g_final):
    T, D = x2.shape
    steps = T // MOE_TD
    dest3 = dest.reshape(steps, MOE_TD, 2).transpose(0, 2, 1)
    row = lambda wd: pl.BlockSpec((MOE_TD, wd), lambda i: (i, 0))
    return pl.pallas_call(
        _combine_kernel,
        grid=(steps,),
        in_specs=[pl.BlockSpec((1, 2, MOE_TD), lambda i: (i, 0, 0), memory_space=pltpu.SMEM), row(D), row(LANES),
                  pl.BlockSpec(g_final.shape, lambda i: (0, 0)), pl.BlockSpec(memory_space=pl.ANY)],
        out_specs=row(D),
        out_shape=jax.ShapeDtypeStruct((T, D), F32),
        scratch_shapes=[pltpu.VMEM((MOE_TD, D), F32), pltpu.VMEM((MOE_TD, D), F32),
                        pltpu.SemaphoreType.DMA(()), pltpu.SemaphoreType.DMA(())],
        compiler_params=_cparams(("arbitrary",)),
        name="moe_combine",
    )(dest3, x2, w, g_final, ys)


def _moe_routed(x2, g, router_p, wg, wu, wd, g_final):
    T, D = x2.shape
    hf, idx, w = _moe_route(x2, g, router_p)
    dest, token_of_row, tile_expert = _moe_plan(idx[:, 0:2].reshape(2 * T), MOE_TM)
    ys = _moe_experts(hf, token_of_row, tile_expert, wg, wu, wd)
    return _moe_combine(x2, w, dest, ys, g_final)


def kernel(x, mem, positions, norm_mix, w_in, nsa_cmp_pe, nsa_cmp_w1, nsa_cmp_w2, pool_w, pool_scale, rwkv_mu, rwkv_w0,
           rwkv_w2, rwkv_a0, rwkv_a2, rwkv_g2, rwkv_v0, rwkv_v1, rwkv_v2, rwkv_k_k, rwkv_k_a, rwkv_r_k, rwkv_ln, gla_a2,
           gla_ab, gla_norm, w_gate, b_gate, w_branch, w_out, norm_ca, mem_norm, w_ca_q, w_ca_kv, w_ca_o, norm_ffn,
           ffn_w_gate, ffn_w_up, ffn_w_down, moe_router, moe_w_gate, moe_w_up, moe_w_down, norm_final):
    B, S, D = x.shape
    depth = norm_mix.shape[0]
    assert depth == 2 and S % 256 == 0, "the final norm is fused into the layer-1 expert mixer"
    cosf, sinf = _rope_tables(positions)
    at_cmp_end = lambda t: jnp.pad(t[:, CMP_LEN - 1::CMP_STRIDE], ((0, 0), (0, 1), (0, 0)))
    cosc, sinc = at_cmp_end(cosf), at_cmp_end(sinf)
    memkv = _memkv(mem, mem_norm[None], jnp.concatenate([w_ca_kv[l] for l in range(depth)], axis=1).astype(BF16))
    v_first = None
    for l in range(depth):
        h, q_t, k2, v_t, cv, gates_t, u, rw, gl = _inproj(x, norm_mix[l][None], _inproj_weights(w_in[l]), cosf, sinf,
                                                          rwkv_mu[l][None])
        kcmp, vcmp_t = _nsa_compress(cv, *_compress_weights(nsa_cmp_pe[l], nsa_cmp_w1[l], nsa_cmp_w2[l]), cosc, sinc)
        o_nsa = _nsa_attend(q_t, k2, v_t, kcmp, vcmp_t, gates_t)
        o_pool = _pool(u, _block_diag(pool_w[l]).astype(BF16), pool_scale[l][None])
        wz, vecs = _rwkv_weights(l, rwkv_w0, rwkv_w2, rwkv_a0, rwkv_a2, rwkv_g2, rwkv_v0, rwkv_k_k, rwkv_k_a,
                                 rwkv_r_k, rwkv_ln)
        if l == 0:
            o_rwkv, v_first = _rwkv(rw, wz, vecs)
        else:
            v1p = jnp.pad(rwkv_v1[l - 1], ((0, 0), (0, LANES - rwkv_v1.shape[2]))).astype(BF16)
            v2p = jnp.pad(rwkv_v2[l - 1], ((0, LANES - rwkv_v2.shape[1]), (0, 0))).astype(BF16)
            o_rwkv = _rwkv(rw, wz, vecs, h, v_first, v1p, v2p)
        a2p = jnp.pad(gla_a2[l], ((0, LANES - gla_a2.shape[1]), (0, 0))).astype(BF16)
        o_gla = _gla(gl, a2p, gla_ab[l][None], jnp.tile(gla_norm[l], 4)[None])
        wb_nsa = w_branch[l, 0].reshape(4, HEAD_DIM, D)[jnp.array([0, 2, 1, 3])].reshape(BRANCH_WIDTH, D)
        wb = jnp.concatenate([wb_nsa[None], w_branch[l, 1:]], axis=0).astype(BF16)
        x = _merge(x, h, o_nsa, o_pool, o_rwkv, o_gla, w_gate[l].astype(BF16), b_gate[l], wb, w_out[l].astype(BF16))
        x = _cross(x, norm_ca[l][None], w_ca_q[l].astype(BF16), memkv, l, w_ca_o[l].astype(BF16))
        j = l // 2
        x2 = x.reshape(B * S, D)
        if l % 2 == 0:
            x2 = _ffn(x2, norm_ffn[l][None], ffn_w_gate[j].astype(BF16), ffn_w_up[j].astype(BF16),
                      ffn_w_down[j].astype(BF16))
        else:
            router_p = jnp.pad(moe_router[j], ((0, 0), (0, LANES - N_EXPERTS)))
            x2 = _moe_routed(x2, norm_ffn[l][None], router_p, moe_w_gate[j].astype(BF16), moe_w_up[j].astype(BF16),
                             moe_w_down[j].astype(BF16), norm_final[None])
        x = x2.reshape(B, S, D)
    return x
```

```python
import functools

import jax
import jax.numpy as jnp
import numpy as np
from jax import lax
from jax.experimental import pallas as pl
from jax.experimental.pallas import tpu as pltpu

F32 = jnp.float32
BF16 = jnp.bfloat16

D_MODEL = 1024
HEAD_DIM = 64
BRANCH_WIDTH = 256
ROPE_THETA = 500000.0
ROPE_DIM = 16
NORM_EPS = 1e-6
NEG_INF = -1e30

NSA_KV = 128
CMP_LEN = 32
CMP_STRIDE = 16
CMP_HIDDEN = 128
SEL_BLOCK = 64
SEL_TOPN = 8
WINDOW = 512

RWKV_IN = 896
RWKV_GN_EPS = 64e-5
RWKV_CHUNK = 64

GLA_DK = 32
GLA_TAU = 16.0
GLA_CHUNK = 64

CA_WIDTH = 256
D_FF = 2816
N_EXPERTS = 8

LANES = 128
VMEM_LIMIT = 56 * 1024 * 1024

_C_Q = 0
_C_KV = 256
_C_CV = 768
_C_GATE = 1024
_C_POOL = 1152
_C_RWKV = 1408
_C_GLA = 2304
_C_END = 3200


def _cparams(sem):
    return pltpu.CompilerParams(dimension_semantics=sem, vmem_limit_bytes=VMEM_LIMIT)


def _rms(x, g):
    ms = jnp.mean(x * x, axis=-1, keepdims=True)
    return x * lax.rsqrt(ms + NORM_EPS) * g


def _rope128(t, cos, sin):
    lane = lax.broadcasted_iota(jnp.int32, t.shape, 1) % HEAD_DIM
    partner = jnp.where(lane < ROPE_DIM // 2, pltpu.roll(t, LANES - ROPE_DIM // 2, 1),
                        pltpu.roll(t, ROPE_DIM // 2, 1))
    return t * cos + partner * sin


def _inproj_kernel(x_ref, g_ref, w_ref, cs_ref, e_ref, ones_ref, mu_ref,
                   h_ref, qt_ref, k_ref, vt_ref, cv_ref, gt_ref, u_ref, rw_ref, gl_ref, prev_ref):
    s = pl.program_id(1)
    hb = _rms(x_ref[0], g_ref[...]).astype(BF16)
    h_ref[0] = hb

    def mm(a, b):
        return jnp.dot(hb, w_ref[:, a:b], preferred_element_type=F32)

    cos, sin = _rope_expand(cs_ref[0], e_ref[...], ones_ref[...])
    scale = HEAD_DIM ** -0.5
    qt_ref[0, 0:128, :] = jnp.transpose(_rope128(mm(_C_Q, _C_Q + 128), cos, sin) * scale).astype(BF16)
    qt_ref[0, 128:256, :] = jnp.transpose(_rope128(mm(_C_Q + 128, _C_Q + 256), cos, sin) * scale).astype(BF16)
    k_ref[0, :, 0:128] = _rope128(mm(_C_KV, _C_KV + 128), cos, sin).astype(BF16)
    vt_ref[0, 0:128, :] = jnp.transpose(mm(_C_KV + 128, _C_KV + 256)).astype(BF16)
    k_ref[0, :, 128:256] = _rope128(mm(_C_KV + 256, _C_KV + 384), cos, sin).astype(BF16)
    vt_ref[0, 128:256, :] = jnp.transpose(mm(_C_KV + 384, _C_KV + 512)).astype(BF16)
    cv_ref[0] = mm(_C_CV, _C_CV + 256).astype(BF16)
    gt_ref[0] = jnp.transpose(jax.nn.sigmoid(mm(_C_GATE, _C_GATE + 128)))
    u_ref[0] = mm(_C_POOL, _C_POOL + 256)
    gl_ref[0] = mm(_C_GLA, _C_END)

    p = mm(_C_RWKV, _C_RWKV + RWKV_IN)
    tm = p.shape[0]

    @pl.when(s == 0)
    def _():
        prev_ref[...] = jnp.zeros_like(prev_ref)

    row = lax.broadcasted_iota(jnp.int32, p.shape, 0)
    shifted = jnp.where(row == 0, prev_ref[0:1, :], pltpu.roll(p, 1, 0))
    prev_ref[0:1, :] = p[tm - 1:tm, :]
    rw_ref[0] = p + (shifted - p) * mu_ref[...]


def _inproj(x, g, w_all, cs, mu, tm=512):
    B, S, D = x.shape
    e, ones = _rope_expander()
    tok = lambda w: pl.BlockSpec((1, tm, w), lambda b, s: (b, s, 0))
    tok_t = lambda w: pl.BlockSpec((1, w, tm), lambda b, s: (b, 0, s))
    full = lambda a: pl.BlockSpec(a.shape, lambda b, s: (0,) * a.ndim)
    outs = [(D, BF16, True), (256, BF16, False), (256, BF16, True), (256, BF16, False), (256, BF16, True),
            (128, F32, False), (256, F32, True), (RWKV_IN, F32, True), (896, F32, True)]
    return pl.pallas_call(
        _inproj_kernel,
        grid=(B, S // tm),
        in_specs=[tok(D), full(g), full(w_all), tok(ROPE_DIM), full(e), full(ones), full(mu)],
        out_specs=[tok(w) if major else tok_t(w) for w, _, major in outs],
        out_shape=[jax.ShapeDtypeStruct((B, S, w) if major else (B, w, S), dt) for w, dt, major in outs],
        scratch_shapes=[pltpu.VMEM((8, RWKV_IN), F32)],
        compiler_params=_cparams(("parallel", "arbitrary")),
        name="inproj",
    )(x, g, w_all, cs, e, ones, mu)


def _inproj_columns():
    span = lambda lo, n: list(range(lo, lo + n))
    pad = lambda n: [-1] * n
    nsa, pool, rwkv, gla = 0, 1036, 1292, 2188
    cols = (span(nsa, 64) + span(nsa + 128, 64) + span(nsa + 64, 64) + span(nsa + 192, 64)
            + span(nsa + 512, 512)
            + span(nsa + 256, 256)
            + span(nsa + 1024, 12) + pad(116)
            + span(pool, 256) + span(rwkv, RWKV_IN)
            + span(gla, 512) + span(gla + 528, 256) + span(gla + 512, 16) + pad(112))
    assert len(cols) == _C_END
    return np.asarray(cols, np.int32)


def _inproj_weights(w):
    onehot = (jnp.arange(w.shape[1], dtype=jnp.int32)[:, None] == jnp.asarray(_inproj_columns())[None, :]).astype(BF16)
    return jnp.dot(w.astype(BF16), onehot, preferred_element_type=F32).astype(BF16)


def _rope_angles(positions):
    inv_freq = ROPE_THETA ** (-jnp.arange(0, ROPE_DIM, 2, dtype=F32) / ROPE_DIM)
    ang = positions.astype(F32)[..., None] * inv_freq
    return jnp.concatenate([jnp.cos(ang), jnp.sin(ang)], axis=-1)


def _rope_expander():
    half = ROPE_DIM // 2
    e = np.zeros((ROPE_DIM, 2 * LANES), np.float32)
    ones = np.zeros((1, LANES), np.float32)
    for lane in range(LANES):
        d = lane % HEAD_DIM
        if d < ROPE_DIM:
            e[d % half, lane] = 1.0
            e[half + d % half, LANES + lane] = -1.0 if d < half else 1.0
        else:
            ones[0, lane] = 1.0
    return jnp.asarray(e, BF16), jnp.asarray(ones)


def _rope_expand(cs, e, ones):
    tab = _dot_split_lhs(cs, e, 3)
    return tab[:, 0:LANES] + ones, tab[:, LANES:2 * LANES]


def _gelu_tanh(x):
    return x * (0.5 * (1.0 + jnp.tanh(np.sqrt(2.0 / np.pi) * (x + 0.044715 * (x * x * x)))))


def _dot_nt(a, b):
    return lax.dot_general(a, b, (((1,), (1,)), ((), ())), preferred_element_type=F32)


def _compress_kernel(x_ref, wa_ref, wb_ref, bias_ref, w2k_ref, w2v_ref, cs_ref, e_ref, ones_ref, kc_ref, vc_ref):
    x = x_ref[0]
    a = jnp.dot(x, wa_ref[...], preferred_element_type=F32)
    b = jnp.dot(x, wb_ref[...], preferred_element_type=F32)
    n = a.shape[0]
    hid = _gelu_tanh(a + pltpu.roll(b, n - 1, 0) + bias_ref[...])
    kc = jnp.dot(hid[:, 0:256].astype(BF16), w2k_ref[...], preferred_element_type=F32)
    vc = jnp.dot(hid[:, 256:512].astype(BF16), w2v_ref[...], preferred_element_type=F32)
    cos, sin = _rope_expand(cs_ref[0], e_ref[...], ones_ref[...])
    kc_ref[0] = _rope128(kc, cos, sin).astype(BF16)
    vc_ref[0] = jnp.transpose(vc).astype(BF16)


def _nsa_compress(cv, wa, wb, bias, w2k, w2v, cs_end):
    B, S, _ = cv.shape
    nseg = S // CMP_STRIDE
    x = cv.reshape(B, nseg, CMP_STRIDE * 256)
    e, ones = _rope_expander()
    per_b = lambda r, w: pl.BlockSpec((1, r, w), lambda b: (b, 0, 0))
    full = lambda a: pl.BlockSpec(a.shape, lambda b: (0,) * a.ndim)
    return pl.pallas_call(
        _compress_kernel,
        grid=(B,),
        in_specs=[per_b(nseg, CMP_STRIDE * 256), full(wa), full(wb), full(bias), full(w2k), full(w2v),
                  per_b(nseg, ROPE_DIM), full(e), full(ones)],
        out_specs=[per_b(nseg, 128), per_b(nseg, 128)],
        out_shape=[jax.ShapeDtypeStruct((B, nseg, 128), BF16)] * 2,
        compiler_params=_cparams(("parallel",)),
        name="nsa_compress",
    )(x, wa, wb, bias, w2k, w2v, cs_end, e, ones)


def _compress_weights(pe, w1, w2):
    eye2 = jnp.eye(2, dtype=F32)
    half = lambda lo: jnp.einsum('jldf,pj,qg->lpqdjgf', w1[:, lo:lo + CMP_STRIDE], eye2, eye2).reshape(
        CMP_STRIDE * 256, 4 * CMP_HIDDEN).astype(BF16)
    bias = jnp.einsum('jld,jldf->jf', pe, w1)
    bias = jnp.broadcast_to(bias[:, None, :], (2, 2, CMP_HIDDEN)).reshape(1, 4 * CMP_HIDDEN)
    bd = lambda w: jnp.einsum('fd,gq->gfqd', w, eye2).reshape(2 * CMP_HIDDEN, 2 * HEAD_DIM).astype(BF16)
    return half(0), half(CMP_STRIDE), bias, bd(w2[0]), bd(w2[1])


def _cover_t(S):
    n_cmp = (S - CMP_LEN) // CMP_STRIDE + 1
    cmp_start = np.arange(n_cmp) * CMP_STRIDE
    slc_start = np.arange(S // SEL_BLOCK) * SEL_BLOCK
    cover = np.clip(np.minimum(cmp_start[:, None] + CMP_LEN, slc_start[None, :] + SEL_BLOCK)
                    - np.maximum(cmp_start[:, None], slc_start[None, :]), 0, None) / CMP_LEN
    out = np.zeros((S // SEL_BLOCK, S // CMP_STRIDE), np.float32)
    out[:, :n_cmp] = cover.T
    return jnp.asarray(out, BF16)


MASKED = 2.0 * NEG_INF


def _softmax_step(s, carry, v_t):
    m, l, acc = carry
    m_new = jnp.maximum(m, jnp.max(s, axis=0, keepdims=True))
    alpha = jnp.exp(m - m_new)
    p = jnp.exp(s - m_new)
    l = alpha * l + jnp.sum(p, axis=0, keepdims=True)
    acc = alpha * acc + jnp.dot(v_t, p.astype(BF16), preferred_element_type=F32)
    return m_new, l, acc


def _softmax_init(n_q):
    return (jnp.full((1, n_q), NEG_INF, F32), jnp.zeros((1, n_q), F32), jnp.zeros((LANES, n_q), F32))


def _nsa_kernel(qt_ref, k_ref, vt_ref, kc_ref, vct_ref, gt_ref, covt_ref, o_ref, *, tq, tk):
    i = pl.program_id(1)
    t0 = i * tq
    n_blk = covt_ref.shape[0]
    n_q = 4 * tq
    qa_t, qb_t = qt_ref[0, 0:128, :], qt_ref[0, 128:256, :]
    sub_grp = lax.broadcasted_iota(jnp.int32, (LANES, tq), 0) // HEAD_DIM
    gates_t = gt_ref[0]
    tok = t0 + (lax.broadcasted_iota(jnp.int32, (1, n_q), 1) & (tq - 1))
    q_all = jnp.concatenate([jnp.where(sub_grp == g, q_t, 0) for g in range(2) for q_t in (qa_t, qb_t)], axis=1)

    s = jnp.dot(kc_ref[0], q_all, preferred_element_type=F32)
    nrow = lax.broadcasted_iota(jnp.int32, s.shape, 0)
    s = jnp.where(nrow * CMP_STRIDE + (CMP_LEN - 1) <= tok, s, MASKED)
    e = jnp.exp(s - jnp.maximum(jnp.max(s, axis=0, keepdims=True), NEG_INF))
    den = jnp.sum(e, axis=0, keepdims=True)
    p = e / jnp.where(den > 0.0, den, 1.0)
    o_cmp = jnp.dot(vct_ref[0], p.astype(BF16), preferred_element_type=F32)

    p_grp = jnp.concatenate([p[:, 2 * g * tq:(2 * g + 1) * tq] + p[:, (2 * g + 1) * tq:(2 * g + 2) * tq]
                             for g in range(2)], axis=1)
    imp = _dot_split_rhs(covt_ref[...], p_grp, 2)
    blk = lax.broadcasted_iota(jnp.int32, imp.shape, 0)
    tok_b = t0 + (lax.broadcasted_iota(jnp.int32, imp.shape, 1) & (tq - 1))
    cur = tok_b // SEL_BLOCK
    forced = (blk == 0) | (blk == cur) | (blk == cur - 1)
    imp = jnp.where(forced, 1e30, jnp.where(blk * SEL_BLOCK <= tok_b, imp, -1e30))
    cnt = jnp.zeros(imp.shape, F32)
    for i2 in range(n_blk):
        row = imp[i2:i2 + 1, :]
        cnt = cnt + jnp.where((row > imp) | ((row == imp) & (blk > i2)), 1.0, 0.0)
    bias = jnp.where(cnt < float(SEL_TOPN), 0.0, MASKED).astype(BF16)
    sel_bias = jnp.concatenate([bias[:, g * tq:(g + 1) * tq] for g in (0, 0, 1, 1)], axis=1)

    q_ext = jnp.concatenate([q_all, sel_bias, jnp.zeros((LANES - n_blk, n_q), BF16)], axis=0)

    def sel_scores(j):
        rows = pl.ds(pl.multiple_of(j * tk, tk), tk)
        blk_of_key = (j * tk + lax.broadcasted_iota(jnp.int32, (tk, LANES), 0)) // SEL_BLOCK
        onehot = (lax.broadcasted_iota(jnp.int32, (tk, LANES), 1) == blk_of_key).astype(BF16)
        k_ext = jnp.concatenate([k_ref[0, rows, 0:128], onehot], axis=1)
        return jnp.dot(k_ext, q_ext, preferred_element_type=F32), vt_ref[0, 0:128, rows]

    def sel_step(j, carry):
        s, v_t = sel_scores(j)
        return _softmax_step(s, carry, v_t)

    last = (t0 + tq - 1) // tk
    carry = lax.fori_loop(0, last, sel_step, _softmax_init(n_q))
    s, v_t = sel_scores(last)
    key = last * tk + lax.broadcasted_iota(jnp.int32, (tk, n_q), 0)
    _, l_sel, acc_sel = _softmax_step(jnp.where(key <= tok, s, MASKED), carry, v_t)
    o_sel = acc_sel / l_sel

    span = WINDOW + tq
    start = pl.multiple_of(jnp.maximum(t0 - WINDOW, 0), tq)
    key = start + lax.broadcasted_iota(jnp.int32, (span, n_q), 0)
    s = jnp.dot(k_ref[0, pl.ds(start, span), 128:256], q_all, preferred_element_type=F32)
    s = jnp.where((key <= tok) & (tok - key < WINDOW), s, MASKED)
    e = jnp.exp(s - jnp.max(s, axis=0, keepdims=True))
    o_win = (jnp.dot(vt_ref[0, 128:256, pl.ds(start, span)], e.astype(BF16), preferred_element_type=F32)
             / jnp.sum(e, axis=0, keepdims=True))

    out = [jnp.zeros((LANES, tq), F32), jnp.zeros((LANES, tq), F32)]
    for g in range(2):
        for r in range(2):
            cols = slice((2 * g + r) * tq, (2 * g + r + 1) * tq)
            c = g * 6 + r * 3
            o = (gates_t[c:c + 1, :] * o_cmp[:, cols] + gates_t[c + 1:c + 2, :] * o_sel[:, cols]
                 + gates_t[c + 2:c + 3, :] * o_win[:, cols])
            out[r] = jnp.where(sub_grp == g, o, out[r])
    o_ref[0, :, 0:128] = jnp.transpose(out[0]).astype(BF16)
    o_ref[0, :, 128:256] = jnp.transpose(out[1]).astype(BF16)


def _nsa_attend(q_t, k2, v_t, kcmp, vcmp_t, gates_t, tq=256, tk=512):
    B, S, _ = k2.shape
    assert tk % tq == 0 and S % tk == 0 and WINDOW % tq == 0
    covt = _cover_t(S)
    tile_t = lambda w: pl.BlockSpec((1, w, tq), lambda b, i: (b, 0, i))
    per_b = lambda r, w: pl.BlockSpec((1, r, w), lambda b, i: (b, 0, 0))
    return pl.pallas_call(
        functools.partial(_nsa_kernel, tq=tq, tk=tk),
        grid=(B, S // tq),
        in_specs=[tile_t(256), per_b(S, 256), per_b(256, S), per_b(S // CMP_STRIDE, 128), per_b(128, S // CMP_STRIDE),
                  tile_t(128), pl.BlockSpec(covt.shape, lambda b, i: (0, 0))],
        out_specs=pl.BlockSpec((1, tq, 256), lambda b, i: (b, i, 0)),
        out_shape=jax.ShapeDtypeStruct((B, S, 256), BF16),
        compiler_params=_cparams(("parallel", "arbitrary")),
        name="nsa_attend",
    )(q_t, k2, v_t, kcmp, vcmp_t, gates_t, covt)


def _pool_kernel(u_ref, w_ref, scale_ref, o_ref):
    u = u_ref[0]
    row = lax.broadcasted_iota(jnp.int32, u.shape, 0)
    grp = lax.broadcasted_iota(jnp.int32, u.shape, 1) // HEAD_DIM

    def back(x, k):
        return jnp.where(row >= k, pltpu.roll(x, k, 0), 0.0)

    s2 = u + back(u, 1)
    s4 = s2 + back(s2, 2)
    s8 = s4 + back(s4, 4)
    s16 = s8 + back(s8, 8)
    total = jnp.where(grp == 0, s2, jnp.where(grp == 1, s4, jnp.where(grp == 2, s8, s16)))
    win = jnp.where(grp == 0, 2, jnp.where(grp == 1, 4, jnp.where(grp == 2, 8, 16)))
    count = jnp.minimum(row + 1, win).astype(F32)
    d = total / count - u
    y = jnp.dot(d.astype(BF16), w_ref[...], preferred_element_type=F32)
    o_ref[0] = (y * scale_ref[...]).astype(BF16)


def _pool(u, w_bd, scale):
    B, S, W = u.shape
    per_b = pl.BlockSpec((1, S, W), lambda b: (b, 0, 0))
    full = lambda a: pl.BlockSpec(a.shape, lambda b: (0,) * a.ndim)
    return pl.pallas_call(
        _pool_kernel,
        grid=(B,),
        in_specs=[per_b, full(w_bd), full(scale)],
        out_specs=per_b,
        out_shape=jax.ShapeDtypeStruct((B, S, W), BF16),
        compiler_params=_cparams(("parallel",)),
        name="pool",
    )(u, w_bd, scale)


def _block_diag(w):
    G, a, b = w.shape
    return jnp.einsum('gab,gh->gahb', w, jnp.eye(G, dtype=w.dtype)).reshape(G * a, G * b)


def _dot_split_lhs(a, b_exact, terms):
    out, rem = None, a
    for i in range(terms):
        hi = rem.astype(BF16)
        d = jnp.dot(hi, b_exact, preferred_element_type=F32)
        out = d if out is None else out + d
        if i + 1 < terms:
            rem = rem - hi.astype(F32)
    return out


def _dot_split_rhs(a_exact, b, terms):
    out, rem = None, b
    for i in range(terms):
        hi = rem.astype(BF16)
        d = jnp.dot(a_exact, hi, preferred_element_type=F32)
        out = d if out is None else out + d
        if i + 1 < terms:
            rem = rem - hi.astype(F32)
    return out


def _dot_tn(a, b):
    return lax.dot_general(a, b, (((0,), (0,)), ((), ())), preferred_element_type=F32)


def _stack_heads(x, width, n_heads=4):
    head = lax.broadcasted_iota(jnp.int32, x.shape, 1) // width
    return jnp.concatenate([jnp.where(head == h, x, 0.0) for h in range(n_heads)], axis=0)


def _unstack_heads(x, C, n_heads=4):
    out = x[0:C]
    for h in range(1, n_heads):
        out = out + x[h * C:(h + 1) * C]
    return out


def _block_masks(C, n_heads=4):
    n = C * n_heads
    row = lax.broadcasted_iota(jnp.int32, (n, n), 0)
    col = lax.broadcasted_iota(jnp.int32, (n, n), 1)
    same = (row // C) == (col // C)
    return same & (row > col), same & (row >= col)


def _tri_ones(C):
    return jnp.asarray(np.tril(np.ones((C, C), np.float32)), BF16)


def _head_ones(width, n_heads=4):
    return jnp.asarray(np.kron(np.eye(n_heads, dtype=np.float32), np.ones((width, width), np.float32)), BF16)


def _gla_kernel(gl_ref, a2_ref, ab_ref, norm_ref, tri_ref, ones_ref, o_ref, state_ref, *, C):
    s = pl.program_id(1)

    @pl.when(s == 0)
    def _():
        state_ref[...] = jnp.zeros_like(state_ref)

    gl = gl_ref[0]
    tm = gl.shape[0]
    q = gl[:, 0:128] * (GLA_DK ** -0.5)
    k = gl[:, 128:256]
    v = gl[:, 256:512]
    og = gl[:, 512:768]
    ad = gl[:, 768:896]
    x = jnp.dot(ad.astype(BF16), a2_ref[...], preferred_element_type=F32) + ab_ref[...]
    log_a = jax.nn.log_sigmoid(x) * (1.0 / GLA_TAU)
    strict, incl = _block_masks(C)
    kv_mask = (lax.broadcasted_iota(jnp.int32, (256, 128), 0) // HEAD_DIM
               == lax.broadcasted_iota(jnp.int32, (256, 128), 1) // GLA_DK)
    chunks = []
    for c in range(tm // C):
        rc = slice(c * C, (c + 1) * C)
        bcum = _dot_split_rhs(tri_ref[...], log_a[rc], 3)
        mid = bcum[C // 2 - 1:C // 2, :]
        last = bcum[C - 1:C, :]
        q_m = _stack_heads(q[rc] * jnp.exp(bcum - mid), GLA_DK).astype(BF16)
        k_m = _stack_heads(k[rc] * jnp.exp(mid - bcum), GLA_DK).astype(BF16)
        k_end = k[rc] * jnp.exp(last - bcum)
        att = jnp.where(incl, _dot_nt(q_m, k_m), 0.0)
        v_st = _stack_heads(v[rc], HEAD_DIM).astype(BF16)
        chunks.append(dict(
            o_intra=_unstack_heads(jnp.dot(att.astype(BF16), v_st, preferred_element_type=F32), C),
            q_in=(q[rc] * jnp.exp(bcum)).astype(BF16), decay=jnp.exp(last),
            upd=jnp.where(kv_mask, _dot_tn(v[rc].astype(BF16), k_end.astype(BF16)), 0.0)))
    outs = []
    state = state_ref[...]
    for ch in chunks:
        outs.append(ch['o_intra'] + _dot_nt(ch['q_in'], state.astype(BF16)))
        state = state * ch['decay'] + ch['upd']
    state_ref[...] = state
    o = jnp.concatenate(outs, axis=0)
    ms = _dot_split_lhs(o * o, ones_ref[...], 2) * (1.0 / HEAD_DIM)
    o = o * lax.rsqrt(ms + NORM_EPS) * norm_ref[...]
    o_ref[0] = (o * (og * jax.nn.sigmoid(og))).astype(BF16)


def _gla(gl, a2p, ab, norm4, tm=512):
    B, S, W = gl.shape
    C = GLA_CHUNK
    tri, ones = _tri_ones(C), _head_ones(HEAD_DIM)
    tile = lambda w: pl.BlockSpec((1, tm, w), lambda b, s: (b, s, 0))
    full = lambda a: pl.BlockSpec(a.shape, lambda b, s: (0,) * a.ndim)
    return pl.pallas_call(
        functools.partial(_gla_kernel, C=C),
        grid=(B, S // tm),
        in_specs=[tile(W), full(a2p), full(ab), full(norm4), full(tri), full(ones)],
        out_specs=tile(256),
        out_shape=jax.ShapeDtypeStruct((B, S, 256), BF16),
        scratch_shapes=[pltpu.VMEM((256, 128), F32)],
        compiler_params=_cparams(("parallel", "arbitrary")),
        name="gla",
    )(gl, a2p, ab, norm4, tri, ones)


_RV_W0, _RV_A0, _RV_KK, _RV_KA, _RV_RK, _RV_LN, _RV_V0 = range(7)


def _rwkv_kernel(*refs, has_vres, C):
    if has_vres:
        (rw_ref, h_ref, vf_ref, wz_ref, vec_ref, v1_ref, v2_ref, tri_ref, ones_ref, o_ref, state_ref) = refs
    else:
        (rw_ref, wz_ref, vec_ref, tri_ref, ones_ref, o_ref, vout_ref, state_ref) = refs
    s = pl.program_id(1)

    @pl.when(s == 0)
    def _():
        state_ref[...] = jnp.zeros_like(state_ref)

    rw = rw_ref[0]
    tm = rw.shape[0]
    r, k, v, z = rw[:, 0:256], rw[:, 256:512], rw[:, 512:768], rw[:, 768:896]
    vec = lambda i: vec_ref[i:i + 1, :]
    zl = lax.broadcasted_iota(jnp.int32, z.shape, 1)
    zf = jnp.where(zl < 32, jnp.tanh(z), jnp.where(zl < 64, z, jax.nn.sigmoid(z)))
    zz = jnp.dot(zf.astype(BF16), wz_ref[...], preferred_element_type=F32)
    w_log = -jax.nn.softplus(-(vec(_RV_W0) + zz[:, 0:256])) - 0.5
    lw = -jnp.exp(w_log)
    a_sig = jax.nn.sigmoid(vec(_RV_A0) + zz[:, 256:512])
    gate = zz[:, 512:768]
    if has_vres:
        low = jnp.dot(h_ref[0], v1_ref[...], preferred_element_type=F32)
        logit = vec(_RV_V0) + jnp.dot(low.astype(BF16), v2_ref[...], preferred_element_type=F32)
        v = v + (vf_ref[0] - v) * jax.nn.sigmoid(logit)
    else:
        vout_ref[0] = v
    kk = k * vec(_RV_KK)
    norm = jnp.sqrt(_dot_split_lhs(kk * kk, ones_ref[...], 2))
    kk = kk / jnp.maximum(norm, 1e-12)
    k = k * (1.0 + (a_sig - 1.0) * vec(_RV_KA))
    a_vec = -kk
    b_vec = kk * a_sig

    strict, incl = _block_masks(C)
    st = lambda t: _stack_heads(t, HEAD_DIM)
    bdot = lambda p, q: jnp.dot(p.astype(BF16), q.astype(BF16), preferred_element_type=F32)
    state_mask = (lax.broadcasted_iota(jnp.int32, (256, 256), 0) // HEAD_DIM
                  == lax.broadcasted_iota(jnp.int32, (256, 256), 1) // HEAD_DIM)
    n4 = 4 * C
    chunks = []
    for c in range(tm // C):
        rc = slice(c * C, (c + 1) * C)
        lcum = _dot_split_rhs(tri_ref[...], lw[rc], 3)
        lex = lcum - lw[rc]
        mid = lcum[C // 2 - 1:C // 2, :]
        last = lcum[C - 1:C, :]
        e_mid = jnp.exp(mid - lcum)
        e_end = jnp.exp(last - lcum)
        left = jnp.concatenate([st(a_vec[rc] * jnp.exp(lex - mid)), st(r[rc] * jnp.exp(lcum - mid))], axis=0)
        right = jnp.concatenate([st(b_vec[rc] * e_mid), st(k[rc] * e_mid)], axis=0)
        g = _dot_nt(left.astype(BF16), right.astype(BF16))
        v_st = st(v[rc])
        n_ak = jnp.where(strict, g[0:n4, n4:2 * n4], 0.0)
        chunks.append(dict(
            p=jnp.where(strict, g[0:n4, 0:n4], 0.0),
            x=jnp.concatenate([st(a_vec[rc] * jnp.exp(lex)), bdot(n_ak, v_st)], axis=1),
            n_rb=jnp.where(incl, g[n4:2 * n4, 0:n4], 0.0),
            y_rk=bdot(jnp.where(incl, g[n4:2 * n4, n4:2 * n4], 0.0), v_st),
            r_abs=r[rc] * jnp.exp(lcum), decay=jnp.exp(last), v=v[rc],
            bk_end=jnp.concatenate([b_vec[rc] * e_end, k[rc] * e_end], axis=0).astype(BF16)))
    eye = (lax.broadcasted_iota(jnp.int32, (n4, n4), 0) == lax.broadcasted_iota(jnp.int32, (n4, n4), 1)).astype(F32)
    for ch in chunks:
        ch['t'] = eye + ch['p']
    for f in range(1, int(np.log2(C))):
        for ch in chunks:
            ch['p'] = bdot(ch['p'], ch['p'])
        for ch in chunks:
            ch['t'] = ch['t'] + bdot(ch['p'], ch['t'])
    for ch in chunks:
        ch['x'] = bdot(ch['t'], ch['x'])
        y_part = bdot(ch['n_rb'], ch['x'])
        a_eff = _unstack_heads(ch['x'][:, 0:256], C)
        u0 = _unstack_heads(ch['x'][:, 256:512], C)
        ch['r_eff'] = (ch['r_abs'] + _unstack_heads(y_part[:, 0:256], C)).astype(BF16)
        ch['y0'] = _unstack_heads(y_part[:, 256:512] + ch['y_rk'], C)
        ch['mix'] = jnp.where(state_mask, _dot_tn(a_eff.astype(BF16), ch['bk_end'][0:C]), 0.0).astype(BF16)
        ch['add'] = jnp.where(state_mask, _dot_tn(jnp.concatenate([u0, ch['v']], axis=0).astype(BF16), ch['bk_end']), 0.0)
    ys = []
    state = state_ref[...]
    for ch in chunks:
        sb = state.astype(BF16)
        ys.append(_dot_nt(ch['r_eff'], sb) + ch['y0'])
        state = state * ch['decay'] + jnp.dot(sb, ch['mix'], preferred_element_type=F32) + ch['add']
    state_ref[...] = state
    y = jnp.concatenate(ys, axis=0)
    inv = 1.0 / HEAD_DIM
    mean = _dot_split_lhs(y, ones_ref[...], 2) * inv
    yc = y - mean
    var = _dot_split_lhs(yc * yc, ones_ref[...], 2) * inv
    y = yc * lax.rsqrt(var + RWKV_GN_EPS) * vec(_RV_LN)
    y = y + _dot_split_lhs(r * k * vec(_RV_RK), ones_ref[...], 2) * v
    o_ref[0] = (y * gate).astype(BF16)


def _rwkv(rw, wz, vecs, h=None, v_first=None, v1p=None, v2p=None, tm=256):
    B, S, W = rw.shape
    C = RWKV_CHUNK
    has_vres = h is not None
    tri, ones = _tri_ones(C), _head_ones(HEAD_DIM)
    tile = lambda w: pl.BlockSpec((1, tm, w), lambda b, s: (b, s, 0))
    full = lambda a: pl.BlockSpec(a.shape, lambda b, s: (0,) * a.ndim)
    if has_vres:
        args = (rw, h, v_first, wz, vecs, v1p, v2p, tri, ones)
        in_specs = [tile(W), tile(D_MODEL), tile(256), full(wz), full(vecs), full(v1p), full(v2p), full(tri), full(ones)]
        out_specs = tile(256)
        out_shape = jax.ShapeDtypeStruct((B, S, 256), BF16)
    else:
        args = (rw, wz, vecs, tri, ones)
        in_specs = [tile(W), full(wz), full(vecs), full(tri), full(ones)]
        out_specs = [tile(256), tile(256)]
        out_shape = [jax.ShapeDtypeStruct((B, S, 256), BF16), jax.ShapeDtypeStruct((B, S, 256), F32)]
    return pl.pallas_call(
        functools.partial(_rwkv_kernel, has_vres=has_vres, C=C),
        grid=(B, S // tm),
        in_specs=in_specs,
        out_specs=out_specs,
        out_shape=out_shape,
        scratch_shapes=[pltpu.VMEM((256, 256), F32)],
        compiler_params=_cparams(("parallel", "arbitrary")),
        name="rwkv_vres" if has_vres else "rwkv",
    )(*args)


def _rwkv_weights(l, rwkv_w0, rwkv_w2, rwkv_a0, rwkv_a2, rwkv_g2, rwkv_v0, rwkv_k_k, rwkv_k_a, rwkv_r_k, rwkv_ln):
    wz = jnp.zeros((128, 768), F32)
    wz = wz.at[0:32, 0:256].set(rwkv_w2[l]).at[32:64, 256:512].set(rwkv_a2[l]).at[64:128, 512:768].set(rwkv_g2[l])
    v0 = rwkv_v0[l - 1] if l > 0 else jnp.zeros((256,), F32)
    vecs = jnp.stack([rwkv_w0[l], rwkv_a0[l], rwkv_k_k[l], rwkv_k_a[l], rwkv_r_k[l].reshape(-1), rwkv_ln[l], v0,
                      jnp.zeros((256,), F32)])
    return wz.astype(BF16), vecs


def _merge_kernel(x_ref, h_ref, o0_ref, o1_ref, o2_ref, o3_ref, wg_ref, bg_ref, wb_ref, wo_ref, out_ref):
    h = h_ref[0]
    merged = None
    for i, o_ref in enumerate((o0_ref, o1_ref, o2_ref, o3_ref)):
        gate = jax.nn.sigmoid(jnp.dot(h, wg_ref[i], preferred_element_type=F32) + bg_ref[i:i + 1, :])
        term = gate * jnp.dot(o_ref[0], wb_ref[i], preferred_element_type=F32)
        merged = term if merged is None else merged + term
    out_ref[0] = x_ref[0] + jnp.dot(merged.astype(BF16), wo_ref[...], preferred_element_type=F32)


def _merge(x, h, o_nsa, o_pool, o_rwkv, o_gla, wg, bg, wb, wo, tm=512):
    B, S, D = x.shape
    tile = lambda w: pl.BlockSpec((1, tm, w), lambda b, s: (b, s, 0))
    full = lambda a: pl.BlockSpec(a.shape, lambda b, s: (0,) * a.ndim)
    return pl.pallas_call(
        _merge_kernel,
        grid=(B, S // tm),
        in_specs=[tile(D), tile(D), tile(256), tile(256), tile(256), tile(256), full(wg), full(bg), full(wb), full(wo)],
        out_specs=tile(D),
        out_shape=jax.ShapeDtypeStruct((B, S, D), F32),
        compiler_params=_cparams(("parallel", "parallel")),
        name="merge",
    )(x, h, o_nsa, o_pool, o_rwkv, o_gla, wg, bg, wb, wo)


def _memkv_kernel(mem_ref, g_ref, w_ref, o_ref):
    mn = _rms(mem_ref[0], g_ref[...]).astype(BF16)
    o_ref[0] = jnp.dot(mn, w_ref[...], preferred_element_type=F32).astype(BF16)


def _memkv(mem, g, wkv_all):
    B, M, D = mem.shape
    N = wkv_all.shape[1]
    return pl.pallas_call(
        _memkv_kernel,
        grid=(B,),
        in_specs=[pl.BlockSpec((1, M, D), lambda b: (b, 0, 0)), pl.BlockSpec(g.shape, lambda b: (0, 0)),
                  pl.BlockSpec(wkv_all.shape, lambda b: (0, 0))],
        out_specs=pl.BlockSpec((1, M, N), lambda b: (b, 0, 0)),
        out_shape=jax.ShapeDtypeStruct((B, M, N), BF16),
        compiler_params=_cparams(("parallel",)),
        name="memkv",
    )(mem, g, wkv_all)


def _cross_kernel(x_ref, g_ref, wq_ref, k_ref, v_ref, wo_ref, out_ref):
    x = x_ref[0]
    hc = _rms(x, g_ref[...]).astype(BF16)
    q = jnp.dot(hc, wq_ref[...], preferred_element_type=F32) * (HEAD_DIM ** -0.5)
    head = lax.broadcasted_iota(jnp.int32, q.shape, 1) // HEAD_DIM
    k, v = k_ref[0], v_ref[0]
    o = jnp.zeros(q.shape, F32)
    for hh in range(CA_WIDTH // HEAD_DIM):
        qh = jnp.where(head == hh, q, 0.0).astype(BF16)
        s = _dot_nt(qh, k)
        e = jnp.exp(s - jnp.max(s, axis=1, keepdims=True))
        p = e / jnp.sum(e, axis=1, keepdims=True)
        o = jnp.where(head == hh, jnp.dot(p.astype(BF16), v, preferred_element_type=F32), o)
    out_ref[0] = x + jnp.dot(o.astype(BF16), wo_ref[...], preferred_element_type=F32)


def _cross(x, g, wq, memkv, l, wo, tm=512):
    B, S, D = x.shape
    M = memkv.shape[1]
    tile = pl.BlockSpec((1, tm, D), lambda b, s: (b, s, 0))
    full = lambda a: pl.BlockSpec(a.shape, lambda b, s: (0,) * a.ndim)
    kspec = pl.BlockSpec((1, M, CA_WIDTH), lambda b, s: (b, 0, 2 * l))
    vspec = pl.BlockSpec((1, M, CA_WIDTH), lambda b, s: (b, 0, 2 * l + 1))
    return pl.pallas_call(
        _cross_kernel,
        grid=(B, S // tm),
        in_specs=[tile, full(g), full(wq), kspec, vspec, full(wo)],
        out_specs=tile,
        out_shape=jax.ShapeDtypeStruct((B, S, D), F32),
        compiler_params=_cparams(("parallel", "parallel")),
        name="cross_attn",
    )(x, g, wq, memkv, memkv, wo)


def _swiglu_cols(xb, wg, wu, wd, tf):
    acc = jnp.zeros((xb.shape[0], wd.shape[-1]), F32)
    for c in range(wg.shape[-1] // tf):
        cols = slice(c * tf, (c + 1) * tf)
        gate = jnp.dot(xb, wg[:, cols], preferred_element_type=F32)
        up = jnp.dot(xb, wu[:, cols], preferred_element_type=F32)
        act = gate * jax.nn.sigmoid(gate) * up
        acc = acc + jnp.dot(act.astype(BF16), wd[cols, :], preferred_element_type=F32)
    return acc


def _ffn_kernel(x_ref, g_ref, wg_ref, wu_ref, wd_ref, out_ref, *, tf):
    x = x_ref[...]
    out_ref[...] = x + _swiglu_cols(_rms(x, g_ref[...]).astype(BF16), wg_ref, wu_ref, wd_ref, tf)


def _ffn(x2, g, wg, wu, wd, tm=512, tf=256):
    T, D = x2.shape
    full = lambda a: pl.BlockSpec(a.shape, lambda i: (0,) * a.ndim)
    return pl.pallas_call(
        functools.partial(_ffn_kernel, tf=tf),
        grid=(T // tm,),
        in_specs=[pl.BlockSpec((tm, D), lambda i: (i, 0)), full(g), full(wg), full(wu), full(wd)],
        out_specs=pl.BlockSpec((tm, D), lambda i: (i, 0)),
        out_shape=jax.ShapeDtypeStruct((T, D), F32),
        compiler_params=_cparams(("parallel",)),
        name="ffn",
    )(x2, g, wg, wu, wd)


MOE_TM = 512


def _route_kernel(x_ref, g_ref, r_ref, hf_ref, idx_ref, w_ref):
    hf = _rms(x_ref[...], g_ref[...])
    hf_ref[...] = hf
    logits = jnp.dot(hf, r_ref[...], preferred_element_type=F32, precision=lax.Precision.HIGHEST)
    lane = lax.broadcasted_iota(jnp.int32, logits.shape, 1)
    big = logits.shape[1]
    logits = jnp.where(lane < N_EXPERTS, logits, -jnp.inf)
    m1 = jnp.max(logits, axis=1, keepdims=True)
    i1 = jnp.min(jnp.where(logits == m1, lane, big), axis=1, keepdims=True)
    rest = jnp.where(lane == i1, -jnp.inf, logits)
    m2 = jnp.max(rest, axis=1, keepdims=True)
    i2 = jnp.min(jnp.where(rest == m2, lane, big), axis=1, keepdims=True)
    e2 = jnp.exp(m2 - m1)
    w1 = 1.0 / (1.0 + e2)
    idx_ref[...] = jnp.where(lane == 0, i1, jnp.where(lane == 1, i2, 0))
    w_ref[...] = jnp.where(lane == 0, w1, jnp.where(lane == 1, e2 * w1, 0.0))


def _moe_route(x2, g, router_p, tm=1024):
    T, D = x2.shape
    row = lambda w: pl.BlockSpec((tm, w), lambda i: (i, 0))
    full = lambda a: pl.BlockSpec(a.shape, lambda i: (0,) * a.ndim)
    return pl.pallas_call(
        _route_kernel,
        grid=(T // tm,),
        in_specs=[row(D), full(g), full(router_p)],
        out_specs=[row(D), row(LANES), row(LANES)],
        out_shape=[jax.ShapeDtypeStruct((T, D), F32), jax.ShapeDtypeStruct((T, LANES), jnp.int32),
                   jax.ShapeDtypeStruct((T, LANES), F32)],
        compiler_params=_cparams(("parallel",)),
        name="moe_route",
    )(x2, g, router_p)


def _moe_plan(expert_of_pair, tm):
    onehot = (expert_of_pair[:, None] == jnp.arange(N_EXPERTS, dtype=jnp.int32)[None, :]).astype(jnp.int32)
    csum = jnp.cumsum(onehot, axis=0)
    rank = jnp.sum((csum - 1) * onehot, axis=1)
    counts = csum[-1]
    padded = ((counts + tm - 1) // tm) * tm
    ends = jnp.cumsum(padded)
    dest = ((ends - padded)[expert_of_pair] + rank).astype(jnp.int32)
    n_rows = expert_of_pair.shape[0] + N_EXPERTS * tm
    tile_expert = jnp.sum((jnp.arange(n_rows // tm, dtype=jnp.int32)[:, None] * tm >= ends[None, :]).astype(jnp.int32), axis=1)
    n_pairs = expert_of_pair.shape[0]
    pair_of_row = jnp.full((n_rows,), -1, jnp.int32).at[dest].set(jnp.arange(n_pairs, dtype=jnp.int32))
    real = pair_of_row >= 0
    token_of_row = jnp.where(real, pair_of_row // 2, 0)
    out_row = jnp.where(real, (pair_of_row % 2) * (n_pairs // 2) + pair_of_row // 2,
                        n_pairs + jnp.arange(n_rows, dtype=jnp.int32) % tm)
    return token_of_row, out_row, jnp.minimum(tile_expert, N_EXPERTS - 1).astype(jnp.int32)


DMA_UNROLL = 8


def _gather_rows(table_hbm, row_of, dst, n, sem, wait):
    def body(c, carry):
        for u in range(DMA_UNROLL):
            r = c * DMA_UNROLL + u
            copy = pltpu.make_async_copy(table_hbm.at[pl.ds(row_of(r), 1)], dst.at[pl.ds(r, 1)], sem)
            copy.wait() if wait else copy.start()
        return carry

    lax.fori_loop(0, n // DMA_UNROLL, body, 0)


def _scatter_rows(src, dst_hbm, row_of, n, sem, wait):
    def body(c, carry):
        for u in range(DMA_UNROLL):
            r = c * DMA_UNROLL + u
            copy = pltpu.make_async_copy(src.at[pl.ds(r, 1)], dst_hbm.at[pl.ds(row_of(r), 1)], sem)
            copy.wait() if wait else copy.start()
        return carry

    lax.fori_loop(0, n // DMA_UNROLL, body, 0)


def _experts_kernel(te_ref, tok_cur_ref, tok_next_ref, pair_prev_ref, pair_wait_ref, hf_hbm, wg_ref, wu_ref, wd_ref,
                    o2_hbm, xin, yout, sems, sem_out, *, tf):
    del te_ref
    i = pl.program_id(0)
    slot = i % 2
    tm = xin.shape[1]
    n_f = wg_ref.shape[2] // tf
    cur_tok = lambda r: tok_cur_ref[0, 0, r]
    next_tok = lambda r: tok_next_ref[0, 0, r]
    prev_pair = lambda r: pair_prev_ref[0, 0, r]

    @pl.when(i == 0)
    def _():
        _gather_rows(hf_hbm, cur_tok, xin.at[0], tm, sems.at[0], wait=False)
        yout[1] = jnp.zeros(yout.shape[1:], F32)

    @pl.when(i >= 1)
    def _():
        _scatter_rows(yout.at[slot], o2_hbm, lambda r: pair_wait_ref[0, 0, r], tm, sem_out, wait=True)

    _gather_rows(hf_hbm, cur_tok, xin.at[slot], tm, sems.at[slot], wait=True)
    xb = xin[slot].astype(BF16)
    acc = jnp.zeros((tm, wd_ref.shape[2]), F32)
    bounds = [round(c * tm / n_f) for c in range(n_f + 1)]
    for c in range(n_f):
        for r in range(bounds[c], bounds[c + 1]):
            pltpu.make_async_copy(hf_hbm.at[pl.ds(next_tok(r), 1)], xin.at[1 - slot, pl.ds(r, 1)],
                                  sems.at[1 - slot]).start()
            pltpu.make_async_copy(yout.at[1 - slot, pl.ds(r, 1)], o2_hbm.at[pl.ds(prev_pair(r), 1)], sem_out).start()
        cols = slice(c * tf, (c + 1) * tf)
        gate = jnp.dot(xb, wg_ref[0, :, cols], preferred_element_type=F32)
        up = jnp.dot(xb, wu_ref[0, :, cols], preferred_element_type=F32)
        act = gate * jax.nn.sigmoid(gate) * up
        acc = acc + jnp.dot(act.astype(BF16), wd_ref[0, cols, :], preferred_element_type=F32)
    yout[slot] = acc

    @pl.when(i == pl.num_programs(0) - 1)
    def _():
        _gather_rows(hf_hbm, next_tok, xin.at[1 - slot], tm, sems.at[1 - slot], wait=True)
        _scatter_rows(yout.at[1 - slot], o2_hbm, prev_pair, tm, sem_out, wait=True)


def _moe_experts(hf, token_of_row, out_row_of_row, tile_expert, wg, wu, wd, tf=256):
    T, D = hf.shape
    tm = MOE_TM
    n_tiles = token_of_row.shape[0] // tm
    tok = token_of_row.reshape(n_tiles, 1, tm)
    tok = jnp.concatenate([tok, tok[-1:], tok[-1:]], axis=0)
    spare = (2 * T + jnp.arange(tm, dtype=jnp.int32)).reshape(1, 1, tm)
    pair = jnp.concatenate([spare, out_row_of_row.reshape(n_tiles, 1, tm)], axis=0)
    te = jnp.concatenate([tile_expert, tile_expert[-1:]])
    smem = lambda index: pl.BlockSpec((1, 1, tm), lambda i, te: (index(i), 0, 0), memory_space=pltpu.SMEM)
    expert = lambda shape: pl.BlockSpec((1,) + shape, lambda i, te: (te[i], 0, 0))
    any_spec = pl.BlockSpec(memory_space=pl.ANY)
    grid_spec = pltpu.PrefetchScalarGridSpec(
        num_scalar_prefetch=1,
        grid=(n_tiles + 1,),
        in_specs=[smem(lambda i: i), smem(lambda i: i + 1), smem(lambda i: i), smem(lambda i: jnp.maximum(i - 1, 0)),
                  any_spec, expert(wg.shape[1:]), expert(wu.shape[1:]), expert(wd.shape[1:])],
        out_specs=any_spec,
        scratch_shapes=[pltpu.VMEM((2, tm, D), F32), pltpu.VMEM((2, tm, D), F32), pltpu.SemaphoreType.DMA((2,)),
                        pltpu.SemaphoreType.DMA(())],
    )
    return pl.pallas_call(
        functools.partial(_experts_kernel, tf=tf),
        grid_spec=grid_spec,
        out_shape=jax.ShapeDtypeStruct((2 * T + tm, D), F32),
        compiler_params=_cparams(("arbitrary",)),
        name="moe_experts",
    )(te, tok, tok, pair, pair, hf, wg, wu, wd)


def _combine_kernel(x_ref, w_ref, gf_ref, y0_ref, y1_ref, out_ref):
    w = w_ref[...]
    out_ref[...] = _rms(x_ref[...] + w[:, 0:1] * y0_ref[...] + w[:, 1:2] * y1_ref[...], gf_ref[...])


def _moe_combine(x2, w, o2, g_final, tm=512):
    T, D = x2.shape
    row = lambda wd, off: pl.BlockSpec((tm, wd), lambda i: (i + off, 0))
    return pl.pallas_call(
        _combine_kernel,
        grid=(T // tm,),
        in_specs=[row(D, 0), row(LANES, 0), pl.BlockSpec(g_final.shape, lambda i: (0, 0)), row(D, 0), row(D, T // tm)],
        out_specs=row(D, 0),
        out_shape=jax.ShapeDtypeStruct((T, D), F32),
        compiler_params=_cparams(("parallel",)),
        name="moe_combine",
    )(x2, w, g_final, o2, o2)


def _moe_routed(x2, g, router_p, wg, wu, wd, g_final):
    T, D = x2.shape
    hf, idx, w = _moe_route(x2, g, router_p)
    token_of_row, out_row_of_row, tile_expert = _moe_plan(idx[:, 0:2].reshape(2 * T), MOE_TM)
    o2 = _moe_experts(hf, token_of_row, out_row_of_row, tile_expert, wg, wu, wd)
    return _moe_combine(x2, w, o2, g_final)


def kernel(x, mem, positions, norm_mix, w_in, nsa_cmp_pe, nsa_cmp_w1, nsa_cmp_w2, pool_w, pool_scale, rwkv_mu, rwkv_w0,
           rwkv_w2, rwkv_a0, rwkv_a2, rwkv_g2, rwkv_v0, rwkv_v1, rwkv_v2, rwkv_k_k, rwkv_k_a, rwkv_r_k, rwkv_ln, gla_a2,
           gla_ab, gla_norm, w_gate, b_gate, w_branch, w_out, norm_ca, mem_norm, w_ca_q, w_ca_kv, w_ca_o, norm_ffn,
           ffn_w_gate, ffn_w_up, ffn_w_down, moe_router, moe_w_gate, moe_w_up, moe_w_down, norm_final):
    B, S, D = x.shape
    depth = norm_mix.shape[0]
    assert depth == 2 and S % 256 == 0, "the final norm is fused into the layer-1 expert mixer"
    cs = _rope_angles(positions)
    cs_end = jnp.pad(cs[:, CMP_LEN - 1::CMP_STRIDE], ((0, 0), (0, 1), (0, 0)))
    memkv = _memkv(mem, mem_norm[None], jnp.concatenate([w_ca_kv[l] for l in range(depth)], axis=1).astype(BF16))
    v_first = None
    for l in range(depth):
        h, q_t, k2, v_t, cv, gates_t, u, rw, gl = _inproj(x, norm_mix[l][None], _inproj_weights(w_in[l]), cs,
                                                          rwkv_mu[l][None])
        kcmp, vcmp_t = _nsa_compress(cv, *_compress_weights(nsa_cmp_pe[l], nsa_cmp_w1[l], nsa_cmp_w2[l]), cs_end)
        o_nsa = _nsa_attend(q_t, k2, v_t, kcmp, vcmp_t, gates_t)
        o_pool = _pool(u, _block_diag(pool_w[l]).astype(BF16), pool_scale[l][None])
        wz, vecs = _rwkv_weights(l, rwkv_w0, rwkv_w2, rwkv_a0, rwkv_a2, rwkv_g2, rwkv_v0, rwkv_k_k, rwkv_k_a,
                                 rwkv_r_k, rwkv_ln)
        if l == 0:
            o_rwkv, v_first = _rwkv(rw, wz, vecs)
        else:
            v1p = jnp.pad(rwkv_v1[l - 1], ((0, 0), (0, LANES - rwkv_v1.shape[2]))).astype(BF16)
            v2p = jnp.pad(rwkv_v2[l - 1], ((0, LANES - rwkv_v2.shape[1]), (0, 0))).astype(BF16)
            o_rwkv = _rwkv(rw, wz, vecs, h, v_first, v1p, v2p)
        a2p = jnp.pad(gla_a2[l], ((0, LANES - gla_a2.shape[1]), (0, 0))).astype(BF16)
        o_gla = _gla(gl, a2p, gla_ab[l][None], jnp.tile(gla_norm[l], 4)[None])
        wb_nsa = w_branch[l, 0].reshape(4, HEAD_DIM, D)[jnp.array([0, 2, 1, 3])].reshape(BRANCH_WIDTH, D)
        wb = jnp.concatenate([wb_nsa[None], w_branch[l, 1:]], axis=0).astype(BF16)
        x = _merge(x, h, o_nsa, o_pool, o_rwkv, o_gla, w_gate[l].astype(BF16), b_gate[l], wb, w_out[l].astype(BF16))
        x = _cross(x, norm_ca[l][None], w_ca_q[l].astype(BF16), memkv, l, w_ca_o[l].astype(BF16))
        j = l // 2
        x2 = x.reshape(B * S, D)
        if l % 2 == 0:
            x2 = _ffn(x2, norm_ffn[l][None], ffn_w_gate[j].astype(BF16), ffn_w_up[j].astype(BF16),
                      ffn_w_down[j].astype(BF16))
        else:
            router_p = jnp.pad(moe_router[j], ((0, 0), (0, LANES - N_EXPERTS)))
            x2 = _moe_routed(x2, norm_ffn[l][None], router_p, moe_w_gate[j].astype(BF16), moe_w_up[j].astype(BF16),
                             moe_w_down[j].astype(BF16), norm_final[None])
        x = x2.reshape(B, S, D)
    return x
```

```python
import functools

import jax
import jax.numpy as jnp
import numpy as np
from jax import lax
from jax.experimental import pallas as pl
from jax.experimental.pallas import tpu as pltpu

F32 = jnp.float32
BF16 = jnp.bfloat16

D_MODEL = 1024
HEAD_DIM = 64
BRANCH_WIDTH = 256
ROPE_THETA = 500000.0
ROPE_DIM = 16
NORM_EPS = 1e-6
NEG_INF = -1e30

NSA_KV = 128
CMP_LEN = 32
CMP_STRIDE = 16
CMP_HIDDEN = 128
SEL_BLOCK = 64
SEL_TOPN = 8
WINDOW = 512

RWKV_IN = 896
RWKV_GN_EPS = 64e-5
RWKV_CHUNK = 64

GLA_DK = 32
GLA_TAU = 16.0
GLA_CHUNK = 64

CA_WIDTH = 256
D_FF = 2816
N_EXPERTS = 8

LANES = 128
VMEM_LIMIT = 56 * 1024 * 1024

_C_Q = 0
_C_KV = 256
_C_CV = 768
_C_GATE = 1024
_C_POOL = 1152
_C_RWKV = 1408
_C_GLA = 2304
_C_END = 3200


def _cparams(sem):
    return pltpu.CompilerParams(dimension_semantics=sem, vmem_limit_bytes=VMEM_LIMIT)


def _rms(x, g):
    ms = jnp.mean(x * x, axis=-1, keepdims=True)
    return x * lax.rsqrt(ms + NORM_EPS) * g


def _rope128(t, cos, sin):
    lane = lax.broadcasted_iota(jnp.int32, t.shape, 1) % HEAD_DIM
    partner = jnp.where(lane < ROPE_DIM // 2, pltpu.roll(t, LANES - ROPE_DIM // 2, 1),
                        pltpu.roll(t, ROPE_DIM // 2, 1))
    return t * cos + partner * sin


def _inproj_kernel(x_ref, g_ref, w_ref, cs_ref, e_ref, ones_ref, mu_ref,
                   h_ref, qt_ref, k_ref, vt_ref, cv_ref, gt_ref, u_ref, rw_ref, gl_ref, prev_ref):
    s = pl.program_id(1)
    hb = _rms(x_ref[0], g_ref[...]).astype(BF16)
    h_ref[0] = hb

    def mm(a, b):
        return jnp.dot(hb, w_ref[:, a:b], preferred_element_type=F32)

    cos, sin = _rope_expand(cs_ref[0], e_ref[...], ones_ref[...])
    scale = HEAD_DIM ** -0.5
    qt_ref[0, 0:128, :] = jnp.transpose(_rope128(mm(_C_Q, _C_Q + 128), cos, sin) * scale).astype(BF16)
    qt_ref[0, 128:256, :] = jnp.transpose(_rope128(mm(_C_Q + 128, _C_Q + 256), cos, sin) * scale).astype(BF16)
    k_ref[0, :, 0:128] = _rope128(mm(_C_KV, _C_KV + 128), cos, sin).astype(BF16)
    vt_ref[0, 0:128, :] = jnp.transpose(mm(_C_KV + 128, _C_KV + 256)).astype(BF16)
    k_ref[0, :, 128:256] = _rope128(mm(_C_KV + 256, _C_KV + 384), cos, sin).astype(BF16)
    vt_ref[0, 128:256, :] = jnp.transpose(mm(_C_KV + 384, _C_KV + 512)).astype(BF16)
    cv_ref[0] = mm(_C_CV, _C_CV + 256).astype(BF16)
    gt_ref[0] = jnp.transpose(jax.nn.sigmoid(mm(_C_GATE, _C_GATE + 128)))
    u_ref[0] = mm(_C_POOL, _C_POOL + 256)
    gl_ref[0] = mm(_C_GLA, _C_END)

    p = mm(_C_RWKV, _C_RWKV + RWKV_IN)
    tm = p.shape[0]

    @pl.when(s == 0)
    def _():
        prev_ref[...] = jnp.zeros_like(prev_ref)

    row = lax.broadcasted_iota(jnp.int32, p.shape, 0)
    shifted = jnp.where(row == 0, prev_ref[0:1, :], pltpu.roll(p, 1, 0))
    prev_ref[0:1, :] = p[tm - 1:tm, :]
    rw_ref[0] = p + (shifted - p) * mu_ref[...]


def _inproj(x, g, w_all, cs, mu, tm=512):
    B, S, D = x.shape
    e, ones = _rope_expander()
    tok = lambda w: pl.BlockSpec((1, tm, w), lambda b, s: (b, s, 0))
    tok_t = lambda w: pl.BlockSpec((1, w, tm), lambda b, s: (b, 0, s))
    full = lambda a: pl.BlockSpec(a.shape, lambda b, s: (0,) * a.ndim)
    outs = [(D, BF16, True), (256, BF16, False), (256, BF16, True), (256, BF16, False), (256, BF16, True),
            (128, F32, False), (256, F32, True), (RWKV_IN, F32, True), (896, F32, True)]
    return pl.pallas_call(
        _inproj_kernel,
        grid=(B, S // tm),
        in_specs=[tok(D), full(g), full(w_all), tok(ROPE_DIM), full(e), full(ones), full(mu)],
        out_specs=[tok(w) if major else tok_t(w) for w, _, major in outs],
        out_shape=[jax.ShapeDtypeStruct((B, S, w) if major else (B, w, S), dt) for w, dt, major in outs],
        scratch_shapes=[pltpu.VMEM((8, RWKV_IN), F32)],
        compiler_params=_cparams(("parallel", "arbitrary")),
        name="inproj",
    )(x, g, w_all, cs, e, ones, mu)


def _inproj_columns():
    span = lambda lo, n: list(range(lo, lo + n))
    pad = lambda n: [-1] * n
    nsa, pool, rwkv, gla = 0, 1036, 1292, 2188
    cols = (span(nsa, 64) + span(nsa + 128, 64) + span(nsa + 64, 64) + span(nsa + 192, 64)
            + span(nsa + 512, 512)
            + span(nsa + 256, 256)
            + span(nsa + 1024, 12) + pad(116)
            + span(pool, 256) + span(rwkv, RWKV_IN)
            + span(gla, 512) + span(gla + 528, 256) + span(gla + 512, 16) + pad(112))
    assert len(cols) == _C_END
    return np.asarray(cols, np.int32)


def _inproj_weights(w):
    onehot = (jnp.arange(w.shape[1], dtype=jnp.int32)[:, None] == jnp.asarray(_inproj_columns())[None, :]).astype(BF16)
    return jnp.dot(w.astype(BF16), onehot, preferred_element_type=F32).astype(BF16)


def _rope_angles(positions):
    inv_freq = ROPE_THETA ** (-jnp.arange(0, ROPE_DIM, 2, dtype=F32) / ROPE_DIM)
    ang = positions.astype(F32)[..., None] * inv_freq
    return jnp.concatenate([jnp.cos(ang), jnp.sin(ang)], axis=-1)


def _rope_expander():
    half = ROPE_DIM // 2
    e = np.zeros((ROPE_DIM, 2 * LANES), np.float32)
    ones = np.zeros((1, LANES), np.float32)
    for lane in range(LANES):
        d = lane % HEAD_DIM
        if d < ROPE_DIM:
            e[d % half, lane] = 1.0
            e[half + d % half, LANES + lane] = -1.0 if d < half else 1.0
        else:
            ones[0, lane] = 1.0
    return jnp.asarray(e, BF16), jnp.asarray(ones)


def _rope_expand(cs, e, ones):
    tab = _dot_split_lhs(cs, e, 3)
    return tab[:, 0:LANES] + ones, tab[:, LANES:2 * LANES]


def _gelu_tanh(x):
    return x * (0.5 * (1.0 + jnp.tanh(np.sqrt(2.0 / np.pi) * (x + 0.044715 * (x * x * x)))))


def _dot_nt(a, b):
    return lax.dot_general(a, b, (((1,), (1,)), ((), ())), preferred_element_type=F32)


def _compress_kernel(x_ref, wa_ref, wb_ref, bias_ref, w2k_ref, w2v_ref, cs_ref, e_ref, ones_ref, kc_ref, vc_ref):
    x = x_ref[0]
    a = jnp.dot(x, wa_ref[...], preferred_element_type=F32)
    b = jnp.dot(x, wb_ref[...], preferred_element_type=F32)
    n = a.shape[0]
    hid = _gelu_tanh(a + pltpu.roll(b, n - 1, 0) + bias_ref[...])
    kc = jnp.dot(hid[:, 0:256].astype(BF16), w2k_ref[...], preferred_element_type=F32)
    vc = jnp.dot(hid[:, 256:512].astype(BF16), w2v_ref[...], preferred_element_type=F32)
    cos, sin = _rope_expand(cs_ref[0], e_ref[...], ones_ref[...])
    kc_ref[0] = _rope128(kc, cos, sin).astype(BF16)
    vc_ref[0] = jnp.transpose(vc).astype(BF16)


def _nsa_compress(cv, wa, wb, bias, w2k, w2v, cs_end):
    B, S, _ = cv.shape
    nseg = S // CMP_STRIDE
    x = cv.reshape(B, nseg, CMP_STRIDE * 256)
    e, ones = _rope_expander()
    per_b = lambda r, w: pl.BlockSpec((1, r, w), lambda b: (b, 0, 0))
    full = lambda a: pl.BlockSpec(a.shape, lambda b: (0,) * a.ndim)
    return pl.pallas_call(
        _compress_kernel,
        grid=(B,),
        in_specs=[per_b(nseg, CMP_STRIDE * 256), full(wa), full(wb), full(bias), full(w2k), full(w2v),
                  per_b(nseg, ROPE_DIM), full(e), full(ones)],
        out_specs=[per_b(nseg, 128), per_b(nseg, 128)],
        out_shape=[jax.ShapeDtypeStruct((B, nseg, 128), BF16)] * 2,
        compiler_params=_cparams(("parallel",)),
        name="nsa_compress",
    )(x, wa, wb, bias, w2k, w2v, cs_end, e, ones)


def _compress_weights(pe, w1, w2):
    eye2 = jnp.eye(2, dtype=F32)
    half = lambda lo: jnp.einsum('jldf,pj,qg->lpqdjgf', w1[:, lo:lo + CMP_STRIDE], eye2, eye2).reshape(
        CMP_STRIDE * 256, 4 * CMP_HIDDEN).astype(BF16)
    bias = jnp.einsum('jld,jldf->jf', pe, w1)
    bias = jnp.broadcast_to(bias[:, None, :], (2, 2, CMP_HIDDEN)).reshape(1, 4 * CMP_HIDDEN)
    bd = lambda w: jnp.einsum('fd,gq->gfqd', w, eye2).reshape(2 * CMP_HIDDEN, 2 * HEAD_DIM).astype(BF16)
    return half(0), half(CMP_STRIDE), bias, bd(w2[0]), bd(w2[1])


def _cover_t(S):
    n_cmp = (S - CMP_LEN) // CMP_STRIDE + 1
    cmp_start = np.arange(n_cmp) * CMP_STRIDE
    slc_start = np.arange(S // SEL_BLOCK) * SEL_BLOCK
    cover = np.clip(np.minimum(cmp_start[:, None] + CMP_LEN, slc_start[None, :] + SEL_BLOCK)
                    - np.maximum(cmp_start[:, None], slc_start[None, :]), 0, None) / CMP_LEN
    out = np.zeros((S // SEL_BLOCK, S // CMP_STRIDE), np.float32)
    out[:, :n_cmp] = cover.T
    return jnp.asarray(out, BF16)


MASKED = 2.0 * NEG_INF


def _softmax_step(s, carry, v_t):
    m, l, acc = carry
    m_new = jnp.maximum(m, jnp.max(s, axis=0, keepdims=True))
    alpha = jnp.exp(m - m_new)
    p = jnp.exp(s - m_new)
    l = alpha * l + jnp.sum(p, axis=0, keepdims=True)
    acc = alpha * acc + jnp.dot(v_t, p.astype(BF16), preferred_element_type=F32)
    return m_new, l, acc


def _softmax_init(n_q):
    return (jnp.full((1, n_q), NEG_INF, F32), jnp.zeros((1, n_q), F32), jnp.zeros((LANES, n_q), F32))


def _nsa_kernel(qt_ref, k_ref, vt_ref, kc_ref, vct_ref, gt_ref, covt_ref, o_ref, *, tq, tk):
    i = pl.program_id(1)
    t0 = i * tq
    n_blk = covt_ref.shape[0]
    n_q = 4 * tq
    qa_t, qb_t = qt_ref[0, 0:128, :], qt_ref[0, 128:256, :]
    sub_grp = lax.broadcasted_iota(jnp.int32, (LANES, tq), 0) // HEAD_DIM
    gates_t = gt_ref[0]
    tok = t0 + (lax.broadcasted_iota(jnp.int32, (1, n_q), 1) & (tq - 1))
    q_all = jnp.concatenate([jnp.where(sub_grp == g, q_t, 0) for g in range(2) for q_t in (qa_t, qb_t)], axis=1)

    s = jnp.dot(kc_ref[0], q_all, preferred_element_type=F32)
    nrow = lax.broadcasted_iota(jnp.int32, s.shape, 0)
    s = jnp.where(nrow * CMP_STRIDE + (CMP_LEN - 1) <= tok, s, MASKED)
    e = jnp.exp(s - jnp.maximum(jnp.max(s, axis=0, keepdims=True), NEG_INF))
    den = jnp.sum(e, axis=0, keepdims=True)
    p = e / jnp.where(den > 0.0, den, 1.0)
    o_cmp = jnp.dot(vct_ref[0], p.astype(BF16), preferred_element_type=F32)

    p_grp = jnp.concatenate([p[:, 2 * g * tq:(2 * g + 1) * tq] + p[:, (2 * g + 1) * tq:(2 * g + 2) * tq]
                             for g in range(2)], axis=1)
    imp = _dot_split_rhs(covt_ref[...], p_grp, 2)
    blk = lax.broadcasted_iota(jnp.int32, imp.shape, 0)
    tok_b = t0 + (lax.broadcasted_iota(jnp.int32, imp.shape, 1) & (tq - 1))
    cur = tok_b // SEL_BLOCK
    forced = (blk == 0) | (blk == cur) | (blk == cur - 1)
    imp = jnp.where(forced, 1e30, jnp.where(blk * SEL_BLOCK <= tok_b, imp, -1e30))
    cnt = jnp.zeros(imp.shape, F32)
    for i2 in range(n_blk):
        row = imp[i2:i2 + 1, :]
        cnt = cnt + jnp.where((row > imp) | ((row == imp) & (blk > i2)), 1.0, 0.0)
    bias = jnp.where(cnt < float(SEL_TOPN), 0.0, MASKED).astype(BF16)
    sel_bias = jnp.concatenate([bias[:, g * tq:(g + 1) * tq] for g in (0, 0, 1, 1)], axis=1)

    q_ext = jnp.concatenate([q_all, sel_bias, jnp.zeros((LANES - n_blk, n_q), BF16)], axis=0)

    def sel_scores(j):
        rows = pl.ds(pl.multiple_of(j * tk, tk), tk)
        blk_of_key = (j * tk + lax.broadcasted_iota(jnp.int32, (tk, LANES), 0)) // SEL_BLOCK
        onehot = (lax.broadcasted_iota(jnp.int32, (tk, LANES), 1) == blk_of_key).astype(BF16)
        k_ext = jnp.concatenate([k_ref[0, rows, 0:128], onehot], axis=1)
        return jnp.dot(k_ext, q_ext, preferred_element_type=F32), vt_ref[0, 0:128, rows]

    def sel_step(j, carry):
        s, v_t = sel_scores(j)
        return _softmax_step(s, carry, v_t)

    last = (t0 + tq - 1) // tk
    carry = lax.fori_loop(0, last, sel_step, _softmax_init(n_q))
    s, v_t = sel_scores(last)
    key = last * tk + lax.broadcasted_iota(jnp.int32, (tk, n_q), 0)
    _, l_sel, acc_sel = _softmax_step(jnp.where(key <= tok, s, MASKED), carry, v_t)
    o_sel = acc_sel / l_sel

    span = WINDOW + tq
    start = pl.multiple_of(jnp.maximum(t0 - WINDOW, 0), tq)
    key = start + lax.broadcasted_iota(jnp.int32, (span, n_q), 0)
    s = jnp.dot(k_ref[0, pl.ds(start, span), 128:256], q_all, preferred_element_type=F32)
    s = jnp.where((key <= tok) & (tok - key < WINDOW), s, MASKED)
    e = jnp.exp(s - jnp.max(s, axis=0, keepdims=True))
    o_win = (jnp.dot(vt_ref[0, 128:256, pl.ds(start, span)], e.astype(BF16), preferred_element_type=F32)
             / jnp.sum(e, axis=0, keepdims=True))

    out = [jnp.zeros((LANES, tq), F32), jnp.zeros((LANES, tq), F32)]
    for g in range(2):
        for r in range(2):
            cols = slice((2 * g + r) * tq, (2 * g + r + 1) * tq)
            c = g * 6 + r * 3
            o = (gates_t[c:c + 1, :] * o_cmp[:, cols] + gates_t[c + 1:c + 2, :] * o_sel[:, cols]
                 + gates_t[c + 2:c + 3, :] * o_win[:, cols])
            out[r] = jnp.where(sub_grp == g, o, out[r])
    o_ref[0, :, 0:128] = jnp.transpose(out[0]).astype(BF16)
    o_ref[0, :, 128:256] = jnp.transpose(out[1]).astype(BF16)


def _nsa_attend(q_t, k2, v_t, kcmp, vcmp_t, gates_t, tq=256, tk=512):
    B, S, _ = k2.shape
    assert tk % tq == 0 and S % tk == 0 and WINDOW % tq == 0
    covt = _cover_t(S)
    tile_t = lambda w: pl.BlockSpec((1, w, tq), lambda b, i: (b, 0, i))
    per_b = lambda r, w: pl.BlockSpec((1, r, w), lambda b, i: (b, 0, 0))
    return pl.pallas_call(
        functools.partial(_nsa_kernel, tq=tq, tk=tk),
        grid=(B, S // tq),
        in_specs=[tile_t(256), per_b(S, 256), per_b(256, S), per_b(S // CMP_STRIDE, 128), per_b(128, S // CMP_STRIDE),
                  tile_t(128), pl.BlockSpec(covt.shape, lambda b, i: (0, 0))],
        out_specs=pl.BlockSpec((1, tq, 256), lambda b, i: (b, i, 0)),
        out_shape=jax.ShapeDtypeStruct((B, S, 256), BF16),
        compiler_params=_cparams(("parallel", "arbitrary")),
        name="nsa_attend",
    )(q_t, k2, v_t, kcmp, vcmp_t, gates_t, covt)


def _pool_kernel(u_ref, w_ref, scale_ref, o_ref):
    u = u_ref[0]
    row = lax.broadcasted_iota(jnp.int32, u.shape, 0)
    grp = lax.broadcasted_iota(jnp.int32, u.shape, 1) // HEAD_DIM

    def back(x, k):
        return jnp.where(row >= k, pltpu.roll(x, k, 0), 0.0)

    s2 = u + back(u, 1)
    s4 = s2 + back(s2, 2)
    s8 = s4 + back(s4, 4)
    s16 = s8 + back(s8, 8)
    total = jnp.where(grp == 0, s2, jnp.where(grp == 1, s4, jnp.where(grp == 2, s8, s16)))
    win = jnp.where(grp == 0, 2, jnp.where(grp == 1, 4, jnp.where(grp == 2, 8, 16)))
    count = jnp.minimum(row + 1, win).astype(F32)
    d = total / count - u
    y = jnp.dot(d.astype(BF16), w_ref[...], preferred_element_type=F32)
    o_ref[0] = (y * scale_ref[...]).astype(BF16)


def _pool(u, w_bd, scale):
    B, S, W = u.shape
    per_b = pl.BlockSpec((1, S, W), lambda b: (b, 0, 0))
    full = lambda a: pl.BlockSpec(a.shape, lambda b: (0,) * a.ndim)
    return pl.pallas_call(
        _pool_kernel,
        grid=(B,),
        in_specs=[per_b, full(w_bd), full(scale)],
        out_specs=per_b,
        out_shape=jax.ShapeDtypeStruct((B, S, W), BF16),
        compiler_params=_cparams(("parallel",)),
        name="pool",
    )(u, w_bd, scale)


def _block_diag(w):
    G, a, b = w.shape
    return jnp.einsum('gab,gh->gahb', w, jnp.eye(G, dtype=w.dtype)).reshape(G * a, G * b)


def _dot_split_lhs(a, b_exact, terms):
    out, rem = None, a
    for i in range(terms):
        hi = rem.astype(BF16)
        d = jnp.dot(hi, b_exact, preferred_element_type=F32)
        out = d if out is None else out + d
        if i + 1 < terms:
            rem = rem - hi.astype(F32)
    return out


def _dot_split_rhs(a_exact, b, terms):
    out, rem = None, b
    for i in range(terms):
        hi = rem.astype(BF16)
        d = jnp.dot(a_exact, hi, preferred_element_type=F32)
        out = d if out is None else out + d
        if i + 1 < terms:
            rem = rem - hi.astype(F32)
    return out


def _dot_tn(a, b):
    return lax.dot_general(a, b, (((0,), (0,)), ((), ())), preferred_element_type=F32)


def _stack_heads(x, width, n_heads=4):
    head = lax.broadcasted_iota(jnp.int32, x.shape, 1) // width
    return jnp.concatenate([jnp.where(head == h, x, 0.0) for h in range(n_heads)], axis=0)


def _unstack_heads(x, C, n_heads=4):
    out = x[0:C]
    for h in range(1, n_heads):
        out = out + x[h * C:(h + 1) * C]
    return out


def _block_masks(C, n_heads=4):
    n = C * n_heads
    row = lax.broadcasted_iota(jnp.int32, (n, n), 0)
    col = lax.broadcasted_iota(jnp.int32, (n, n), 1)
    same = (row // C) == (col // C)
    return same & (row > col), same & (row >= col)


def _tri_ones(C):
    return jnp.asarray(np.tril(np.ones((C, C), np.float32)), BF16)


def _head_ones(width, n_heads=4):
    return jnp.asarray(np.kron(np.eye(n_heads, dtype=np.float32), np.ones((width, width), np.float32)), BF16)


def _gla_kernel(gl_ref, a2_ref, ab_ref, norm_ref, tri_ref, ones_ref, o_ref, state_ref, *, C):
    s = pl.program_id(1)

    @pl.when(s == 0)
    def _():
        state_ref[...] = jnp.zeros_like(state_ref)

    gl = gl_ref[0]
    tm = gl.shape[0]
    q = gl[:, 0:128] * (GLA_DK ** -0.5)
    k = gl[:, 128:256]
    v = gl[:, 256:512]
    og = gl[:, 512:768]
    ad = gl[:, 768:896]
    x = jnp.dot(ad.astype(BF16), a2_ref[...], preferred_element_type=F32) + ab_ref[...]
    log_a = jax.nn.log_sigmoid(x) * (1.0 / GLA_TAU)
    strict, incl = _block_masks(C)
    kv_mask = (lax.broadcasted_iota(jnp.int32, (256, 128), 0) // HEAD_DIM
               == lax.broadcasted_iota(jnp.int32, (256, 128), 1) // GLA_DK)
    chunks = []
    for c in range(tm // C):
        rc = slice(c * C, (c + 1) * C)
        bcum = _dot_split_rhs(tri_ref[...], log_a[rc], 3)
        mid = bcum[C // 2 - 1:C // 2, :]
        last = bcum[C - 1:C, :]
        q_m = _stack_heads(q[rc] * jnp.exp(bcum - mid), GLA_DK).astype(BF16)
        k_m = _stack_heads(k[rc] * jnp.exp(mid - bcum), GLA_DK).astype(BF16)
        k_end = k[rc] * jnp.exp(last - bcum)
        att = jnp.where(incl, _dot_nt(q_m, k_m), 0.0)
        v_st = _stack_heads(v[rc], HEAD_DIM).astype(BF16)
        chunks.append(dict(
            o_intra=_unstack_heads(jnp.dot(att.astype(BF16), v_st, preferred_element_type=F32), C),
            q_in=(q[rc] * jnp.exp(bcum)).astype(BF16), decay=jnp.exp(last),
            upd=jnp.where(kv_mask, _dot_tn(v[rc].astype(BF16), k_end.astype(BF16)), 0.0)))
    outs = []
    state = state_ref[...]
    for ch in chunks:
        outs.append(ch['o_intra'] + _dot_nt(ch['q_in'], state.astype(BF16)))
        state = state * ch['decay'] + ch['upd']
    state_ref[...] = state
    o = jnp.concatenate(outs, axis=0)
    ms = _dot_split_lhs(o * o, ones_ref[...], 2) * (1.0 / HEAD_DIM)
    o = o * lax.rsqrt(ms + NORM_EPS) * norm_ref[...]
    o_ref[0] = (o * (og * jax.nn.sigmoid(og))).astype(BF16)


def _gla(gl, a2p, ab, norm4, tm=512):
    B, S, W = gl.shape
    C = GLA_CHUNK
    tri, ones = _tri_ones(C), _head_ones(HEAD_DIM)
    tile = lambda w: pl.BlockSpec((1, tm, w), lambda b, s: (b, s, 0))
    full = lambda a: pl.BlockSpec(a.shape, lambda b, s: (0,) * a.ndim)
    return pl.pallas_call(
        functools.partial(_gla_kernel, C=C),
        grid=(B, S // tm),
        in_specs=[tile(W), full(a2p), full(ab), full(norm4), full(tri), full(ones)],
        out_specs=tile(256),
        out_shape=jax.ShapeDtypeStruct((B, S, 256), BF16),
        scratch_shapes=[pltpu.VMEM((256, 128), F32)],
        compiler_params=_cparams(("parallel", "arbitrary")),
        name="gla",
    )(gl, a2p, ab, norm4, tri, ones)


_RV_W0, _RV_A0, _RV_KK, _RV_KA, _RV_RK, _RV_LN, _RV_V0 = range(7)


def _rwkv_kernel(*refs, has_vres, C):
    if has_vres:
        (rw_ref, h_ref, vf_ref, wz_ref, vec_ref, v1_ref, v2_ref, tri_ref, ones_ref, o_ref, state_ref) = refs
    else:
        (rw_ref, wz_ref, vec_ref, tri_ref, ones_ref, o_ref, vout_ref, state_ref) = refs
    s = pl.program_id(1)

    @pl.when(s == 0)
    def _():
        state_ref[...] = jnp.zeros_like(state_ref)

    rw = rw_ref[0]
    tm = rw.shape[0]
    r, k, v, z = rw[:, 0:256], rw[:, 256:512], rw[:, 512:768], rw[:, 768:896]
    vec = lambda i: vec_ref[i:i + 1, :]
    zl = lax.broadcasted_iota(jnp.int32, z.shape, 1)
    zf = jnp.where(zl < 32, jnp.tanh(z), jnp.where(zl < 64, z, jax.nn.sigmoid(z)))
    zz = jnp.dot(zf.astype(BF16), wz_ref[...], preferred_element_type=F32)
    w_log = -jax.nn.softplus(-(vec(_RV_W0) + zz[:, 0:256])) - 0.5
    lw = -jnp.exp(w_log)
    a_sig = jax.nn.sigmoid(vec(_RV_A0) + zz[:, 256:512])
    gate = zz[:, 512:768]
    if has_vres:
        low = jnp.dot(h_ref[0], v1_ref[...], preferred_element_type=F32)
        logit = vec(_RV_V0) + jnp.dot(low.astype(BF16), v2_ref[...], preferred_element_type=F32)
        v = v + (vf_ref[0] - v) * jax.nn.sigmoid(logit)
    else:
        vout_ref[0] = v
    kk = k * vec(_RV_KK)
    norm = jnp.sqrt(_dot_split_lhs(kk * kk, ones_ref[...], 2))
    kk = kk / jnp.maximum(norm, 1e-12)
    k = k * (1.0 + (a_sig - 1.0) * vec(_RV_KA))
    a_vec = -kk
    b_vec = kk * a_sig

    strict, incl = _block_masks(C)
    st = lambda t: _stack_heads(t, HEAD_DIM)
    bdot = lambda p, q: jnp.dot(p.astype(BF16), q.astype(BF16), preferred_element_type=F32)
    state_mask = (lax.broadcasted_iota(jnp.int32, (256, 256), 0) // HEAD_DIM
                  == lax.broadcasted_iota(jnp.int32, (256, 256), 1) // HEAD_DIM)
    n4 = 4 * C
    chunks = []
    for c in range(tm // C):
        rc = slice(c * C, (c + 1) * C)
        lcum = _dot_split_rhs(tri_ref[...], lw[rc], 3)
        lex = lcum - lw[rc]
        mid = lcum[C // 2 - 1:C // 2, :]
        last = lcum[C - 1:C, :]
        e_mid = jnp.exp(mid - lcum)
        e_end = jnp.exp(last - lcum)
        left = jnp.concatenate([st(a_vec[rc] * jnp.exp(lex - mid)), st(r[rc] * jnp.exp(lcum - mid))], axis=0)
        right = jnp.concatenate([st(b_vec[rc] * e_mid), st(k[rc] * e_mid)], axis=0)
        g = _dot_nt(left.astype(BF16), right.astype(BF16))
        v_st = st(v[rc])
        n_ak = jnp.where(strict, g[0:n4, n4:2 * n4], 0.0)
        chunks.append(dict(
            p=jnp.where(strict, g[0:n4, 0:n4], 0.0),
            x=jnp.concatenate([st(a_vec[rc] * jnp.exp(lex)), bdot(n_ak, v_st)], axis=1),
            n_rb=jnp.where(incl, g[n4:2 * n4, 0:n4], 0.0),
            y_rk=bdot(jnp.where(incl, g[n4:2 * n4, n4:2 * n4], 0.0), v_st),
            r_abs=r[rc] * jnp.exp(lcum), decay=jnp.exp(last), v=v[rc],
            bk_end=jnp.concatenate([b_vec[rc] * e_end, k[rc] * e_end], axis=0).astype(BF16)))
    eye = (lax.broadcasted_iota(jnp.int32, (n4, n4), 0) == lax.broadcasted_iota(jnp.int32, (n4, n4), 1)).astype(F32)
    for ch in chunks:
        ch['t'] = eye + ch['p']
    for f in range(1, int(np.log2(C))):
        for ch in chunks:
            ch['p'] = bdot(ch['p'], ch['p'])
        for ch in chunks:
            ch['t'] = ch['t'] + bdot(ch['p'], ch['t'])
    for ch in chunks:
        ch['x'] = bdot(ch['t'], ch['x'])
        y_part = bdot(ch['n_rb'], ch['x'])
        a_eff = _unstack_heads(ch['x'][:, 0:256], C)
        u0 = _unstack_heads(ch['x'][:, 256:512], C)
        ch['r_eff'] = (ch['r_abs'] + _unstack_heads(y_part[:, 0:256], C)).astype(BF16)
        ch['y0'] = _unstack_heads(y_part[:, 256:512] + ch['y_rk'], C)
        ch['mix'] = jnp.where(state_mask, _dot_tn(a_eff.astype(BF16), ch['bk_end'][0:C]), 0.0).astype(BF16)
        ch['add'] = jnp.where(state_mask, _dot_tn(jnp.concatenate([u0, ch['v']], axis=0).astype(BF16), ch['bk_end']), 0.0)
    ys = []
    state = state_ref[...]
    for ch in chunks:
        sb = state.astype(BF16)
        ys.append(_dot_nt(ch['r_eff'], sb) + ch['y0'])
        state = state * ch['decay'] + jnp.dot(sb, ch['mix'], preferred_element_type=F32) + ch['add']
    state_ref[...] = state
    y = jnp.concatenate(ys, axis=0)
    inv = 1.0 / HEAD_DIM
    mean = _dot_split_lhs(y, ones_ref[...], 2) * inv
    yc = y - mean
    var = _dot_split_lhs(yc * yc, ones_ref[...], 2) * inv
    y = yc * lax.rsqrt(var + RWKV_GN_EPS) * vec(_RV_LN)
    y = y + _dot_split_lhs(r * k * vec(_RV_RK), ones_ref[...], 2) * v
    o_ref[0] = (y * gate).astype(BF16)


def _rwkv(rw, wz, vecs, h=None, v_first=None, v1p=None, v2p=None, tm=256):
    B, S, W = rw.shape
    C = RWKV_CHUNK
    has_vres = h is not None
    tri, ones = _tri_ones(C), _head_ones(HEAD_DIM)
    tile = lambda w: pl.BlockSpec((1, tm, w), lambda b, s: (b, s, 0))
    full = lambda a: pl.BlockSpec(a.shape, lambda b, s: (0,) * a.ndim)
    if has_vres:
        args = (rw, h, v_first, wz, vecs, v1p, v2p, tri, ones)
        in_specs = [tile(W), tile(D_MODEL), tile(256), full(wz), full(vecs), full(v1p), full(v2p), full(tri), full(ones)]
        out_specs = tile(256)
        out_shape = jax.ShapeDtypeStruct((B, S, 256), BF16)
    else:
        args = (rw, wz, vecs, tri, ones)
        in_specs = [tile(W), full(wz), full(vecs), full(tri), full(ones)]
        out_specs = [tile(256), tile(256)]
        out_shape = [jax.ShapeDtypeStruct((B, S, 256), BF16), jax.ShapeDtypeStruct((B, S, 256), F32)]
    return pl.pallas_call(
        functools.partial(_rwkv_kernel, has_vres=has_vres, C=C),
        grid=(B, S // tm),
        in_specs=in_specs,
        out_specs=out_specs,
        out_shape=out_shape,
        scratch_shapes=[pltpu.VMEM((256, 256), F32)],
        compiler_params=_cparams(("parallel", "arbitrary")),
        name="rwkv_vres" if has_vres else "rwkv",
    )(*args)


def _rwkv_weights(l, rwkv_w0, rwkv_w2, rwkv_a0, rwkv_a2, rwkv_g2, rwkv_v0, rwkv_k_k, rwkv_k_a, rwkv_r_k, rwkv_ln):
    wz = jnp.zeros((128, 768), F32)
    wz = wz.at[0:32, 0:256].set(rwkv_w2[l]).at[32:64, 256:512].set(rwkv_a2[l]).at[64:128, 512:768].set(rwkv_g2[l])
    v0 = rwkv_v0[l - 1] if l > 0 else jnp.zeros((256,), F32)
    vecs = jnp.stack([rwkv_w0[l], rwkv_a0[l], rwkv_k_k[l], rwkv_k_a[l], rwkv_r_k[l].reshape(-1), rwkv_ln[l], v0,
                      jnp.zeros((256,), F32)])
    return wz.astype(BF16), vecs


def _merge_kernel(x_ref, h_ref, o0_ref, o1_ref, o2_ref, o3_ref, wg_ref, bg_ref, wb_ref, wo_ref, out_ref):
    h = h_ref[0]
    merged = None
    for i, o_ref in enumerate((o0_ref, o1_ref, o2_ref, o3_ref)):
        gate = jax.nn.sigmoid(jnp.dot(h, wg_ref[i], preferred_element_type=F32) + bg_ref[i:i + 1, :])
        term = gate * jnp.dot(o_ref[0], wb_ref[i], preferred_element_type=F32)
        merged = term if merged is None else merged + term
    out_ref[0] = x_ref[0] + jnp.dot(merged.astype(BF16), wo_ref[...], preferred_element_type=F32)


def _merge(x, h, o_nsa, o_pool, o_rwkv, o_gla, wg, bg, wb, wo, tm=512):
    B, S, D = x.shape
    tile = lambda w: pl.BlockSpec((1, tm, w), lambda b, s: (b, s, 0))
    full = lambda a: pl.BlockSpec(a.shape, lambda b, s: (0,) * a.ndim)
    return pl.pallas_call(
        _merge_kernel,
        grid=(B, S // tm),
        in_specs=[tile(D), tile(D), tile(256), tile(256), tile(256), tile(256), full(wg), full(bg), full(wb), full(wo)],
        out_specs=tile(D),
        out_shape=jax.ShapeDtypeStruct((B, S, D), F32),
        compiler_params=_cparams(("parallel", "parallel")),
        name="merge",
    )(x, h, o_nsa, o_pool, o_rwkv, o_gla, wg, bg, wb, wo)


def _memkv_kernel(mem_ref, g_ref, w_ref, o_ref):
    mn = _rms(mem_ref[0], g_ref[...]).astype(BF16)
    o_ref[0] = jnp.dot(mn, w_ref[...], preferred_element_type=F32).astype(BF16)


def _memkv(mem, g, wkv_all):
    B, M, D = mem.shape
    N = wkv_all.shape[1]
    return pl.pallas_call(
        _memkv_kernel,
        grid=(B,),
        in_specs=[pl.BlockSpec((1, M, D), lambda b: (b, 0, 0)), pl.BlockSpec(g.shape, lambda b: (0, 0)),
                  pl.BlockSpec(wkv_all.shape, lambda b: (0, 0))],
        out_specs=pl.BlockSpec((1, M, N), lambda b: (b, 0, 0)),
        out_shape=jax.ShapeDtypeStruct((B, M, N), BF16),
        compiler_params=_cparams(("parallel",)),
        name="memkv",
    )(mem, g, wkv_all)


def _cross_kernel(x_ref, g_ref, wq_ref, k_ref, v_ref, wo_ref, out_ref):
    x = x_ref[0]
    hc = _rms(x, g_ref[...]).astype(BF16)
    q = jnp.dot(hc, wq_ref[...], preferred_element_type=F32) * (HEAD_DIM ** -0.5)
    head = lax.broadcasted_iota(jnp.int32, q.shape, 1) // HEAD_DIM
    k, v = k_ref[0], v_ref[0]
    o = jnp.zeros(q.shape, F32)
    for hh in range(CA_WIDTH // HEAD_DIM):
        qh = jnp.where(head == hh, q, 0.0).astype(BF16)
        s = _dot_nt(qh, k)
        e = jnp.exp(s - jnp.max(s, axis=1, keepdims=True))
        p = e / jnp.sum(e, axis=1, keepdims=True)
        o = jnp.where(head == hh, jnp.dot(p.astype(BF16), v, preferred_element_type=F32), o)
    out_ref[0] = x + jnp.dot(o.astype(BF16), wo_ref[...], preferred_element_type=F32)


def _cross(x, g, wq, memkv, l, wo, tm=512):
    B, S, D = x.shape
    M = memkv.shape[1]
    tile = pl.BlockSpec((1, tm, D), lambda b, s: (b, s, 0))
    full = lambda a: pl.BlockSpec(a.shape, lambda b, s: (0,) * a.ndim)
    kspec = pl.BlockSpec((1, M, CA_WIDTH), lambda b, s: (b, 0, 2 * l))
    vspec = pl.BlockSpec((1, M, CA_WIDTH), lambda b, s: (b, 0, 2 * l + 1))
    return pl.pallas_call(
        _cross_kernel,
        grid=(B, S // tm),
        in_specs=[tile, full(g), full(wq), kspec, vspec, full(wo)],
        out_specs=tile,
        out_shape=jax.ShapeDtypeStruct((B, S, D), F32),
        compiler_params=_cparams(("parallel", "parallel")),
        name="cross_attn",
    )(x, g, wq, memkv, memkv, wo)


def _swiglu_cols(xb, wg, wu, wd, tf):
    acc = jnp.zeros((xb.shape[0], wd.shape[-1]), F32)
    for c in range(wg.shape[-1] // tf):
        cols = slice(c * tf, (c + 1) * tf)
        gate = jnp.dot(xb, wg[:, cols], preferred_element_type=F32)
        up = jnp.dot(xb, wu[:, cols], preferred_element_type=F32)
        act = gate * jax.nn.sigmoid(gate) * up
        acc = acc + jnp.dot(act.astype(BF16), wd[cols, :], preferred_element_type=F32)
    return acc


def _ffn_kernel(x_ref, g_ref, wg_ref, wu_ref, wd_ref, out_ref, *, tf):
    x = x_ref[...]
    out_ref[...] = x + _swiglu_cols(_rms(x, g_ref[...]).astype(BF16), wg_ref, wu_ref, wd_ref, tf)


def _ffn(x2, g, wg, wu, wd, tm=512, tf=256):
    T, D = x2.shape
    full = lambda a: pl.BlockSpec(a.shape, lambda i: (0,) * a.ndim)
    return pl.pallas_call(
        functools.partial(_ffn_kernel, tf=tf),
        grid=(T // tm,),
        in_specs=[pl.BlockSpec((tm, D), lambda i: (i, 0)), full(g), full(wg), full(wu), full(wd)],
        out_specs=pl.BlockSpec((tm, D), lambda i: (i, 0)),
        out_shape=jax.ShapeDtypeStruct((T, D), F32),
        compiler_params=_cparams(("parallel",)),
        name="ffn",
    )(x2, g, wg, wu, wd)


MOE_TM = 512


SUBLANES = 8


def _to_slabs(ref, x):
    w = ref.shape[-1]
    for j in range(ref.shape[-2]):
        ref[:, j, :] = x[:, j * w:(j + 1) * w]


def _from_slabs(ref):
    return jnp.concatenate([ref[:, j, :] for j in range(ref.shape[-2])], axis=1)


def _route_kernel(x_ref, g_ref, r_ref, hf_ref, idx_ref, w_ref):
    hf = _rms(x_ref[...], g_ref[...])
    _to_slabs(hf_ref, hf)
    logits = jnp.dot(hf, r_ref[...], preferred_element_type=F32, precision=lax.Precision.HIGHEST)
    lane = lax.broadcasted_iota(jnp.int32, logits.shape, 1)
    big = logits.shape[1]
    logits = jnp.where(lane < N_EXPERTS, logits, -jnp.inf)
    m1 = jnp.max(logits, axis=1, keepdims=True)
    i1 = jnp.min(jnp.where(logits == m1, lane, big), axis=1, keepdims=True)
    rest = jnp.where(lane == i1, -jnp.inf, logits)
    m2 = jnp.max(rest, axis=1, keepdims=True)
    i2 = jnp.min(jnp.where(rest == m2, lane, big), axis=1, keepdims=True)
    e2 = jnp.exp(m2 - m1)
    w1 = 1.0 / (1.0 + e2)
    idx_ref[...] = jnp.where(lane == 0, i1, jnp.where(lane == 1, i2, 0))
    w_ref[...] = jnp.where(lane == 0, w1, jnp.where(lane == 1, e2 * w1, 0.0))


def _moe_route(x2, g, router_p, tm=1024):
    T, D = x2.shape
    row = lambda w: pl.BlockSpec((tm, w), lambda i: (i, 0))
    full = lambda a: pl.BlockSpec(a.shape, lambda i: (0,) * a.ndim)
    return pl.pallas_call(
        _route_kernel,
        grid=(T // tm,),
        in_specs=[row(D), full(g), full(router_p)],
        out_specs=[pl.BlockSpec((tm, SUBLANES, D // SUBLANES), lambda i: (i, 0, 0)), row(LANES), row(LANES)],
        out_shape=[jax.ShapeDtypeStruct((T, SUBLANES, D // SUBLANES), F32), jax.ShapeDtypeStruct((T, LANES), jnp.int32),
                   jax.ShapeDtypeStruct((T, LANES), F32)],
        compiler_params=_cparams(("parallel",)),
        name="moe_route",
    )(x2, g, router_p)


def _moe_plan(expert_of_pair, tm):
    onehot = (expert_of_pair[:, None] == jnp.arange(N_EXPERTS, dtype=jnp.int32)[None, :]).astype(jnp.int32)
    csum = jnp.cumsum(onehot, axis=0)
    rank = jnp.sum((csum - 1) * onehot, axis=1)
    counts = csum[-1]
    padded = ((counts + tm - 1) // tm) * tm
    ends = jnp.cumsum(padded)
    dest = ((ends - padded)[expert_of_pair] + rank).astype(jnp.int32)
    n_rows = expert_of_pair.shape[0] + N_EXPERTS * tm
    tile_expert = jnp.sum((jnp.arange(n_rows // tm, dtype=jnp.int32)[:, None] * tm >= ends[None, :]).astype(jnp.int32), axis=1)
    n_pairs = expert_of_pair.shape[0]
    pair_of_row = jnp.full((n_rows,), -1, jnp.int32).at[dest].set(jnp.arange(n_pairs, dtype=jnp.int32))
    real = pair_of_row >= 0
    token_of_row = jnp.where(real, pair_of_row // 2, 0)
    out_row = jnp.where(real, (pair_of_row % 2) * (n_pairs // 2) + pair_of_row // 2,
                        n_pairs + jnp.arange(n_rows, dtype=jnp.int32) % tm)
    return token_of_row, out_row, jnp.minimum(tile_expert, N_EXPERTS - 1).astype(jnp.int32)


DMA_UNROLL = 8


def _gather_rows(table_hbm, row_of, dst, n, sem, wait):
    def body(c, carry):
        for u in range(DMA_UNROLL):
            r = c * DMA_UNROLL + u
            copy = pltpu.make_async_copy(table_hbm.at[pl.ds(row_of(r), 1)], dst.at[pl.ds(r, 1)], sem)
            copy.wait() if wait else copy.start()
        return carry

    lax.fori_loop(0, n // DMA_UNROLL, body, 0)


def _scatter_rows(src, dst_hbm, row_of, n, sem, wait):
    def body(c, carry):
        for u in range(DMA_UNROLL):
            r = c * DMA_UNROLL + u
            copy = pltpu.make_async_copy(src.at[pl.ds(r, 1)], dst_hbm.at[pl.ds(row_of(r), 1)], sem)
            copy.wait() if wait else copy.start()
        return carry

    lax.fori_loop(0, n // DMA_UNROLL, body, 0)


def _experts_kernel(te_ref, tok_cur_ref, tok_next_ref, pair_prev_ref, pair_wait_ref, hf_hbm, wg_ref, wu_ref, wd_ref,
                    o2_hbm, xin, yout, sems, sem_out, *, tf):
    del te_ref
    i = pl.program_id(0)
    slot = i % 2
    tm = xin.shape[1]
    n_f = wg_ref.shape[2] // tf
    cur_tok = lambda r: tok_cur_ref[0, 0, r]
    next_tok = lambda r: tok_next_ref[0, 0, r]
    prev_pair = lambda r: pair_prev_ref[0, 0, r]

    @pl.when(i == 0)
    def _():
        _gather_rows(hf_hbm, cur_tok, xin.at[0], tm, sems.at[0], wait=False)
        yout[1] = jnp.zeros(yout.shape[1:], F32)

    @pl.when(i >= 1)
    def _():
        _scatter_rows(yout.at[slot], o2_hbm, lambda r: pair_wait_ref[0, 0, r], tm, sem_out, wait=True)

    _gather_rows(hf_hbm, cur_tok, xin.at[slot], tm, sems.at[slot], wait=True)
    xb = _from_slabs(xin.at[slot]).astype(BF16)
    acc = jnp.zeros((tm, wd_ref.shape[2]), F32)
    bounds = [round(c * tm / n_f) for c in range(n_f + 1)]
    for c in range(n_f):
        for r in range(bounds[c], bounds[c + 1]):
            pltpu.make_async_copy(hf_hbm.at[pl.ds(next_tok(r), 1)], xin.at[1 - slot, pl.ds(r, 1)],
                                  sems.at[1 - slot]).start()
            pltpu.make_async_copy(yout.at[1 - slot, pl.ds(r, 1)], o2_hbm.at[pl.ds(prev_pair(r), 1)], sem_out).start()
        cols = slice(c * tf, (c + 1) * tf)
        gate = jnp.dot(xb, wg_ref[0, :, cols], preferred_element_type=F32)
        up = jnp.dot(xb, wu_ref[0, :, cols], preferred_element_type=F32)
        act = gate * jax.nn.sigmoid(gate) * up
        acc = acc + jnp.dot(act.astype(BF16), wd_ref[0, cols, :], preferred_element_type=F32)
    _to_slabs(yout.at[slot], acc)

    @pl.when(i == pl.num_programs(0) - 1)
    def _():
        _gather_rows(hf_hbm, next_tok, xin.at[1 - slot], tm, sems.at[1 - slot], wait=True)
        _scatter_rows(yout.at[1 - slot], o2_hbm, prev_pair, tm, sem_out, wait=True)


def _moe_experts(hf, token_of_row, out_row_of_row, tile_expert, wg, wu, wd, tf=256):
    T = hf.shape[0]
    slab = hf.shape[1:]
    tm = MOE_TM
    n_tiles = token_of_row.shape[0] // tm
    tok = token_of_row.reshape(n_tiles, 1, tm)
    tok = jnp.concatenate([tok, tok[-1:], tok[-1:]], axis=0)
    spare = (2 * T + jnp.arange(tm, dtype=jnp.int32)).reshape(1, 1, tm)
    pair = jnp.concatenate([spare, out_row_of_row.reshape(n_tiles, 1, tm)], axis=0)
    te = jnp.concatenate([tile_expert, tile_expert[-1:]])
    smem = lambda index: pl.BlockSpec((1, 1, tm), lambda i, te: (index(i), 0, 0), memory_space=pltpu.SMEM)
    expert = lambda shape: pl.BlockSpec((1,) + shape, lambda i, te: (te[i], 0, 0))
    any_spec = pl.BlockSpec(memory_space=pl.ANY)
    grid_spec = pltpu.PrefetchScalarGridSpec(
        num_scalar_prefetch=1,
        grid=(n_tiles + 1,),
        in_specs=[smem(lambda i: i), smem(lambda i: i + 1), smem(lambda i: i), smem(lambda i: jnp.maximum(i - 1, 0)),
                  any_spec, expert(wg.shape[1:]), expert(wu.shape[1:]), expert(wd.shape[1:])],
        out_specs=any_spec,
        scratch_shapes=[pltpu.VMEM((2, tm) + slab, F32), pltpu.VMEM((2, tm) + slab, F32), pltpu.SemaphoreType.DMA((2,)),
                        pltpu.SemaphoreType.DMA(())],
    )
    return pl.pallas_call(
        functools.partial(_experts_kernel, tf=tf),
        grid_spec=grid_spec,
        out_shape=jax.ShapeDtypeStruct((2 * T + tm,) + slab, F32),
        compiler_params=_cparams(("arbitrary",)),
        name="moe_experts",
    )(te, tok, tok, pair, pair, hf, wg, wu, wd)


def _combine_kernel(x_ref, w_ref, gf_ref, y0_ref, y1_ref, out_ref):
    w = w_ref[...]
    y = x_ref[...] + w[:, 0:1] * _from_slabs(y0_ref) + w[:, 1:2] * _from_slabs(y1_ref)
    out_ref[...] = _rms(y, gf_ref[...])


def _moe_combine(x2, w, o2, g_final, tm=512):
    T, D = x2.shape
    row = lambda wd, off: pl.BlockSpec((tm, wd), lambda i: (i + off, 0))
    slab = lambda off: pl.BlockSpec((tm,) + o2.shape[1:], lambda i: (i + off, 0, 0))
    return pl.pallas_call(
        _combine_kernel,
        grid=(T // tm,),
        in_specs=[row(D, 0), row(LANES, 0), pl.BlockSpec(g_final.shape, lambda i: (0, 0)), slab(0), slab(T // tm)],
        out_specs=row(D, 0),
        out_shape=jax.ShapeDtypeStruct((T, D), F32),
        compiler_params=_cparams(("parallel",)),
        name="moe_combine",
    )(x2, w, g_final, o2, o2)


def _moe_routed(x2, g, router_p, wg, wu, wd, g_final):
    T, D = x2.shape
    hf, idx, w = _moe_route(x2, g, router_p)
    token_of_row, out_row_of_row, tile_expert = _moe_plan(idx[:, 0:2].reshape(2 * T), MOE_TM)
    o2 = _moe_experts(hf, token_of_row, out_row_of_row, tile_expert, wg, wu, wd)
    return _moe_combine(x2, w, o2, g_final)


def kernel(x, mem, positions, norm_mix, w_in, nsa_cmp_pe, nsa_cmp_w1, nsa_cmp_w2, pool_w, pool_scale, rwkv_mu, rwkv_w0,
           rwkv_w2, rwkv_a0, rwkv_a2, rwkv_g2, rwkv_v0, rwkv_v1, rwkv_v2, rwkv_k_k, rwkv_k_a, rwkv_r_k, rwkv_ln, gla_a2,
           gla_ab, gla_norm, w_gate, b_gate, w_branch, w_out, norm_ca, mem_norm, w_ca_q, w_ca_kv, w_ca_o, norm_ffn,
           ffn_w_gate, ffn_w_up, ffn_w_down, moe_router, moe_w_gate, moe_w_up, moe_w_down, norm_final):
    B, S, D = x.shape
    depth = norm_mix.shape[0]
    assert depth == 2 and S % 256 == 0, "the final norm is fused into the layer-1 expert mixer"
    cs = _rope_angles(positions)
    cs_end = jnp.pad(cs[:, CMP_LEN - 1::CMP_STRIDE], ((0, 0), (0, 1), (0, 0)))
    memkv = _memkv(mem, mem_norm[None], jnp.concatenate([w_ca_kv[l] for l in range(depth)], axis=1).astype(BF16))
    v_first = None
    for l in range(depth):
        h, q_t, k2, v_t, cv, gates_t, u, rw, gl = _inproj(x, norm_mix[l][None], _inproj_weights(w_in[l]), cs,
                                                          rwkv_mu[l][None])
        kcmp, vcmp_t = _nsa_compress(cv, *_compress_weights(nsa_cmp_pe[l], nsa_cmp_w1[l], nsa_cmp_w2[l]), cs_end)
        o_nsa = _nsa_attend(q_t, k2, v_t, kcmp, vcmp_t, gates_t)
        o_pool = _pool(u, _block_diag(pool_w[l]).astype(BF16), pool_scale[l][None])
        wz, vecs = _rwkv_weights(l, rwkv_w0, rwkv_w2, rwkv_a0, rwkv_a2, rwkv_g2, rwkv_v0, rwkv_k_k, rwkv_k_a,
                                 rwkv_r_k, rwkv_ln)
        if l == 0:
            o_rwkv, v_first = _rwkv(rw, wz, vecs)
        else:
            v1p = jnp.pad(rwkv_v1[l - 1], ((0, 0), (0, LANES - rwkv_v1.shape[2]))).astype(BF16)
            v2p = jnp.pad(rwkv_v2[l - 1], ((0, LANES - rwkv_v2.shape[1]), (0, 0))).astype(BF16)
            o_rwkv = _rwkv(rw, wz, vecs, h, v_first, v1p, v2p)
        a2p = jnp.pad(gla_a2[l], ((0, LANES - gla_a2.shape[1]), (0, 0))).astype(BF16)
        o_gla = _gla(gl, a2p, gla_ab[l][None], jnp.tile(gla_norm[l], 4)[None])
        wb_nsa = w_branch[l, 0].reshape(4, HEAD_DIM, D)[jnp.array([0, 2, 1, 3])].reshape(BRANCH_WIDTH, D)
        wb = jnp.concatenate([wb_nsa[None], w_branch[l, 1:]], axis=0).astype(BF16)
        x = _merge(x, h, o_nsa, o_pool, o_rwkv, o_gla, w_gate[l].astype(BF16), b_gate[l], wb, w_out[l].astype(BF16))
        x = _cross(x, norm_ca[l][None], w_ca_q[l].astype(BF16), memkv, l, w_ca_o[l].astype(BF16))
        j = l // 2
        x2 = x.reshape(B * S, D)
        if l % 2 == 0:
            x2 = _ffn(x2, norm_ffn[l][None], ffn_w_gate[j].astype(BF16), ffn_w_up[j].astype(BF16),
                      ffn_w_down[j].astype(BF16))
        else:
            router_p = jnp.pad(moe_router[j], ((0, 0), (0, LANES - N_EXPERTS)))
            x2 = _moe_routed(x2, norm_ffn[l][None], router_p, moe_w_gate[j].astype(BF16), moe_w_up[j].astype(BF16),
                             moe_w_down[j].astype(BF16), norm_final[None])
        x = x2.reshape(B, S, D)
    return x
```

```python
import functools

import jax
import jax.numpy as jnp
import numpy as np
from jax import lax
from jax.experimental import pallas as pl
from jax.experimental.pallas import tpu as pltpu

F32 = jnp.float32
BF16 = jnp.bfloat16

D_MODEL = 1024
HEAD_DIM = 64
BRANCH_WIDTH = 256
ROPE_THETA = 500000.0
ROPE_DIM = 16
NORM_EPS = 1e-6
NEG_INF = -1e30

CMP_LEN = 32
CMP_STRIDE = 16
CMP_HIDDEN = 128
SEL_BLOCK = 64
SEL_TOPN = 8
WINDOW = 512

RWKV_IN = 896
RWKV_GN_EPS = 64e-5
RWKV_CHUNK = 64

GLA_DK = 32
GLA_TAU = 16.0
GLA_CHUNK = 64

CA_WIDTH = 256
N_EXPERTS = 8

LANES = 128
VMEM_LIMIT = 56 * 1024 * 1024

_C_Q = 0
_C_KV = 256
_C_CV = 768
_C_GATE = 1024
_C_POOL = 1152
_C_RWKV = 1408
_C_GLA = 2304
_C_END = 3200


def _cparams(sem):
    return pltpu.CompilerParams(dimension_semantics=sem, vmem_limit_bytes=VMEM_LIMIT)


def _rms(x, g):
    ms = jnp.mean(x * x, axis=-1, keepdims=True)
    return x * lax.rsqrt(ms + NORM_EPS) * g


def _rope128(t, cos, sin):
    lane = lax.broadcasted_iota(jnp.int32, t.shape, 1) % HEAD_DIM
    partner = jnp.where(lane < ROPE_DIM // 2, pltpu.roll(t, LANES - ROPE_DIM // 2, 1),
                        pltpu.roll(t, ROPE_DIM // 2, 1))
    return t * cos + partner * sin


def _inproj_kernel(x_ref, g_ref, w_ref, cs_ref, e_ref, ones_ref, mu_ref,
                   h_ref, qt_ref, k_ref, vt_ref, cv_ref, gt_ref, u_ref, rw_ref, gl_ref, prev_ref):
    s = pl.program_id(1)
    hb = _rms(x_ref[0], g_ref[...]).astype(BF16)
    h_ref[0] = hb

    proj = jnp.dot(hb, w_ref[...], preferred_element_type=F32)

    def mm(a, b):
        return proj[:, a:b]

    cos, sin = _rope_expand(cs_ref[0], e_ref[...], ones_ref[...])
    scale = HEAD_DIM ** -0.5
    qt_ref[0, 0:128, :] = jnp.transpose(_rope128(mm(_C_Q, _C_Q + 128), cos, sin) * scale).astype(BF16)
    qt_ref[0, 128:256, :] = jnp.transpose(_rope128(mm(_C_Q + 128, _C_Q + 256), cos, sin) * scale).astype(BF16)
    k_ref[0, :, 0:128] = _rope128(mm(_C_KV, _C_KV + 128), cos, sin).astype(BF16)
    vt_ref[0, 0:128, :] = jnp.transpose(mm(_C_KV + 128, _C_KV + 256)).astype(BF16)
    k_ref[0, :, 128:256] = _rope128(mm(_C_KV + 256, _C_KV + 384), cos, sin).astype(BF16)
    vt_ref[0, 128:256, :] = jnp.transpose(mm(_C_KV + 384, _C_KV + 512)).astype(BF16)
    cv_ref[0] = mm(_C_CV, _C_CV + 256).astype(BF16)
    gt_ref[0] = jnp.transpose(jax.nn.sigmoid(mm(_C_GATE, _C_GATE + 128)))
    u_ref[0] = mm(_C_POOL, _C_POOL + 256)
    gl_ref[0] = mm(_C_GLA, _C_END)

    p = mm(_C_RWKV, _C_RWKV + RWKV_IN)
    tm = p.shape[0]

    @pl.when(s == 0)
    def _():
        prev_ref[...] = jnp.zeros_like(prev_ref)

    row = lax.broadcasted_iota(jnp.int32, p.shape, 0)
    shifted = jnp.where(row == 0, prev_ref[0:1, :], pltpu.roll(p, 1, 0))
    prev_ref[0:1, :] = p[tm - 1:tm, :]
    rw_ref[0] = p + (shifted - p) * mu_ref[...]


def _inproj(x, g, w_all, cs, mu, tm=512):
    B, S, D = x.shape
    e, ones = _rope_expander()
    tok = lambda w: pl.BlockSpec((1, tm, w), lambda b, s: (b, s, 0))
    tok_t = lambda w: pl.BlockSpec((1, w, tm), lambda b, s: (b, 0, s))
    full = lambda a: pl.BlockSpec(a.shape, lambda b, s: (0,) * a.ndim)
    outs = [(D, BF16, True), (256, BF16, False), (256, BF16, True), (256, BF16, False), (256, BF16, True),
            (128, F32, False), (256, F32, True), (RWKV_IN, F32, True), (896, F32, True)]
    return pl.pallas_call(
        _inproj_kernel,
        grid=(B, S // tm),
        in_specs=[tok(D), full(g), full(w_all), tok(ROPE_DIM), full(e), full(ones), full(mu)],
        out_specs=[tok(w) if major else tok_t(w) for w, _, major in outs],
        out_shape=[jax.ShapeDtypeStruct((B, S, w) if major else (B, w, S), dt) for w, dt, major in outs],
        scratch_shapes=[pltpu.VMEM((8, RWKV_IN), F32)],
        compiler_params=_cparams(("parallel", "arbitrary")),
        name="inproj",
    )(x, g, w_all, cs, e, ones, mu)


def _inproj_columns():
    span = lambda lo, n: list(range(lo, lo + n))
    pad = lambda n: [-1] * n
    nsa, pool, rwkv, gla = 0, 1036, 1292, 2188
    cols = (span(nsa, 64) + span(nsa + 128, 64) + span(nsa + 64, 64) + span(nsa + 192, 64)
            + span(nsa + 512, 512)
            + span(nsa + 256, 256)
            + span(nsa + 1024, 12) + pad(116)
            + span(pool, 256) + span(rwkv, RWKV_IN)
            + span(gla, 512) + span(gla + 528, 256) + span(gla + 512, 16) + pad(112))
    assert len(cols) == _C_END
    return np.asarray(cols, np.int32)


def _inproj_weights(w):
    onehot = (jnp.arange(w.shape[1], dtype=jnp.int32)[:, None] == jnp.asarray(_inproj_columns())[None, :]).astype(BF16)
    return jnp.dot(w.astype(BF16), onehot, preferred_element_type=F32).astype(BF16)


def _rope_angles(positions):
    inv_freq = ROPE_THETA ** (-jnp.arange(0, ROPE_DIM, 2, dtype=F32) / ROPE_DIM)
    ang = positions.astype(F32)[..., None] * inv_freq
    return jnp.concatenate([jnp.cos(ang), jnp.sin(ang)], axis=-1)


def _rope_expander():
    half = ROPE_DIM // 2
    e = np.zeros((ROPE_DIM, 2 * LANES), np.float32)
    ones = np.zeros((1, LANES), np.float32)
    for lane in range(LANES):
        d = lane % HEAD_DIM
        if d < ROPE_DIM:
            e[d % half, lane] = 1.0
            e[half + d % half, LANES + lane] = -1.0 if d < half else 1.0
        else:
            ones[0, lane] = 1.0
    return jnp.asarray(e, BF16), jnp.asarray(ones)


def _rope_expand(cs, e, ones):
    tab = _dot_split_lhs(cs, e, 3)
    return tab[:, 0:LANES] + ones, tab[:, LANES:2 * LANES]


def _gelu_tanh(x):
    return x * (0.5 * (1.0 + jnp.tanh(np.sqrt(2.0 / np.pi) * (x + 0.044715 * (x * x * x)))))


def _dot_nt(a, b):
    return lax.dot_general(a, b, (((1,), (1,)), ((), ())), preferred_element_type=F32)


def _compress_kernel(x_ref, wa_ref, wb_ref, bias_ref, w2k_ref, w2v_ref, cs_ref, e_ref, ones_ref, kc_ref, vc_ref):
    x = x_ref[0]
    a = jnp.dot(x, wa_ref[...], preferred_element_type=F32)
    b = jnp.dot(x, wb_ref[...], preferred_element_type=F32)
    n = a.shape[0]
    hid = _gelu_tanh(a + pltpu.roll(b, n - 1, 0) + bias_ref[...])
    kc = jnp.dot(hid[:, 0:256].astype(BF16), w2k_ref[...], preferred_element_type=F32)
    vc = jnp.dot(hid[:, 256:512].astype(BF16), w2v_ref[...], preferred_element_type=F32)
    cos, sin = _rope_expand(cs_ref[0], e_ref[...], ones_ref[...])
    kc_ref[0] = _rope128(kc, cos, sin).astype(BF16)
    vc_ref[0] = jnp.transpose(vc).astype(BF16)


def _nsa_compress(cv, wa, wb, bias, w2k, w2v, cs_end):
    B, S, _ = cv.shape
    nseg = S // CMP_STRIDE
    x = cv.reshape(B, nseg, CMP_STRIDE * 256)
    e, ones = _rope_expander()
    per_b = lambda r, w: pl.BlockSpec((1, r, w), lambda b: (b, 0, 0))
    full = lambda a: pl.BlockSpec(a.shape, lambda b: (0,) * a.ndim)
    return pl.pallas_call(
        _compress_kernel,
        grid=(B,),
        in_specs=[per_b(nseg, CMP_STRIDE * 256), full(wa), full(wb), full(bias), full(w2k), full(w2v),
                  per_b(nseg, ROPE_DIM), full(e), full(ones)],
        out_specs=[per_b(nseg, 128), per_b(nseg, 128)],
        out_shape=[jax.ShapeDtypeStruct((B, nseg, 128), BF16)] * 2,
        compiler_params=_cparams(("parallel",)),
        name="nsa_compress",
    )(x, wa, wb, bias, w2k, w2v, cs_end, e, ones)


def _compress_weights(pe, w1, w2):
    eye2 = jnp.eye(2, dtype=F32)
    half = lambda lo: jnp.einsum('jldf,pj,qg->lpqdjgf', w1[:, lo:lo + CMP_STRIDE], eye2, eye2).reshape(
        CMP_STRIDE * 256, 4 * CMP_HIDDEN).astype(BF16)
    bias = jnp.einsum('jld,jldf->jf', pe, w1)
    bias = jnp.broadcast_to(bias[:, None, :], (2, 2, CMP_HIDDEN)).reshape(1, 4 * CMP_HIDDEN)
    bd = lambda w: jnp.einsum('fd,gq->gfqd', w, eye2).reshape(2 * CMP_HIDDEN, 2 * HEAD_DIM).astype(BF16)
    return half(0), half(CMP_STRIDE), bias, bd(w2[0]), bd(w2[1])


def _cover_t(S):
    n_cmp = (S - CMP_LEN) // CMP_STRIDE + 1
    cmp_start = np.arange(n_cmp) * CMP_STRIDE
    slc_start = np.arange(S // SEL_BLOCK) * SEL_BLOCK
    cover = np.clip(np.minimum(cmp_start[:, None] + CMP_LEN, slc_start[None, :] + SEL_BLOCK)
                    - np.maximum(cmp_start[:, None], slc_start[None, :]), 0, None) / CMP_LEN
    out = np.zeros((S // SEL_BLOCK, S // CMP_STRIDE), np.float32)
    out[:, :n_cmp] = cover.T
    return jnp.asarray(out, BF16)


MASKED = 2.0 * NEG_INF


def _softmax_step(s, carry, v_t):
    m, l, acc = carry
    m_new = jnp.maximum(m, jnp.max(s, axis=0, keepdims=True))
    alpha = jnp.exp(m - m_new)
    p = jnp.exp(s - m_new)
    l = alpha * l + jnp.sum(p, axis=0, keepdims=True)
    acc = alpha * acc + jnp.dot(v_t, p.astype(BF16), preferred_element_type=F32)
    return m_new, l, acc


def _softmax_init(n_q):
    return (jnp.full((1, n_q), NEG_INF, F32), jnp.zeros((1, n_q), F32), jnp.zeros((LANES, n_q), F32))


def _nsa_kernel(qt_ref, k_ref, vt_ref, kc_ref, vct_ref, gt_ref, covt_ref, o_ref, *, tq, tk):
    i = pl.program_id(1)
    t0 = i * tq
    n_blk = covt_ref.shape[0]
    n_q = 4 * tq
    qa_t, qb_t = qt_ref[0, 0:128, :], qt_ref[0, 128:256, :]
    sub_grp = lax.broadcasted_iota(jnp.int32, (LANES, tq), 0) // HEAD_DIM
    gates_t = gt_ref[0]
    tok = t0 + (lax.broadcasted_iota(jnp.int32, (1, n_q), 1) & (tq - 1))
    q_all = jnp.concatenate([jnp.where(sub_grp == g, q_t, 0) for g in range(2) for q_t in (qa_t, qb_t)], axis=1)

    s = jnp.dot(kc_ref[0], q_all, preferred_element_type=F32)
    nrow = lax.broadcasted_iota(jnp.int32, s.shape, 0)
    s = jnp.where(nrow * CMP_STRIDE + (CMP_LEN - 1) <= tok, s, MASKED)
    e = jnp.exp(s - jnp.maximum(jnp.max(s, axis=0, keepdims=True), NEG_INF))
    den = jnp.sum(e, axis=0, keepdims=True)
    p = e / jnp.where(den > 0.0, den, 1.0)
    o_cmp = jnp.dot(vct_ref[0], p.astype(BF16), preferred_element_type=F32)

    p_grp = jnp.concatenate([p[:, 2 * g * tq:(2 * g + 1) * tq] + p[:, (2 * g + 1) * tq:(2 * g + 2) * tq]
                             for g in range(2)], axis=1)
    imp = _dot_split_rhs(covt_ref[...], p_grp, 2)
    blk = lax.broadcasted_iota(jnp.int32, imp.shape, 0)
    tok_b = t0 + (lax.broadcasted_iota(jnp.int32, imp.shape, 1) & (tq - 1))
    cur = tok_b // SEL_BLOCK
    forced = (blk == 0) | (blk == cur) | (blk == cur - 1)
    imp = jnp.where(forced, 1e30, jnp.where(blk * SEL_BLOCK <= tok_b, imp, -1e30))
    cnt = jnp.zeros(imp.shape, F32)
    for i2 in range(n_blk):
        row = imp[i2:i2 + 1, :]
        cnt = cnt + jnp.where((row > imp) | ((row == imp) & (blk > i2)), 1.0, 0.0)
    bias = jnp.where(cnt < float(SEL_TOPN), 0.0, MASKED).astype(BF16)
    sel_bias = jnp.concatenate([bias[:, g * tq:(g + 1) * tq] for g in (0, 0, 1, 1)], axis=1)

    q_ext = jnp.concatenate([q_all, sel_bias, jnp.zeros((LANES - n_blk, n_q), BF16)], axis=0)

    def sel_scores(j):
        rows = pl.ds(pl.multiple_of(j * tk, tk), tk)
        blk_of_key = (j * tk + lax.broadcasted_iota(jnp.int32, (tk, LANES), 0)) // SEL_BLOCK
        onehot = (lax.broadcasted_iota(jnp.int32, (tk, LANES), 1) == blk_of_key).astype(BF16)
        k_ext = jnp.concatenate([k_ref[0, rows, 0:128], onehot], axis=1)
        return jnp.dot(k_ext, q_ext, preferred_element_type=F32), vt_ref[0, 0:128, rows]

    def sel_step(j, carry):
        s, v_t = sel_scores(j)
        return _softmax_step(s, carry, v_t)

    last = (t0 + tq - 1) // tk
    carry = lax.fori_loop(0, last, sel_step, _softmax_init(n_q))
    s, v_t = sel_scores(last)
    key = last * tk + lax.broadcasted_iota(jnp.int32, (tk, n_q), 0)
    _, l_sel, acc_sel = _softmax_step(jnp.where(key <= tok, s, MASKED), carry, v_t)
    o_sel = acc_sel / l_sel

    span = WINDOW + tq
    start = pl.multiple_of(jnp.maximum(t0 - WINDOW, 0), tq)
    key = start + lax.broadcasted_iota(jnp.int32, (span, n_q), 0)
    s = jnp.dot(k_ref[0, pl.ds(start, span), 128:256], q_all, preferred_element_type=F32)
    in_window = lax.bitcast_convert_type(tok - key, jnp.uint32) < jnp.uint32(WINDOW)
    s = jnp.where(in_window, s, MASKED)
    e = jnp.exp(s - jnp.max(s, axis=0, keepdims=True))
    o_win = (jnp.dot(vt_ref[0, 128:256, pl.ds(start, span)], e.astype(BF16), preferred_element_type=F32)
             / jnp.sum(e, axis=0, keepdims=True))

    out = [jnp.zeros((LANES, tq), F32), jnp.zeros((LANES, tq), F32)]
    for g in range(2):
        for r in range(2):
            cols = slice((2 * g + r) * tq, (2 * g + r + 1) * tq)
            c = g * 6 + r * 3
            o = (gates_t[c:c + 1, :] * o_cmp[:, cols] + gates_t[c + 1:c + 2, :] * o_sel[:, cols]
                 + gates_t[c + 2:c + 3, :] * o_win[:, cols])
            out[r] = jnp.where(sub_grp == g, o, out[r])
    o_ref[0, :, 0:128] = jnp.transpose(out[0]).astype(BF16)
    o_ref[0, :, 128:256] = jnp.transpose(out[1]).astype(BF16)


def _nsa_attend(q_t, k2, v_t, kcmp, vcmp_t, gates_t, tq=256, tk=512):
    B, S, _ = k2.shape
    assert tk % tq == 0 and S % tk == 0 and WINDOW % tq == 0
    covt = _cover_t(S)
    tile_t = lambda w: pl.BlockSpec((1, w, tq), lambda b, i: (b, 0, i))
    per_b = lambda r, w: pl.BlockSpec((1, r, w), lambda b, i: (b, 0, 0))
    return pl.pallas_call(
        functools.partial(_nsa_kernel, tq=tq, tk=tk),
        grid=(B, S // tq),
        in_specs=[tile_t(256), per_b(S, 256), per_b(256, S), per_b(S // CMP_STRIDE, 128), per_b(128, S // CMP_STRIDE),
                  tile_t(128), pl.BlockSpec(covt.shape, lambda b, i: (0, 0))],
        out_specs=pl.BlockSpec((1, tq, 256), lambda b, i: (b, i, 0)),
        out_shape=jax.ShapeDtypeStruct((B, S, 256), BF16),
        compiler_params=_cparams(("parallel", "arbitrary")),
        name="nsa_attend",
    )(q_t, k2, v_t, kcmp, vcmp_t, gates_t, covt)


def _pool_kernel(u_ref, w_ref, scale_ref, o_ref):
    u = u_ref[0]
    row = lax.broadcasted_iota(jnp.int32, u.shape, 0)
    grp = lax.broadcasted_iota(jnp.int32, u.shape, 1) // HEAD_DIM

    def back(x, k):
        return jnp.where(row >= k, pltpu.roll(x, k, 0), 0.0)

    s2 = u + back(u, 1)
    s4 = s2 + back(s2, 2)
    s8 = s4 + back(s4, 4)
    s16 = s8 + back(s8, 8)
    total = jnp.where(grp == 0, s2, jnp.where(grp == 1, s4, jnp.where(grp == 2, s8, s16)))
    win = jnp.where(grp == 0, 2, jnp.where(grp == 1, 4, jnp.where(grp == 2, 8, 16)))
    count = jnp.minimum(row + 1, win).astype(F32)
    d = total / count - u
    y = jnp.dot(d.astype(BF16), w_ref[...], preferred_element_type=F32)
    o_ref[0] = (y * scale_ref[...]).astype(BF16)


def _pool(u, w_bd, scale):
    B, S, W = u.shape
    per_b = pl.BlockSpec((1, S, W), lambda b: (b, 0, 0))
    full = lambda a: pl.BlockSpec(a.shape, lambda b: (0,) * a.ndim)
    return pl.pallas_call(
        _pool_kernel,
        grid=(B,),
        in_specs=[per_b, full(w_bd), full(scale)],
        out_specs=per_b,
        out_shape=jax.ShapeDtypeStruct((B, S, W), BF16),
        compiler_params=_cparams(("parallel",)),
        name="pool",
    )(u, w_bd, scale)


def _block_diag(w):
    G, a, b = w.shape
    return jnp.einsum('gab,gh->gahb', w, jnp.eye(G, dtype=w.dtype)).reshape(G * a, G * b)


def _dot_split_lhs(a, b_exact, terms):
    out, rem = None, a
    for i in range(terms):
        hi = rem.astype(BF16)
        d = jnp.dot(hi, b_exact, preferred_element_type=F32)
        out = d if out is None else out + d
        if i + 1 < terms:
            rem = rem - hi.astype(F32)
    return out


def _dot_split_rhs(a_exact, b, terms):
    out, rem = None, b
    for i in range(terms):
        hi = rem.astype(BF16)
        d = jnp.dot(a_exact, hi, preferred_element_type=F32)
        out = d if out is None else out + d
        if i + 1 < terms:
            rem = rem - hi.astype(F32)
    return out


def _dot_tn(a, b):
    return lax.dot_general(a, b, (((0,), (0,)), ((), ())), preferred_element_type=F32)


def _stack_heads(x, width, n_heads=4):
    head = lax.broadcasted_iota(jnp.int32, x.shape, 1) // width
    return jnp.concatenate([jnp.where(head == h, x, 0.0) for h in range(n_heads)], axis=0)


def _unstack_heads(x, C, n_heads=4):
    out = x[0:C]
    for h in range(1, n_heads):
        out = out + x[h * C:(h + 1) * C]
    return out


def _block_masks(C, n_heads=4):
    n = C * n_heads
    row = lax.broadcasted_iota(jnp.int32, (n, n), 0)
    col = lax.broadcasted_iota(jnp.int32, (n, n), 1)
    same = (row // C) == (col // C)
    return same & (row > col), same & (row >= col)


def _tri_ones(C):
    return jnp.asarray(np.tril(np.ones((C, C), np.float32)), BF16)


def _head_ones(width, n_heads=4):
    return jnp.asarray(np.kron(np.eye(n_heads, dtype=np.float32), np.ones((width, width), np.float32)), BF16)


def _gla_kernel(gl_ref, a2_ref, ab_ref, norm_ref, tri_ref, ones_ref, o_ref, state_ref, *, C):
    s = pl.program_id(1)

    @pl.when(s == 0)
    def _():
        state_ref[...] = jnp.zeros_like(state_ref)

    gl = gl_ref[0]
    tm = gl.shape[0]
    q = gl[:, 0:128] * (GLA_DK ** -0.5)
    k = gl[:, 128:256]
    v = gl[:, 256:512]
    og = gl[:, 512:768]
    ad = gl[:, 768:896]
    x = jnp.dot(ad.astype(BF16), a2_ref[...], preferred_element_type=F32) + ab_ref[...]
    log_a = jax.nn.log_sigmoid(x) * (1.0 / GLA_TAU)
    strict, incl = _block_masks(C)
    kv_mask = (lax.broadcasted_iota(jnp.int32, (256, 128), 0) // HEAD_DIM
               == lax.broadcasted_iota(jnp.int32, (256, 128), 1) // GLA_DK)
    chunks = []
    for c in range(tm // C):
        rc = slice(c * C, (c + 1) * C)
        bcum = _dot_split_rhs(tri_ref[...], log_a[rc], 3)
        mid = bcum[C // 2 - 1:C // 2, :]
        last = bcum[C - 1:C, :]
        q_m = _stack_heads(q[rc] * jnp.exp(bcum - mid), GLA_DK).astype(BF16)
        k_m = _stack_heads(k[rc] * jnp.exp(mid - bcum), GLA_DK).astype(BF16)
        k_end = k[rc] * jnp.exp(last - bcum)
        att = jnp.where(incl, _dot_nt(q_m, k_m), 0.0)
        v_st = _stack_heads(v[rc], HEAD_DIM).astype(BF16)
        chunks.append(dict(
            o_intra=_unstack_heads(jnp.dot(att.astype(BF16), v_st, preferred_element_type=F32), C),
            q_in=(q[rc] * jnp.exp(bcum)).astype(BF16), decay=jnp.exp(last),
            upd=jnp.where(kv_mask, _dot_tn(v[rc].astype(BF16), k_end.astype(BF16)), 0.0)))
    outs = []
    state = state_ref[...]
    for ch in chunks:
        outs.append(ch['o_intra'] + _dot_nt(ch['q_in'], state.astype(BF16)))
        state = state * ch['decay'] + ch['upd']
    state_ref[...] = state
    o = jnp.concatenate(outs, axis=0)
    ms = _dot_split_lhs(o * o, ones_ref[...], 2) * (1.0 / HEAD_DIM)
    o = o * lax.rsqrt(ms + NORM_EPS) * norm_ref[...]
    o_ref[0] = (o * (og * jax.nn.sigmoid(og))).astype(BF16)


def _gla(gl, a2p, ab, norm4, tm=512):
    B, S, W = gl.shape
    C = GLA_CHUNK
    tri, ones = _tri_ones(C), _head_ones(HEAD_DIM)
    tile = lambda w: pl.BlockSpec((1, tm, w), lambda b, s: (b, s, 0))
    full = lambda a: pl.BlockSpec(a.shape, lambda b, s: (0,) * a.ndim)
    return pl.pallas_call(
        functools.partial(_gla_kernel, C=C),
        grid=(B, S // tm),
        in_specs=[tile(W), full(a2p), full(ab), full(norm4), full(tri), full(ones)],
        out_specs=tile(256),
        out_shape=jax.ShapeDtypeStruct((B, S, 256), BF16),
        scratch_shapes=[pltpu.VMEM((256, 128), F32)],
        compiler_params=_cparams(("parallel", "arbitrary")),
        name="gla",
    )(gl, a2p, ab, norm4, tri, ones)


_RV_W0, _RV_A0, _RV_KK, _RV_KA, _RV_RK, _RV_LN, _RV_V0 = range(7)


def _rwkv_kernel(*refs, has_vres, C):
    if has_vres:
        (rw_ref, h_ref, vf_ref, wz_ref, vec_ref, v1_ref, v2_ref, tri_ref, ones_ref, o_ref, state_ref) = refs
    else:
        (rw_ref, wz_ref, vec_ref, tri_ref, ones_ref, o_ref, vout_ref, state_ref) = refs
    s = pl.program_id(1)

    @pl.when(s == 0)
    def _():
        state_ref[...] = jnp.zeros_like(state_ref)

    rw = rw_ref[0]
    tm = rw.shape[0]
    r, k, v, z = rw[:, 0:256], rw[:, 256:512], rw[:, 512:768], rw[:, 768:896]
    vec = lambda i: vec_ref[i:i + 1, :]
    zl = lax.broadcasted_iota(jnp.int32, z.shape, 1)
    zf = jnp.where(zl < 32, jnp.tanh(z), jnp.where(zl < 64, z, jax.nn.sigmoid(z)))
    zz = jnp.dot(zf.astype(BF16), wz_ref[...], preferred_element_type=F32)
    w_log = -jax.nn.softplus(-(vec(_RV_W0) + zz[:, 0:256])) - 0.5
    lw = -jnp.exp(w_log)
    a_sig = jax.nn.sigmoid(vec(_RV_A0) + zz[:, 256:512])
    gate = zz[:, 512:768]
    if has_vres:
        low = jnp.dot(h_ref[0], v1_ref[...], preferred_element_type=F32)
        logit = vec(_RV_V0) + jnp.dot(low.astype(BF16), v2_ref[...], preferred_element_type=F32)
        v = v + (vf_ref[0] - v) * jax.nn.sigmoid(logit)
    else:
        vout_ref[0] = v
    kk = k * vec(_RV_KK)
    norm = jnp.sqrt(_dot_split_lhs(kk * kk, ones_ref[...], 2))
    kk = kk / jnp.maximum(norm, 1e-12)
    k = k * (1.0 + (a_sig - 1.0) * vec(_RV_KA))
    a_vec = -kk
    b_vec = kk * a_sig

    strict, incl = _block_masks(C)
    st = lambda t: _stack_heads(t, HEAD_DIM)
    bdot = lambda p, q: jnp.dot(p.astype(BF16), q.astype(BF16), preferred_element_type=F32)
    state_mask = (lax.broadcasted_iota(jnp.int32, (256, 256), 0) // HEAD_DIM
                  == lax.broadcasted_iota(jnp.int32, (256, 256), 1) // HEAD_DIM)
    n4 = 4 * C
    chunks = []
    for c in range(tm // C):
        rc = slice(c * C, (c + 1) * C)
        lcum = _dot_split_rhs(tri_ref[...], lw[rc], 3)
        lex = lcum - lw[rc]
        mid = lcum[C // 2 - 1:C // 2, :]
        last = lcum[C - 1:C, :]
        e_mid = jnp.exp(mid - lcum)
        e_end = jnp.exp(last - lcum)
        left = jnp.concatenate([st(a_vec[rc] * jnp.exp(lex - mid)), st(r[rc] * jnp.exp(lcum - mid))], axis=0)
        right = jnp.concatenate([st(b_vec[rc] * e_mid), st(k[rc] * e_mid)], axis=0)
        g = _dot_nt(left.astype(BF16), right.astype(BF16))
        v_st = st(v[rc])
        n_ak = jnp.where(strict, g[0:n4, n4:2 * n4], 0.0)
        chunks.append(dict(
            p=jnp.where(strict, g[0:n4, 0:n4], 0.0),
            x=jnp.concatenate([st(a_vec[rc] * jnp.exp(lex)), bdot(n_ak, v_st)], axis=1),
            n_rb=jnp.where(incl, g[n4:2 * n4, 0:n4], 0.0),
            y_rk=bdot(jnp.where(incl, g[n4:2 * n4, n4:2 * n4], 0.0), v_st),
            r_abs=r[rc] * jnp.exp(lcum), decay=jnp.exp(last), v=v[rc],
            bk_end=jnp.concatenate([b_vec[rc] * e_end, k[rc] * e_end], axis=0).astype(BF16)))
    eye = (lax.broadcasted_iota(jnp.int32, (n4, n4), 0) == lax.broadcasted_iota(jnp.int32, (n4, n4), 1)).astype(F32)
    for ch in chunks:
        ch['t'] = eye + ch['p']
    for f in range(1, int(np.log2(C))):
        for ch in chunks:
            ch['p'] = bdot(ch['p'], ch['p'])
        for ch in chunks:
            ch['t'] = ch['t'] + bdot(ch['p'], ch['t'])
    for ch in chunks:
        ch['x'] = bdot(ch['t'], ch['x'])
        y_part = bdot(ch['n_rb'], ch['x'])
        a_eff = _unstack_heads(ch['x'][:, 0:256], C)
        u0 = _unstack_heads(ch['x'][:, 256:512], C)
        ch['r_eff'] = (ch['r_abs'] + _unstack_heads(y_part[:, 0:256], C)).astype(BF16)
        ch['y0'] = _unstack_heads(y_part[:, 256:512] + ch['y_rk'], C)
        ch['mix'] = jnp.where(state_mask, _dot_tn(a_eff.astype(BF16), ch['bk_end'][0:C]), 0.0).astype(BF16)
        ch['add'] = jnp.where(state_mask, _dot_tn(jnp.concatenate([u0, ch['v']], axis=0).astype(BF16), ch['bk_end']), 0.0)
    ys = []
    state = state_ref[...]
    for ch in chunks:
        sb = state.astype(BF16)
        ys.append(_dot_nt(ch['r_eff'], sb) + ch['y0'])
        state = state * ch['decay'] + jnp.dot(sb, ch['mix'], preferred_element_type=F32) + ch['add']
    state_ref[...] = state
    y = jnp.concatenate(ys, axis=0)
    inv = 1.0 / HEAD_DIM
    mean = _dot_split_lhs(y, ones_ref[...], 2) * inv
    yc = y - mean
    var = _dot_split_lhs(yc * yc, ones_ref[...], 2) * inv
    y = yc * lax.rsqrt(var + RWKV_GN_EPS) * vec(_RV_LN)
    y = y + _dot_split_lhs(r * k * vec(_RV_RK), ones_ref[...], 2) * v
    o_ref[0] = (y * gate).astype(BF16)


def _rwkv(rw, wz, vecs, h=None, v_first=None, v1p=None, v2p=None, tm=256):
    B, S, W = rw.shape
    C = RWKV_CHUNK
    has_vres = h is not None
    tri, ones = _tri_ones(C), _head_ones(HEAD_DIM)
    tile = lambda w: pl.BlockSpec((1, tm, w), lambda b, s: (b, s, 0))
    full = lambda a: pl.BlockSpec(a.shape, lambda b, s: (0,) * a.ndim)
    if has_vres:
        args = (rw, h, v_first, wz, vecs, v1p, v2p, tri, ones)
        in_specs = [tile(W), tile(D_MODEL), tile(256), full(wz), full(vecs), full(v1p), full(v2p), full(tri), full(ones)]
        out_specs = tile(256)
        out_shape = jax.ShapeDtypeStruct((B, S, 256), BF16)
    else:
        args = (rw, wz, vecs, tri, ones)
        in_specs = [tile(W), full(wz), full(vecs), full(tri), full(ones)]
        out_specs = [tile(256), tile(256)]
        out_shape = [jax.ShapeDtypeStruct((B, S, 256), BF16), jax.ShapeDtypeStruct((B, S, 256), F32)]
    return pl.pallas_call(
        functools.partial(_rwkv_kernel, has_vres=has_vres, C=C),
        grid=(B, S // tm),
        in_specs=in_specs,
        out_specs=out_specs,
        out_shape=out_shape,
        scratch_shapes=[pltpu.VMEM((256, 256), F32)],
        compiler_params=_cparams(("parallel", "arbitrary")),
        name="rwkv_vres" if has_vres else "rwkv",
    )(*args)


def _rwkv_weights(l, rwkv_w0, rwkv_w2, rwkv_a0, rwkv_a2, rwkv_g2, rwkv_v0, rwkv_k_k, rwkv_k_a, rwkv_r_k, rwkv_ln):
    wz = jnp.zeros((128, 768), F32)
    wz = wz.at[0:32, 0:256].set(rwkv_w2[l]).at[32:64, 256:512].set(rwkv_a2[l]).at[64:128, 512:768].set(rwkv_g2[l])
    v0 = rwkv_v0[l - 1] if l > 0 else jnp.zeros((256,), F32)
    vecs = jnp.stack([rwkv_w0[l], rwkv_a0[l], rwkv_k_k[l], rwkv_k_a[l], rwkv_r_k[l].reshape(-1), rwkv_ln[l], v0,
                      jnp.zeros((256,), F32)])
    return wz.astype(BF16), vecs


def _merge_kernel(x_ref, h_ref, o0_ref, o1_ref, o2_ref, o3_ref, wg_ref, bg_ref, wb_ref, wo_ref, out_ref):
    h = h_ref[0]
    merged = None
    for i, o_ref in enumerate((o0_ref, o1_ref, o2_ref, o3_ref)):
        gate = jax.nn.sigmoid(jnp.dot(h, wg_ref[i], preferred_element_type=F32) + bg_ref[i:i + 1, :])
        term = gate * jnp.dot(o_ref[0], wb_ref[i], preferred_element_type=F32)
        merged = term if merged is None else merged + term
    out_ref[0] = x_ref[0] + jnp.dot(merged.astype(BF16), wo_ref[...], preferred_element_type=F32)


def _merge(x, h, o_nsa, o_pool, o_rwkv, o_gla, wg, bg, wb, wo, tm=512):
    B, S, D = x.shape
    tile = lambda w: pl.BlockSpec((1, tm, w), lambda b, s: (b, s, 0))
    full = lambda a: pl.BlockSpec(a.shape, lambda b, s: (0,) * a.ndim)
    return pl.pallas_call(
        _merge_kernel,
        grid=(B, S // tm),
        in_specs=[tile(D), tile(D), tile(256), tile(256), tile(256), tile(256), full(wg), full(bg), full(wb), full(wo)],
        out_specs=tile(D),
        out_shape=jax.ShapeDtypeStruct((B, S, D), F32),
        compiler_params=_cparams(("parallel", "parallel")),
        name="merge",
    )(x, h, o_nsa, o_pool, o_rwkv, o_gla, wg, bg, wb, wo)


def _memkv_kernel(mem_ref, g_ref, w_ref, o_ref):
    mn = _rms(mem_ref[0], g_ref[...]).astype(BF16)
    o_ref[0] = jnp.dot(mn, w_ref[...], preferred_element_type=F32).astype(BF16)


def _memkv(mem, g, wkv_all):
    B, M, D = mem.shape
    N = wkv_all.shape[1]
    return pl.pallas_call(
        _memkv_kernel,
        grid=(B,),
        in_specs=[pl.BlockSpec((1, M, D), lambda b: (b, 0, 0)), pl.BlockSpec(g.shape, lambda b: (0, 0)),
                  pl.BlockSpec(wkv_all.shape, lambda b: (0, 0))],
        out_specs=pl.BlockSpec((1, M, N), lambda b: (b, 0, 0)),
        out_shape=jax.ShapeDtypeStruct((B, M, N), BF16),
        compiler_params=_cparams(("parallel",)),
        name="memkv",
    )(mem, g, wkv_all)


def _cross_kernel(x_ref, g_ref, wq_ref, k_ref, v_ref, wo_ref, out_ref):
    x = x_ref[0]
    hc = _rms(x, g_ref[...]).astype(BF16)
    q = jnp.dot(hc, wq_ref[...], preferred_element_type=F32) * (HEAD_DIM ** -0.5)
    head = lax.broadcasted_iota(jnp.int32, q.shape, 1) // HEAD_DIM
    k, v = k_ref[0], v_ref[0]
    o = jnp.zeros(q.shape, F32)
    for hh in range(CA_WIDTH // HEAD_DIM):
        qh = jnp.where(head == hh, q, 0.0).astype(BF16)
        s = _dot_nt(qh, k)
        e = jnp.exp(s - jnp.max(s, axis=1, keepdims=True))
        p = e / jnp.sum(e, axis=1, keepdims=True)
        o = jnp.where(head == hh, jnp.dot(p.astype(BF16), v, preferred_element_type=F32), o)
    out_ref[0] = x + jnp.dot(o.astype(BF16), wo_ref[...], preferred_element_type=F32)


def _cross(x, g, wq, memkv, l, wo, tm=512):
    B, S, D = x.shape
    M = memkv.shape[1]
    tile = pl.BlockSpec((1, tm, D), lambda b, s: (b, s, 0))
    full = lambda a: pl.BlockSpec(a.shape, lambda b, s: (0,) * a.ndim)
    kspec = pl.BlockSpec((1, M, CA_WIDTH), lambda b, s: (b, 0, 2 * l))
    vspec = pl.BlockSpec((1, M, CA_WIDTH), lambda b, s: (b, 0, 2 * l + 1))
    return pl.pallas_call(
        _cross_kernel,
        grid=(B, S // tm),
        in_specs=[tile, full(g), full(wq), kspec, vspec, full(wo)],
        out_specs=tile,
        out_shape=jax.ShapeDtypeStruct((B, S, D), F32),
        compiler_params=_cparams(("parallel", "parallel")),
        name="cross_attn",
    )(x, g, wq, memkv, memkv, wo)


def _swiglu_cols(xb, wg, wu, wd, tf):
    acc = jnp.zeros((xb.shape[0], wd.shape[-1]), F32)
    for c in range(wg.shape[-1] // tf):
        cols = slice(c * tf, (c + 1) * tf)
        gate = jnp.dot(xb, wg[:, cols], preferred_element_type=F32)
        up = jnp.dot(xb, wu[:, cols], preferred_element_type=F32)
        act = gate * jax.nn.sigmoid(gate) * up
        acc = acc + jnp.dot(act.astype(BF16), wd[cols, :], preferred_element_type=F32)
    return acc


def _ffn_kernel(x_ref, g_ref, wg_ref, wu_ref, wd_ref, out_ref, *, tf):
    x = x_ref[...]
    out_ref[...] = x + _swiglu_cols(_rms(x, g_ref[...]).astype(BF16), wg_ref, wu_ref, wd_ref, tf)


def _ffn(x2, g, wg, wu, wd, tm=512, tf=256):
    T, D = x2.shape
    full = lambda a: pl.BlockSpec(a.shape, lambda i: (0,) * a.ndim)
    return pl.pallas_call(
        functools.partial(_ffn_kernel, tf=tf),
        grid=(T // tm,),
        in_specs=[pl.BlockSpec((tm, D), lambda i: (i, 0)), full(g), full(wg), full(wu), full(wd)],
        out_specs=pl.BlockSpec((tm, D), lambda i: (i, 0)),
        out_shape=jax.ShapeDtypeStruct((T, D), F32),
        compiler_params=_cparams(("parallel",)),
        name="ffn",
    )(x2, g, wg, wu, wd)


MOE_TM = 512


SUBLANES = 8


def _to_slabs(ref, x):
    w = ref.shape[-1]
    for j in range(ref.shape[-2]):
        ref[:, j, :] = x[:, j * w:(j + 1) * w]


def _from_slabs(ref):
    return jnp.concatenate([ref[:, j, :] for j in range(ref.shape[-2])], axis=1)


def _route_kernel(x_ref, g_ref, r_ref, hf_ref, idx_ref, w_ref):
    hf = _rms(x_ref[...], g_ref[...])
    _to_slabs(hf_ref, hf)
    logits = jnp.dot(hf, r_ref[...], preferred_element_type=F32, precision=lax.Precision.HIGHEST)
    lane = lax.broadcasted_iota(jnp.int32, logits.shape, 1)
    big = logits.shape[1]
    logits = jnp.where(lane < N_EXPERTS, logits, -jnp.inf)
    m1 = jnp.max(logits, axis=1, keepdims=True)
    i1 = jnp.min(jnp.where(logits == m1, lane, big), axis=1, keepdims=True)
    rest = jnp.where(lane == i1, -jnp.inf, logits)
    m2 = jnp.max(rest, axis=1, keepdims=True)
    i2 = jnp.min(jnp.where(rest == m2, lane, big), axis=1, keepdims=True)
    e2 = jnp.exp(m2 - m1)
    w1 = 1.0 / (1.0 + e2)
    idx_ref[...] = jnp.where(lane == 0, i1, jnp.where(lane == 1, i2, 0))
    w_ref[...] = jnp.where(lane == 0, w1, jnp.where(lane == 1, e2 * w1, 0.0))


def _moe_route(x2, g, router_p, tm=1024):
    T, D = x2.shape
    row = lambda w: pl.BlockSpec((tm, w), lambda i: (i, 0))
    full = lambda a: pl.BlockSpec(a.shape, lambda i: (0,) * a.ndim)
    return pl.pallas_call(
        _route_kernel,
        grid=(T // tm,),
        in_specs=[row(D), full(g), full(router_p)],
        out_specs=[pl.BlockSpec((tm, SUBLANES, D // SUBLANES), lambda i: (i, 0, 0)), row(LANES), row(LANES)],
        out_shape=[jax.ShapeDtypeStruct((T, SUBLANES, D // SUBLANES), F32), jax.ShapeDtypeStruct((T, LANES), jnp.int32),
                   jax.ShapeDtypeStruct((T, LANES), F32)],
        compiler_params=_cparams(("parallel",)),
        name="moe_route",
    )(x2, g, router_p)


def _moe_plan(expert_of_pair, tm):
    onehot = (expert_of_pair[:, None] == jnp.arange(N_EXPERTS, dtype=jnp.int32)[None, :]).astype(jnp.int32)
    csum = jnp.cumsum(onehot, axis=0)
    rank = jnp.sum((csum - 1) * onehot, axis=1)
    counts = csum[-1]
    padded = ((counts + tm - 1) // tm) * tm
    ends = jnp.cumsum(padded)
    dest = ((ends - padded)[expert_of_pair] + rank).astype(jnp.int32)
    n_rows = expert_of_pair.shape[0] + N_EXPERTS * tm
    tile_expert = jnp.sum((jnp.arange(n_rows // tm, dtype=jnp.int32)[:, None] * tm >= ends[None, :]).astype(jnp.int32), axis=1)
    n_pairs = expert_of_pair.shape[0]
    pair_of_row = jnp.full((n_rows,), -1, jnp.int32).at[dest].set(
        jnp.arange(n_pairs, dtype=jnp.int32), unique_indices=True, mode='promise_in_bounds')
    real = pair_of_row >= 0
    token_of_row = jnp.where(real, pair_of_row // 2, 0)
    out_row = jnp.where(real, (pair_of_row % 2) * (n_pairs // 2) + pair_of_row // 2,
                        n_pairs + jnp.arange(n_rows, dtype=jnp.int32) % tm)
    return token_of_row, out_row, jnp.minimum(tile_expert, N_EXPERTS - 1).astype(jnp.int32)


DMA_UNROLL = 8


def _gather_rows(table_hbm, row_of, dst, n, sem, wait):
    def body(c, carry):
        for u in range(DMA_UNROLL):
            r = c * DMA_UNROLL + u
            copy = pltpu.make_async_copy(table_hbm.at[pl.ds(row_of(r), 1)], dst.at[pl.ds(r, 1)], sem)
            copy.wait() if wait else copy.start()
        return carry

    lax.fori_loop(0, n // DMA_UNROLL, body, 0)


def _scatter_rows(src, dst_hbm, row_of, n, sem, wait):
    def body(c, carry):
        for u in range(DMA_UNROLL):
            r = c * DMA_UNROLL + u
            copy = pltpu.make_async_copy(src.at[pl.ds(r, 1)], dst_hbm.at[pl.ds(row_of(r), 1)], sem)
            copy.wait() if wait else copy.start()
        return carry

    lax.fori_loop(0, n // DMA_UNROLL, body, 0)


def _experts_kernel(te_ref, tok_cur_ref, tok_next_ref, pair_prev_ref, pair_wait_ref, hf_hbm, wg_ref, wu_ref, wd_ref,
                    o2_hbm, xin, yout, sems, sem_out, *, tf):
    del te_ref
    i = pl.program_id(0)
    slot = i % 2
    tm = xin.shape[1]
    n_f = wg_ref.shape[2] // tf
    cur_tok = lambda r: tok_cur_ref[0, 0, r]
    next_tok = lambda r: tok_next_ref[0, 0, r]
    prev_pair = lambda r: pair_prev_ref[0, 0, r]

    @pl.when(i == 0)
    def _():
        _gather_rows(hf_hbm, cur_tok, xin.at[0], tm, sems.at[0], wait=False)
        yout[1] = jnp.zeros(yout.shape[1:], F32)

    @pl.when(i >= 1)
    def _():
        _scatter_rows(yout.at[slot], o2_hbm, lambda r: pair_wait_ref[0, 0, r], tm, sem_out, wait=True)

    _gather_rows(hf_hbm, cur_tok, xin.at[slot], tm, sems.at[slot], wait=True)
    xb = _from_slabs(xin.at[slot]).astype(BF16)
    acc = jnp.zeros((tm, wd_ref.shape[2]), F32)
    n_issue = max(n_f - 2, 1)
    bounds = [min(round(c * tm / n_issue), tm) for c in range(n_f + 1)]
    for c in range(n_f):
        for r in range(bounds[c], bounds[c + 1]):
            pltpu.make_async_copy(hf_hbm.at[pl.ds(next_tok(r), 1)], xin.at[1 - slot, pl.ds(r, 1)],
                                  sems.at[1 - slot]).start()
            pltpu.make_async_copy(yout.at[1 - slot, pl.ds(r, 1)], o2_hbm.at[pl.ds(prev_pair(r), 1)], sem_out).start()
        cols = slice(c * tf, (c + 1) * tf)
        gate = jnp.dot(xb, wg_ref[0, :, cols], preferred_element_type=F32)
        up = jnp.dot(xb, wu_ref[0, :, cols], preferred_element_type=F32)
        act = gate * jax.nn.sigmoid(gate) * up
        acc = acc + jnp.dot(act.astype(BF16), wd_ref[0, cols, :], preferred_element_type=F32)
    _to_slabs(yout.at[slot], acc)

    @pl.when(i == pl.num_programs(0) - 1)
    def _():
        _gather_rows(hf_hbm, next_tok, xin.at[1 - slot], tm, sems.at[1 - slot], wait=True)
        _scatter_rows(yout.at[1 - slot], o2_hbm, prev_pair, tm, sem_out, wait=True)


def _moe_experts(hf, token_of_row, out_row_of_row, tile_expert, wg, wu, wd, tf=256):
    T = hf.shape[0]
    slab = hf.shape[1:]
    tm = MOE_TM
    n_tiles = token_of_row.shape[0] // tm
    tok = token_of_row.reshape(n_tiles, 1, tm)
    tok = jnp.concatenate([tok, tok[-1:], tok[-1:]], axis=0)
    spare = (2 * T + jnp.arange(tm, dtype=jnp.int32)).reshape(1, 1, tm)
    pair = jnp.concatenate([spare, out_row_of_row.reshape(n_tiles, 1, tm)], axis=0)
    te = jnp.concatenate([tile_expert, tile_expert[-1:]])
    smem = lambda index: pl.BlockSpec((1, 1, tm), lambda i, te: (index(i), 0, 0), memory_space=pltpu.SMEM)
    expert = lambda shape: pl.BlockSpec((1,) + shape, lambda i, te: (te[i], 0, 0))
    any_spec = pl.BlockSpec(memory_space=pl.ANY)
    grid_spec = pltpu.PrefetchScalarGridSpec(
        num_scalar_prefetch=1,
        grid=(n_tiles + 1,),
        in_specs=[smem(lambda i: i), smem(lambda i: i + 1), smem(lambda i: i), smem(lambda i: jnp.maximum(i - 1, 0)),
                  any_spec, expert(wg.shape[1:]), expert(wu.shape[1:]), expert(wd.shape[1:])],
        out_specs=any_spec,
        scratch_shapes=[pltpu.VMEM((2, tm) + slab, F32), pltpu.VMEM((2, tm) + slab, F32), pltpu.SemaphoreType.DMA((2,)),
                        pltpu.SemaphoreType.DMA(())],
    )
    return pl.pallas_call(
        functools.partial(_experts_kernel, tf=tf),
        grid_spec=grid_spec,
        out_shape=jax.ShapeDtypeStruct((2 * T + tm,) + slab, F32),
        compiler_params=_cparams(("arbitrary",)),
        name="moe_experts",
    )(te, tok, tok, pair, pair, hf, wg, wu, wd)


def _combine_kernel(x_ref, w_ref, gf_ref, y0_ref, y1_ref, out_ref):
    w = w_ref[...]
    y = x_ref[...] + w[:, 0:1] * _from_slabs(y0_ref) + w[:, 1:2] * _from_slabs(y1_ref)
    out_ref[...] = _rms(y, gf_ref[...])


def _moe_combine(x2, w, o2, g_final, tm=512):
    T, D = x2.shape
    row = lambda wd, off: pl.BlockSpec((tm, wd), lambda i: (i + off, 0))
    slab = lambda off: pl.BlockSpec((tm,) + o2.shape[1:], lambda i: (i + off, 0, 0))
    return pl.pallas_call(
        _combine_kernel,
        grid=(T // tm,),
        in_specs=[row(D, 0), row(LANES, 0), pl.BlockSpec(g_final.shape, lambda i: (0, 0)), slab(0), slab(T // tm)],
        out_specs=row(D, 0),
        out_shape=jax.ShapeDtypeStruct((T, D), F32),
        compiler_params=_cparams(("parallel",)),
        name="moe_combine",
    )(x2, w, g_final, o2, o2)


def _moe_routed(x2, g, router_p, wg, wu, wd, g_final):
    T, D = x2.shape
    hf, idx, w = _moe_route(x2, g, router_p)
    token_of_row, out_row_of_row, tile_expert = _moe_plan(idx[:, 0:2].reshape(2 * T), MOE_TM)
    o2 = _moe_experts(hf, token_of_row, out_row_of_row, tile_expert, wg, wu, wd)
    return _moe_combine(x2, w, o2, g_final)


def kernel(x, mem, positions, norm_mix, w_in, nsa_cmp_pe, nsa_cmp_w1, nsa_cmp_w2, pool_w, pool_scale, rwkv_mu, rwkv_w0,
           rwkv_w2, rwkv_a0, rwkv_a2, rwkv_g2, rwkv_v0, rwkv_v1, rwkv_v2, rwkv_k_k, rwkv_k_a, rwkv_r_k, rwkv_ln, gla_a2,
           gla_ab, gla_norm, w_gate, b_gate, w_branch, w_out, norm_ca, mem_norm, w_ca_q, w_ca_kv, w_ca_o, norm_ffn,
           ffn_w_gate, ffn_w_up, ffn_w_down, moe_router, moe_w_gate, moe_w_up, moe_w_down, norm_final):
    B, S, D = x.shape
    depth = norm_mix.shape[0]
    assert depth == 2 and S % 256 == 0, "the final norm is fused into the layer-1 expert mixer"
    cs = _rope_angles(positions)
    cs_end = jnp.pad(cs[:, CMP_LEN - 1::CMP_STRIDE], ((0, 0), (0, 1), (0, 0)))
    memkv = _memkv(mem, mem_norm[None], jnp.concatenate([w_ca_kv[l] for l in range(depth)], axis=1).astype(BF16))
    v_first = None
    for l in range(depth):
        h, q_t, k2, v_t, cv, gates_t, u, rw, gl = _inproj(x, norm_mix[l][None], _inproj_weights(w_in[l]), cs,
                                                          rwkv_mu[l][None])
        kcmp, vcmp_t = _nsa_compress(cv, *_compress_weights(nsa_cmp_pe[l], nsa_cmp_w1[l], nsa_cmp_w2[l]), cs_end)
        o_nsa = _nsa_attend(q_t, k2, v_t, kcmp, vcmp_t, gates_t)
        o_pool = _pool(u, _block_diag(pool_w[l]).astype(BF16), pool_scale[l][None])
        wz, vecs = _rwkv_weights(l, rwkv_w0, rwkv_w2, rwkv_a0, rwkv_a2, rwkv_g2, rwkv_v0, rwkv_k_k, rwkv_k_a,
                                 rwkv_r_k, rwkv_ln)
        if l == 0:
            o_rwkv, v_first = _rwkv(rw, wz, vecs)
        else:
            v1p = jnp.pad(rwkv_v1[l - 1], ((0, 0), (0, LANES - rwkv_v1.shape[2]))).astype(BF16)
            v2p = jnp.pad(rwkv_v2[l - 1], ((0, LANES - rwkv_v2.shape[1]), (0, 0))).astype(BF16)
            o_rwkv = _rwkv(rw, wz, vecs, h, v_first, v1p, v2p)
        a2p = jnp.pad(gla_a2[l], ((0, LANES - gla_a2.shape[1]), (0, 0))).astype(BF16)
        o_gla = _gla(gl, a2p, gla_ab[l][None], jnp.tile(gla_norm[l], 4)[None])
        wb_nsa = w_branch[l, 0].reshape(4, HEAD_DIM, D)[jnp.array([0, 2, 1, 3])].reshape(BRANCH_WIDTH, D)
        wb = jnp.concatenate([wb_nsa[None], w_branch[l, 1:]], axis=0).astype(BF16)
        x = _merge(x, h, o_nsa, o_pool, o_rwkv, o_gla, w_gate[l].astype(BF16), b_gate[l], wb, w_out[l].astype(BF16))
        x = _cross(x, norm_ca[l][None], w_ca_q[l].astype(BF16), memkv, l, w_ca_o[l].astype(BF16))
        j = l // 2
        x2 = x.reshape(B * S, D)
        if l % 2 == 0:
            x2 = _ffn(x2, norm_ffn[l][None], ffn_w_gate[j].astype(BF16), ffn_w_up[j].astype(BF16),
                      ffn_w_down[j].astype(BF16))
        else:
            router_p = jnp.pad(moe_router[j], ((0, 0), (0, LANES - N_EXPERTS)))
            x2 = _moe_routed(x2, norm_ffn[l][None], router_p, moe_w_gate[j].astype(BF16), moe_w_up[j].astype(BF16),
                             moe_w_down[j].astype(BF16), norm_final[None])
        x = x2.reshape(B, S, D)
    return x
```

```python
import functools

import jax
import jax.numpy as jnp
import numpy as np
from jax import lax
from jax.experimental import pallas as pl
from jax.experimental.pallas import tpu as pltpu

F32 = jnp.float32
BF16 = jnp.bfloat16

D_MODEL = 1024
HEAD_DIM = 64
BRANCH_WIDTH = 256
ROPE_THETA = 500000.0
ROPE_DIM = 16
NORM_EPS = 1e-6
NEG_INF = -1e30

CMP_LEN = 32
CMP_STRIDE = 16
CMP_HIDDEN = 128
SEL_BLOCK = 64
SEL_TOPN = 8
WINDOW = 512

RWKV_IN = 896
RWKV_GN_EPS = 64e-5
RWKV_CHUNK = 64

GLA_DK = 32
GLA_TAU = 16.0
GLA_CHUNK = 64

CA_WIDTH = 256
N_EXPERTS = 8

LANES = 128
VMEM_LIMIT = 56 * 1024 * 1024

_C_Q = 0
_C_KV = 256
_C_CV = 768
_C_GATE = 1024
_C_POOL = 1152
_C_RWKV = 1408
_C_GLA = 2304
_C_END = 3200


def _cparams(sem):
    return pltpu.CompilerParams(dimension_semantics=sem, vmem_limit_bytes=VMEM_LIMIT)


def _rms(x, g):
    ms = jnp.mean(x * x, axis=-1, keepdims=True)
    return x * lax.rsqrt(ms + NORM_EPS) * g


def _rope128(t, cos, sin):
    lane = lax.broadcasted_iota(jnp.int32, t.shape, 1) % HEAD_DIM
    partner = jnp.where(lane < ROPE_DIM // 2, pltpu.roll(t, LANES - ROPE_DIM // 2, 1),
                        pltpu.roll(t, ROPE_DIM // 2, 1))
    return t * cos + partner * sin


def _inproj_kernel(x_ref, g_ref, w_ref, cs_ref, e_ref, ones_ref, mu_ref,
                   h_ref, qt_ref, k_ref, vt_ref, cv_ref, gt_ref, u_ref, rw_ref, gl_ref, prev_ref):
    s = pl.program_id(1)
    hb = _rms(x_ref[0], g_ref[...]).astype(BF16)
    h_ref[0] = hb

    proj = jnp.dot(hb, w_ref[...], preferred_element_type=F32)

    def mm(a, b):
        return proj[:, a:b]

    cos, sin = _rope_expand(cs_ref[0], e_ref[...], ones_ref[...])
    scale = HEAD_DIM ** -0.5
    qt_ref[0, 0:128, :] = jnp.transpose(_rope128(mm(_C_Q, _C_Q + 128), cos, sin) * scale).astype(BF16)
    qt_ref[0, 128:256, :] = jnp.transpose(_rope128(mm(_C_Q + 128, _C_Q + 256), cos, sin) * scale).astype(BF16)
    k_ref[0, :, 0:128] = _rope128(mm(_C_KV, _C_KV + 128), cos, sin).astype(BF16)
    vt_ref[0, 0:128, :] = jnp.transpose(mm(_C_KV + 128, _C_KV + 256)).astype(BF16)
    k_ref[0, :, 128:256] = _rope128(mm(_C_KV + 256, _C_KV + 384), cos, sin).astype(BF16)
    vt_ref[0, 128:256, :] = jnp.transpose(mm(_C_KV + 384, _C_KV + 512)).astype(BF16)
    cv_ref[0] = mm(_C_CV, _C_CV + 256).astype(BF16)
    gt_ref[0] = jnp.transpose(jax.nn.sigmoid(mm(_C_GATE, _C_GATE + 128)))
    u_ref[0] = mm(_C_POOL, _C_POOL + 256)
    gl_ref[0] = mm(_C_GLA, _C_END)

    p = mm(_C_RWKV, _C_RWKV + RWKV_IN)
    tm = p.shape[0]

    @pl.when(s == 0)
    def _():
        prev_ref[...] = jnp.zeros_like(prev_ref)

    row = lax.broadcasted_iota(jnp.int32, p.shape, 0)
    shifted = jnp.where(row == 0, prev_ref[0:1, :], pltpu.roll(p, 1, 0))
    prev_ref[0:1, :] = p[tm - 1:tm, :]
    rw_ref[0] = p + (shifted - p) * mu_ref[...]


def _inproj(x, g, w_all, cs, mu, tm=512):
    B, S, D = x.shape
    e, ones = _rope_expander()
    tok = lambda w: pl.BlockSpec((1, tm, w), lambda b, s: (b, s, 0))
    tok_t = lambda w: pl.BlockSpec((1, w, tm), lambda b, s: (b, 0, s))
    full = lambda a: pl.BlockSpec(a.shape, lambda b, s: (0,) * a.ndim)
    outs = [(D, BF16, True), (256, BF16, False), (256, BF16, True), (256, BF16, False), (256, BF16, True),
            (128, F32, False), (256, F32, True), (RWKV_IN, F32, True), (896, F32, True)]
    return pl.pallas_call(
        _inproj_kernel,
        grid=(B, S // tm),
        in_specs=[tok(D), full(g), full(w_all), tok(ROPE_DIM), full(e), full(ones), full(mu)],
        out_specs=[tok(w) if major else tok_t(w) for w, _, major in outs],
        out_shape=[jax.ShapeDtypeStruct((B, S, w) if major else (B, w, S), dt) for w, dt, major in outs],
        scratch_shapes=[pltpu.VMEM((8, RWKV_IN), F32)],
        compiler_params=_cparams(("parallel", "arbitrary")),
        name="inproj",
    )(x, g, w_all, cs, e, ones, mu)


def _inproj_columns():
    span = lambda lo, n: list(range(lo, lo + n))
    pad = lambda n: [-1] * n
    nsa, pool, rwkv, gla = 0, 1036, 1292, 2188
    cols = (span(nsa, 64) + span(nsa + 128, 64) + span(nsa + 64, 64) + span(nsa + 192, 64)
            + span(nsa + 512, 512)
            + span(nsa + 256, 256)
            + span(nsa + 1024, 12) + pad(116)
            + span(pool, 256) + span(rwkv, RWKV_IN)
            + span(gla, 512) + span(gla + 528, 256) + span(gla + 512, 16) + pad(112))
    assert len(cols) == _C_END
    return np.asarray(cols, np.int32)


def _inproj_weights(w):
    onehot = (jnp.arange(w.shape[1], dtype=jnp.int32)[:, None] == jnp.asarray(_inproj_columns())[None, :]).astype(BF16)
    return jnp.dot(w.astype(BF16), onehot, preferred_element_type=F32).astype(BF16)


def _rope_angles(positions):
    inv_freq = ROPE_THETA ** (-jnp.arange(0, ROPE_DIM, 2, dtype=F32) / ROPE_DIM)
    ang = positions.astype(F32)[..., None] * inv_freq
    return jnp.concatenate([jnp.cos(ang), jnp.sin(ang)], axis=-1)


def _rope_expander():
    half = ROPE_DIM // 2
    e = np.zeros((ROPE_DIM, 2 * LANES), np.float32)
    ones = np.zeros((1, LANES), np.float32)
    for lane in range(LANES):
        d = lane % HEAD_DIM
        if d < ROPE_DIM:
            e[d % half, lane] = 1.0
            e[half + d % half, LANES + lane] = -1.0 if d < half else 1.0
        else:
            ones[0, lane] = 1.0
    return jnp.asarray(e, BF16), jnp.asarray(ones)


def _rope_expand(cs, e, ones):
    tab = _dot_split_lhs(cs, e, 3)
    return tab[:, 0:LANES] + ones, tab[:, LANES:2 * LANES]


def _gelu_tanh(x):
    return x * (0.5 * (1.0 + jnp.tanh(np.sqrt(2.0 / np.pi) * (x + 0.044715 * (x * x * x)))))


def _dot_nt(a, b):
    return lax.dot_general(a, b, (((1,), (1,)), ((), ())), preferred_element_type=F32)


def _compress_kernel(x_ref, wa_ref, wb_ref, bias_ref, w2k_ref, w2v_ref, cs_ref, e_ref, ones_ref, kc_ref, vc_ref):
    x = x_ref[0]
    a = jnp.dot(x, wa_ref[...], preferred_element_type=F32)
    b = jnp.dot(x, wb_ref[...], preferred_element_type=F32)
    n = a.shape[0]
    hid = _gelu_tanh(a + pltpu.roll(b, n - 1, 0) + bias_ref[...])
    kc = jnp.dot(hid[:, 0:256].astype(BF16), w2k_ref[...], preferred_element_type=F32)
    vc = jnp.dot(hid[:, 256:512].astype(BF16), w2v_ref[...], preferred_element_type=F32)
    cos, sin = _rope_expand(cs_ref[0], e_ref[...], ones_ref[...])
    kc_ref[0] = _rope128(kc, cos, sin).astype(BF16)
    vc_ref[0] = jnp.transpose(vc).astype(BF16)


def _nsa_compress(cv, wa, wb, bias, w2k, w2v, cs_end):
    B, S, _ = cv.shape
    nseg = S // CMP_STRIDE
    x = cv.reshape(B, nseg, CMP_STRIDE * 256)
    e, ones = _rope_expander()
    per_b = lambda r, w: pl.BlockSpec((1, r, w), lambda b: (b, 0, 0))
    full = lambda a: pl.BlockSpec(a.shape, lambda b: (0,) * a.ndim)
    return pl.pallas_call(
        _compress_kernel,
        grid=(B,),
        in_specs=[per_b(nseg, CMP_STRIDE * 256), full(wa), full(wb), full(bias), full(w2k), full(w2v),
                  per_b(nseg, ROPE_DIM), full(e), full(ones)],
        out_specs=[per_b(nseg, 128), per_b(nseg, 128)],
        out_shape=[jax.ShapeDtypeStruct((B, nseg, 128), BF16)] * 2,
        compiler_params=_cparams(("parallel",)),
        name="nsa_compress",
    )(x, wa, wb, bias, w2k, w2v, cs_end, e, ones)


def _compress_weights(pe, w1, w2):
    eye2 = jnp.eye(2, dtype=F32)
    half = lambda lo: jnp.einsum('jldf,pj,qg->lpqdjgf', w1[:, lo:lo + CMP_STRIDE], eye2, eye2).reshape(
        CMP_STRIDE * 256, 4 * CMP_HIDDEN).astype(BF16)
    bias = jnp.einsum('jld,jldf->jf', pe, w1)
    bias = jnp.broadcast_to(bias[:, None, :], (2, 2, CMP_HIDDEN)).reshape(1, 4 * CMP_HIDDEN)
    bd = lambda w: jnp.einsum('fd,gq->gfqd', w, eye2).reshape(2 * CMP_HIDDEN, 2 * HEAD_DIM).astype(BF16)
    return half(0), half(CMP_STRIDE), bias, bd(w2[0]), bd(w2[1])


def _cover_t(S):
    n_cmp = (S - CMP_LEN) // CMP_STRIDE + 1
    cmp_start = np.arange(n_cmp) * CMP_STRIDE
    slc_start = np.arange(S // SEL_BLOCK) * SEL_BLOCK
    cover = np.clip(np.minimum(cmp_start[:, None] + CMP_LEN, slc_start[None, :] + SEL_BLOCK)
                    - np.maximum(cmp_start[:, None], slc_start[None, :]), 0, None) / CMP_LEN
    out = np.zeros((S // SEL_BLOCK, S // CMP_STRIDE), np.float32)
    out[:, :n_cmp] = cover.T
    return jnp.asarray(out, BF16)


MASKED = 2.0 * NEG_INF


def _softmax_step(s, carry, v_t):
    m, l, acc = carry
    m_new = jnp.maximum(m, jnp.max(s, axis=0, keepdims=True))
    alpha = jnp.exp(m - m_new)
    p = jnp.exp(s - m_new)
    l = alpha * l + jnp.sum(p, axis=0, keepdims=True)
    acc = alpha * acc + jnp.dot(v_t, p.astype(BF16), preferred_element_type=F32)
    return m_new, l, acc


def _softmax_init(n_q):
    return (jnp.full((1, n_q), NEG_INF, F32), jnp.zeros((1, n_q), F32), jnp.zeros((LANES, n_q), F32))


def _nsa_kernel(qt_ref, k_ref, vt_ref, kc_ref, vct_ref, gt_ref, covt_ref, o_ref, *, tq, tk):
    i = pl.program_id(1)
    t0 = i * tq
    n_blk = covt_ref.shape[0]
    n_q = 4 * tq
    qa_t, qb_t = qt_ref[0, 0:128, :], qt_ref[0, 128:256, :]
    sub_grp = lax.broadcasted_iota(jnp.int32, (LANES, tq), 0) // HEAD_DIM
    gates_t = gt_ref[0]
    tok = t0 + (lax.broadcasted_iota(jnp.int32, (1, n_q), 1) & (tq - 1))
    q_all = jnp.concatenate([jnp.where(sub_grp == g, q_t, 0) for g in range(2) for q_t in (qa_t, qb_t)], axis=1)

    s = jnp.dot(kc_ref[0], q_all, preferred_element_type=F32)
    nrow = lax.broadcasted_iota(jnp.int32, s.shape, 0)
    s = jnp.where(nrow * CMP_STRIDE + (CMP_LEN - 1) <= tok, s, MASKED)
    e = jnp.exp(s - jnp.maximum(jnp.max(s, axis=0, keepdims=True), NEG_INF))
    den = jnp.sum(e, axis=0, keepdims=True)
    p = e / jnp.where(den > 0.0, den, 1.0)
    o_cmp = jnp.dot(vct_ref[0], p.astype(BF16), preferred_element_type=F32)

    p_grp = jnp.concatenate([p[:, 2 * g * tq:(2 * g + 1) * tq] + p[:, (2 * g + 1) * tq:(2 * g + 2) * tq]
                             for g in range(2)], axis=1)
    imp = _dot_split_rhs(covt_ref[...], p_grp, 2)
    blk = lax.broadcasted_iota(jnp.int32, imp.shape, 0)
    tok_b = t0 + (lax.broadcasted_iota(jnp.int32, imp.shape, 1) & (tq - 1))
    cur = tok_b // SEL_BLOCK
    forced = (blk == 0) | (blk == cur) | (blk == cur - 1)
    imp = jnp.where(forced, 1e30, jnp.where(blk * SEL_BLOCK <= tok_b, imp, -1e30))
    cnt = jnp.zeros(imp.shape, F32)
    for i2 in range(n_blk):
        row = imp[i2:i2 + 1, :]
        cnt = cnt + jnp.where((row > imp) | ((row == imp) & (blk > i2)), 1.0, 0.0)
    bias = jnp.where(cnt < float(SEL_TOPN), 0.0, MASKED).astype(BF16)
    sel_bias = jnp.concatenate([bias[:, g * tq:(g + 1) * tq] for g in (0, 0, 1, 1)], axis=1)

    q_ext = jnp.concatenate([q_all, sel_bias, jnp.zeros((LANES - n_blk, n_q), BF16)], axis=0)

    def sel_scores(j):
        rows = pl.ds(pl.multiple_of(j * tk, tk), tk)
        blk_of_key = (j * tk + lax.broadcasted_iota(jnp.int32, (tk, LANES), 0)) // SEL_BLOCK
        onehot = (lax.broadcasted_iota(jnp.int32, (tk, LANES), 1) == blk_of_key).astype(BF16)
        k_ext = jnp.concatenate([k_ref[0, rows, 0:128], onehot], axis=1)
        return jnp.dot(k_ext, q_ext, preferred_element_type=F32), vt_ref[0, 0:128, rows]

    def sel_step(j, carry):
        s, v_t = sel_scores(j)
        return _softmax_step(s, carry, v_t)

    last = (t0 + tq - 1) // tk
    carry = lax.fori_loop(0, last, sel_step, _softmax_init(n_q))
    s, v_t = sel_scores(last)
    key = last * tk + lax.broadcasted_iota(jnp.int32, (tk, n_q), 0)
    _, l_sel, acc_sel = _softmax_step(jnp.where(key <= tok, s, MASKED), carry, v_t)
    o_sel = acc_sel / l_sel

    span = WINDOW + tq
    start = pl.multiple_of(jnp.maximum(t0 - WINDOW, 0), tq)
    key = start + lax.broadcasted_iota(jnp.int32, (span, n_q), 0)
    s = jnp.dot(k_ref[0, pl.ds(start, span), 128:256], q_all, preferred_element_type=F32)
    in_window = lax.bitcast_convert_type(tok - key, jnp.uint32) < jnp.uint32(WINDOW)
    s = jnp.where(in_window, s, MASKED)
    e = jnp.exp(s - jnp.max(s, axis=0, keepdims=True))
    o_win = (jnp.dot(vt_ref[0, 128:256, pl.ds(start, span)], e.astype(BF16), preferred_element_type=F32)
             / jnp.sum(e, axis=0, keepdims=True))

    out = [jnp.zeros((LANES, tq), F32), jnp.zeros((LANES, tq), F32)]
    for g in range(2):
        for r in range(2):
            cols = slice((2 * g + r) * tq, (2 * g + r + 1) * tq)
            c = g * 6 + r * 3
            o = (gates_t[c:c + 1, :] * o_cmp[:, cols] + gates_t[c + 1:c + 2, :] * o_sel[:, cols]
                 + gates_t[c + 2:c + 3, :] * o_win[:, cols])
            out[r] = jnp.where(sub_grp == g, o, out[r])
    o_ref[0, :, 0:128] = jnp.transpose(out[0]).astype(BF16)
    o_ref[0, :, 128:256] = jnp.transpose(out[1]).astype(BF16)


def _nsa_attend(q_t, k2, v_t, kcmp, vcmp_t, gates_t, tq=256, tk=512):
    B, S, _ = k2.shape
    assert tk % tq == 0 and S % tk == 0 and WINDOW % tq == 0
    covt = _cover_t(S)
    tile_t = lambda w: pl.BlockSpec((1, w, tq), lambda b, i: (b, 0, i))
    per_b = lambda r, w: pl.BlockSpec((1, r, w), lambda b, i: (b, 0, 0))
    return pl.pallas_call(
        functools.partial(_nsa_kernel, tq=tq, tk=tk),
        grid=(B, S // tq),
        in_specs=[tile_t(256), per_b(S, 256), per_b(256, S), per_b(S // CMP_STRIDE, 128), per_b(128, S // CMP_STRIDE),
                  tile_t(128), pl.BlockSpec(covt.shape, lambda b, i: (0, 0))],
        out_specs=pl.BlockSpec((1, tq, 256), lambda b, i: (b, i, 0)),
        out_shape=jax.ShapeDtypeStruct((B, S, 256), BF16),
        compiler_params=_cparams(("parallel", "arbitrary")),
        name="nsa_attend",
    )(q_t, k2, v_t, kcmp, vcmp_t, gates_t, covt)


def _pool_kernel(u_ref, w_ref, scale_ref, o_ref):
    u = u_ref[0]
    row = lax.broadcasted_iota(jnp.int32, u.shape, 0)
    grp = lax.broadcasted_iota(jnp.int32, u.shape, 1) // HEAD_DIM

    def back(x, k):
        return jnp.where(row >= k, pltpu.roll(x, k, 0), 0.0)

    s2 = u + back(u, 1)
    s4 = s2 + back(s2, 2)
    s8 = s4 + back(s4, 4)
    s16 = s8 + back(s8, 8)
    total = jnp.where(grp == 0, s2, jnp.where(grp == 1, s4, jnp.where(grp == 2, s8, s16)))
    win = jnp.where(grp == 0, 2, jnp.where(grp == 1, 4, jnp.where(grp == 2, 8, 16)))
    count = jnp.minimum(row + 1, win).astype(F32)
    d = total / count - u
    y = jnp.dot(d.astype(BF16), w_ref[...], preferred_element_type=F32)
    o_ref[0] = (y * scale_ref[...]).astype(BF16)


def _pool(u, w_bd, scale):
    B, S, W = u.shape
    per_b = pl.BlockSpec((1, S, W), lambda b: (b, 0, 0))
    full = lambda a: pl.BlockSpec(a.shape, lambda b: (0,) * a.ndim)
    return pl.pallas_call(
        _pool_kernel,
        grid=(B,),
        in_specs=[per_b, full(w_bd), full(scale)],
        out_specs=per_b,
        out_shape=jax.ShapeDtypeStruct((B, S, W), BF16),
        compiler_params=_cparams(("parallel",)),
        name="pool",
    )(u, w_bd, scale)


def _block_diag(w):
    G, a, b = w.shape
    return jnp.einsum('gab,gh->gahb', w, jnp.eye(G, dtype=w.dtype)).reshape(G * a, G * b)


def _dot_split_lhs(a, b_exact, terms):
    out, rem = None, a
    for i in range(terms):
        hi = rem.astype(BF16)
        d = jnp.dot(hi, b_exact, preferred_element_type=F32)
        out = d if out is None else out + d
        if i + 1 < terms:
            rem = rem - hi.astype(F32)
    return out


def _dot_split_rhs(a_exact, b, terms):
    out, rem = None, b
    for i in range(terms):
        hi = rem.astype(BF16)
        d = jnp.dot(a_exact, hi, preferred_element_type=F32)
        out = d if out is None else out + d
        if i + 1 < terms:
            rem = rem - hi.astype(F32)
    return out


def _dot_tn(a, b):
    return lax.dot_general(a, b, (((0,), (0,)), ((), ())), preferred_element_type=F32)


def _stack_heads(x, width, n_heads=4):
    head = lax.broadcasted_iota(jnp.int32, x.shape, 1) // width
    return jnp.concatenate([jnp.where(head == h, x, 0.0) for h in range(n_heads)], axis=0)


def _unstack_heads(x, C, n_heads=4):
    out = x[0:C]
    for h in range(1, n_heads):
        out = out + x[h * C:(h + 1) * C]
    return out


def _block_masks(C, n_heads=4):
    n = C * n_heads
    row = lax.broadcasted_iota(jnp.int32, (n, n), 0)
    col = lax.broadcasted_iota(jnp.int32, (n, n), 1)
    same = (row // C) == (col // C)
    return same & (row > col), same & (row >= col)


def _tri_ones(C):
    return jnp.asarray(np.tril(np.ones((C, C), np.float32)), BF16)


def _head_ones(width, n_heads=4):
    return jnp.asarray(np.kron(np.eye(n_heads, dtype=np.float32), np.ones((width, width), np.float32)), BF16)


def _gla_kernel(gl_ref, a2_ref, ab_ref, norm_ref, tri_ref, ones_ref, o_ref, state_ref, *, C):
    s = pl.program_id(1)

    @pl.when(s == 0)
    def _():
        state_ref[...] = jnp.zeros_like(state_ref)

    gl = gl_ref[0]
    tm = gl.shape[0]
    q = gl[:, 0:128] * (GLA_DK ** -0.5)
    k = gl[:, 128:256]
    v = gl[:, 256:512]
    og = gl[:, 512:768]
    ad = gl[:, 768:896]
    x = jnp.dot(ad.astype(BF16), a2_ref[...], preferred_element_type=F32) + ab_ref[...]
    log_a = jax.nn.log_sigmoid(x) * (1.0 / GLA_TAU)
    strict, incl = _block_masks(C)
    kv_mask = (lax.broadcasted_iota(jnp.int32, (256, 128), 0) // HEAD_DIM
               == lax.broadcasted_iota(jnp.int32, (256, 128), 1) // GLA_DK)
    rcs = [slice(c * C, (c + 1) * C) for c in range(tm // C)]
    bcum = [_dot_split_rhs(tri_ref[...], log_a[rc], 3) for rc in rcs]
    mid = [b[C // 2 - 1:C // 2, :] for b in bcum]
    last = [b[C - 1:C, :] for b in bcum]
    q_m = [_stack_heads(q[rc] * jnp.exp(b - m), GLA_DK).astype(BF16) for rc, b, m in zip(rcs, bcum, mid)]
    k_m = [_stack_heads(k[rc] * jnp.exp(m - b), GLA_DK).astype(BF16) for rc, b, m in zip(rcs, bcum, mid)]
    att = [jnp.where(incl, _dot_nt(qm, km), 0.0).astype(BF16) for qm, km in zip(q_m, k_m)]
    v_st = [_stack_heads(v[rc], HEAD_DIM).astype(BF16) for rc in rcs]
    o_intra = [_unstack_heads(jnp.dot(a, vs, preferred_element_type=F32), C) for a, vs in zip(att, v_st)]
    k_end = [(k[rc] * jnp.exp(l - b)).astype(BF16) for rc, b, l in zip(rcs, bcum, last)]
    upd = [jnp.where(kv_mask, _dot_tn(v[rc].astype(BF16), ke), 0.0) for rc, ke in zip(rcs, k_end)]
    q_in = [(q[rc] * jnp.exp(b)).astype(BF16) for rc, b in zip(rcs, bcum)]
    decay = [jnp.exp(l) for l in last]
    outs = []
    state = state_ref[...]
    for c in range(len(rcs)):
        outs.append(o_intra[c] + _dot_nt(q_in[c], state.astype(BF16)))
        state = state * decay[c] + upd[c]
    state_ref[...] = state
    o = jnp.concatenate(outs, axis=0)
    ms = _dot_split_lhs(o * o, ones_ref[...], 2) * (1.0 / HEAD_DIM)
    o = o * lax.rsqrt(ms + NORM_EPS) * norm_ref[...]
    o_ref[0] = (o * (og * jax.nn.sigmoid(og))).astype(BF16)


def _gla(gl, a2p, ab, norm4, tm=512):
    B, S, W = gl.shape
    C = GLA_CHUNK
    tri, ones = _tri_ones(C), _head_ones(HEAD_DIM)
    tile = lambda w: pl.BlockSpec((1, tm, w), lambda b, s: (b, s, 0))
    full = lambda a: pl.BlockSpec(a.shape, lambda b, s: (0,) * a.ndim)
    return pl.pallas_call(
        functools.partial(_gla_kernel, C=C),
        grid=(B, S // tm),
        in_specs=[tile(W), full(a2p), full(ab), full(norm4), full(tri), full(ones)],
        out_specs=tile(256),
        out_shape=jax.ShapeDtypeStruct((B, S, 256), BF16),
        scratch_shapes=[pltpu.VMEM((256, 128), F32)],
        compiler_params=_cparams(("parallel", "arbitrary")),
        name="gla",
    )(gl, a2p, ab, norm4, tri, ones)


_RV_W0, _RV_A0, _RV_KK, _RV_KA, _RV_RK, _RV_LN, _RV_V0 = range(7)


def _rwkv_kernel(*refs, has_vres, C):
    if has_vres:
        (rw_ref, h_ref, vf_ref, wz_ref, vec_ref, v1_ref, v2_ref, tri_ref, ones_ref, o_ref, state_ref) = refs
    else:
        (rw_ref, wz_ref, vec_ref, tri_ref, ones_ref, o_ref, vout_ref, state_ref) = refs
    s = pl.program_id(1)

    @pl.when(s == 0)
    def _():
        state_ref[...] = jnp.zeros_like(state_ref)

    rw = rw_ref[0]
    tm = rw.shape[0]
    r, k, v, z = rw[:, 0:256], rw[:, 256:512], rw[:, 512:768], rw[:, 768:896]
    vec = lambda i: vec_ref[i:i + 1, :]
    zl = lax.broadcasted_iota(jnp.int32, z.shape, 1)
    zf = jnp.where(zl < 32, jnp.tanh(z), jnp.where(zl < 64, z, jax.nn.sigmoid(z)))
    zz = jnp.dot(zf.astype(BF16), wz_ref[...], preferred_element_type=F32)
    w_log = -jax.nn.softplus(-(vec(_RV_W0) + zz[:, 0:256])) - 0.5
    lw = -jnp.exp(w_log)
    a_sig = jax.nn.sigmoid(vec(_RV_A0) + zz[:, 256:512])
    gate = zz[:, 512:768]
    if has_vres:
        low = jnp.dot(h_ref[0], v1_ref[...], preferred_element_type=F32)
        logit = vec(_RV_V0) + jnp.dot(low.astype(BF16), v2_ref[...], preferred_element_type=F32)
        v = v + (vf_ref[0] - v) * jax.nn.sigmoid(logit)
    else:
        vout_ref[0] = v
    kk = k * vec(_RV_KK)
    norm = jnp.sqrt(_dot_split_lhs(kk * kk, ones_ref[...], 2))
    kk = kk / jnp.maximum(norm, 1e-12)
    k = k * (1.0 + (a_sig - 1.0) * vec(_RV_KA))
    a_vec = -kk
    b_vec = kk * a_sig

    strict, incl = _block_masks(C)
    st = lambda t: _stack_heads(t, HEAD_DIM)
    bdot = lambda p, q: jnp.dot(p.astype(BF16), q.astype(BF16), preferred_element_type=F32)
    state_mask = (lax.broadcasted_iota(jnp.int32, (256, 256), 0) // HEAD_DIM
                  == lax.broadcasted_iota(jnp.int32, (256, 256), 1) // HEAD_DIM)
    n4 = 4 * C
    rcs = [slice(c * C, (c + 1) * C) for c in range(tm // C)]
    each = lambda fn, *lists: [fn(*args) for args in zip(*lists)]
    lcum = [_dot_split_rhs(tri_ref[...], lw[rc], 3) for rc in rcs]
    lex = each(lambda rc, lc: lc - lw[rc], rcs, lcum)
    mid = [lc[C // 2 - 1:C // 2, :] for lc in lcum]
    last = [lc[C - 1:C, :] for lc in lcum]
    e_mid = each(lambda lc, m: jnp.exp(m - lc), lcum, mid)
    e_end = each(lambda lc, l: jnp.exp(l - lc), lcum, last)
    left = each(lambda rc, lc, lx, m: jnp.concatenate(
        [st(a_vec[rc] * jnp.exp(lx - m)), st(r[rc] * jnp.exp(lc - m))], axis=0).astype(BF16), rcs, lcum, lex, mid)
    right = each(lambda rc, em: jnp.concatenate([st(b_vec[rc] * em), st(k[rc] * em)], axis=0).astype(BF16), rcs, e_mid)
    g = each(_dot_nt, left, right)
    v_st = [st(v[rc]) for rc in rcs]
    p = [jnp.where(strict, gc[0:n4, 0:n4], 0.0) for gc in g]
    rhs = each(lambda gc, vs: bdot(jnp.where(strict, gc[0:n4, n4:2 * n4], 0.0), vs), g, v_st)
    y_rk = each(lambda gc, vs: bdot(jnp.where(incl, gc[n4:2 * n4, n4:2 * n4], 0.0), vs), g, v_st)
    n_rb = [jnp.where(incl, gc[n4:2 * n4, 0:n4], 0.0) for gc in g]
    x = each(lambda rc, lx, rh: jnp.concatenate([st(a_vec[rc] * jnp.exp(lx)), rh], axis=1), rcs, lex, rhs)
    bk_end = each(lambda rc, ee: jnp.concatenate([b_vec[rc] * ee, k[rc] * ee], axis=0).astype(BF16), rcs, e_end)
    eye = (lax.broadcasted_iota(jnp.int32, (n4, n4), 0) == lax.broadcasted_iota(jnp.int32, (n4, n4), 1)).astype(F32)
    t = [eye + pc for pc in p]
    for f in range(1, int(np.log2(C))):
        p = [bdot(pc, pc) for pc in p]
        t = each(lambda tc, pc: tc + bdot(pc, tc), t, p)
    x = each(bdot, t, x)
    y_part = each(bdot, n_rb, x)
    a_eff = [_unstack_heads(xc[:, 0:256], C).astype(BF16) for xc in x]
    u0 = [_unstack_heads(xc[:, 256:512], C) for xc in x]
    r_eff = each(lambda rc, lc, yp: (r[rc] * jnp.exp(lc) + _unstack_heads(yp[:, 0:256], C)).astype(BF16),
                 rcs, lcum, y_part)
    y0 = each(lambda yp, yr: _unstack_heads(yp[:, 256:512] + yr, C), y_part, y_rk)
    mix = each(lambda ae, bk: jnp.where(state_mask, _dot_tn(ae, bk[0:C]), 0.0).astype(BF16), a_eff, bk_end)
    add = each(lambda rc, u, bk: jnp.where(
        state_mask, _dot_tn(jnp.concatenate([u, v[rc]], axis=0).astype(BF16), bk), 0.0), rcs, u0, bk_end)
    decay = [jnp.exp(l) for l in last]
    ys = []
    state = state_ref[...]
    for c in range(len(rcs)):
        sb = state.astype(BF16)
        ys.append(_dot_nt(r_eff[c], sb) + y0[c])
        state = state * decay[c] + jnp.dot(sb, mix[c], preferred_element_type=F32) + add[c]
    state_ref[...] = state
    y = jnp.concatenate(ys, axis=0)
    inv = 1.0 / HEAD_DIM
    mean = _dot_split_lhs(y, ones_ref[...], 2) * inv
    yc = y - mean
    var = _dot_split_lhs(yc * yc, ones_ref[...], 2) * inv
    y = yc * lax.rsqrt(var + RWKV_GN_EPS) * vec(_RV_LN)
    y = y + _dot_split_lhs(r * k * vec(_RV_RK), ones_ref[...], 2) * v
    o_ref[0] = (y * gate).astype(BF16)


def _rwkv(rw, wz, vecs, h=None, v_first=None, v1p=None, v2p=None, tm=256):
    B, S, W = rw.shape
    C = RWKV_CHUNK
    has_vres = h is not None
    tri, ones = _tri_ones(C), _head_ones(HEAD_DIM)
    tile = lambda w: pl.BlockSpec((1, tm, w), lambda b, s: (b, s, 0))
    full = lambda a: pl.BlockSpec(a.shape, lambda b, s: (0,) * a.ndim)
    if has_vres:
        args = (rw, h, v_first, wz, vecs, v1p, v2p, tri, ones)
        in_specs = [tile(W), tile(D_MODEL), tile(256), full(wz), full(vecs), full(v1p), full(v2p), full(tri), full(ones)]
        out_specs = tile(256)
        out_shape = jax.ShapeDtypeStruct((B, S, 256), BF16)
    else:
        args = (rw, wz, vecs, tri, ones)
        in_specs = [tile(W), full(wz), full(vecs), full(tri), full(ones)]
        out_specs = [tile(256), tile(256)]
        out_shape = [jax.ShapeDtypeStruct((B, S, 256), BF16), jax.ShapeDtypeStruct((B, S, 256), F32)]
    return pl.pallas_call(
        functools.partial(_rwkv_kernel, has_vres=has_vres, C=C),
        grid=(B, S // tm),
        in_specs=in_specs,
        out_specs=out_specs,
        out_shape=out_shape,
        scratch_shapes=[pltpu.VMEM((256, 256), F32)],
        compiler_params=_cparams(("parallel", "arbitrary")),
        name="rwkv_vres" if has_vres else "rwkv",
    )(*args)


def _rwkv_weights(l, rwkv_w0, rwkv_w2, rwkv_a0, rwkv_a2, rwkv_g2, rwkv_v0, rwkv_k_k, rwkv_k_a, rwkv_r_k, rwkv_ln):
    wz = jnp.zeros((128, 768), F32)
    wz = wz.at[0:32, 0:256].set(rwkv_w2[l]).at[32:64, 256:512].set(rwkv_a2[l]).at[64:128, 512:768].set(rwkv_g2[l])
    v0 = rwkv_v0[l - 1] if l > 0 else jnp.zeros((256,), F32)
    vecs = jnp.stack([rwkv_w0[l], rwkv_a0[l], rwkv_k_k[l], rwkv_k_a[l], rwkv_r_k[l].reshape(-1), rwkv_ln[l], v0,
                      jnp.zeros((256,), F32)])
    return wz.astype(BF16), vecs


def _merge_kernel(x_ref, h_ref, o0_ref, o1_ref, o2_ref, o3_ref, wg_ref, bg_ref, wb_ref, wo_ref, out_ref):
    h = h_ref[0]
    merged = None
    for i, o_ref in enumerate((o0_ref, o1_ref, o2_ref, o3_ref)):
        gate = jax.nn.sigmoid(jnp.dot(h, wg_ref[i], preferred_element_type=F32) + bg_ref[i:i + 1, :])
        term = gate * jnp.dot(o_ref[0], wb_ref[i], preferred_element_type=F32)
        merged = term if merged is None else merged + term
    out_ref[0] = x_ref[0] + jnp.dot(merged.astype(BF16), wo_ref[...], preferred_element_type=F32)


def _merge(x, h, o_nsa, o_pool, o_rwkv, o_gla, wg, bg, wb, wo, tm=512):
    B, S, D = x.shape
    tile = lambda w: pl.BlockSpec((1, tm, w), lambda b, s: (b, s, 0))
    full = lambda a: pl.BlockSpec(a.shape, lambda b, s: (0,) * a.ndim)
    return pl.pallas_call(
        _merge_kernel,
        grid=(B, S // tm),
        in_specs=[tile(D), tile(D), tile(256), tile(256), tile(256), tile(256), full(wg), full(bg), full(wb), full(wo)],
        out_specs=tile(D),
        out_shape=jax.ShapeDtypeStruct((B, S, D), F32),
        compiler_params=_cparams(("parallel", "parallel")),
        name="merge",
    )(x, h, o_nsa, o_pool, o_rwkv, o_gla, wg, bg, wb, wo)


def _memkv_kernel(mem_ref, g_ref, w_ref, o_ref):
    mn = _rms(mem_ref[0], g_ref[...]).astype(BF16)
    o_ref[0] = jnp.dot(mn, w_ref[...], preferred_element_type=F32).astype(BF16)


def _memkv(mem, g, wkv_all):
    B, M, D = mem.shape
    N = wkv_all.shape[1]
    return pl.pallas_call(
        _memkv_kernel,
        grid=(B,),
        in_specs=[pl.BlockSpec((1, M, D), lambda b: (b, 0, 0)), pl.BlockSpec(g.shape, lambda b: (0, 0)),
                  pl.BlockSpec(wkv_all.shape, lambda b: (0, 0))],
        out_specs=pl.BlockSpec((1, M, N), lambda b: (b, 0, 0)),
        out_shape=jax.ShapeDtypeStruct((B, M, N), BF16),
        compiler_params=_cparams(("parallel",)),
        name="memkv",
    )(mem, g, wkv_all)


def _cross_kernel(x_ref, g_ref, wq_ref, k_ref, v_ref, wo_ref, out_ref):
    x = x_ref[0]
    hc = _rms(x, g_ref[...]).astype(BF16)
    q = jnp.dot(hc, wq_ref[...], preferred_element_type=F32) * (HEAD_DIM ** -0.5)
    head = lax.broadcasted_iota(jnp.int32, q.shape, 1) // HEAD_DIM
    tm = q.shape[0]
    s = _dot_nt(_stack_heads(q, HEAD_DIM).astype(BF16), k_ref[0])
    e = jnp.exp(s - jnp.max(s, axis=1, keepdims=True))
    p = e / jnp.sum(e, axis=1, keepdims=True)
    o_all = jnp.dot(p.astype(BF16), v_ref[0], preferred_element_type=F32)
    o = jnp.zeros(q.shape, F32)
    for hh in range(CA_WIDTH // HEAD_DIM):
        o = jnp.where(head == hh, o_all[hh * tm:(hh + 1) * tm], o)
    out_ref[0] = x + jnp.dot(o.astype(BF16), wo_ref[...], preferred_element_type=F32)


def _cross(x, g, wq, memkv, l, wo, tm=512):
    B, S, D = x.shape
    M = memkv.shape[1]
    tile = pl.BlockSpec((1, tm, D), lambda b, s: (b, s, 0))
    full = lambda a: pl.BlockSpec(a.shape, lambda b, s: (0,) * a.ndim)
    kspec = pl.BlockSpec((1, M, CA_WIDTH), lambda b, s: (b, 0, 2 * l))
    vspec = pl.BlockSpec((1, M, CA_WIDTH), lambda b, s: (b, 0, 2 * l + 1))
    return pl.pallas_call(
        _cross_kernel,
        grid=(B, S // tm),
        in_specs=[tile, full(g), full(wq), kspec, vspec, full(wo)],
        out_specs=tile,
        out_shape=jax.ShapeDtypeStruct((B, S, D), F32),
        compiler_params=_cparams(("parallel", "parallel")),
        name="cross_attn",
    )(x, g, wq, memkv, memkv, wo)


def _swiglu_cols(xb, wg, wu, wd, tf):
    acc = jnp.zeros((xb.shape[0], wd.shape[-1]), F32)
    for c in range(wg.shape[-1] // tf):
        cols = slice(c * tf, (c + 1) * tf)
        gate = jnp.dot(xb, wg[:, cols], preferred_element_type=F32)
        up = jnp.dot(xb, wu[:, cols], preferred_element_type=F32)
        act = gate * jax.nn.sigmoid(gate) * up
        acc = acc + jnp.dot(act.astype(BF16), wd[cols, :], preferred_element_type=F32)
    return acc


def _ffn_kernel(x_ref, g_ref, wg_ref, wu_ref, wd_ref, out_ref, *, tf):
    x = x_ref[...]
    out_ref[...] = x + _swiglu_cols(_rms(x, g_ref[...]).astype(BF16), wg_ref, wu_ref, wd_ref, tf)


def _ffn(x2, g, wg, wu, wd, tm=512, tf=256):
    T, D = x2.shape
    full = lambda a: pl.BlockSpec(a.shape, lambda i: (0,) * a.ndim)
    return pl.pallas_call(
        functools.partial(_ffn_kernel, tf=tf),
        grid=(T // tm,),
        in_specs=[pl.BlockSpec((tm, D), lambda i: (i, 0)), full(g), full(wg), full(wu), full(wd)],
        out_specs=pl.BlockSpec((tm, D), lambda i: (i, 0)),
        out_shape=jax.ShapeDtypeStruct((T, D), F32),
        compiler_params=_cparams(("parallel",)),
        name="ffn",
    )(x2, g, wg, wu, wd)


MOE_TM = 512


SUBLANES = 8


def _to_slabs(ref, x):
    w = ref.shape[-1]
    for j in range(ref.shape[-2]):
        ref[:, j, :] = x[:, j * w:(j + 1) * w]


def _from_slabs(ref):
    return jnp.concatenate([ref[:, j, :] for j in range(ref.shape[-2])], axis=1)


def _route_kernel(x_ref, g_ref, r_ref, hf_ref, idx_ref, w_ref):
    hf = _rms(x_ref[...], g_ref[...])
    _to_slabs(hf_ref, hf)
    logits = jnp.dot(hf, r_ref[...], preferred_element_type=F32, precision=lax.Precision.HIGHEST)
    lane = lax.broadcasted_iota(jnp.int32, logits.shape, 1)
    big = logits.shape[1]
    logits = jnp.where(lane < N_EXPERTS, logits, -jnp.inf)
    m1 = jnp.max(logits, axis=1, keepdims=True)
    i1 = jnp.min(jnp.where(logits == m1, lane, big), axis=1, keepdims=True)
    rest = jnp.where(lane == i1, -jnp.inf, logits)
    m2 = jnp.max(rest, axis=1, keepdims=True)
    i2 = jnp.min(jnp.where(rest == m2, lane, big), axis=1, keepdims=True)
    e2 = jnp.exp(m2 - m1)
    w1 = 1.0 / (1.0 + e2)
    idx_ref[...] = jnp.where(lane == 0, i1, jnp.where(lane == 1, i2, 0))
    w_ref[...] = jnp.where(lane == 0, w1, jnp.where(lane == 1, e2 * w1, 0.0))


def _moe_route(x2, g, router_p, tm=1024):
    T, D = x2.shape
    row = lambda w: pl.BlockSpec((tm, w), lambda i: (i, 0))
    full = lambda a: pl.BlockSpec(a.shape, lambda i: (0,) * a.ndim)
    return pl.pallas_call(
        _route_kernel,
        grid=(T // tm,),
        in_specs=[row(D), full(g), full(router_p)],
        out_specs=[pl.BlockSpec((tm, SUBLANES, D // SUBLANES), lambda i: (i, 0, 0)), row(LANES), row(LANES)],
        out_shape=[jax.ShapeDtypeStruct((T, SUBLANES, D // SUBLANES), F32), jax.ShapeDtypeStruct((T, LANES), jnp.int32),
                   jax.ShapeDtypeStruct((T, LANES), F32)],
        compiler_params=_cparams(("parallel",)),
        name="moe_route",
    )(x2, g, router_p)


def _moe_plan(expert_of_pair, tm):
    onehot = (expert_of_pair[:, None] == jnp.arange(N_EXPERTS, dtype=jnp.int32)[None, :]).astype(jnp.int32)
    csum = jnp.cumsum(onehot, axis=0)
    rank = jnp.sum((csum - 1) * onehot, axis=1)
    counts = csum[-1]
    padded = ((counts + tm - 1) // tm) * tm
    ends = jnp.cumsum(padded)
    dest = ((ends - padded)[expert_of_pair] + rank).astype(jnp.int32)
    n_rows = expert_of_pair.shape[0] + N_EXPERTS * tm
    tile_expert = jnp.sum((jnp.arange(n_rows // tm, dtype=jnp.int32)[:, None] * tm >= ends[None, :]).astype(jnp.int32), axis=1)
    n_pairs = expert_of_pair.shape[0]
    pair_of_row = jnp.full((n_rows,), -1, jnp.int32).at[dest].set(
        jnp.arange(n_pairs, dtype=jnp.int32), unique_indices=True, mode='promise_in_bounds')
    real = pair_of_row >= 0
    token_of_row = jnp.where(real, pair_of_row // 2, 0)
    out_row = jnp.where(real, (pair_of_row % 2) * (n_pairs // 2) + pair_of_row // 2,
                        n_pairs + jnp.arange(n_rows, dtype=jnp.int32) % tm)
    return token_of_row, out_row, jnp.minimum(tile_expert, N_EXPERTS - 1).astype(jnp.int32)


DMA_UNROLL = 8


def _gather_rows(table_hbm, row_of, dst, n, sem, wait):
    def body(c, carry):
        for u in range(DMA_UNROLL):
            r = c * DMA_UNROLL + u
            copy = pltpu.make_async_copy(table_hbm.at[pl.ds(row_of(r), 1)], dst.at[pl.ds(r, 1)], sem)
            copy.wait() if wait else copy.start()
        return carry

    lax.fori_loop(0, n // DMA_UNROLL, body, 0)


def _scatter_rows(src, dst_hbm, row_of, n, sem, wait):
    def body(c, carry):
        for u in range(DMA_UNROLL):
            r = c * DMA_UNROLL + u
            copy = pltpu.make_async_copy(src.at[pl.ds(r, 1)], dst_hbm.at[pl.ds(row_of(r), 1)], sem)
            copy.wait() if wait else copy.start()
        return carry

    lax.fori_loop(0, n // DMA_UNROLL, body, 0)


def _experts_kernel(te_ref, tok_cur_ref, tok_next_ref, pair_prev_ref, pair_wait_ref, hf_hbm, wg_ref, wu_ref, wd_ref,
                    o2_hbm, xin, yout, sems, sem_out, *, tf):
    del te_ref
    i = pl.program_id(0)
    slot = i % 2
    tm = xin.shape[1]
    n_f = wg_ref.shape[2] // tf
    cur_tok = lambda r: tok_cur_ref[0, 0, r]
    next_tok = lambda r: tok_next_ref[0, 0, r]
    prev_pair = lambda r: pair_prev_ref[0, 0, r]

    @pl.when(i == 0)
    def _():
        _gather_rows(hf_hbm, cur_tok, xin.at[0], tm, sems.at[0], wait=False)
        yout[1] = jnp.zeros(yout.shape[1:], F32)

    @pl.when(i >= 1)
    def _():
        _scatter_rows(yout.at[slot], o2_hbm, lambda r: pair_wait_ref[0, 0, r], tm, sem_out, wait=True)

    _gather_rows(hf_hbm, cur_tok, xin.at[slot], tm, sems.at[slot], wait=True)
    xb = _from_slabs(xin.at[slot]).astype(BF16)
    acc = jnp.zeros((tm, wd_ref.shape[2]), F32)
    n_issue = max(n_f - 2, 1)
    bounds = [min(round(c * tm / n_issue), tm) for c in range(n_f + 1)]
    for c in range(n_f):
        for r in range(bounds[c], bounds[c + 1]):
            pltpu.make_async_copy(hf_hbm.at[pl.ds(next_tok(r), 1)], xin.at[1 - slot, pl.ds(r, 1)],
                                  sems.at[1 - slot]).start()
            pltpu.make_async_copy(yout.at[1 - slot, pl.ds(r, 1)], o2_hbm.at[pl.ds(prev_pair(r), 1)], sem_out).start()
        cols = slice(c * tf, (c + 1) * tf)
        gate = jnp.dot(xb, wg_ref[0, :, cols], preferred_element_type=F32)
        up = jnp.dot(xb, wu_ref[0, :, cols], preferred_element_type=F32)
        act = gate * jax.nn.sigmoid(gate) * up
        acc = acc + jnp.dot(act.astype(BF16), wd_ref[0, cols, :], preferred_element_type=F32)
    _to_slabs(yout.at[slot], acc)

    @pl.when(i == pl.num_programs(0) - 1)
    def _():
        _gather_rows(hf_hbm, next_tok, xin.at[1 - slot], tm, sems.at[1 - slot], wait=True)
        _scatter_rows(yout.at[1 - slot], o2_hbm, prev_pair, tm, sem_out, wait=True)


def _moe_experts(hf, token_of_row, out_row_of_row, tile_expert, wg, wu, wd, tf=256):
    T = hf.shape[0]
    slab = hf.shape[1:]
    tm = MOE_TM
    n_tiles = token_of_row.shape[0] // tm
    tok = token_of_row.reshape(n_tiles, 1, tm)
    tok = jnp.concatenate([tok, tok[-1:], tok[-1:]], axis=0)
    spare = (2 * T + jnp.arange(tm, dtype=jnp.int32)).reshape(1, 1, tm)
    pair = jnp.concatenate([spare, out_row_of_row.reshape(n_tiles, 1, tm)], axis=0)
    te = jnp.concatenate([tile_expert, tile_expert[-1:]])
    smem = lambda index: pl.BlockSpec((1, 1, tm), lambda i, te: (index(i), 0, 0), memory_space=pltpu.SMEM)
    expert = lambda shape: pl.BlockSpec((1,) + shape, lambda i, te: (te[i], 0, 0))
    any_spec = pl.BlockSpec(memory_space=pl.ANY)
    grid_spec = pltpu.PrefetchScalarGridSpec(
        num_scalar_prefetch=1,
        grid=(n_tiles + 1,),
        in_specs=[smem(lambda i: i), smem(lambda i: i + 1), smem(lambda i: i), smem(lambda i: jnp.maximum(i - 1, 0)),
                  any_spec, expert(wg.shape[1:]), expert(wu.shape[1:]), expert(wd.shape[1:])],
        out_specs=any_spec,
        scratch_shapes=[pltpu.VMEM((2, tm) + slab, F32), pltpu.VMEM((2, tm) + slab, F32), pltpu.SemaphoreType.DMA((2,)),
                        pltpu.SemaphoreType.DMA(())],
    )
    return pl.pallas_call(
        functools.partial(_experts_kernel, tf=tf),
        grid_spec=grid_spec,
        out_shape=jax.ShapeDtypeStruct((2 * T + tm,) + slab, F32),
        compiler_params=_cparams(("arbitrary",)),
        name="moe_experts",
    )(te, tok, tok, pair, pair, hf, wg, wu, wd)


def _combine_kernel(x_ref, w_ref, gf_ref, y0_ref, y1_ref, out_ref):
    w = w_ref[...]
    y = x_ref[...] + w[:, 0:1] * _from_slabs(y0_ref) + w[:, 1:2] * _from_slabs(y1_ref)
    out_ref[...] = _rms(y, gf_ref[...])


def _moe_combine(x2, w, o2, g_final, tm=512):
    T, D = x2.shape
    row = lambda wd, off: pl.BlockSpec((tm, wd), lambda i: (i + off, 0))
    slab = lambda off: pl.BlockSpec((tm,) + o2.shape[1:], lambda i: (i + off, 0, 0))
    return pl.pallas_call(
        _combine_kernel,
        grid=(T // tm,),
        in_specs=[row(D, 0), row(LANES, 0), pl.BlockSpec(g_final.shape, lambda i: (0, 0)), slab(0), slab(T // tm)],
        out_specs=row(D, 0),
        out_shape=jax.ShapeDtypeStruct((T, D), F32),
        compiler_params=_cparams(("parallel",)),
        name="moe_combine",
    )(x2, w, g_final, o2, o2)


def _moe_routed(x2, g, router_p, wg, wu, wd, g_final):
    T, D = x2.shape
    hf, idx, w = _moe_route(x2, g, router_p)
    token_of_row, out_row_of_row, tile_expert = _moe_plan(idx[:, 0:2].reshape(2 * T), MOE_TM)
    o2 = _moe_experts(hf, token_of_row, out_row_of_row, tile_expert, wg, wu, wd)
    return _moe_combine(x2, w, o2, g_final)


def kernel(x, mem, positions, norm_mix, w_in, nsa_cmp_pe, nsa_cmp_w1, nsa_cmp_w2, pool_w, pool_scale, rwkv_mu, rwkv_w0,
           rwkv_w2, rwkv_a0, rwkv_a2, rwkv_g2, rwkv_v0, rwkv_v1, rwkv_v2, rwkv_k_k, rwkv_k_a, rwkv_r_k, rwkv_ln, gla_a2,
           gla_ab, gla_norm, w_gate, b_gate, w_branch, w_out, norm_ca, mem_norm, w_ca_q, w_ca_kv, w_ca_o, norm_ffn,
           ffn_w_gate, ffn_w_up, ffn_w_down, moe_router, moe_w_gate, moe_w_up, moe_w_down, norm_final):
    B, S, D = x.shape
    depth = norm_mix.shape[0]
    assert depth == 2 and S % 256 == 0, "the final norm is fused into the layer-1 expert mixer"
    cs = _rope_angles(positions)
    cs_end = jnp.pad(cs[:, CMP_LEN - 1::CMP_STRIDE], ((0, 0), (0, 1), (0, 0)))
    memkv = _memkv(mem, mem_norm[None], jnp.concatenate([w_ca_kv[l] for l in range(depth)], axis=1).astype(BF16))
    v_first = None
    for l in range(depth):
        h, q_t, k2, v_t, cv, gates_t, u, rw, gl = _inproj(x, norm_mix[l][None], _inproj_weights(w_in[l]), cs,
                                                          rwkv_mu[l][None])
        kcmp, vcmp_t = _nsa_compress(cv, *_compress_weights(nsa_cmp_pe[l], nsa_cmp_w1[l], nsa_cmp_w2[l]), cs_end)
        o_nsa = _nsa_attend(q_t, k2, v_t, kcmp, vcmp_t, gates_t)
        o_pool = _pool(u, _block_diag(pool_w[l]).astype(BF16), pool_scale[l][None])
        wz, vecs = _rwkv_weights(l, rwkv_w0, rwkv_w2, rwkv_a0, rwkv_a2, rwkv_g2, rwkv_v0, rwkv_k_k, rwkv_k_a,
                                 rwkv_r_k, rwkv_ln)
        if l == 0:
            o_rwkv, v_first = _rwkv(rw, wz, vecs)
        else:
            v1p = jnp.pad(rwkv_v1[l - 1], ((0, 0), (0, LANES - rwkv_v1.shape[2]))).astype(BF16)
            v2p = jnp.pad(rwkv_v2[l - 1], ((0, LANES - rwkv_v2.shape[1]), (0, 0))).astype(BF16)
            o_rwkv = _rwkv(rw, wz, vecs, h, v_first, v1p, v2p)
        a2p = jnp.pad(gla_a2[l], ((0, LANES - gla_a2.shape[1]), (0, 0))).astype(BF16)
        o_gla = _gla(gl, a2p, gla_ab[l][None], jnp.tile(gla_norm[l], 4)[None])
        wb_nsa = w_branch[l, 0].reshape(4, HEAD_DIM, D)[jnp.array([0, 2, 1, 3])].reshape(BRANCH_WIDTH, D)
        wb = jnp.concatenate([wb_nsa[None], w_branch[l, 1:]], axis=0).astype(BF16)
        x = _merge(x, h, o_nsa, o_pool, o_rwkv, o_gla, w_gate[l].astype(BF16), b_gate[l], wb, w_out[l].astype(BF16))
        x = _cross(x, norm_ca[l][None], w_ca_q[l].astype(BF16), memkv, l, w_ca_o[l].astype(BF16))
        j = l // 2
        x2 = x.reshape(B * S, D)
        if l % 2 == 0:
            x2 = _ffn(x2, norm_ffn[l][None], ffn_w_gate[j].astype(BF16), ffn_w_up[j].astype(BF16),
                      ffn_w_down[j].astype(BF16))
        else:
            router_p = jnp.pad(moe_router[j], ((0, 0), (0, LANES - N_EXPERTS)))
            x2 = _moe_routed(x2, norm_ffn[l][None], router_p, moe_w_gate[j].astype(BF16), moe_w_up[j].astype(BF16),
                             moe_w_down[j].astype(BF16), norm_final[None])
        x = x2.reshape(B, S, D)
    return x
```

```python
import functools

import jax
import jax.numpy as jnp
import numpy as np
from jax import lax
from jax.experimental import pallas as pl
from jax.experimental.pallas import tpu as pltpu

F32 = jnp.float32
BF16 = jnp.bfloat16

D_MODEL = 1024
HEAD_DIM = 64
BRANCH_WIDTH = 256
ROPE_THETA = 500000.0
ROPE_DIM = 16
NORM_EPS = 1e-6
NEG_INF = -1e30

CMP_LEN = 32
CMP_STRIDE = 16
CMP_HIDDEN = 128
SEL_BLOCK = 64
SEL_TOPN = 8
WINDOW = 512

RWKV_IN = 896
RWKV_GN_EPS = 64e-5
RWKV_CHUNK = 64

GLA_DK = 32
GLA_TAU = 16.0
GLA_CHUNK = 64

CA_WIDTH = 256
N_EXPERTS = 8

LANES = 128
VMEM_LIMIT = 56 * 1024 * 1024

_C_Q = 0
_C_KV = 256
_C_CV = 768
_C_GATE = 1024
_C_POOL = 1152
_C_RWKV = 1408
_C_GLA = 2304
_C_END = 3200


def _cparams(sem):
    return pltpu.CompilerParams(dimension_semantics=sem, vmem_limit_bytes=VMEM_LIMIT)


def _rms(x, g):
    ms = jnp.mean(x * x, axis=-1, keepdims=True)
    return x * lax.rsqrt(ms + NORM_EPS) * g


def _rope128(t, cos, sin):
    lane = lax.broadcasted_iota(jnp.int32, t.shape, 1) % HEAD_DIM
    partner = jnp.where(lane < ROPE_DIM // 2, pltpu.roll(t, LANES - ROPE_DIM // 2, 1),
                        pltpu.roll(t, ROPE_DIM // 2, 1))
    return t * cos + partner * sin


def _inproj_kernel(x_ref, g_ref, w_ref, cs_ref, e_ref, ones_ref, mu_ref,
                   h_ref, qt_ref, k_ref, vt_ref, cv_ref, gt_ref, u_ref, rw_ref, gl_ref, prev_ref):
    s = pl.program_id(1)
    hb = _rms(x_ref[0], g_ref[...]).astype(BF16)
    h_ref[0] = hb

    proj = jnp.dot(hb, w_ref[...], preferred_element_type=F32)

    def mm(a, b):
        return proj[:, a:b]

    cos, sin = _rope_expand(cs_ref[0], e_ref[...], ones_ref[...])
    scale = HEAD_DIM ** -0.5
    qt_ref[0, 0:128, :] = jnp.transpose(_rope128(mm(_C_Q, _C_Q + 128), cos, sin) * scale).astype(BF16)
    qt_ref[0, 128:256, :] = jnp.transpose(_rope128(mm(_C_Q + 128, _C_Q + 256), cos, sin) * scale).astype(BF16)
    k_ref[0, :, 0:128] = _rope128(mm(_C_KV, _C_KV + 128), cos, sin).astype(BF16)
    vt_ref[0, 0:128, :] = jnp.transpose(mm(_C_KV + 128, _C_KV + 256)).astype(BF16)
    k_ref[0, :, 128:256] = _rope128(mm(_C_KV + 256, _C_KV + 384), cos, sin).astype(BF16)
    vt_ref[0, 128:256, :] = jnp.transpose(mm(_C_KV + 384, _C_KV + 512)).astype(BF16)
    cv_ref[0] = mm(_C_CV, _C_CV + 256).astype(BF16)
    gt_ref[0] = jnp.transpose(jax.nn.sigmoid(mm(_C_GATE, _C_GATE + 128)))
    u_ref[0] = mm(_C_POOL, _C_POOL + 256)
    gl_ref[0] = mm(_C_GLA, _C_END)

    p = mm(_C_RWKV, _C_RWKV + RWKV_IN)
    tm = p.shape[0]

    @pl.when(s == 0)
    def _():
        prev_ref[...] = jnp.zeros_like(prev_ref)

    row = lax.broadcasted_iota(jnp.int32, p.shape, 0)
    shifted = jnp.where(row == 0, prev_ref[0:1, :], pltpu.roll(p, 1, 0))
    prev_ref[0:1, :] = p[tm - 1:tm, :]
    rw_ref[0] = p + (shifted - p) * mu_ref[...]


def _inproj(x, g, w_all, cs, mu, tm=512):
    B, S, D = x.shape
    e, ones = _rope_expander()
    tok = lambda w: pl.BlockSpec((1, tm, w), lambda b, s: (b, s, 0))
    tok_t = lambda w: pl.BlockSpec((1, w, tm), lambda b, s: (b, 0, s))
    full = lambda a: pl.BlockSpec(a.shape, lambda b, s: (0,) * a.ndim)
    outs = [(D, BF16, True), (256, BF16, False), (256, BF16, True), (256, BF16, False), (256, BF16, True),
            (128, F32, False), (256, F32, True), (RWKV_IN, F32, True), (896, F32, True)]
    return pl.pallas_call(
        _inproj_kernel,
        grid=(B, S // tm),
        in_specs=[tok(D), full(g), full(w_all), tok(ROPE_DIM), full(e), full(ones), full(mu)],
        out_specs=[tok(w) if major else tok_t(w) for w, _, major in outs],
        out_shape=[jax.ShapeDtypeStruct((B, S, w) if major else (B, w, S), dt) for w, dt, major in outs],
        scratch_shapes=[pltpu.VMEM((8, RWKV_IN), F32)],
        compiler_params=_cparams(("parallel", "arbitrary")),
        name="inproj",
    )(x, g, w_all, cs, e, ones, mu)


def _inproj_columns():
    span = lambda lo, n: list(range(lo, lo + n))
    pad = lambda n: [-1] * n
    nsa, pool, rwkv, gla = 0, 1036, 1292, 2188
    cols = (span(nsa, 64) + span(nsa + 128, 64) + span(nsa + 64, 64) + span(nsa + 192, 64)
            + span(nsa + 512, 512)
            + span(nsa + 256, 256)
            + span(nsa + 1024, 12) + pad(116)
            + span(pool, 256) + span(rwkv, RWKV_IN)
            + span(gla, 512) + span(gla + 528, 256) + span(gla + 512, 16) + pad(112))
    assert len(cols) == _C_END
    return np.asarray(cols, np.int32)


def _inproj_weights(w):
    onehot = (jnp.arange(w.shape[1], dtype=jnp.int32)[:, None] == jnp.asarray(_inproj_columns())[None, :]).astype(BF16)
    return jnp.dot(w.astype(BF16), onehot, preferred_element_type=F32).astype(BF16)


def _rope_angles(positions):
    inv_freq = ROPE_THETA ** (-jnp.arange(0, ROPE_DIM, 2, dtype=F32) / ROPE_DIM)
    ang = positions.astype(F32)[..., None] * inv_freq
    return jnp.concatenate([jnp.cos(ang), jnp.sin(ang)], axis=-1)


def _rope_expander():
    half = ROPE_DIM // 2
    e = np.zeros((ROPE_DIM, 2 * LANES), np.float32)
    ones = np.zeros((1, LANES), np.float32)
    for lane in range(LANES):
        d = lane % HEAD_DIM
        if d < ROPE_DIM:
            e[d % half, lane] = 1.0
            e[half + d % half, LANES + lane] = -1.0 if d < half else 1.0
        else:
            ones[0, lane] = 1.0
    return jnp.asarray(e, BF16), jnp.asarray(ones)


def _rope_expand(cs, e, ones):
    tab = _dot_split_lhs(cs, e, 3)
    return tab[:, 0:LANES] + ones, tab[:, LANES:2 * LANES]


def _gelu_tanh(x):
    return x * (0.5 * (1.0 + jnp.tanh(np.sqrt(2.0 / np.pi) * (x + 0.044715 * (x * x * x)))))


def _dot_nt(a, b):
    return lax.dot_general(a, b, (((1,), (1,)), ((), ())), preferred_element_type=F32)


def _compress_kernel(x_ref, wa_ref, wb_ref, bias_ref, w2k_ref, w2v_ref, cs_ref, e_ref, ones_ref, kc_ref, vc_ref):
    x = x_ref[0]
    a = jnp.dot(x, wa_ref[...], preferred_element_type=F32)
    b = jnp.dot(x, wb_ref[...], preferred_element_type=F32)
    n = a.shape[0]
    hid = _gelu_tanh(a + pltpu.roll(b, n - 1, 0) + bias_ref[...])
    kc = jnp.dot(hid[:, 0:256].astype(BF16), w2k_ref[...], preferred_element_type=F32)
    vc = jnp.dot(hid[:, 256:512].astype(BF16), w2v_ref[...], preferred_element_type=F32)
    cos, sin = _rope_expand(cs_ref[0], e_ref[...], ones_ref[...])
    kc_ref[0] = _rope128(kc, cos, sin).astype(BF16)
    vc_ref[0] = jnp.transpose(vc).astype(BF16)


def _nsa_compress(cv, wa, wb, bias, w2k, w2v, cs_end):
    B, S, _ = cv.shape
    nseg = S // CMP_STRIDE
    x = cv.reshape(B, nseg, CMP_STRIDE * 256)
    e, ones = _rope_expander()
    per_b = lambda r, w: pl.BlockSpec((1, r, w), lambda b: (b, 0, 0))
    full = lambda a: pl.BlockSpec(a.shape, lambda b: (0,) * a.ndim)
    return pl.pallas_call(
        _compress_kernel,
        grid=(B,),
        in_specs=[per_b(nseg, CMP_STRIDE * 256), full(wa), full(wb), full(bias), full(w2k), full(w2v),
                  per_b(nseg, ROPE_DIM), full(e), full(ones)],
        out_specs=[per_b(nseg, 128), per_b(nseg, 128)],
        out_shape=[jax.ShapeDtypeStruct((B, nseg, 128), BF16)] * 2,
        compiler_params=_cparams(("parallel",)),
        name="nsa_compress",
    )(x, wa, wb, bias, w2k, w2v, cs_end, e, ones)


def _compress_weights(pe, w1, w2):
    eye2 = jnp.eye(2, dtype=F32)
    half = lambda lo: jnp.einsum('jldf,pj,qg->lpqdjgf', w1[:, lo:lo + CMP_STRIDE], eye2, eye2).reshape(
        CMP_STRIDE * 256, 4 * CMP_HIDDEN).astype(BF16)
    bias = jnp.einsum('jld,jldf->jf', pe, w1)
    bias = jnp.broadcast_to(bias[:, None, :], (2, 2, CMP_HIDDEN)).reshape(1, 4 * CMP_HIDDEN)
    bd = lambda w: jnp.einsum('fd,gq->gfqd', w, eye2).reshape(2 * CMP_HIDDEN, 2 * HEAD_DIM).astype(BF16)
    return half(0), half(CMP_STRIDE), bias, bd(w2[0]), bd(w2[1])


def _cover_t(S):
    n_cmp = (S - CMP_LEN) // CMP_STRIDE + 1
    cmp_start = np.arange(n_cmp) * CMP_STRIDE
    slc_start = np.arange(S // SEL_BLOCK) * SEL_BLOCK
    cover = np.clip(np.minimum(cmp_start[:, None] + CMP_LEN, slc_start[None, :] + SEL_BLOCK)
                    - np.maximum(cmp_start[:, None], slc_start[None, :]), 0, None) / CMP_LEN
    out = np.zeros((S // SEL_BLOCK, S // CMP_STRIDE), np.float32)
    out[:, :n_cmp] = cover.T
    return jnp.asarray(out, BF16)


MASKED = 2.0 * NEG_INF


def _softmax_step(s, carry, v_t):
    m, l, acc = carry
    m_new = jnp.maximum(m, jnp.max(s, axis=0, keepdims=True))
    alpha = jnp.exp(m - m_new)
    p = jnp.exp(s - m_new)
    l = alpha * l + jnp.sum(p, axis=0, keepdims=True)
    acc = alpha * acc + jnp.dot(v_t, p.astype(BF16), preferred_element_type=F32)
    return m_new, l, acc


def _softmax_init(n_q):
    return (jnp.full((1, n_q), NEG_INF, F32), jnp.zeros((1, n_q), F32), jnp.zeros((LANES, n_q), F32))


def _nsa_kernel(qt_ref, k_ref, vt_ref, kc_ref, vct_ref, gt_ref, covt_ref, o_ref, *, tq, tk):
    i = pl.program_id(1)
    t0 = i * tq
    n_blk = covt_ref.shape[0]
    n_q = 4 * tq
    qa_t, qb_t = qt_ref[0, 0:128, :], qt_ref[0, 128:256, :]
    sub_grp = lax.broadcasted_iota(jnp.int32, (LANES, tq), 0) // HEAD_DIM
    gates_t = gt_ref[0]
    tok = t0 + (lax.broadcasted_iota(jnp.int32, (1, n_q), 1) & (tq - 1))
    q_all = jnp.concatenate([jnp.where(sub_grp == g, q_t, 0) for g in range(2) for q_t in (qa_t, qb_t)], axis=1)

    s = jnp.dot(kc_ref[0], q_all, preferred_element_type=F32)
    nrow = lax.broadcasted_iota(jnp.int32, s.shape, 0)
    s = jnp.where(nrow * CMP_STRIDE + (CMP_LEN - 1) <= tok, s, MASKED)
    e = jnp.exp(s - jnp.maximum(jnp.max(s, axis=0, keepdims=True), NEG_INF))
    den = jnp.sum(e, axis=0, keepdims=True)
    p = e / jnp.where(den > 0.0, den, 1.0)
    o_cmp = jnp.dot(vct_ref[0], p.astype(BF16), preferred_element_type=F32)

    p_grp = jnp.concatenate([p[:, 2 * g * tq:(2 * g + 1) * tq] + p[:, (2 * g + 1) * tq:(2 * g + 2) * tq]
                             for g in range(2)], axis=1)
    imp = _dot_split_rhs(covt_ref[...], p_grp, 2)
    blk = lax.broadcasted_iota(jnp.int32, imp.shape, 0)
    tok_b = t0 + (lax.broadcasted_iota(jnp.int32, imp.shape, 1) & (tq - 1))
    cur = tok_b // SEL_BLOCK
    forced = (blk == 0) | (blk == cur) | (blk == cur - 1)
    imp = jnp.where(forced, 1e30, jnp.where(blk * SEL_BLOCK <= tok_b, imp, -1e30))
    cnt = jnp.zeros(imp.shape, F32)
    for i2 in range(n_blk):
        row = imp[i2:i2 + 1, :]
        cnt = cnt + jnp.where((row > imp) | ((row == imp) & (blk > i2)), 1.0, 0.0)
    bias = jnp.where(cnt < float(SEL_TOPN), 0.0, MASKED).astype(BF16)
    sel_bias = jnp.concatenate([bias[:, g * tq:(g + 1) * tq] for g in (0, 0, 1, 1)], axis=1)

    q_ext = jnp.concatenate([q_all, sel_bias, jnp.zeros((LANES - n_blk, n_q), BF16)], axis=0)

    def sel_scores(j):
        rows = pl.ds(pl.multiple_of(j * tk, tk), tk)
        blk_of_key = (j * tk + lax.broadcasted_iota(jnp.int32, (tk, LANES), 0)) // SEL_BLOCK
        onehot = (lax.broadcasted_iota(jnp.int32, (tk, LANES), 1) == blk_of_key).astype(BF16)
        k_ext = jnp.concatenate([k_ref[0, rows, 0:128], onehot], axis=1)
        return jnp.dot(k_ext, q_ext, preferred_element_type=F32), vt_ref[0, 0:128, rows]

    def sel_step(j, carry):
        s, v_t = sel_scores(j)
        return _softmax_step(s, carry, v_t)

    last = (t0 + tq - 1) // tk
    carry = lax.fori_loop(0, last, sel_step, _softmax_init(n_q))
    s, v_t = sel_scores(last)
    key = last * tk + lax.broadcasted_iota(jnp.int32, (tk, n_q), 0)
    _, l_sel, acc_sel = _softmax_step(jnp.where(key <= tok, s, MASKED), carry, v_t)
    o_sel = acc_sel / l_sel

    span = WINDOW + tq
    start = pl.multiple_of(jnp.maximum(t0 - WINDOW, 0), tq)
    key = start + lax.broadcasted_iota(jnp.int32, (span, n_q), 0)
    s = jnp.dot(k_ref[0, pl.ds(start, span), 128:256], q_all, preferred_element_type=F32)
    in_window = lax.bitcast_convert_type(tok - key, jnp.uint32) < jnp.uint32(WINDOW)
    s = jnp.where(in_window, s, MASKED)
    e = jnp.exp(s - jnp.max(s, axis=0, keepdims=True))
    o_win = (jnp.dot(vt_ref[0, 128:256, pl.ds(start, span)], e.astype(BF16), preferred_element_type=F32)
             / jnp.sum(e, axis=0, keepdims=True))

    out = [jnp.zeros((LANES, tq), F32), jnp.zeros((LANES, tq), F32)]
    for g in range(2):
        for r in range(2):
            cols = slice((2 * g + r) * tq, (2 * g + r + 1) * tq)
            c = g * 6 + r * 3
            o = (gates_t[c:c + 1, :] * o_cmp[:, cols] + gates_t[c + 1:c + 2, :] * o_sel[:, cols]
                 + gates_t[c + 2:c + 3, :] * o_win[:, cols])
            out[r] = jnp.where(sub_grp == g, o, out[r])
    o_ref[0, :, 0:128] = jnp.transpose(out[0]).astype(BF16)
    o_ref[0, :, 128:256] = jnp.transpose(out[1]).astype(BF16)


def _nsa_attend(q_t, k2, v_t, kcmp, vcmp_t, gates_t, tq=256, tk=512):
    B, S, _ = k2.shape
    assert tk % tq == 0 and S % tk == 0 and WINDOW % tq == 0
    covt = _cover_t(S)
    tile_t = lambda w: pl.BlockSpec((1, w, tq), lambda b, i: (b, 0, i))
    per_b = lambda r, w: pl.BlockSpec((1, r, w), lambda b, i: (b, 0, 0))
    return pl.pallas_call(
        functools.partial(_nsa_kernel, tq=tq, tk=tk),
        grid=(B, S // tq),
        in_specs=[tile_t(256), per_b(S, 256), per_b(256, S), per_b(S // CMP_STRIDE, 128), per_b(128, S // CMP_STRIDE),
                  tile_t(128), pl.BlockSpec(covt.shape, lambda b, i: (0, 0))],
        out_specs=pl.BlockSpec((1, tq, 256), lambda b, i: (b, i, 0)),
        out_shape=jax.ShapeDtypeStruct((B, S, 256), BF16),
        compiler_params=_cparams(("parallel", "arbitrary")),
        name="nsa_attend",
    )(q_t, k2, v_t, kcmp, vcmp_t, gates_t, covt)


def _pool_kernel(u_ref, w_ref, scale_ref, o_ref):
    u = u_ref[0]
    row = lax.broadcasted_iota(jnp.int32, u.shape, 0)
    grp = lax.broadcasted_iota(jnp.int32, u.shape, 1) // HEAD_DIM

    def back(x, k):
        return jnp.where(row >= k, pltpu.roll(x, k, 0), 0.0)

    s2 = u + back(u, 1)
    s4 = s2 + back(s2, 2)
    s8 = s4 + back(s4, 4)
    s16 = s8 + back(s8, 8)
    total = jnp.where(grp == 0, s2, jnp.where(grp == 1, s4, jnp.where(grp == 2, s8, s16)))
    win = jnp.where(grp == 0, 2, jnp.where(grp == 1, 4, jnp.where(grp == 2, 8, 16)))
    count = jnp.minimum(row + 1, win).astype(F32)
    d = total / count - u
    y = jnp.dot(d.astype(BF16), w_ref[...], preferred_element_type=F32)
    o_ref[0] = (y * scale_ref[...]).astype(BF16)


def _pool(u, w_bd, scale):
    B, S, W = u.shape
    per_b = pl.BlockSpec((1, S, W), lambda b: (b, 0, 0))
    full = lambda a: pl.BlockSpec(a.shape, lambda b: (0,) * a.ndim)
    return pl.pallas_call(
        _pool_kernel,
        grid=(B,),
        in_specs=[per_b, full(w_bd), full(scale)],
        out_specs=per_b,
        out_shape=jax.ShapeDtypeStruct((B, S, W), BF16),
        compiler_params=_cparams(("parallel",)),
        name="pool",
    )(u, w_bd, scale)


def _block_diag(w):
    G, a, b = w.shape
    return jnp.einsum('gab,gh->gahb', w, jnp.eye(G, dtype=w.dtype)).reshape(G * a, G * b)


def _dot_split_lhs(a, b_exact, terms):
    out, rem = None, a
    for i in range(terms):
        hi = rem.astype(BF16)
        d = jnp.dot(hi, b_exact, preferred_element_type=F32)
        out = d if out is None else out + d
        if i + 1 < terms:
            rem = rem - hi.astype(F32)
    return out


def _dot_split_rhs(a_exact, b, terms):
    out, rem = None, b
    for i in range(terms):
        hi = rem.astype(BF16)
        d = jnp.dot(a_exact, hi, preferred_element_type=F32)
        out = d if out is None else out + d
        if i + 1 < terms:
            rem = rem - hi.astype(F32)
    return out


def _dot_tn(a, b):
    return lax.dot_general(a, b, (((0,), (0,)), ((), ())), preferred_element_type=F32)


def _stack_heads(x, width, n_heads=4):
    head = lax.broadcasted_iota(jnp.int32, x.shape, 1) // width
    return jnp.concatenate([jnp.where(head == h, x, 0.0) for h in range(n_heads)], axis=0)


def _unstack_heads(x, C, n_heads=4):
    out = x[0:C]
    for h in range(1, n_heads):
        out = out + x[h * C:(h + 1) * C]
    return out


def _block_masks(C, n_heads=4):
    n = C * n_heads
    row = lax.broadcasted_iota(jnp.int32, (n, n), 0)
    col = lax.broadcasted_iota(jnp.int32, (n, n), 1)
    same = (row // C) == (col // C)
    return same & (row > col), same & (row >= col)


def _tri_ones(C):
    return jnp.asarray(np.tril(np.ones((C, C), np.float32)), BF16)


def _head_ones(width, n_heads=4):
    return jnp.asarray(np.kron(np.eye(n_heads, dtype=np.float32), np.ones((width, width), np.float32)), BF16)


def _gla_kernel(gl_ref, a2_ref, ab_ref, norm_ref, tri_ref, ones_ref, o_ref, state_ref, *, C):
    s = pl.program_id(1)

    @pl.when(s == 0)
    def _():
        state_ref[...] = jnp.zeros_like(state_ref)

    gl = gl_ref[0]
    tm = gl.shape[0]
    q = gl[:, 0:128] * (GLA_DK ** -0.5)
    k = gl[:, 128:256]
    v = gl[:, 256:512]
    og = gl[:, 512:768]
    ad = gl[:, 768:896]
    x = jnp.dot(ad.astype(BF16), a2_ref[...], preferred_element_type=F32) + ab_ref[...]
    log_a = jax.nn.log_sigmoid(x) * (1.0 / GLA_TAU)
    strict, incl = _block_masks(C)
    kv_mask = (lax.broadcasted_iota(jnp.int32, (256, 128), 0) // HEAD_DIM
               == lax.broadcasted_iota(jnp.int32, (256, 128), 1) // GLA_DK)
    rcs = [slice(c * C, (c + 1) * C) for c in range(tm // C)]
    bcum = [_dot_split_rhs(tri_ref[...], log_a[rc], 3) for rc in rcs]
    mid = [b[C // 2 - 1:C // 2, :] for b in bcum]
    last = [b[C - 1:C, :] for b in bcum]
    q_m = [_stack_heads(q[rc] * jnp.exp(b - m), GLA_DK).astype(BF16) for rc, b, m in zip(rcs, bcum, mid)]
    k_m = [_stack_heads(k[rc] * jnp.exp(m - b), GLA_DK).astype(BF16) for rc, b, m in zip(rcs, bcum, mid)]
    att = [jnp.where(incl, _dot_nt(qm, km), 0.0).astype(BF16) for qm, km in zip(q_m, k_m)]
    v_st = [_stack_heads(v[rc], HEAD_DIM).astype(BF16) for rc in rcs]
    o_intra = [_unstack_heads(jnp.dot(a, vs, preferred_element_type=F32), C) for a, vs in zip(att, v_st)]
    k_end = [(k[rc] * jnp.exp(l - b)).astype(BF16) for rc, b, l in zip(rcs, bcum, last)]
    upd = [jnp.where(kv_mask, _dot_tn(v[rc].astype(BF16), ke), 0.0) for rc, ke in zip(rcs, k_end)]
    q_in = [(q[rc] * jnp.exp(b)).astype(BF16) for rc, b in zip(rcs, bcum)]
    decay = [jnp.exp(l) for l in last]
    outs = []
    state = state_ref[...]
    for c in range(len(rcs)):
        outs.append(o_intra[c] + _dot_nt(q_in[c], state.astype(BF16)))
        state = state * decay[c] + upd[c]
    state_ref[...] = state
    o = jnp.concatenate(outs, axis=0)
    ms = _dot_split_lhs(o * o, ones_ref[...], 2) * (1.0 / HEAD_DIM)
    o = o * lax.rsqrt(ms + NORM_EPS) * norm_ref[...]
    o_ref[0] = (o * (og * jax.nn.sigmoid(og))).astype(BF16)


def _gla(gl, a2p, ab, norm4, tm=512):
    B, S, W = gl.shape
    C = GLA_CHUNK
    tri, ones = _tri_ones(C), _head_ones(HEAD_DIM)
    tile = lambda w: pl.BlockSpec((1, tm, w), lambda b, s: (b, s, 0))
    full = lambda a: pl.BlockSpec(a.shape, lambda b, s: (0,) * a.ndim)
    return pl.pallas_call(
        functools.partial(_gla_kernel, C=C),
        grid=(B, S // tm),
        in_specs=[tile(W), full(a2p), full(ab), full(norm4), full(tri), full(ones)],
        out_specs=tile(256),
        out_shape=jax.ShapeDtypeStruct((B, S, 256), BF16),
        scratch_shapes=[pltpu.VMEM((256, 128), F32)],
        compiler_params=_cparams(("parallel", "arbitrary")),
        name="gla",
    )(gl, a2p, ab, norm4, tri, ones)


_RV_W0, _RV_A0, _RV_KK, _RV_KA, _RV_RK, _RV_LN, _RV_V0 = range(7)


def _rwkv_kernel(*refs, has_vres, C):
    if has_vres:
        (rw_ref, h_ref, vf_ref, wz_ref, vec_ref, v1_ref, v2_ref, tri_ref, ones_ref, o_ref, state_ref) = refs
    else:
        (rw_ref, wz_ref, vec_ref, tri_ref, ones_ref, o_ref, vout_ref, state_ref) = refs
    s = pl.program_id(1)

    @pl.when(s == 0)
    def _():
        state_ref[...] = jnp.zeros_like(state_ref)

    rw = rw_ref[0]
    tm = rw.shape[0]
    r, k, v, z = rw[:, 0:256], rw[:, 256:512], rw[:, 512:768], rw[:, 768:896]
    vec = lambda i: vec_ref[i:i + 1, :]
    zl = lax.broadcasted_iota(jnp.int32, z.shape, 1)
    zf = jnp.where(zl < 32, jnp.tanh(z), jnp.where(zl < 64, z, jax.nn.sigmoid(z)))
    zz = jnp.dot(zf.astype(BF16), wz_ref[...], preferred_element_type=F32)
    w_log = -jax.nn.softplus(-(vec(_RV_W0) + zz[:, 0:256])) - 0.5
    lw = -jnp.exp(w_log)
    a_sig = jax.nn.sigmoid(vec(_RV_A0) + zz[:, 256:512])
    gate = zz[:, 512:768]
    if has_vres:
        low = jnp.dot(h_ref[0], v1_ref[...], preferred_element_type=F32)
        logit = vec(_RV_V0) + jnp.dot(low.astype(BF16), v2_ref[...], preferred_element_type=F32)
        v = v + (vf_ref[0] - v) * jax.nn.sigmoid(logit)
    else:
        vout_ref[0] = v
    kk = k * vec(_RV_KK)
    norm = jnp.sqrt(_dot_split_lhs(kk * kk, ones_ref[...], 2))
    kk = kk / jnp.maximum(norm, 1e-12)
    k = k * (1.0 + (a_sig - 1.0) * vec(_RV_KA))
    a_vec = -kk
    b_vec = kk * a_sig

    strict, incl = _block_masks(C)
    st = lambda t: _stack_heads(t, HEAD_DIM)
    bdot = lambda p, q: jnp.dot(p.astype(BF16), q.astype(BF16), preferred_element_type=F32)
    state_mask = (lax.broadcasted_iota(jnp.int32, (256, 256), 0) // HEAD_DIM
                  == lax.broadcasted_iota(jnp.int32, (256, 256), 1) // HEAD_DIM)
    n4 = 4 * C
    rcs = [slice(c * C, (c + 1) * C) for c in range(tm // C)]
    each = lambda fn, *lists: [fn(*args) for args in zip(*lists)]
    lcum = [_dot_split_rhs(tri_ref[...], lw[rc], 3) for rc in rcs]
    lex = each(lambda rc, lc: lc - lw[rc], rcs, lcum)
    mid = [lc[C // 2 - 1:C // 2, :] for lc in lcum]
    last = [lc[C - 1:C, :] for lc in lcum]
    e_mid = each(lambda lc, m: jnp.exp(m - lc), lcum, mid)
    e_end = each(lambda lc, l: jnp.exp(l - lc), lcum, last)
    left = each(lambda rc, lc, lx, m: jnp.concatenate(
        [st(a_vec[rc] * jnp.exp(lx - m)), st(r[rc] * jnp.exp(lc - m))], axis=0).astype(BF16), rcs, lcum, lex, mid)
    right = each(lambda rc, em: jnp.concatenate([st(b_vec[rc] * em), st(k[rc] * em)], axis=0).astype(BF16), rcs, e_mid)
    g = each(_dot_nt, left, right)
    v_st = [st(v[rc]) for rc in rcs]
    p = [jnp.where(strict, gc[0:n4, 0:n4], 0.0) for gc in g]
    rhs = each(lambda gc, vs: bdot(jnp.where(strict, gc[0:n4, n4:2 * n4], 0.0), vs), g, v_st)
    y_rk = each(lambda gc, vs: bdot(jnp.where(incl, gc[n4:2 * n4, n4:2 * n4], 0.0), vs), g, v_st)
    n_rb = [jnp.where(incl, gc[n4:2 * n4, 0:n4], 0.0) for gc in g]
    x = each(lambda rc, lx, rh: jnp.concatenate([st(a_vec[rc] * jnp.exp(lx)), rh], axis=1), rcs, lex, rhs)
    bk_end = each(lambda rc, ee: jnp.concatenate([b_vec[rc] * ee, k[rc] * ee], axis=0).astype(BF16), rcs, e_end)
    eye = (lax.broadcasted_iota(jnp.int32, (n4, n4), 0) == lax.broadcasted_iota(jnp.int32, (n4, n4), 1)).astype(F32)
    t = [eye + pc for pc in p]
    for f in range(1, int(np.log2(C))):
        p = [bdot(pc, pc) for pc in p]
        t = each(lambda tc, pc: tc + bdot(pc, tc), t, p)
    x = each(bdot, t, x)
    y_part = each(bdot, n_rb, x)
    a_eff = [_unstack_heads(xc[:, 0:256], C).astype(BF16) for xc in x]
    u0 = [_unstack_heads(xc[:, 256:512], C) for xc in x]
    r_eff = each(lambda rc, lc, yp: (r[rc] * jnp.exp(lc) + _unstack_heads(yp[:, 0:256], C)).astype(BF16),
                 rcs, lcum, y_part)
    y0 = each(lambda yp, yr: _unstack_heads(yp[:, 256:512] + yr, C), y_part, y_rk)
    mix = each(lambda ae, bk: jnp.where(state_mask, _dot_tn(ae, bk[0:C]), 0.0).astype(BF16), a_eff, bk_end)
    add = each(lambda rc, u, bk: jnp.where(
        state_mask, _dot_tn(jnp.concatenate([u, v[rc]], axis=0).astype(BF16), bk), 0.0), rcs, u0, bk_end)
    decay = [jnp.exp(l) for l in last]
    ys = []
    state = state_ref[...]
    for c in range(len(rcs)):
        sb = state.astype(BF16)
        ys.append(_dot_nt(r_eff[c], sb) + y0[c])
        state = state * decay[c] + jnp.dot(sb, mix[c], preferred_element_type=F32) + add[c]
    state_ref[...] = state
    y = jnp.concatenate(ys, axis=0)
    inv = 1.0 / HEAD_DIM
    mean = _dot_split_lhs(y, ones_ref[...], 2) * inv
    yc = y - mean
    var = _dot_split_lhs(yc * yc, ones_ref[...], 2) * inv
    y = yc * lax.rsqrt(var + RWKV_GN_EPS) * vec(_RV_LN)
    y = y + _dot_split_lhs(r * k * vec(_RV_RK), ones_ref[...], 2) * v
    o_ref[0] = (y * gate).astype(BF16)


def _rwkv(rw, wz, vecs, h=None, v_first=None, v1p=None, v2p=None, tm=256):
    B, S, W = rw.shape
    C = RWKV_CHUNK
    has_vres = h is not None
    tri, ones = _tri_ones(C), _head_ones(HEAD_DIM)
    tile = lambda w: pl.BlockSpec((1, tm, w), lambda b, s: (b, s, 0))
    full = lambda a: pl.BlockSpec(a.shape, lambda b, s: (0,) * a.ndim)
    if has_vres:
        args = (rw, h, v_first, wz, vecs, v1p, v2p, tri, ones)
        in_specs = [tile(W), tile(D_MODEL), tile(256), full(wz), full(vecs), full(v1p), full(v2p), full(tri), full(ones)]
        out_specs = tile(256)
        out_shape = jax.ShapeDtypeStruct((B, S, 256), BF16)
    else:
        args = (rw, wz, vecs, tri, ones)
        in_specs = [tile(W), full(wz), full(vecs), full(tri), full(ones)]
        out_specs = [tile(256), tile(256)]
        out_shape = [jax.ShapeDtypeStruct((B, S, 256), BF16), jax.ShapeDtypeStruct((B, S, 256), F32)]
    return pl.pallas_call(
        functools.partial(_rwkv_kernel, has_vres=has_vres, C=C),
        grid=(B, S // tm),
        in_specs=in_specs,
        out_specs=out_specs,
        out_shape=out_shape,
        scratch_shapes=[pltpu.VMEM((256, 256), F32)],
        compiler_params=_cparams(("parallel", "arbitrary")),
        name="rwkv_vres" if has_vres else "rwkv",
    )(*args)


def _rwkv_weights(l, rwkv_w0, rwkv_w2, rwkv_a0, rwkv_a2, rwkv_g2, rwkv_v0, rwkv_k_k, rwkv_k_a, rwkv_r_k, rwkv_ln):
    wz = jnp.zeros((128, 768), F32)
    wz = wz.at[0:32, 0:256].set(rwkv_w2[l]).at[32:64, 256:512].set(rwkv_a2[l]).at[64:128, 512:768].set(rwkv_g2[l])
    v0 = rwkv_v0[l - 1] if l > 0 else jnp.zeros((256,), F32)
    vecs = jnp.stack([rwkv_w0[l], rwkv_a0[l], rwkv_k_k[l], rwkv_k_a[l], rwkv_r_k[l].reshape(-1), rwkv_ln[l], v0,
                      jnp.zeros((256,), F32)])
    return wz.astype(BF16), vecs


def _merge_kernel(x_ref, h_ref, o0_ref, o1_ref, o2_ref, o3_ref, wg_ref, bg_ref, wb_ref, wo_ref, out_ref):
    h = h_ref[0]
    merged = None
    for i, o_ref in enumerate((o0_ref, o1_ref, o2_ref, o3_ref)):
        gate = jax.nn.sigmoid(jnp.dot(h, wg_ref[i], preferred_element_type=F32) + bg_ref[i:i + 1, :])
        term = gate * jnp.dot(o_ref[0], wb_ref[i], preferred_element_type=F32)
        merged = term if merged is None else merged + term
    out_ref[0] = x_ref[0] + jnp.dot(merged.astype(BF16), wo_ref[...], preferred_element_type=F32)


def _merge(x, h, o_nsa, o_pool, o_rwkv, o_gla, wg, bg, wb, wo, tm=512):
    B, S, D = x.shape
    tile = lambda w: pl.BlockSpec((1, tm, w), lambda b, s: (b, s, 0))
    full = lambda a: pl.BlockSpec(a.shape, lambda b, s: (0,) * a.ndim)
    return pl.pallas_call(
        _merge_kernel,
        grid=(B, S // tm),
        in_specs=[tile(D), tile(D), tile(256), tile(256), tile(256), tile(256), full(wg), full(bg), full(wb), full(wo)],
        out_specs=tile(D),
        out_shape=jax.ShapeDtypeStruct((B, S, D), F32),
        compiler_params=_cparams(("parallel", "parallel")),
        name="merge",
    )(x, h, o_nsa, o_pool, o_rwkv, o_gla, wg, bg, wb, wo)


def _memkv_kernel(mem_ref, g_ref, w_ref, o_ref):
    mn = _rms(mem_ref[0], g_ref[...]).astype(BF16)
    o_ref[0] = jnp.dot(mn, w_ref[...], preferred_element_type=F32).astype(BF16)


def _memkv(mem, g, wkv_all):
    B, M, D = mem.shape
    N = wkv_all.shape[1]
    return pl.pallas_call(
        _memkv_kernel,
        grid=(B,),
        in_specs=[pl.BlockSpec((1, M, D), lambda b: (b, 0, 0)), pl.BlockSpec(g.shape, lambda b: (0, 0)),
                  pl.BlockSpec(wkv_all.shape, lambda b: (0, 0))],
        out_specs=pl.BlockSpec((1, M, N), lambda b: (b, 0, 0)),
        out_shape=jax.ShapeDtypeStruct((B, M, N), BF16),
        compiler_params=_cparams(("parallel",)),
        name="memkv",
    )(mem, g, wkv_all)


def _cross_kernel(x_ref, g_ref, wq_ref, k_ref, v_ref, wo_ref, out_ref):
    x = x_ref[0]
    hc = _rms(x, g_ref[...]).astype(BF16)
    q = jnp.dot(hc, wq_ref[...], preferred_element_type=F32) * (HEAD_DIM ** -0.5)
    head = lax.broadcasted_iota(jnp.int32, q.shape, 1) // HEAD_DIM
    tm = q.shape[0]
    s = _dot_nt(_stack_heads(q, HEAD_DIM).astype(BF16), k_ref[0])
    e = jnp.exp(s - jnp.max(s, axis=1, keepdims=True))
    p = e / jnp.sum(e, axis=1, keepdims=True)
    o_all = jnp.dot(p.astype(BF16), v_ref[0], preferred_element_type=F32)
    o = jnp.zeros(q.shape, F32)
    for hh in range(CA_WIDTH // HEAD_DIM):
        o = jnp.where(head == hh, o_all[hh * tm:(hh + 1) * tm], o)
    out_ref[0] = x + jnp.dot(o.astype(BF16), wo_ref[...], preferred_element_type=F32)


def _cross(x, g, wq, memkv, l, wo, tm=512):
    B, S, D = x.shape
    M = memkv.shape[1]
    tile = pl.BlockSpec((1, tm, D), lambda b, s: (b, s, 0))
    full = lambda a: pl.BlockSpec(a.shape, lambda b, s: (0,) * a.ndim)
    kspec = pl.BlockSpec((1, M, CA_WIDTH), lambda b, s: (b, 0, 2 * l))
    vspec = pl.BlockSpec((1, M, CA_WIDTH), lambda b, s: (b, 0, 2 * l + 1))
    return pl.pallas_call(
        _cross_kernel,
        grid=(B, S // tm),
        in_specs=[tile, full(g), full(wq), kspec, vspec, full(wo)],
        out_specs=tile,
        out_shape=jax.ShapeDtypeStruct((B, S, D), F32),
        compiler_params=_cparams(("parallel", "parallel")),
        name="cross_attn",
    )(x, g, wq, memkv, memkv, wo)


def _swiglu_cols(xb, wg, wu, wd, tf):
    acc = jnp.zeros((xb.shape[0], wd.shape[-1]), F32)
    for c in range(wg.shape[-1] // tf):
        cols = slice(c * tf, (c + 1) * tf)
        gate = jnp.dot(xb, wg[:, cols], preferred_element_type=F32)
        up = jnp.dot(xb, wu[:, cols], preferred_element_type=F32)
        act = gate * jax.nn.sigmoid(gate) * up
        acc = acc + jnp.dot(act.astype(BF16), wd[cols, :], preferred_element_type=F32)
    return acc


def _ffn_kernel(x_ref, g_ref, wg_ref, wu_ref, wd_ref, out_ref, *, tf):
    x = x_ref[...]
    out_ref[...] = x + _swiglu_cols(_rms(x, g_ref[...]).astype(BF16), wg_ref, wu_ref, wd_ref, tf)


def _ffn(x2, g, wg, wu, wd, tm=512, tf=256):
    T, D = x2.shape
    full = lambda a: pl.BlockSpec(a.shape, lambda i: (0,) * a.ndim)
    return pl.pallas_call(
        functools.partial(_ffn_kernel, tf=tf),
        grid=(T // tm,),
        in_specs=[pl.BlockSpec((tm, D), lambda i: (i, 0)), full(g), full(wg), full(wu), full(wd)],
        out_specs=pl.BlockSpec((tm, D), lambda i: (i, 0)),
        out_shape=jax.ShapeDtypeStruct((T, D), F32),
        compiler_params=_cparams(("parallel",)),
        name="ffn",
    )(x2, g, wg, wu, wd)


MOE_TM = 512


SUBLANES = 8


def _to_slabs(ref, x):
    w = ref.shape[-1]
    for j in range(ref.shape[-2]):
        ref[:, j, :] = x[:, j * w:(j + 1) * w]


def _from_slabs(ref):
    return jnp.concatenate([ref[:, j, :] for j in range(ref.shape[-2])], axis=1)


def _route_kernel(x_ref, g_ref, r_ref, hf_ref, idx_ref, w_ref):
    hf = _rms(x_ref[...], g_ref[...])
    _to_slabs(hf_ref, hf)
    logits = jnp.dot(hf, r_ref[...], preferred_element_type=F32, precision=lax.Precision.HIGHEST)
    lane = lax.broadcasted_iota(jnp.int32, logits.shape, 1)
    big = logits.shape[1]
    logits = jnp.where(lane < N_EXPERTS, logits, -jnp.inf)
    m1 = jnp.max(logits, axis=1, keepdims=True)
    i1 = jnp.min(jnp.where(logits == m1, lane, big), axis=1, keepdims=True)
    rest = jnp.where(lane == i1, -jnp.inf, logits)
    m2 = jnp.max(rest, axis=1, keepdims=True)
    i2 = jnp.min(jnp.where(rest == m2, lane, big), axis=1, keepdims=True)
    e2 = jnp.exp(m2 - m1)
    w1 = 1.0 / (1.0 + e2)
    idx_ref[...] = jnp.where(lane == 0, i1, jnp.where(lane == 1, i2, 0))
    w_ref[...] = jnp.where(lane == 0, w1, jnp.where(lane == 1, e2 * w1, 0.0))


def _moe_route(x2, g, router_p, tm=1024):
    T, D = x2.shape
    row = lambda w: pl.BlockSpec((tm, w), lambda i: (i, 0))
    full = lambda a: pl.BlockSpec(a.shape, lambda i: (0,) * a.ndim)
    return pl.pallas_call(
        _route_kernel,
        grid=(T // tm,),
        in_specs=[row(D), full(g), full(router_p)],
        out_specs=[pl.BlockSpec((tm, SUBLANES, D // SUBLANES), lambda i: (i, 0, 0)), row(LANES), row(LANES)],
        out_shape=[jax.ShapeDtypeStruct((T, SUBLANES, D // SUBLANES), F32), jax.ShapeDtypeStruct((T, LANES), jnp.int32),
                   jax.ShapeDtypeStruct((T, LANES), F32)],
        compiler_params=_cparams(("parallel",)),
        name="moe_route",
    )(x2, g, router_p)


def _moe_plan(expert_of_pair, tm):
    onehot = (expert_of_pair[:, None] == jnp.arange(N_EXPERTS, dtype=jnp.int32)[None, :]).astype(jnp.int32)
    csum = jnp.cumsum(onehot, axis=0)
    rank = jnp.sum((csum - 1) * onehot, axis=1)
    counts = csum[-1]
    padded = ((counts + tm - 1) // tm) * tm
    ends = jnp.cumsum(padded)
    dest = ((ends - padded)[expert_of_pair] + rank).astype(jnp.int32)
    n_rows = expert_of_pair.shape[0] + N_EXPERTS * tm
    tile_expert = jnp.sum((jnp.arange(n_rows // tm, dtype=jnp.int32)[:, None] * tm >= ends[None, :]).astype(jnp.int32), axis=1)
    n_pairs = expert_of_pair.shape[0]
    pair_of_row = jnp.full((n_rows,), -1, jnp.int32).at[dest].set(
        jnp.arange(n_pairs, dtype=jnp.int32), unique_indices=True, mode='promise_in_bounds')
    real = pair_of_row >= 0
    token_of_row = jnp.where(real, pair_of_row // 2, 0)
    out_row = jnp.where(real, (pair_of_row % 2) * (n_pairs // 2) + pair_of_row // 2,
                        n_pairs + jnp.arange(n_rows, dtype=jnp.int32) % tm)
    return token_of_row, out_row, jnp.minimum(tile_expert, N_EXPERTS - 1).astype(jnp.int32)


DMA_UNROLL = 8


def _gather_rows(table_hbm, row_of, dst, n, sem, wait):
    def body(c, carry):
        for u in range(DMA_UNROLL):
            r = c * DMA_UNROLL + u
            copy = pltpu.make_async_copy(table_hbm.at[pl.ds(row_of(r), 1)], dst.at[pl.ds(r, 1)], sem)
            copy.wait() if wait else copy.start()
        return carry

    lax.fori_loop(0, n // DMA_UNROLL, body, 0)


def _scatter_rows(src, dst_hbm, row_of, n, sem, wait):
    def body(c, carry):
        for u in range(DMA_UNROLL):
            r = c * DMA_UNROLL + u
            copy = pltpu.make_async_copy(src.at[pl.ds(r, 1)], dst_hbm.at[pl.ds(row_of(r), 1)], sem)
            copy.wait() if wait else copy.start()
        return carry

    lax.fori_loop(0, n // DMA_UNROLL, body, 0)


def _experts_kernel(te_ref, tok_cur_ref, tok_next_ref, pair_prev_ref, pair_wait_ref, hf_hbm, wg_ref, wu_ref, wd_ref,
                    o2_hbm, xin, yout, sems, sem_out, *, tf):
    del te_ref
    i = pl.program_id(0)
    slot = i % 2
    tm = xin.shape[1]
    n_f = wg_ref.shape[2] // tf
    cur_tok = lambda r: tok_cur_ref[0, 0, r]
    next_tok = lambda r: tok_next_ref[0, 0, r]
    prev_pair = lambda r: pair_prev_ref[0, 0, r]

    @pl.when(i == 0)
    def _():
        _gather_rows(hf_hbm, cur_tok, xin.at[0], tm, sems.at[0], wait=False)
        yout[1] = jnp.zeros(yout.shape[1:], F32)

    @pl.when(i >= 1)
    def _():
        _scatter_rows(yout.at[slot], o2_hbm, lambda r: pair_wait_ref[0, 0, r], tm, sem_out, wait=True)

    _gather_rows(hf_hbm, cur_tok, xin.at[slot], tm, sems.at[slot], wait=True)
    xb = _from_slabs(xin.at[slot]).astype(BF16)
    acc = jnp.zeros((tm, wd_ref.shape[2]), F32)
    n_issue = max(n_f - 2, 1)
    bounds = [min(round(c * tm / n_issue), tm) for c in range(n_f + 1)]
    for c in range(n_f):
        for r in range(bounds[c], bounds[c + 1]):
            pltpu.make_async_copy(hf_hbm.at[pl.ds(next_tok(r), 1)], xin.at[1 - slot, pl.ds(r, 1)],
                                  sems.at[1 - slot]).start()
            pltpu.make_async_copy(yout.at[1 - slot, pl.ds(r, 1)], o2_hbm.at[pl.ds(prev_pair(r), 1)], sem_out).start(
                priority=1)
        cols = slice(c * tf, (c + 1) * tf)
        gate = jnp.dot(xb, wg_ref[0, :, cols], preferred_element_type=F32)
        up = jnp.dot(xb, wu_ref[0, :, cols], preferred_element_type=F32)
        act = gate * jax.nn.sigmoid(gate) * up
        acc = acc + jnp.dot(act.astype(BF16), wd_ref[0, cols, :], preferred_element_type=F32)
    _to_slabs(yout.at[slot], acc)

    @pl.when(i == pl.num_programs(0) - 1)
    def _():
        _gather_rows(hf_hbm, next_tok, xin.at[1 - slot], tm, sems.at[1 - slot], wait=True)
        _scatter_rows(yout.at[1 - slot], o2_hbm, prev_pair, tm, sem_out, wait=True)


def _moe_experts(hf, token_of_row, out_row_of_row, tile_expert, wg, wu, wd, tf=256):
    T = hf.shape[0]
    slab = hf.shape[1:]
    tm = MOE_TM
    n_tiles = token_of_row.shape[0] // tm
    tok = token_of_row.reshape(n_tiles, 1, tm)
    tok = jnp.concatenate([tok, tok[-1:], tok[-1:]], axis=0)
    spare = (2 * T + jnp.arange(tm, dtype=jnp.int32)).reshape(1, 1, tm)
    pair = jnp.concatenate([spare, out_row_of_row.reshape(n_tiles, 1, tm)], axis=0)
    te = jnp.concatenate([tile_expert, tile_expert[-1:]])
    smem = lambda index: pl.BlockSpec((1, 1, tm), lambda i, te: (index(i), 0, 0), memory_space=pltpu.SMEM)
    expert = lambda shape: pl.BlockSpec((1,) + shape, lambda i, te: (te[i], 0, 0))
    any_spec = pl.BlockSpec(memory_space=pl.ANY)
    grid_spec = pltpu.PrefetchScalarGridSpec(
        num_scalar_prefetch=1,
        grid=(n_tiles + 1,),
        in_specs=[smem(lambda i: i), smem(lambda i: i + 1), smem(lambda i: i), smem(lambda i: jnp.maximum(i - 1, 0)),
                  any_spec, expert(wg.shape[1:]), expert(wu.shape[1:]), expert(wd.shape[1:])],
        out_specs=any_spec,
        scratch_shapes=[pltpu.VMEM((2, tm) + slab, F32), pltpu.VMEM((2, tm) + slab, F32), pltpu.SemaphoreType.DMA((2,)),
                        pltpu.SemaphoreType.DMA(())],
    )
    return pl.pallas_call(
        functools.partial(_experts_kernel, tf=tf),
        grid_spec=grid_spec,
        out_shape=jax.ShapeDtypeStruct((2 * T + tm,) + slab, F32),
        compiler_params=_cparams(("arbitrary",)),
        name="moe_experts",
    )(te, tok, tok, pair, pair, hf, wg, wu, wd)


def _combine_kernel(x_ref, w_ref, gf_ref, y0_ref, y1_ref, out_ref):
    w = w_ref[...]
    y = x_ref[...] + w[:, 0:1] * _from_slabs(y0_ref) + w[:, 1:2] * _from_slabs(y1_ref)
    out_ref[...] = _rms(y, gf_ref[...])


def _moe_combine(x2, w, o2, g_final, tm=512):
    T, D = x2.shape
    row = lambda wd, off: pl.BlockSpec((tm, wd), lambda i: (i + off, 0))
    slab = lambda off: pl.BlockSpec((tm,) + o2.shape[1:], lambda i: (i + off, 0, 0))
    return pl.pallas_call(
        _combine_kernel,
        grid=(T // tm,),
        in_specs=[row(D, 0), row(LANES, 0), pl.BlockSpec(g_final.shape, lambda i: (0, 0)), slab(0), slab(T // tm)],
        out_specs=row(D, 0),
        out_shape=jax.ShapeDtypeStruct((T, D), F32),
        compiler_params=_cparams(("parallel",)),
        name="moe_combine",
    )(x2, w, g_final, o2, o2)


def _moe_routed(x2, g, router_p, wg, wu, wd, g_final):
    T, D = x2.shape
    hf, idx, w = _moe_route(x2, g, router_p)
    token_of_row, out_row_of_row, tile_expert = _moe_plan(idx[:, 0:2].reshape(2 * T), MOE_TM)
    o2 = _moe_experts(hf, token_of_row, out_row_of_row, tile_expert, wg, wu, wd)
    return _moe_combine(x2, w, o2, g_final)


def kernel(x, mem, positions, norm_mix, w_in, nsa_cmp_pe, nsa_cmp_w1, nsa_cmp_w2, pool_w, pool_scale, rwkv_mu, rwkv_w0,
           rwkv_w2, rwkv_a0, rwkv_a2, rwkv_g2, rwkv_v0, rwkv_v1, rwkv_v2, rwkv_k_k, rwkv_k_a, rwkv_r_k, rwkv_ln, gla_a2,
           gla_ab, gla_norm, w_gate, b_gate, w_branch, w_out, norm_ca, mem_norm, w_ca_q, w_ca_kv, w_ca_o, norm_ffn,
           ffn_w_gate, ffn_w_up, ffn_w_down, moe_router, moe_w_gate, moe_w_up, moe_w_down, norm_final):
    B, S, D = x.shape
    depth = norm_mix.shape[0]
    assert depth == 2 and S % 256 == 0, "the final norm is fused into the layer-1 expert mixer"
    cs = _rope_angles(positions)
    cs_end = jnp.pad(cs[:, CMP_LEN - 1::CMP_STRIDE], ((0, 0), (0, 1), (0, 0)))
    memkv = _memkv(mem, mem_norm[None], jnp.concatenate([w_ca_kv[l] for l in range(depth)], axis=1).astype(BF16))
    v_first = None
    for l in range(depth):
        h, q_t, k2, v_t, cv, gates_t, u, rw, gl = _inproj(x, norm_mix[l][None], _inproj_weights(w_in[l]), cs,
                                                          rwkv_mu[l][None])
        kcmp, vcmp_t = _nsa_compress(cv, *_compress_weights(nsa_cmp_pe[l], nsa_cmp_w1[l], nsa_cmp_w2[l]), cs_end)
        o_nsa = _nsa_attend(q_t, k2, v_t, kcmp, vcmp_t, gates_t)
        o_pool = _pool(u, _block_diag(pool_w[l]).astype(BF16), pool_scale[l][None])
        wz, vecs = _rwkv_weights(l, rwkv_w0, rwkv_w2, rwkv_a0, rwkv_a2, rwkv_g2, rwkv_v0, rwkv_k_k, rwkv_k_a,
                                 rwkv_r_k, rwkv_ln)
        if l == 0:
            o_rwkv, v_first = _rwkv(rw, wz, vecs)
        else:
            v1p = jnp.pad(rwkv_v1[l - 1], ((0, 0), (0, LANES - rwkv_v1.shape[2]))).astype(BF16)
            v2p = jnp.pad(rwkv_v2[l - 1], ((0, LANES - rwkv_v2.shape[1]), (0, 0))).astype(BF16)
            o_rwkv = _rwkv(rw, wz, vecs, h, v_first, v1p, v2p)
        a2p = jnp.pad(gla_a2[l], ((0, LANES - gla_a2.shape[1]), (0, 0))).astype(BF16)
        o_gla = _gla(gl, a2p, gla_ab[l][None], jnp.tile(gla_norm[l], 4)[None])
        wb_nsa = w_branch[l, 0].reshape(4, HEAD_DIM, D)[jnp.array([0, 2, 1, 3])].reshape(BRANCH_WIDTH, D)
        wb = jnp.concatenate([wb_nsa[None], w_branch[l, 1:]], axis=0).astype(BF16)
        x = _merge(x, h, o_nsa, o_pool, o_rwkv, o_gla, w_gate[l].astype(BF16), b_gate[l], wb, w_out[l].astype(BF16))
        x = _cross(x, norm_ca[l][None], w_ca_q[l].astype(BF16), memkv, l, w_ca_o[l].astype(BF16))
        j = l // 2
        x2 = x.reshape(B * S, D)
        if l % 2 == 0:
            x2 = _ffn(x2, norm_ffn[l][None], ffn_w_gate[j].astype(BF16), ffn_w_up[j].astype(BF16),
                      ffn_w_down[j].astype(BF16))
        else:
            router_p = jnp.pad(moe_router[j], ((0, 0), (0, LANES - N_EXPERTS)))
            x2 = _moe_routed(x2, norm_ffn[l][None], router_p, moe_w_gate[j].astype(BF16), moe_w_up[j].astype(BF16),
                             moe_w_down[j].astype(BF16), norm_final[None])
        x = x2.reshape(B, S, D)
    return x
```

```python
import functools

import jax
import jax.numpy as jnp
import numpy as np
from jax import lax
from jax.experimental import pallas as pl
from jax.experimental.pallas import tpu as pltpu

F32 = jnp.float32
BF16 = jnp.bfloat16

D_MODEL = 1024
HEAD_DIM = 64
BRANCH_WIDTH = 256
ROPE_THETA = 500000.0
ROPE_DIM = 16
NORM_EPS = 1e-6
NEG_INF = -1e30

CMP_LEN = 32
CMP_STRIDE = 16
CMP_HIDDEN = 128
SEL_BLOCK = 64
SEL_TOPN = 8
WINDOW = 512

RWKV_IN = 896
RWKV_GN_EPS = 64e-5
RWKV_CHUNK = 64

GLA_DK = 32
GLA_TAU = 16.0
GLA_CHUNK = 32

CA_WIDTH = 256
N_EXPERTS = 8

LANES = 128
VMEM_LIMIT = 56 * 1024 * 1024

_C_Q = 0
_C_KV = 256
_C_CV = 768
_C_GATE = 1024
_C_POOL = 1152
_C_RWKV = 1408
_C_GLA = 2304
_C_END = 3200


def _cparams(sem):
    return pltpu.CompilerParams(dimension_semantics=sem, vmem_limit_bytes=VMEM_LIMIT)


def _rms(x, g):
    ms = jnp.mean(x * x, axis=-1, keepdims=True)
    return x * lax.rsqrt(ms + NORM_EPS) * g


def _rope128(t, cos, sin):
    lane = lax.broadcasted_iota(jnp.int32, t.shape, 1) % HEAD_DIM
    partner = jnp.where(lane < ROPE_DIM // 2, pltpu.roll(t, LANES - ROPE_DIM // 2, 1),
                        pltpu.roll(t, ROPE_DIM // 2, 1))
    return t * cos + partner * sin


def _inproj_kernel(x_ref, g_ref, w_ref, cs_ref, e_ref, ones_ref, mu_ref,
                   h_ref, qt_ref, k_ref, vt_ref, cv_ref, gt_ref, u_ref, rw_ref, gl_ref, prev_ref):
    s = pl.program_id(1)
    hb = _rms(x_ref[0], g_ref[...]).astype(BF16)
    h_ref[0] = hb

    proj = jnp.dot(hb, w_ref[...], preferred_element_type=F32)

    def mm(a, b):
        return proj[:, a:b]

    cos, sin = _rope_expand(cs_ref[0], e_ref[...], ones_ref[...])
    scale = HEAD_DIM ** -0.5
    qt_ref[0, 0:128, :] = jnp.transpose(_rope128(mm(_C_Q, _C_Q + 128), cos, sin) * scale).astype(BF16)
    qt_ref[0, 128:256, :] = jnp.transpose(_rope128(mm(_C_Q + 128, _C_Q + 256), cos, sin) * scale).astype(BF16)
    k_ref[0, :, 0:128] = _rope128(mm(_C_KV, _C_KV + 128), cos, sin).astype(BF16)
    vt_ref[0, 0:128, :] = jnp.transpose(mm(_C_KV + 128, _C_KV + 256)).astype(BF16)
    k_ref[0, :, 128:256] = _rope128(mm(_C_KV + 256, _C_KV + 384), cos, sin).astype(BF16)
    vt_ref[0, 128:256, :] = jnp.transpose(mm(_C_KV + 384, _C_KV + 512)).astype(BF16)
    cv_ref[0] = mm(_C_CV, _C_CV + 256).astype(BF16)
    gt_ref[0] = jnp.transpose(jax.nn.sigmoid(mm(_C_GATE, _C_GATE + 128)))
    u_ref[0] = mm(_C_POOL, _C_POOL + 256)
    gl_ref[0] = mm(_C_GLA, _C_END)

    p = mm(_C_RWKV, _C_RWKV + RWKV_IN)
    tm = p.shape[0]

    @pl.when(s == 0)
    def _():
        prev_ref[...] = jnp.zeros_like(prev_ref)

    row = lax.broadcasted_iota(jnp.int32, p.shape, 0)
    shifted = jnp.where(row == 0, prev_ref[0:1, :], pltpu.roll(p, 1, 0))
    prev_ref[0:1, :] = p[tm - 1:tm, :]
    rw_ref[0] = p + (shifted - p) * mu_ref[...]


def _inproj(x, g, w_all, cs, mu, tm=512):
    B, S, D = x.shape
    e, ones = _rope_expander()
    tok = lambda w: pl.BlockSpec((1, tm, w), lambda b, s: (b, s, 0))
    tok_t = lambda w: pl.BlockSpec((1, w, tm), lambda b, s: (b, 0, s))
    full = lambda a: pl.BlockSpec(a.shape, lambda b, s: (0,) * a.ndim)
    outs = [(D, BF16, True), (256, BF16, False), (256, BF16, True), (256, BF16, False), (256, BF16, True),
            (128, F32, False), (256, F32, True), (RWKV_IN, F32, True), (896, F32, True)]
    return pl.pallas_call(
        _inproj_kernel,
        grid=(B, S // tm),
        in_specs=[tok(D), full(g), full(w_all), tok(ROPE_DIM), full(e), full(ones), full(mu)],
        out_specs=[tok(w) if major else tok_t(w) for w, _, major in outs],
        out_shape=[jax.ShapeDtypeStruct((B, S, w) if major else (B, w, S), dt) for w, dt, major in outs],
        scratch_shapes=[pltpu.VMEM((8, RWKV_IN), F32)],
        compiler_params=_cparams(("parallel", "arbitrary")),
        name="inproj",
    )(x, g, w_all, cs, e, ones, mu)


def _inproj_columns():
    span = lambda lo, n: list(range(lo, lo + n))
    pad = lambda n: [-1] * n
    nsa, pool, rwkv, gla = 0, 1036, 1292, 2188
    cols = (span(nsa, 64) + span(nsa + 128, 64) + span(nsa + 64, 64) + span(nsa + 192, 64)
            + span(nsa + 512, 512)
            + span(nsa + 256, 256)
            + span(nsa + 1024, 12) + pad(116)
            + span(pool, 256) + span(rwkv, RWKV_IN)
            + span(gla, 512) + span(gla + 528, 256) + span(gla + 512, 16) + pad(112))
    assert len(cols) == _C_END
    return np.asarray(cols, np.int32)


def _inproj_weights(w):
    onehot = (jnp.arange(w.shape[1], dtype=jnp.int32)[:, None] == jnp.asarray(_inproj_columns())[None, :]).astype(BF16)
    return jnp.dot(w.astype(BF16), onehot, preferred_element_type=F32).astype(BF16)


def _rope_angles(positions):
    inv_freq = ROPE_THETA ** (-jnp.arange(0, ROPE_DIM, 2, dtype=F32) / ROPE_DIM)
    ang = positions.astype(F32)[..., None] * inv_freq
    return jnp.concatenate([jnp.cos(ang), jnp.sin(ang)], axis=-1)


def _rope_expander():
    half = ROPE_DIM // 2
    e = np.zeros((ROPE_DIM, 2 * LANES), np.float32)
    ones = np.zeros((1, LANES), np.float32)
    for lane in range(LANES):
        d = lane % HEAD_DIM
        if d < ROPE_DIM:
            e[d % half, lane] = 1.0
            e[half + d % half, LANES + lane] = -1.0 if d < half else 1.0
        else:
            ones[0, lane] = 1.0
    return jnp.asarray(e, BF16), jnp.asarray(ones)


def _rope_expand(cs, e, ones):
    tab = _dot_split_lhs(cs, e, 3)
    return tab[:, 0:LANES] + ones, tab[:, LANES:2 * LANES]


def _gelu_tanh(x):
    return x * (0.5 * (1.0 + jnp.tanh(np.sqrt(2.0 / np.pi) * (x + 0.044715 * (x * x * x)))))


def _dot_nt(a, b):
    return lax.dot_general(a, b, (((1,), (1,)), ((), ())), preferred_element_type=F32)


def _compress_kernel(x_ref, wa_ref, wb_ref, bias_ref, w2k_ref, w2v_ref, cs_ref, e_ref, ones_ref, kc_ref, vc_ref):
    x = x_ref[0]
    a = jnp.dot(x, wa_ref[...], preferred_element_type=F32)
    b = jnp.dot(x, wb_ref[...], preferred_element_type=F32)
    n = a.shape[0]
    hid = _gelu_tanh(a + pltpu.roll(b, n - 1, 0) + bias_ref[...])
    kc = jnp.dot(hid[:, 0:256].astype(BF16), w2k_ref[...], preferred_element_type=F32)
    vc = jnp.dot(hid[:, 256:512].astype(BF16), w2v_ref[...], preferred_element_type=F32)
    cos, sin = _rope_expand(cs_ref[0], e_ref[...], ones_ref[...])
    kc_ref[0] = _rope128(kc, cos, sin).astype(BF16)
    vc_ref[0] = jnp.transpose(vc).astype(BF16)


def _nsa_compress(cv, wa, wb, bias, w2k, w2v, cs_end):
    B, S, _ = cv.shape
    nseg = S // CMP_STRIDE
    x = cv.reshape(B, nseg, CMP_STRIDE * 256)
    e, ones = _rope_expander()
    per_b = lambda r, w: pl.BlockSpec((1, r, w), lambda b: (b, 0, 0))
    full = lambda a: pl.BlockSpec(a.shape, lambda b: (0,) * a.ndim)
    return pl.pallas_call(
        _compress_kernel,
        grid=(B,),
        in_specs=[per_b(nseg, CMP_STRIDE * 256), full(wa), full(wb), full(bias), full(w2k), full(w2v),
                  per_b(nseg, ROPE_DIM), full(e), full(ones)],
        out_specs=[per_b(nseg, 128), per_b(nseg, 128)],
        out_shape=[jax.ShapeDtypeStruct((B, nseg, 128), BF16)] * 2,
        compiler_params=_cparams(("parallel",)),
        name="nsa_compress",
    )(x, wa, wb, bias, w2k, w2v, cs_end, e, ones)


def _compress_weights(pe, w1, w2):
    eye2 = jnp.eye(2, dtype=F32)
    half = lambda lo: jnp.einsum('jldf,pj,qg->lpqdjgf', w1[:, lo:lo + CMP_STRIDE], eye2, eye2).reshape(
        CMP_STRIDE * 256, 4 * CMP_HIDDEN).astype(BF16)
    bias = jnp.einsum('jld,jldf->jf', pe, w1)
    bias = jnp.broadcast_to(bias[:, None, :], (2, 2, CMP_HIDDEN)).reshape(1, 4 * CMP_HIDDEN)
    bd = lambda w: jnp.einsum('fd,gq->gfqd', w, eye2).reshape(2 * CMP_HIDDEN, 2 * HEAD_DIM).astype(BF16)
    return half(0), half(CMP_STRIDE), bias, bd(w2[0]), bd(w2[1])


def _cover_t(S):
    n_cmp = (S - CMP_LEN) // CMP_STRIDE + 1
    cmp_start = np.arange(n_cmp) * CMP_STRIDE
    slc_start = np.arange(S // SEL_BLOCK) * SEL_BLOCK
    cover = np.clip(np.minimum(cmp_start[:, None] + CMP_LEN, slc_start[None, :] + SEL_BLOCK)
                    - np.maximum(cmp_start[:, None], slc_start[None, :]), 0, None) / CMP_LEN
    out = np.zeros((S // SEL_BLOCK, S // CMP_STRIDE), np.float32)
    out[:, :n_cmp] = cover.T
    return jnp.asarray(out, BF16)


MASKED = 2.0 * NEG_INF


def _softmax_step(s, carry, v_t):
    m, l, acc = carry
    m_new = jnp.maximum(m, jnp.max(s, axis=0, keepdims=True))
    alpha = jnp.exp(m - m_new)
    p = jnp.exp(s - m_new)
    l = alpha * l + jnp.sum(p, axis=0, keepdims=True)
    acc = alpha * acc + jnp.dot(v_t, p.astype(BF16), preferred_element_type=F32)
    return m_new, l, acc


def _softmax_init(n_q):
    return (jnp.full((1, n_q), NEG_INF, F32), jnp.zeros((1, n_q), F32), jnp.zeros((LANES, n_q), F32))


def _nsa_kernel(qt_ref, k_ref, vt_ref, kc_ref, vct_ref, gt_ref, covt_ref, o_ref, *, tq, tk):
    i = pl.program_id(1)
    t0 = i * tq
    n_blk = covt_ref.shape[0]
    n_q = 4 * tq
    qa_t, qb_t = qt_ref[0, 0:128, :], qt_ref[0, 128:256, :]
    sub_grp = lax.broadcasted_iota(jnp.int32, (LANES, tq), 0) // HEAD_DIM
    gates_t = gt_ref[0]
    tok = t0 + (lax.broadcasted_iota(jnp.int32, (1, n_q), 1) & (tq - 1))
    q_all = jnp.concatenate([jnp.where(sub_grp == g, q_t, 0) for g in range(2) for q_t in (qa_t, qb_t)], axis=1)

    s = jnp.dot(kc_ref[0], q_all, preferred_element_type=F32)
    nrow = lax.broadcasted_iota(jnp.int32, s.shape, 0)
    s = jnp.where(nrow * CMP_STRIDE + (CMP_LEN - 1) <= tok, s, MASKED)
    e = jnp.exp(s - jnp.maximum(jnp.max(s, axis=0, keepdims=True), NEG_INF))
    den = jnp.sum(e, axis=0, keepdims=True)
    p = e / jnp.where(den > 0.0, den, 1.0)
    o_cmp = jnp.dot(vct_ref[0], p.astype(BF16), preferred_element_type=F32)

    p_grp = jnp.concatenate([p[:, 2 * g * tq:(2 * g + 1) * tq] + p[:, (2 * g + 1) * tq:(2 * g + 2) * tq]
                             for g in range(2)], axis=1)
    imp = _dot_split_rhs(covt_ref[...], p_grp, 2)
    blk = lax.broadcasted_iota(jnp.int32, imp.shape, 0)
    tok_b = t0 + (lax.broadcasted_iota(jnp.int32, imp.shape, 1) & (tq - 1))
    cur = tok_b // SEL_BLOCK
    forced = (blk == 0) | (blk == cur) | (blk == cur - 1)
    imp = jnp.where(forced, 1e30, jnp.where(blk * SEL_BLOCK <= tok_b, imp, -1e30))
    cnt = jnp.zeros(imp.shape, F32)
    for i2 in range(n_blk):
        row = imp[i2:i2 + 1, :]
        cnt = cnt + jnp.where((row > imp) | ((row == imp) & (blk > i2)), 1.0, 0.0)
    bias = jnp.where(cnt < float(SEL_TOPN), 0.0, MASKED).astype(BF16)
    sel_bias = jnp.concatenate([bias[:, g * tq:(g + 1) * tq] for g in (0, 0, 1, 1)], axis=1)

    q_ext = jnp.concatenate([q_all, sel_bias, jnp.zeros((LANES - n_blk, n_q), BF16)], axis=0)

    def sel_scores(j):
        rows = pl.ds(pl.multiple_of(j * tk, tk), tk)
        blk_of_key = (j * tk + lax.broadcasted_iota(jnp.int32, (tk, LANES), 0)) // SEL_BLOCK
        onehot = (lax.broadcasted_iota(jnp.int32, (tk, LANES), 1) == blk_of_key).astype(BF16)
        k_ext = jnp.concatenate([k_ref[0, rows, 0:128], onehot], axis=1)
        return jnp.dot(k_ext, q_ext, preferred_element_type=F32), vt_ref[0, 0:128, rows]

    def sel_step(j, carry):
        s, v_t = sel_scores(j)
        return _softmax_step(s, carry, v_t)

    last = (t0 + tq - 1) // tk
    carry = lax.fori_loop(0, last, sel_step, _softmax_init(n_q))
    s, v_t = sel_scores(last)
    key = last * tk + lax.broadcasted_iota(jnp.int32, (tk, n_q), 0)
    _, l_sel, acc_sel = _softmax_step(jnp.where(key <= tok, s, MASKED), carry, v_t)
    o_sel = acc_sel / l_sel

    span = WINDOW + tq
    start = pl.multiple_of(jnp.maximum(t0 - WINDOW, 0), tq)
    key = start + lax.broadcasted_iota(jnp.int32, (span, n_q), 0)
    s = jnp.dot(k_ref[0, pl.ds(start, span), 128:256], q_all, preferred_element_type=F32)
    in_window = lax.bitcast_convert_type(tok - key, jnp.uint32) < jnp.uint32(WINDOW)
    s = jnp.where(in_window, s, MASKED)
    e = jnp.exp(s - jnp.max(s, axis=0, keepdims=True))
    o_win = (jnp.dot(vt_ref[0, 128:256, pl.ds(start, span)], e.astype(BF16), preferred_element_type=F32)
             / jnp.sum(e, axis=0, keepdims=True))

    out = [jnp.zeros((LANES, tq), F32), jnp.zeros((LANES, tq), F32)]
    for g in range(2):
        for r in range(2):
            cols = slice((2 * g + r) * tq, (2 * g + r + 1) * tq)
            c = g * 6 + r * 3
            o = (gates_t[c:c + 1, :] * o_cmp[:, cols] + gates_t[c + 1:c + 2, :] * o_sel[:, cols]
                 + gates_t[c + 2:c + 3, :] * o_win[:, cols])
            out[r] = jnp.where(sub_grp == g, o, out[r])
    o_ref[0, :, 0:128] = jnp.transpose(out[0]).astype(BF16)
    o_ref[0, :, 128:256] = jnp.transpose(out[1]).astype(BF16)


def _nsa_attend(q_t, k2, v_t, kcmp, vcmp_t, gates_t, tq=256, tk=512):
    B, S, _ = k2.shape
    assert tk % tq == 0 and S % tk == 0 and WINDOW % tq == 0
    covt = _cover_t(S)
    tile_t = lambda w: pl.BlockSpec((1, w, tq), lambda b, i: (b, 0, i))
    per_b = lambda r, w: pl.BlockSpec((1, r, w), lambda b, i: (b, 0, 0))
    return pl.pallas_call(
        functools.partial(_nsa_kernel, tq=tq, tk=tk),
        grid=(B, S // tq),
        in_specs=[tile_t(256), per_b(S, 256), per_b(256, S), per_b(S // CMP_STRIDE, 128), per_b(128, S // CMP_STRIDE),
                  tile_t(128), pl.BlockSpec(covt.shape, lambda b, i: (0, 0))],
        out_specs=pl.BlockSpec((1, tq, 256), lambda b, i: (b, i, 0)),
        out_shape=jax.ShapeDtypeStruct((B, S, 256), BF16),
        compiler_params=_cparams(("parallel", "arbitrary")),
        name="nsa_attend",
    )(q_t, k2, v_t, kcmp, vcmp_t, gates_t, covt)


def _pool_kernel(u_ref, w_ref, scale_ref, o_ref):
    u = u_ref[0]
    row = lax.broadcasted_iota(jnp.int32, u.shape, 0)
    grp = lax.broadcasted_iota(jnp.int32, u.shape, 1) // HEAD_DIM

    def back(x, k):
        return jnp.where(row >= k, pltpu.roll(x, k, 0), 0.0)

    s2 = u + back(u, 1)
    s4 = s2 + back(s2, 2)
    s8 = s4 + back(s4, 4)
    s16 = s8 + back(s8, 8)
    total = jnp.where(grp == 0, s2, jnp.where(grp == 1, s4, jnp.where(grp == 2, s8, s16)))
    win = jnp.where(grp == 0, 2, jnp.where(grp == 1, 4, jnp.where(grp == 2, 8, 16)))
    count = jnp.minimum(row + 1, win).astype(F32)
    d = total / count - u
    y = jnp.dot(d.astype(BF16), w_ref[...], preferred_element_type=F32)
    o_ref[0] = (y * scale_ref[...]).astype(BF16)


def _pool(u, w_bd, scale):
    B, S, W = u.shape
    per_b = pl.BlockSpec((1, S, W), lambda b: (b, 0, 0))
    full = lambda a: pl.BlockSpec(a.shape, lambda b: (0,) * a.ndim)
    return pl.pallas_call(
        _pool_kernel,
        grid=(B,),
        in_specs=[per_b, full(w_bd), full(scale)],
        out_specs=per_b,
        out_shape=jax.ShapeDtypeStruct((B, S, W), BF16),
        compiler_params=_cparams(("parallel",)),
        name="pool",
    )(u, w_bd, scale)


def _block_diag(w):
    G, a, b = w.shape
    return jnp.einsum('gab,gh->gahb', w, jnp.eye(G, dtype=w.dtype)).reshape(G * a, G * b)


def _dot_split_lhs(a, b_exact, terms):
    out, rem = None, a
    for i in range(terms):
        hi = rem.astype(BF16)
        d = jnp.dot(hi, b_exact, preferred_element_type=F32)
        out = d if out is None else out + d
        if i + 1 < terms:
            rem = rem - hi.astype(F32)
    return out


def _dot_split_rhs(a_exact, b, terms):
    out, rem = None, b
    for i in range(terms):
        hi = rem.astype(BF16)
        d = jnp.dot(a_exact, hi, preferred_element_type=F32)
        out = d if out is None else out + d
        if i + 1 < terms:
            rem = rem - hi.astype(F32)
    return out


def _dot_tn(a, b):
    return lax.dot_general(a, b, (((0,), (0,)), ((), ())), preferred_element_type=F32)


def _stack_heads(x, width, n_heads=4):
    head = lax.broadcasted_iota(jnp.int32, x.shape, 1) // width
    return jnp.concatenate([jnp.where(head == h, x, 0.0) for h in range(n_heads)], axis=0)


def _unstack_heads(x, C, n_heads=4):
    out = x[0:C]
    for h in range(1, n_heads):
        out = out + x[h * C:(h + 1) * C]
    return out


def _block_masks(C, n_heads=4):
    n = C * n_heads
    row = lax.broadcasted_iota(jnp.int32, (n, n), 0)
    col = lax.broadcasted_iota(jnp.int32, (n, n), 1)
    same = (row // C) == (col // C)
    return same & (row > col), same & (row >= col)


def _tri_ones(C):
    return jnp.asarray(np.tril(np.ones((C, C), np.float32)), BF16)


def _head_ones(width, n_heads=4):
    return jnp.asarray(np.kron(np.eye(n_heads, dtype=np.float32), np.ones((width, width), np.float32)), BF16)


def _gla_kernel(gl_ref, a2_ref, ab_ref, norm_ref, tri_ref, ones_ref, o_ref, state_ref, *, C):
    s = pl.program_id(1)

    @pl.when(s == 0)
    def _():
        state_ref[...] = jnp.zeros_like(state_ref)

    gl = gl_ref[0]
    tm = gl.shape[0]
    q = gl[:, 0:128] * (GLA_DK ** -0.5)
    k = gl[:, 128:256]
    v = gl[:, 256:512]
    og = gl[:, 512:768]
    ad = gl[:, 768:896]
    x = jnp.dot(ad.astype(BF16), a2_ref[...], preferred_element_type=F32) + ab_ref[...]
    log_a = jax.nn.log_sigmoid(x) * (1.0 / GLA_TAU)
    strict, incl = _block_masks(C)
    kv_mask = (lax.broadcasted_iota(jnp.int32, (256, 128), 0) // HEAD_DIM
               == lax.broadcasted_iota(jnp.int32, (256, 128), 1) // GLA_DK)
    rcs = [slice(c * C, (c + 1) * C) for c in range(tm // C)]
    bcum = [_dot_split_rhs(tri_ref[...], log_a[rc], 3) for rc in rcs]
    mid = [b[C // 2 - 1:C // 2, :] for b in bcum]
    last = [b[C - 1:C, :] for b in bcum]
    q_m = [_stack_heads(q[rc] * jnp.exp(b - m), GLA_DK).astype(BF16) for rc, b, m in zip(rcs, bcum, mid)]
    k_m = [_stack_heads(k[rc] * jnp.exp(m - b), GLA_DK).astype(BF16) for rc, b, m in zip(rcs, bcum, mid)]
    att = [jnp.where(incl, _dot_nt(qm, km), 0.0).astype(BF16) for qm, km in zip(q_m, k_m)]
    v_st = [_stack_heads(v[rc], HEAD_DIM).astype(BF16) for rc in rcs]
    o_intra = [_unstack_heads(jnp.dot(a, vs, preferred_element_type=F32), C) for a, vs in zip(att, v_st)]
    k_end = [(k[rc] * jnp.exp(l - b)).astype(BF16) for rc, b, l in zip(rcs, bcum, last)]
    upd = [jnp.where(kv_mask, _dot_tn(v[rc].astype(BF16), ke), 0.0) for rc, ke in zip(rcs, k_end)]
    q_in = [(q[rc] * jnp.exp(b)).astype(BF16) for rc, b in zip(rcs, bcum)]
    decay = [jnp.exp(l) for l in last]
    outs = []
    state = state_ref[...]
    for c in range(len(rcs)):
        outs.append(o_intra[c] + _dot_nt(q_in[c], state.astype(BF16)))
        state = state * decay[c] + upd[c]
    state_ref[...] = state
    o = jnp.concatenate(outs, axis=0)
    ms = _dot_split_lhs(o * o, ones_ref[...], 2) * (1.0 / HEAD_DIM)
    o = o * lax.rsqrt(ms + NORM_EPS) * norm_ref[...]
    o_ref[0] = (o * (og * jax.nn.sigmoid(og))).astype(BF16)


def _gla(gl, a2p, ab, norm4, tm=512):
    B, S, W = gl.shape
    C = GLA_CHUNK
    tri, ones = _tri_ones(C), _head_ones(HEAD_DIM)
    tile = lambda w: pl.BlockSpec((1, tm, w), lambda b, s: (b, s, 0))
    full = lambda a: pl.BlockSpec(a.shape, lambda b, s: (0,) * a.ndim)
    return pl.pallas_call(
        functools.partial(_gla_kernel, C=C),
        grid=(B, S // tm),
        in_specs=[tile(W), full(a2p), full(ab), full(norm4), full(tri), full(ones)],
        out_specs=tile(256),
        out_shape=jax.ShapeDtypeStruct((B, S, 256), BF16),
        scratch_shapes=[pltpu.VMEM((256, 128), F32)],
        compiler_params=_cparams(("parallel", "arbitrary")),
        name="gla",
    )(gl, a2p, ab, norm4, tri, ones)


_RV_W0, _RV_A0, _RV_KK, _RV_KA, _RV_RK, _RV_LN, _RV_V0 = range(7)


def _rwkv_kernel(*refs, has_vres, C):
    if has_vres:
        (rw_ref, h_ref, vf_ref, wz_ref, vec_ref, v1_ref, v2_ref, tri_ref, ones_ref, o_ref, state_ref) = refs
    else:
        (rw_ref, wz_ref, vec_ref, tri_ref, ones_ref, o_ref, vout_ref, state_ref) = refs
    s = pl.program_id(1)

    @pl.when(s == 0)
    def _():
        state_ref[...] = jnp.zeros_like(state_ref)

    rw = rw_ref[0]
    tm = rw.shape[0]
    r, k, v, z = rw[:, 0:256], rw[:, 256:512], rw[:, 512:768], rw[:, 768:896]
    vec = lambda i: vec_ref[i:i + 1, :]
    zl = lax.broadcasted_iota(jnp.int32, z.shape, 1)
    zf = jnp.where(zl < 32, jnp.tanh(z), jnp.where(zl < 64, z, jax.nn.sigmoid(z)))
    zz = jnp.dot(zf.astype(BF16), wz_ref[...], preferred_element_type=F32)
    w_log = -jax.nn.softplus(-(vec(_RV_W0) + zz[:, 0:256])) - 0.5
    lw = -jnp.exp(w_log)
    a_sig = jax.nn.sigmoid(vec(_RV_A0) + zz[:, 256:512])
    gate = zz[:, 512:768]
    if has_vres:
        low = jnp.dot(h_ref[0], v1_ref[...], preferred_element_type=F32)
        logit = vec(_RV_V0) + jnp.dot(low.astype(BF16), v2_ref[...], preferred_element_type=F32)
        v = v + (vf_ref[0] - v) * jax.nn.sigmoid(logit)
    else:
        vout_ref[0] = v
    kk = k * vec(_RV_KK)
    norm = jnp.sqrt(_dot_split_lhs(kk * kk, ones_ref[...], 2))
    kk = kk / jnp.maximum(norm, 1e-12)
    k = k * (1.0 + (a_sig - 1.0) * vec(_RV_KA))
    a_vec = -kk
    b_vec = kk * a_sig

    strict, incl = _block_masks(C)
    st = lambda t: _stack_heads(t, HEAD_DIM)
    bdot = lambda p, q: jnp.dot(p.astype(BF16), q.astype(BF16), preferred_element_type=F32)
    state_mask = (lax.broadcasted_iota(jnp.int32, (256, 256), 0) // HEAD_DIM
                  == lax.broadcasted_iota(jnp.int32, (256, 256), 1) // HEAD_DIM)
    n4 = 4 * C
    rcs = [slice(c * C, (c + 1) * C) for c in range(tm // C)]
    each = lambda fn, *lists: [fn(*args) for args in zip(*lists)]
    lcum = [_dot_split_rhs(tri_ref[...], lw[rc], 3) for rc in rcs]
    lex = each(lambda rc, lc: lc - lw[rc], rcs, lcum)
    mid = [lc[C // 2 - 1:C // 2, :] for lc in lcum]
    last = [lc[C - 1:C, :] for lc in lcum]
    e_mid = each(lambda lc, m: jnp.exp(m - lc), lcum, mid)
    e_end = each(lambda lc, l: jnp.exp(l - lc), lcum, last)
    left = each(lambda rc, lc, lx, m: jnp.concatenate(
        [st(a_vec[rc] * jnp.exp(lx - m)), st(r[rc] * jnp.exp(lc - m))], axis=0).astype(BF16), rcs, lcum, lex, mid)
    right = each(lambda rc, em: jnp.concatenate([st(b_vec[rc] * em), st(k[rc] * em)], axis=0).astype(BF16), rcs, e_mid)
    g = each(_dot_nt, left, right)
    v_st = [st(v[rc]) for rc in rcs]
    p = [jnp.where(strict, gc[0:n4, 0:n4], 0.0) for gc in g]
    rhs = each(lambda gc, vs: bdot(jnp.where(strict, gc[0:n4, n4:2 * n4], 0.0), vs), g, v_st)
    y_rk = each(lambda gc, vs: bdot(jnp.where(incl, gc[n4:2 * n4, n4:2 * n4], 0.0), vs), g, v_st)
    n_rb = [jnp.where(incl, gc[n4:2 * n4, 0:n4], 0.0) for gc in g]
    x = each(lambda rc, lx, rh: jnp.concatenate([st(a_vec[rc] * jnp.exp(lx)), rh], axis=1), rcs, lex, rhs)
    bk_end = each(lambda rc, ee: jnp.concatenate([b_vec[rc] * ee, k[rc] * ee], axis=0).astype(BF16), rcs, e_end)
    eye = (lax.broadcasted_iota(jnp.int32, (n4, n4), 0) == lax.broadcasted_iota(jnp.int32, (n4, n4), 1)).astype(F32)
    t = [eye + pc for pc in p]
    for f in range(1, int(np.log2(C))):
        p = [bdot(pc, pc) for pc in p]
        t = each(lambda tc, pc: tc + bdot(pc, tc), t, p)
    x = each(bdot, t, x)
    y_part = each(bdot, n_rb, x)
    a_eff = [_unstack_heads(xc[:, 0:256], C).astype(BF16) for xc in x]
    u0 = [_unstack_heads(xc[:, 256:512], C) for xc in x]
    r_eff = each(lambda rc, lc, yp: (r[rc] * jnp.exp(lc) + _unstack_heads(yp[:, 0:256], C)).astype(BF16),
                 rcs, lcum, y_part)
    y0 = each(lambda yp, yr: _unstack_heads(yp[:, 256:512] + yr, C), y_part, y_rk)
    mix = each(lambda ae, bk: jnp.where(state_mask, _dot_tn(ae, bk[0:C]), 0.0).astype(BF16), a_eff, bk_end)
    add = each(lambda rc, u, bk: jnp.where(
        state_mask, _dot_tn(jnp.concatenate([u, v[rc]], axis=0).astype(BF16), bk), 0.0), rcs, u0, bk_end)
    decay = [jnp.exp(l) for l in last]
    ys = []
    state = state_ref[...]
    for c in range(len(rcs)):
        sb = state.astype(BF16)
        ys.append(_dot_nt(r_eff[c], sb) + y0[c])
        state = state * decay[c] + jnp.dot(sb, mix[c], preferred_element_type=F32) + add[c]
    state_ref[...] = state
    y = jnp.concatenate(ys, axis=0)
    inv = 1.0 / HEAD_DIM
    mean = _dot_split_lhs(y, ones_ref[...], 2) * inv
    yc = y - mean
    var = _dot_split_lhs(yc * yc, ones_ref[...], 2) * inv
    y = yc * lax.rsqrt(var + RWKV_GN_EPS) * vec(_RV_LN)
    y = y + _dot_split_lhs(r * k * vec(_RV_RK), ones_ref[...], 2) * v
    o_ref[0] = (y * gate).astype(BF16)


def _rwkv(rw, wz, vecs, h=None, v_first=None, v1p=None, v2p=None, tm=256):
    B, S, W = rw.shape
    C = RWKV_CHUNK
    has_vres = h is not None
    tri, ones = _tri_ones(C), _head_ones(HEAD_DIM)
    tile = lambda w: pl.BlockSpec((1, tm, w), lambda b, s: (b, s, 0))
    full = lambda a: pl.BlockSpec(a.shape, lambda b, s: (0,) * a.ndim)
    if has_vres:
        args = (rw, h, v_first, wz, vecs, v1p, v2p, tri, ones)
        in_specs = [tile(W), tile(D_MODEL), tile(256), full(wz), full(vecs), full(v1p), full(v2p), full(tri), full(ones)]
        out_specs = tile(256)
        out_shape = jax.ShapeDtypeStruct((B, S, 256), BF16)
    else:
        args = (rw, wz, vecs, tri, ones)
        in_specs = [tile(W), full(wz), full(vecs), full(tri), full(ones)]
        out_specs = [tile(256), tile(256)]
        out_shape = [jax.ShapeDtypeStruct((B, S, 256), BF16), jax.ShapeDtypeStruct((B, S, 256), F32)]
    return pl.pallas_call(
        functools.partial(_rwkv_kernel, has_vres=has_vres, C=C),
        grid=(B, S // tm),
        in_specs=in_specs,
        out_specs=out_specs,
        out_shape=out_shape,
        scratch_shapes=[pltpu.VMEM((256, 256), F32)],
        compiler_params=_cparams(("parallel", "arbitrary")),
        name="rwkv_vres" if has_vres else "rwkv",
    )(*args)


def _rwkv_weights(l, rwkv_w0, rwkv_w2, rwkv_a0, rwkv_a2, rwkv_g2, rwkv_v0, rwkv_k_k, rwkv_k_a, rwkv_r_k, rwkv_ln):
    wz = jnp.zeros((128, 768), F32)
    wz = wz.at[0:32, 0:256].set(rwkv_w2[l]).at[32:64, 256:512].set(rwkv_a2[l]).at[64:128, 512:768].set(rwkv_g2[l])
    v0 = rwkv_v0[l - 1] if l > 0 else jnp.zeros((256,), F32)
    vecs = jnp.stack([rwkv_w0[l], rwkv_a0[l], rwkv_k_k[l], rwkv_k_a[l], rwkv_r_k[l].reshape(-1), rwkv_ln[l], v0,
                      jnp.zeros((256,), F32)])
    return wz.astype(BF16), vecs


def _merge_kernel(x_ref, h_ref, o0_ref, o1_ref, o2_ref, o3_ref, wg_ref, bg_ref, wb_ref, wo_ref, out_ref):
    h = h_ref[0]
    merged = None
    for i, o_ref in enumerate((o0_ref, o1_ref, o2_ref, o3_ref)):
        gate = jax.nn.sigmoid(jnp.dot(h, wg_ref[i], preferred_element_type=F32) + bg_ref[i:i + 1, :])
        term = gate * jnp.dot(o_ref[0], wb_ref[i], preferred_element_type=F32)
        merged = term if merged is None else merged + term
    out_ref[0] = x_ref[0] + jnp.dot(merged.astype(BF16), wo_ref[...], preferred_element_type=F32)


def _merge(x, h, o_nsa, o_pool, o_rwkv, o_gla, wg, bg, wb, wo, tm=512):
    B, S, D = x.shape
    tile = lambda w: pl.BlockSpec((1, tm, w), lambda b, s: (b, s, 0))
    full = lambda a: pl.BlockSpec(a.shape, lambda b, s: (0,) * a.ndim)
    return pl.pallas_call(
        _merge_kernel,
        grid=(B, S // tm),
        in_specs=[tile(D), tile(D), tile(256), tile(256), tile(256), tile(256), full(wg), full(bg), full(wb), full(wo)],
        out_specs=tile(D),
        out_shape=jax.ShapeDtypeStruct((B, S, D), F32),
        compiler_params=_cparams(("parallel", "parallel")),
        name="merge",
    )(x, h, o_nsa, o_pool, o_rwkv, o_gla, wg, bg, wb, wo)


def _memkv_kernel(mem_ref, g_ref, w_ref, o_ref):
    mn = _rms(mem_ref[0], g_ref[...]).astype(BF16)
    o_ref[0] = jnp.dot(mn, w_ref[...], preferred_element_type=F32).astype(BF16)


def _memkv(mem, g, wkv_all):
    B, M, D = mem.shape
    N = wkv_all.shape[1]
    return pl.pallas_call(
        _memkv_kernel,
        grid=(B,),
        in_specs=[pl.BlockSpec((1, M, D), lambda b: (b, 0, 0)), pl.BlockSpec(g.shape, lambda b: (0, 0)),
                  pl.BlockSpec(wkv_all.shape, lambda b: (0, 0))],
        out_specs=pl.BlockSpec((1, M, N), lambda b: (b, 0, 0)),
        out_shape=jax.ShapeDtypeStruct((B, M, N), BF16),
        compiler_params=_cparams(("parallel",)),
        name="memkv",
    )(mem, g, wkv_all)


def _cross_kernel(x_ref, g_ref, wq_ref, k_ref, v_ref, wo_ref, out_ref):
    x = x_ref[0]
    hc = _rms(x, g_ref[...]).astype(BF16)
    q = jnp.dot(hc, wq_ref[...], preferred_element_type=F32) * (HEAD_DIM ** -0.5)
    head = lax.broadcasted_iota(jnp.int32, q.shape, 1) // HEAD_DIM
    tm = q.shape[0]
    s = _dot_nt(_stack_heads(q, HEAD_DIM).astype(BF16), k_ref[0])
    e = jnp.exp(s - jnp.max(s, axis=1, keepdims=True))
    p = e / jnp.sum(e, axis=1, keepdims=True)
    o_all = jnp.dot(p.astype(BF16), v_ref[0], preferred_element_type=F32)
    o = jnp.zeros(q.shape, F32)
    for hh in range(CA_WIDTH // HEAD_DIM):
        o = jnp.where(head == hh, o_all[hh * tm:(hh + 1) * tm], o)
    out_ref[0] = x + jnp.dot(o.astype(BF16), wo_ref[...], preferred_element_type=F32)


def _cross(x, g, wq, memkv, l, wo, tm=512):
    B, S, D = x.shape
    M = memkv.shape[1]
    tile = pl.BlockSpec((1, tm, D), lambda b, s: (b, s, 0))
    full = lambda a: pl.BlockSpec(a.shape, lambda b, s: (0,) * a.ndim)
    kspec = pl.BlockSpec((1, M, CA_WIDTH), lambda b, s: (b, 0, 2 * l))
    vspec = pl.BlockSpec((1, M, CA_WIDTH), lambda b, s: (b, 0, 2 * l + 1))
    return pl.pallas_call(
        _cross_kernel,
        grid=(B, S // tm),
        in_specs=[tile, full(g), full(wq), kspec, vspec, full(wo)],
        out_specs=tile,
        out_shape=jax.ShapeDtypeStruct((B, S, D), F32),
        compiler_params=_cparams(("parallel", "parallel")),
        name="cross_attn",
    )(x, g, wq, memkv, memkv, wo)


def _swiglu_cols(xb, wg, wu, wd, tf):
    acc = jnp.zeros((xb.shape[0], wd.shape[-1]), F32)
    for c in range(wg.shape[-1] // tf):
        cols = slice(c * tf, (c + 1) * tf)
        gate = jnp.dot(xb, wg[:, cols], preferred_element_type=F32)
        up = jnp.dot(xb, wu[:, cols], preferred_element_type=F32)
        act = gate * jax.nn.sigmoid(gate) * up
        acc = acc + jnp.dot(act.astype(BF16), wd[cols, :], preferred_element_type=F32)
    return acc


def _ffn_kernel(x_ref, g_ref, wg_ref, wu_ref, wd_ref, out_ref, *, tf):
    x = x_ref[...]
    out_ref[...] = x + _swiglu_cols(_rms(x, g_ref[...]).astype(BF16), wg_ref, wu_ref, wd_ref, tf)


def _ffn(x2, g, wg, wu, wd, tm=512, tf=256):
    T, D = x2.shape
    full = lambda a: pl.BlockSpec(a.shape, lambda i: (0,) * a.ndim)
    return pl.pallas_call(
        functools.partial(_ffn_kernel, tf=tf),
        grid=(T // tm,),
        in_specs=[pl.BlockSpec((tm, D), lambda i: (i, 0)), full(g), full(wg), full(wu), full(wd)],
        out_specs=pl.BlockSpec((tm, D), lambda i: (i, 0)),
        out_shape=jax.ShapeDtypeStruct((T, D), F32),
        compiler_params=_cparams(("parallel",)),
        name="ffn",
    )(x2, g, wg, wu, wd)


MOE_TM = 512


SUBLANES = 8


def _to_slabs(ref, x):
    w = ref.shape[-1]
    for j in range(ref.shape[-2]):
        ref[:, j, :] = x[:, j * w:(j + 1) * w]


def _from_slabs(ref):
    return jnp.concatenate([ref[:, j, :] for j in range(ref.shape[-2])], axis=1)


def _route_kernel(x_ref, g_ref, r_ref, hf_ref, idx_ref, w_ref):
    hf = _rms(x_ref[...], g_ref[...])
    _to_slabs(hf_ref, hf)
    logits = jnp.dot(hf, r_ref[...], preferred_element_type=F32, precision=lax.Precision.HIGHEST)
    lane = lax.broadcasted_iota(jnp.int32, logits.shape, 1)
    big = logits.shape[1]
    logits = jnp.where(lane < N_EXPERTS, logits, -jnp.inf)
    m1 = jnp.max(logits, axis=1, keepdims=True)
    i1 = jnp.min(jnp.where(logits == m1, lane, big), axis=1, keepdims=True)
    rest = jnp.where(lane == i1, -jnp.inf, logits)
    m2 = jnp.max(rest, axis=1, keepdims=True)
    i2 = jnp.min(jnp.where(rest == m2, lane, big), axis=1, keepdims=True)
    e2 = jnp.exp(m2 - m1)
    w1 = 1.0 / (1.0 + e2)
    idx_ref[...] = jnp.where(lane == 0, i1, jnp.where(lane == 1, i2, 0))
    w_ref[...] = jnp.where(lane == 0, w1, jnp.where(lane == 1, e2 * w1, 0.0))


def _moe_route(x2, g, router_p, tm=1024):
    T, D = x2.shape
    row = lambda w: pl.BlockSpec((tm, w), lambda i: (i, 0))
    full = lambda a: pl.BlockSpec(a.shape, lambda i: (0,) * a.ndim)
    return pl.pallas_call(
        _route_kernel,
        grid=(T // tm,),
        in_specs=[row(D), full(g), full(router_p)],
        out_specs=[pl.BlockSpec((tm, SUBLANES, D // SUBLANES), lambda i: (i, 0, 0)), row(LANES), row(LANES)],
        out_shape=[jax.ShapeDtypeStruct((T, SUBLANES, D // SUBLANES), F32), jax.ShapeDtypeStruct((T, LANES), jnp.int32),
                   jax.ShapeDtypeStruct((T, LANES), F32)],
        compiler_params=_cparams(("parallel",)),
        name="moe_route",
    )(x2, g, router_p)


def _moe_plan(expert_of_pair, tm):
    onehot = (expert_of_pair[:, None] == jnp.arange(N_EXPERTS, dtype=jnp.int32)[None, :]).astype(jnp.int32)
    csum = jnp.cumsum(onehot, axis=0)
    rank = jnp.sum((csum - 1) * onehot, axis=1)
    counts = csum[-1]
    padded = ((counts + tm - 1) // tm) * tm
    ends = jnp.cumsum(padded)
    dest = ((ends - padded)[expert_of_pair] + rank).astype(jnp.int32)
    n_rows = expert_of_pair.shape[0] + N_EXPERTS * tm
    tile_expert = jnp.sum((jnp.arange(n_rows // tm, dtype=jnp.int32)[:, None] * tm >= ends[None, :]).astype(jnp.int32), axis=1)
    n_pairs = expert_of_pair.shape[0]
    pair_of_row = jnp.full((n_rows,), -1, jnp.int32).at[dest].set(
        jnp.arange(n_pairs, dtype=jnp.int32), unique_indices=True, mode='promise_in_bounds')
    real = pair_of_row >= 0
    token_of_row = jnp.where(real, pair_of_row // 2, 0)
    out_row = jnp.where(real, (pair_of_row % 2) * (n_pairs // 2) + pair_of_row // 2,
                        n_pairs + jnp.arange(n_rows, dtype=jnp.int32) % tm)
    return token_of_row, out_row, jnp.minimum(tile_expert, N_EXPERTS - 1).astype(jnp.int32)


DMA_UNROLL = 8


def _gather_rows(table_hbm, row_of, dst, n, sem, wait):
    def body(c, carry):
        for u in range(DMA_UNROLL):
            r = c * DMA_UNROLL + u
            copy = pltpu.make_async_copy(table_hbm.at[pl.ds(row_of(r), 1)], dst.at[pl.ds(r, 1)], sem)
            copy.wait() if wait else copy.start()
        return carry

    lax.fori_loop(0, n // DMA_UNROLL, body, 0)


def _scatter_rows(src, dst_hbm, row_of, n, sem, wait):
    def body(c, carry):
        for u in range(DMA_UNROLL):
            r = c * DMA_UNROLL + u
            copy = pltpu.make_async_copy(src.at[pl.ds(r, 1)], dst_hbm.at[pl.ds(row_of(r), 1)], sem)
            copy.wait() if wait else copy.start()
        return carry

    lax.fori_loop(0, n // DMA_UNROLL, body, 0)


def _experts_kernel(te_ref, tok_cur_ref, tok_next_ref, pair_prev_ref, pair_wait_ref, hf_hbm, wg_ref, wu_ref, wd_ref,
                    o2_hbm, xin, yout, sems, sem_out, *, tf):
    del te_ref
    i = pl.program_id(0)
    slot = i % 2
    tm = xin.shape[1]
    n_f = wg_ref.shape[2] // tf
    cur_tok = lambda r: tok_cur_ref[0, 0, r]
    next_tok = lambda r: tok_next_ref[0, 0, r]
    prev_pair = lambda r: pair_prev_ref[0, 0, r]

    @pl.when(i == 0)
    def _():
        _gather_rows(hf_hbm, cur_tok, xin.at[0], tm, sems.at[0], wait=False)
        yout[1] = jnp.zeros(yout.shape[1:], F32)

    @pl.when(i >= 1)
    def _():
        _scatter_rows(yout.at[slot], o2_hbm, lambda r: pair_wait_ref[0, 0, r], tm, sem_out, wait=True)

    _gather_rows(hf_hbm, cur_tok, xin.at[slot], tm, sems.at[slot], wait=True)
    xb = _from_slabs(xin.at[slot]).astype(BF16)
    acc = jnp.zeros((tm, wd_ref.shape[2]), F32)
    n_issue = max(n_f - 2, 1)
    bounds = [min(round(c * tm / n_issue), tm) for c in range(n_f + 1)]
    for c in range(n_f):
        for r in range(bounds[c], bounds[c + 1]):
            pltpu.make_async_copy(hf_hbm.at[pl.ds(next_tok(r), 1)], xin.at[1 - slot, pl.ds(r, 1)],
                                  sems.at[1 - slot]).start()
            pltpu.make_async_copy(yout.at[1 - slot, pl.ds(r, 1)], o2_hbm.at[pl.ds(prev_pair(r), 1)], sem_out).start()
        cols = slice(c * tf, (c + 1) * tf)
        gate = jnp.dot(xb, wg_ref[0, :, cols], preferred_element_type=F32)
        up = jnp.dot(xb, wu_ref[0, :, cols], preferred_element_type=F32)
        act = gate * jax.nn.sigmoid(gate) * up
        acc = acc + jnp.dot(act.astype(BF16), wd_ref[0, cols, :], preferred_element_type=F32)
    _to_slabs(yout.at[slot], acc)

    @pl.when(i == pl.num_programs(0) - 1)
    def _():
        _gather_rows(hf_hbm, next_tok, xin.at[1 - slot], tm, sems.at[1 - slot], wait=True)
        _scatter_rows(yout.at[1 - slot], o2_hbm, prev_pair, tm, sem_out, wait=True)


def _moe_experts(hf, token_of_row, out_row_of_row, tile_expert, wg, wu, wd, tf=256):
    T = hf.shape[0]
    slab = hf.shape[1:]
    tm = MOE_TM
    n_tiles = token_of_row.shape[0] // tm
    tok = token_of_row.reshape(n_tiles, 1, tm)
    tok = jnp.concatenate([tok, tok[-1:], tok[-1:]], axis=0)
    spare = (2 * T + jnp.arange(tm, dtype=jnp.int32)).reshape(1, 1, tm)
    pair = jnp.concatenate([spare, out_row_of_row.reshape(n_tiles, 1, tm)], axis=0)
    te = jnp.concatenate([tile_expert, tile_expert[-1:]])
    smem = lambda index: pl.BlockSpec((1, 1, tm), lambda i, te: (index(i), 0, 0), memory_space=pltpu.SMEM)
    expert = lambda shape: pl.BlockSpec((1,) + shape, lambda i, te: (te[i], 0, 0))
    any_spec = pl.BlockSpec(memory_space=pl.ANY)
    grid_spec = pltpu.PrefetchScalarGridSpec(
        num_scalar_prefetch=1,
        grid=(n_tiles + 1,),
        in_specs=[smem(lambda i: i), smem(lambda i: i + 1), smem(lambda i: i), smem(lambda i: jnp.maximum(i - 1, 0)),
                  any_spec, expert(wg.shape[1:]), expert(wu.shape[1:]), expert(wd.shape[1:])],
        out_specs=any_spec,
        scratch_shapes=[pltpu.VMEM((2, tm) + slab, F32), pltpu.VMEM((2, tm) + slab, F32), pltpu.SemaphoreType.DMA((2,)),
                        pltpu.SemaphoreType.DMA(())],
    )
    return pl.pallas_call(
        functools.partial(_experts_kernel, tf=tf),
        grid_spec=grid_spec,
        out_shape=jax.ShapeDtypeStruct((2 * T + tm,) + slab, F32),
        compiler_params=_cparams(("arbitrary",)),
        name="moe_experts",
    )(te, tok, tok, pair, pair, hf, wg, wu, wd)


def _combine_kernel(x_ref, w_ref, gf_ref, y0_ref, y1_ref, out_ref):
    w = w_ref[...]
    y = x_ref[...] + w[:, 0:1] * _from_slabs(y0_ref) + w[:, 1:2] * _from_slabs(y1_ref)
    out_ref[...] = _rms(y, gf_ref[...])


def _moe_combine(x2, w, o2, g_final, tm=512):
    T, D = x2.shape
    row = lambda wd, off: pl.BlockSpec((tm, wd), lambda i: (i + off, 0))
    slab = lambda off: pl.BlockSpec((tm,) + o2.shape[1:], lambda i: (i + off, 0, 0))
    return pl.pallas_call(
        _combine_kernel,
        grid=(T // tm,),
        in_specs=[row(D, 0), row(LANES, 0), pl.BlockSpec(g_final.shape, lambda i: (0, 0)), slab(0), slab(T // tm)],
        out_specs=row(D, 0),
        out_shape=jax.ShapeDtypeStruct((T, D), F32),
        compiler_params=_cparams(("parallel",)),
        name="moe_combine",
    )(x2, w, g_final, o2, o2)


def _moe_routed(x2, g, router_p, wg, wu, wd, g_final):
    T, D = x2.shape
    hf, idx, w = _moe_route(x2, g, router_p)
    token_of_row, out_row_of_row, tile_expert = _moe_plan(idx[:, 0:2].reshape(2 * T), MOE_TM)
    o2 = _moe_experts(hf, token_of_row, out_row_of_row, tile_expert, wg, wu, wd)
    return _moe_combine(x2, w, o2, g_final)


def kernel(x, mem, positions, norm_mix, w_in, nsa_cmp_pe, nsa_cmp_w1, nsa_cmp_w2, pool_w, pool_scale, rwkv_mu, rwkv_w0,
           rwkv_w2, rwkv_a0, rwkv_a2, rwkv_g2, rwkv_v0, rwkv_v1, rwkv_v2, rwkv_k_k, rwkv_k_a, rwkv_r_k, rwkv_ln, gla_a2,
           gla_ab, gla_norm, w_gate, b_gate, w_branch, w_out, norm_ca, mem_norm, w_ca_q, w_ca_kv, w_ca_o, norm_ffn,
           ffn_w_gate, ffn_w_up, ffn_w_down, moe_router, moe_w_gate, moe_w_up, moe_w_down, norm_final):
    B, S, D = x.shape
    depth = norm_mix.shape[0]
    assert depth == 2 and S % 256 == 0, "the final norm is fused into the layer-1 expert mixer"
    cs = _rope_angles(positions)
    cs_end = jnp.pad(cs[:, CMP_LEN - 1::CMP_STRIDE], ((0, 0), (0, 1), (0, 0)))
    memkv = _memkv(mem, mem_norm[None], jnp.concatenate([w_ca_kv[l] for l in range(depth)], axis=1).astype(BF16))
    v_first = None
    for l in range(depth):
        h, q_t, k2, v_t, cv, gates_t, u, rw, gl = _inproj(x, norm_mix[l][None], _inproj_weights(w_in[l]), cs,
                                                          rwkv_mu[l][None])
        kcmp, vcmp_t = _nsa_compress(cv, *_compress_weights(nsa_cmp_pe[l], nsa_cmp_w1[l], nsa_cmp_w2[l]), cs_end)
        o_nsa = _nsa_attend(q_t, k2, v_t, kcmp, vcmp_t, gates_t)
        o_pool = _pool(u, _block_diag(pool_w[l]).astype(BF16), pool_scale[l][None])
        wz, vecs = _rwkv_weights(l, rwkv_w0, rwkv_w2, rwkv_a0, rwkv_a2, rwkv_g2, rwkv_v0, rwkv_k_k, rwkv_k_a,
                                 rwkv_r_k, rwkv_ln)
        if l == 0:
            o_rwkv, v_first = _rwkv(rw, wz, vecs)
        else:
            v1p = jnp.pad(rwkv_v1[l - 1], ((0, 0), (0, LANES - rwkv_v1.shape[2]))).astype(BF16)
            v2p = jnp.pad(rwkv_v2[l - 1], ((0, LANES - rwkv_v2.shape[1]), (0, 0))).astype(BF16)
            o_rwkv = _rwkv(rw, wz, vecs, h, v_first, v1p, v2p)
        a2p = jnp.pad(gla_a2[l], ((0, LANES - gla_a2.shape[1]), (0, 0))).astype(BF16)
        o_gla = _gla(gl, a2p, gla_ab[l][None], jnp.tile(gla_norm[l], 4)[None])
        wb_nsa = w_branch[l, 0].reshape(4, HEAD_DIM, D)[jnp.array([0, 2, 1, 3])].reshape(BRANCH_WIDTH, D)
        wb = jnp.concatenate([wb_nsa[None], w_branch[l, 1:]], axis=0).astype(BF16)
        x = _merge(x, h, o_nsa, o_pool, o_rwkv, o_gla, w_gate[l].astype(BF16), b_gate[l], wb, w_out[l].astype(BF16))
        x = _cross(x, norm_ca[l][None], w_ca_q[l].astype(BF16), memkv, l, w_ca_o[l].astype(BF16))
        j = l // 2
        x2 = x.reshape(B * S, D)
        if l % 2 == 0:
            x2 = _ffn(x2, norm_ffn[l][None], ffn_w_gate[j].astype(BF16), ffn_w_up[j].astype(BF16),
                      ffn_w_down[j].astype(BF16))
        else:
            router_p = jnp.pad(moe_router[j], ((0, 0), (0, LANES - N_EXPERTS)))
            x2 = _moe_routed(x2, norm_ffn[l][None], router_p, moe_w_gate[j].astype(BF16), moe_w_up[j].astype(BF16),
                             moe_w_down[j].astype(BF16), norm_final[None])
        x = x2.reshape(B, S, D)
    return x
```
